```python
import math
import jax, jax.numpy as jnp
from jax import lax
import numpy as np

D_MODEL = 1024
BATCH = 4
SEQ = 8192
DEPTH = 1

N_META = 16
N_Q_HEADS = 8
N_KV_HEADS = 2
HEAD_DIM = 64
Q_GROUP = N_Q_HEADS // N_KV_HEADS
ATTN_DIM = N_Q_HEADS * HEAD_DIM
KV_DIM = N_KV_HEADS * HEAD_DIM
WINDOW = 128
BLK = 128
PAD = BLK - N_META
N_BUCKETS = 32
MAX_DISTANCE = 128
CONV_CH = D_MODEL // 2
CONV_W = 3
IN_PROJ = ATTN_DIM + 2 * KV_DIM + 3 * CONV_CH
MIX_DIM = ATTN_DIM + CONV_CH
SPLITS = (ATTN_DIM, ATTN_DIM + KV_DIM, ATTN_DIM + 2 * KV_DIM,
          ATTN_DIM + 2 * KV_DIM + CONV_CH, ATTN_DIM + 2 * KV_DIM + 2 * CONV_CH)
N_GROUPS = 4
EXPERTS_PER_GROUP = 8
N_EXPERTS = N_GROUPS * EXPERTS_PER_GROUP
TOP_K = 2
D_EXPERT = D_MODEL // 2
MOE_BLK = 128
EPS = 1e-6
NEG_INF = -1e30

kernel_name = 'hymba_swa_sink_shortconv_hier_moe'


def rms_norm(x, g):
    xf = x.astype(jnp.float32)
    y = xf * lax.rsqrt(jnp.mean(xf * xf, axis=-1, keepdims=True) + EPS)
    return y.astype(x.dtype) * g


def t5_causal_bucket(n):
    max_exact = N_BUCKETS // 2
    nf = jnp.maximum(n, 1).astype(jnp.float32)
    large = max_exact + (jnp.log(nf / max_exact) / math.log(MAX_DISTANCE / max_exact)
                         * (N_BUCKETS - max_exact)).astype(jnp.int32)
    large = jnp.minimum(large, N_BUCKETS - 1)
    return jnp.where(n < max_exact, n, large)


def band_bias_and_mask(rel_bias, n_blocks):
    qi = jnp.arange(BLK)[:, None]
    sj = jnp.arange(2 * BLK)[None, :]
    dist = BLK + qi - sj
    band = (dist >= 0) & (dist < WINDOW)
    bucket = t5_causal_bucket(jnp.maximum(dist, 0))
    bias = jnp.transpose(rel_bias[bucket].astype(jnp.float32), (2, 0, 1))
    bias = bias.reshape(N_KV_HEADS, Q_GROUP, BLK, 2 * BLK)
    key_pos = (jnp.arange(n_blocks)[:, None, None] - 1) * BLK + sj[None]
    mask = band[None] & (key_pos >= PAD)
    return bias, mask


def sliding_window_sink_attention(q, k, v, sinks, rel_bias):
    bsz, seq = q.shape[0], q.shape[1]
    lp = seq + PAD
    nb = lp // BLK
    pad = ((0, 0), (PAD, 0), (0, 0), (0, 0))
    q = jnp.pad(q, pad).reshape(bsz, nb, BLK, N_KV_HEADS, Q_GROUP, HEAD_DIM)
    k = jnp.pad(k, pad).reshape(bsz, nb, BLK, N_KV_HEADS, HEAD_DIM)
    v = jnp.pad(v, pad).reshape(bsz, nb, BLK, N_KV_HEADS, HEAD_DIM)
    prev = lambda t: jnp.pad(t, ((0, 0), (1, 0), (0, 0), (0, 0), (0, 0)))[:, :-1]
    kw = jnp.concatenate([prev(k), k], axis=2)
    vw = jnp.concatenate([prev(v), v], axis=2)
    bias, mask = band_bias_and_mask(rel_bias, nb)
    s = jnp.einsum('bnqkgd,bnskd->bnkgqs', q, kw).astype(jnp.float32) * (HEAD_DIM ** -0.5) + bias
    s = jnp.where(mask[None, :, None, None], s, NEG_INF)
    sink = sinks.astype(jnp.float32).reshape(1, 1, N_KV_HEADS, Q_GROUP, 1, 1)
    m = jnp.maximum(jnp.max(s, axis=-1, keepdims=True), sink)
    p = jnp.exp(s - m)
    p = p / (jnp.sum(p, axis=-1, keepdims=True) + jnp.exp(sink - m))
    o = jnp.einsum('bnkgqs,bnskd->bnqkgd', p.astype(v.dtype), vw)
    return o.reshape(bsz, lp, ATTN_DIM)[:, PAD:]


def short_gated_conv(b_gate, c_gate, h, conv_w):
    u = c_gate * h
    y = lax.conv_general_dilated(u, conv_w[:, None, :].astype(u.dtype), (1,), [(CONV_W - 1, 0)],
                                 dimension_numbers=('NWC', 'WIO', 'NWC'),
                                 feature_group_count=CONV_CH)
    return b_gate * y


def hierarchical_moe(xt, w_gr, b_gr, w_er, b_er, w_g, w_u, w_d):
    n_tok = xt.shape[0]
    xf = xt.astype(jnp.float32)
    g_prob = jax.nn.softmax(xf @ w_gr.astype(jnp.float32) + b_gr.astype(jnp.float32), axis=-1)
    g_p, g_idx = lax.top_k(g_prob, 1)
    e_logits = jnp.einsum('td,dge->tge', xf, w_er.astype(jnp.float32)) + b_er.astype(jnp.float32)
    e_logits = jnp.take_along_axis(e_logits, g_idx[:, :, None], axis=1)[:, 0]
    e_val, e_idx = lax.top_k(e_logits, TOP_K)
    gates = g_p * jax.nn.softmax(e_val, axis=-1)
    expert_id = g_idx * EXPERTS_PER_GROUP + e_idx
    n_asg = n_tok * TOP_K
    eid = expert_id.reshape(n_asg)
    order = jnp.argsort(eid)
    eid_s = eid[order]
    tok_s = order // TOP_K
    gate_s = gates.reshape(n_asg)[order]
    counts = jnp.bincount(eid, length=N_EXPERTS)
    padded = (counts + MOE_BLK - 1) // MOE_BLK * MOE_BLK
    pend = jnp.cumsum(padded)
    start = jnp.cumsum(counts) - counts
    dest = (pend - padded)[eid_s] + jnp.arange(n_asg) - start[eid_s]
    n_blocks = -(-n_asg // MOE_BLK) + N_EXPERTS
    xbuf = jnp.zeros((n_blocks * MOE_BLK, D_MODEL), xt.dtype).at[dest].set(xt[tok_s])
    block_e = jnp.minimum(jnp.searchsorted(pend, jnp.arange(n_blocks) * MOE_BLK, side='right'),
                          N_EXPERTS - 1)

    def expert_block(args):
        xb, e = args
        return (jax.nn.silu(xb @ w_g[e]) * (xb @ w_u[e])) @ w_d[e]

    ybuf = lax.map(expert_block, (xbuf.reshape(n_blocks, MOE_BLK, D_MODEL), block_e))
    y = ybuf.reshape(n_blocks * MOE_BLK, D_MODEL)[dest] * gate_s[:, None].astype(xt.dtype)
    return jax.ops.segment_sum(y, tok_s, num_segments=n_tok)


def setup_inputs(seed: int = 0) -> dict:
    key = jax.random.key(seed)
    ks = jax.random.split(key, 20)
    f32 = jnp.float32

    def nrm(k, shape, scale):
        return jax.random.normal(k, shape, f32) * scale

    def gain(k, shape):
        return 1.0 + 0.02 * jax.random.normal(k, shape, f32)

    return {
        'x': nrm(ks[0], (BATCH, SEQ, D_MODEL), 1.0),
        'meta_tokens': nrm(ks[1], (N_META, D_MODEL), 1.0),
        'rel_bias': nrm(ks[2], (N_BUCKETS, N_Q_HEADS), 0.2),
        'mix_norm_g': gain(ks[3], (DEPTH, D_MODEL)),
        'w_in': nrm(ks[4], (DEPTH, D_MODEL, IN_PROJ), D_MODEL ** -0.5),
        'q_norm_g': gain(ks[5], (DEPTH, HEAD_DIM)),
        'k_norm_g': gain(ks[6], (DEPTH, HEAD_DIM)),
        'attn_sinks': nrm(ks[7], (DEPTH, N_Q_HEADS), 0.5),
        'conv_w': nrm(ks[8], (DEPTH, CONV_W, CONV_CH), CONV_W ** -0.5),
        'attn_out_norm_g': gain(ks[9], (DEPTH, ATTN_DIM)),
        'conv_out_norm_g': gain(ks[10], (DEPTH, CONV_CH)),
        'w_out': nrm(ks[11], (DEPTH, MIX_DIM, D_MODEL), MIX_DIM ** -0.5),
        'ffn_norm_g': gain(ks[12], (DEPTH, D_MODEL)),
        'w_group_router': nrm(ks[13], (DEPTH, D_MODEL, N_GROUPS), D_MODEL ** -0.5),
        'b_group_router': nrm(ks[14], (DEPTH, N_GROUPS), 0.01),
        'w_expert_router': nrm(ks[15], (DEPTH, D_MODEL, N_GROUPS, EXPERTS_PER_GROUP), D_MODEL ** -0.5),
        'b_expert_router': nrm(ks[16], (DEPTH, N_GROUPS, EXPERTS_PER_GROUP), 0.01),
        'w_gate': nrm(ks[17], (DEPTH, N_EXPERTS, D_MODEL, D_EXPERT), D_MODEL ** -0.5),
        'w_up': nrm(ks[18], (DEPTH, N_EXPERTS, D_MODEL, D_EXPERT), D_MODEL ** -0.5),
        'w_down': nrm(ks[19], (DEPTH, N_EXPERTS, D_EXPERT, D_MODEL), D_EXPERT ** -0.5),
    }


def reference(x, meta_tokens, rel_bias, mix_norm_g, w_in, q_norm_g, k_norm_g, attn_sinks, conv_w,
              attn_out_norm_g, conv_out_norm_g, w_out, ffn_norm_g, w_group_router, b_group_router,
              w_expert_router, b_expert_router, w_gate, w_up, w_down):
    bsz = x.shape[0]
    meta = jnp.broadcast_to(meta_tokens.astype(x.dtype)[None], (bsz, N_META, D_MODEL))
    h = jnp.concatenate([meta, x], axis=1)
    seq = h.shape[1]
    for l in range(DEPTH):
        hn = rms_norm(h, mix_norm_g[l])
        proj = hn @ w_in[l]
        q, k, v, cb, cc, ch = jnp.split(proj, SPLITS, axis=-1)
        q = rms_norm(q.reshape(bsz, seq, N_Q_HEADS, HEAD_DIM), q_norm_g[l])
        k = rms_norm(k.reshape(bsz, seq, N_KV_HEADS, HEAD_DIM), k_norm_g[l])
        v = v.reshape(bsz, seq, N_KV_HEADS, HEAD_DIM)
        a = sliding_window_sink_attention(q, k, v, attn_sinks[l], rel_bias)
        c = short_gated_conv(cb, cc, ch, conv_w[l])
        mixed = jnp.concatenate([rms_norm(a, attn_out_norm_g[l]),
                                 rms_norm(c, conv_out_norm_g[l])], axis=-1)
        h = h + mixed @ w_out[l]
        hn = rms_norm(h, ffn_norm_g[l])
        y = hierarchical_moe(hn.reshape(bsz * seq, D_MODEL), w_group_router[l], b_group_router[l],
                             w_expert_router[l], b_expert_router[l], w_gate[l], w_up[l], w_down[l])
        h = h + y.reshape(bsz, seq, D_MODEL)
    return h[:, N_META:]
```

```python
import functools

import numpy as np
import jax
import jax.numpy as jnp
from jax import lax
from jax.experimental import pallas as pl
from jax.experimental.pallas import tpu as pltpu

D_MODEL = 1024
N_META = 16
N_Q_HEADS = 8
N_KV_HEADS = 2
HEAD_DIM = 64
Q_GROUP = N_Q_HEADS // N_KV_HEADS
ATTN_DIM = N_Q_HEADS * HEAD_DIM
KV_DIM = N_KV_HEADS * HEAD_DIM
BLK = 128
PAD = BLK - N_META
N_BUCKETS = 32
MAX_DISTANCE = 128
CONV_CH = D_MODEL // 2
IN_PROJ = ATTN_DIM + 2 * KV_DIM + 3 * CONV_CH
N_GROUPS = 4
EXPERTS_PER_GROUP = 8
N_EXPERTS = N_GROUPS * EXPERTS_PER_GROUP
D_EXPERT = D_MODEL // 2
EPS = 1e-6
NEG_INF = -1e30

LANES = 128
SUBLANES = 8
ROW_CHUNKS = D_MODEL // LANES
TM = 512
TME = 256
TMC = 256
ROUTE_W = 8
VMEM_LIMIT = 56 * 1024 * 1024

Q_OFF, K_OFF, V_OFF = 0, ATTN_DIM, ATTN_DIM + KV_DIM
CB_OFF = ATTN_DIM + 2 * KV_DIM
CC_OFF = CB_OFF + CONV_CH
CH_OFF = CC_OFF + CONV_CH

_f32 = jnp.float32
_bf16 = jnp.bfloat16


def _rms(x, g):
    return x * lax.rsqrt(jnp.mean(x * x, axis=-1, keepdims=True) + EPS) * g


def _dot(a, b):
    return jnp.dot(a, b, preferred_element_type=_f32)


def _dup_halves(x):
    lane = lax.broadcasted_iota(jnp.int32, x.shape, 1)
    sw = pltpu.roll(x, HEAD_DIM, axis=1)
    lo = lane < HEAD_DIM
    return jnp.where(lo, x, sw).astype(_bf16), jnp.where(lo, sw, x).astype(_bf16)


def _kv_state(hn_bf, win_ref, kg_ref, bdk_ref):
    kv = _dot(hn_bf, win_ref[:, K_OFF:K_OFF + 2 * KV_DIM])
    k = kv[:, :KV_DIM]
    v = kv[:, KV_DIM:]
    ssk = _dot((k * k).astype(_bf16), bdk_ref[...])
    kn = k * lax.rsqrt(ssk * (1.0 / HEAD_DIM) + EPS) * kg_ref[...]
    return _dup_halves(kn) + _dup_halves(v)


def _prep_kernel(rb_ref, meta_ref, mixg_ref, win_ref, kg_ref, bdk_ref, bucket_ref,
                 k0_ref, k1_ref, v0_ref, v1_ref, ut_ref, bias_ref):
    hn = _rms(meta_ref[...], mixg_ref[...]).astype(_bf16)
    k0, k1, v0, v1 = _kv_state(hn, win_ref, kg_ref, bdk_ref)
    k0_ref[...] = k0
    k1_ref[...] = k1
    v0_ref[...] = v0
    v1_ref[...] = v1
    cch = _dot(hn, win_ref[:, CC_OFF:CC_OFF + 2 * CONV_CH])
    u = cch[:, :CONV_CH] * cch[:, CONV_CH:]
    ut_ref[...] = u[BLK - SUBLANES:, :]
    for f in range(2):
        bk = bucket_ref[f]
        for h in range(N_Q_HEADS):
            acc = jnp.full((BLK, 2 * BLK), NEG_INF, _f32)
            for b in range(N_BUCKETS):
                acc = jnp.where(bk == b, rb_ref[b, h], acc)
            bias_ref[f, h // Q_GROUP, (h % Q_GROUP) * BLK:(h % Q_GROUP + 1) * BLK, :] = acc


def _mixer_kernel(sinks_ref,
                  x_ref, mixg_ref, win_ref, qg_ref, kg_ref, bias_ref, convw_ref, ag_ref, cg_ref,
                  wout_ref, fg_ref, wr_ref, br_ref, k0m_ref, k1m_ref, v0m_ref, v1m_ref, utm_ref,
                  bdq_ref, bdk_ref, tri_ref,
                  h2_ref, hn2_ref, route_ref, cnt_ref,
                  kp0, kp1, vp0, vp1, ubuf, a_scr, cnt_acc):
    b = pl.program_id(0)
    t = pl.program_id(1)

    @pl.when(t == 0)
    def _():
        kp0[...] = k0m_ref[...]
        kp1[...] = k1m_ref[...]
        vp0[...] = v0m_ref[...]
        vp1[...] = v1m_ref[...]
        ubuf[0:SUBLANES, :] = utm_ref[...]

    @pl.when(jnp.logical_and(b == 0, t == 0))
    def _():
        cnt_acc[...] = jnp.zeros_like(cnt_acc)

    x = x_ref[...]
    hn = _rms(x, mixg_ref[...]).astype(_bf16)

    q = _dot(hn, win_ref[:, Q_OFF:Q_OFF + ATTN_DIM])
    ssq = _dot((q * q).astype(_bf16), bdq_ref[...])
    qn = (q * lax.rsqrt(ssq * (1.0 / HEAD_DIM) + EPS) * qg_ref[...]).astype(_bf16)
    kd0, kd1, vd0, vd1 = _kv_state(hn, win_ref, kg_ref, bdk_ref)
    kd = (kd0, kd1)
    vd = (vd0, vd1)
    kp = (kp0, kp1)
    vp = (vp0, vp1)

    lane_q = lax.broadcasted_iota(jnp.int32, (BLK, LANES), 1)
    lo_half = lane_q < HEAD_DIM
    row4 = lax.broadcasted_iota(jnp.int32, (Q_GROUP * BLK, 1), 0) // BLK
    first = jnp.where(t == 0, 0, 1)
    zero_bf = jnp.zeros((BLK, LANES), _bf16)

    for j in range(TM // BLK):
        rows = slice(j * BLK, (j + 1) * BLK)
        for g in range(N_KV_HEADS):
            if j == 0:
                kcat = jnp.concatenate([kp[g][...], kd[g][rows]], axis=0)
                vcat = jnp.concatenate([vp[g][...], vd[g][rows]], axis=0)
                bias = bias_ref[first, g]
            else:
                kcat = kd[g][(j - 1) * BLK:(j + 1) * BLK]
                vcat = vd[g][(j - 1) * BLK:(j + 1) * BLK]
                bias = bias_ref[1, g]
            qs = []
            for hh in range(Q_GROUP):
                h = g * Q_GROUP + hh
                qc = qn[rows, (h // 2) * LANES:(h // 2 + 1) * LANES]
                keep = lo_half if h % 2 == 0 else jnp.logical_not(lo_half)
                qs.append(jnp.where(keep, qc, zero_bf))
            q4 = jnp.concatenate(qs, axis=0)
            s = lax.dot_general(q4, kcat, (((1,), (1,)), ((), ())),
                                preferred_element_type=_f32) + bias
            sink = jnp.full((Q_GROUP * BLK, 1), sinks_ref[g * Q_GROUP], _f32)
            for hh in range(1, Q_GROUP):
                sink = jnp.where(row4 == hh, sinks_ref[g * Q_GROUP + hh], sink)
            m = jnp.maximum(jnp.max(s, axis=-1, keepdims=True), sink)
            p = jnp.exp(s - m)
            l = jnp.sum(p, axis=-1, keepdims=True) + jnp.exp(sink - m)
            o = _dot(p.astype(_bf16), vcat) / l
            for pp in range(Q_GROUP // 2):
                ev = o[(2 * pp) * BLK:(2 * pp + 1) * BLK]
                od = o[(2 * pp + 1) * BLK:(2 * pp + 2) * BLK]
                col = g * (Q_GROUP // 2) + pp
                a_scr[rows, col * LANES:(col + 1) * LANES] = jnp.where(lo_half, ev, od)

    last = slice(TM - BLK, TM)
    kp0[...] = kd0[last]
    kp1[...] = kd1[last]
    vp0[...] = vd0[last]
    vp1[...] = vd1[last]

    cb = _dot(hn, win_ref[:, CB_OFF:CB_OFF + CONV_CH])
    cch = _dot(hn, win_ref[:, CC_OFF:CC_OFF + 2 * CONV_CH])
    u = cch[:, :CONV_CH] * cch[:, CONV_CH:]
    ubuf[SUBLANES:, :] = u
    u1 = ubuf[SUBLANES - 1:SUBLANES - 1 + TM, :]
    u2 = ubuf[SUBLANES - 2:SUBLANES - 2 + TM, :]
    c = cb * (convw_ref[0:1, :] * u2 + convw_ref[1:2, :] * u1 + convw_ref[2:3, :] * u)
    ubuf[0:SUBLANES, :] = u[TM - SUBLANES:, :]

    an = _rms(a_scr[...], ag_ref[...]).astype(_bf16)
    cn = _rms(c, cg_ref[...]).astype(_bf16)
    h2 = x + _dot(an, wout_ref[0:ATTN_DIM, :]) + _dot(cn, wout_ref[ATTN_DIM:, :])
    h2_ref[...] = h2

    hn2 = _rms(h2, fg_ref[...])
    for cc in range(ROW_CHUNKS):
        hn2_ref[pl.ds(cc, TM, stride=ROW_CHUNKS), :] = hn2[:, cc * LANES:(cc + 1) * LANES]

    hi = hn2.astype(_bf16)
    lo = (hn2 - hi.astype(_f32)).astype(_bf16)
    r1 = _dot(hi, wr_ref[...])
    r2 = _dot(lo, wr_ref[...])
    lg = r1 + pltpu.roll(r1, HEAD_DIM, axis=1) + r2 + br_ref[...]
    lane = lax.broadcasted_iota(jnp.int32, (TM, LANES), 1)
    lanef = lane.astype(_f32)
    ninf = jnp.float32(-jnp.inf)
    big = jnp.float32(LANES)

    gl = jnp.where(lane < N_GROUPS, lg, ninf)
    gmax = jnp.max(gl, axis=-1, keepdims=True)
    gsum = jnp.sum(jnp.exp(gl - gmax), axis=-1, keepdims=True)
    g_p = 1.0 / gsum
    g_idx = jnp.min(jnp.where(gl == gmax, lanef, big), axis=-1, keepdims=True)
    e_lo = N_GROUPS + EXPERTS_PER_GROUP * g_idx
    el = jnp.where(jnp.logical_and(lanef >= e_lo, lanef < e_lo + EXPERTS_PER_GROUP), lg, ninf)
    m1 = jnp.max(el, axis=-1, keepdims=True)
    i1 = jnp.min(jnp.where(el == m1, lanef, big), axis=-1, keepdims=True)
    el2 = jnp.where(lanef == i1, ninf, el)
    m2 = jnp.max(el2, axis=-1, keepdims=True)
    i2 = jnp.min(jnp.where(el2 == m2, lanef, big), axis=-1, keepdims=True)
    ex = jnp.exp(m2 - m1)
    den = 1.0 / (1.0 + ex)
    gate0 = g_p * den
    gate1 = g_p * ex * den
    e0 = i1 - N_GROUPS
    e1 = i2 - N_GROUPS

    oh0 = lanef == e0
    oh1 = lanef == e1
    cmat = (jnp.where(oh0, 1.0, 0.0) + jnp.where(oh1, 1.0, 0.0))
    prefix = _dot(tri_ref[...], cmat.astype(_bf16)) + cnt_acc[...]
    rank0 = jnp.sum(jnp.where(oh0, prefix, 0.0), axis=-1, keepdims=True)
    rank1 = jnp.sum(jnp.where(oh1, prefix, 0.0), axis=-1, keepdims=True)
    cnt_new = cnt_acc[...] + jnp.sum(cmat, axis=0, keepdims=True)
    cnt_acc[...] = cnt_new
    cnt_ref[...] = cnt_new

    lane8 = lax.broadcasted_iota(jnp.int32, (TM, ROUTE_W), 1)
    rec = jnp.zeros((TM, ROUTE_W), _f32)
    for idx, val in enumerate((e0, e1, gate0, gate1, rank0, rank1)):
        rec = jnp.where(lane8 == idx, val, rec)
    route_ref[...] = rec


def _row_tile(ref, row):
    return ref.at[pl.ds(pl.multiple_of(row * ROW_CHUNKS, ROW_CHUNKS), ROW_CHUNKS), :]


def _experts_kernel(te_ref, tv_ref, src_ref,
                    hn2_hbm, wg_ref, wu_ref, wd_ref,
                    y_ref,
                    xbuf, sem, wgb, wub, wdb):
    i = pl.program_id(0)
    n = pl.num_programs(0)
    slot = i % 2

    def issue(tile, sl):
        base = tile * TME

        def body(r, carry):
            pltpu.make_async_copy(_row_tile(hn2_hbm, src_ref[base + r]),
                                  _row_tile(xbuf.at[sl], r), sem.at[sl]).start()
            return carry

        lax.fori_loop(0, TME, body, 0)

    @pl.when(jnp.logical_and(i == 0, tv_ref[0] == 1))
    def _():
        issue(0, 0)

    @pl.when(jnp.logical_and(i + 1 < n, tv_ref[jnp.minimum(i + 1, n - 1)] == 1))
    def _():
        issue(i + 1, 1 - slot)

    changed = jnp.logical_or(i == 0, te_ref[i] != te_ref[jnp.maximum(i - 1, 0)])

    @pl.when(jnp.logical_and(changed, tv_ref[i] == 1))
    def _():
        wgb[...] = wg_ref[...].astype(_bf16)
        wub[...] = wu_ref[...].astype(_bf16)
        wdb[...] = wd_ref[...].astype(_bf16)

    @pl.when(tv_ref[i] == 1)
    def _():
        pltpu.make_async_copy(hn2_hbm.at[pl.ds(0, TME * ROW_CHUNKS), :], xbuf.at[slot], sem.at[slot]).wait()
        xs = [xbuf[slot, pl.ds(cc, TME, stride=ROW_CHUNKS), :] for cc in range(ROW_CHUNKS)]
        xb = jnp.concatenate(xs, axis=1).astype(_bf16)
        gate = _dot(xb, wgb[...])
        up = _dot(xb, wub[...])
        act = (gate * jax.nn.sigmoid(gate) * up).astype(_bf16)
        y = _dot(act, wdb[...])
        for cc in range(ROW_CHUNKS):
            y_ref[pl.ds(cc, TME, stride=ROW_CHUNKS), :] = y[:, cc * LANES:(cc + 1) * LANES]

    @pl.when(tv_ref[i] == 0)
    def _():
        y_ref[...] = jnp.zeros_like(y_ref)


def _combine_kernel(pos_ref,
                    y_hbm, h2_ref, route_ref,
                    out_ref,
                    ybuf, sem):
    i = pl.program_id(0)
    n = pl.num_programs(0)
    slot = i % 2

    def issue(tile, sl):
        base = tile * (2 * TMC)

        def body(r, carry):
            for k in range(2):
                pltpu.make_async_copy(_row_tile(y_hbm, pos_ref[base + 2 * r + k]),
                                      _row_tile(ybuf.at[sl, k], r), sem.at[sl]).start()
            return carry

        lax.fori_loop(0, TMC, body, 0)

    @pl.when(i == 0)
    def _():
        issue(0, 0)

    @pl.when(i + 1 < n)
    def _():
        issue(i + 1, 1 - slot)

    for k in range(2):
        pltpu.make_async_copy(y_hbm.at[pl.ds(0, TMC * ROW_CHUNKS), :], ybuf.at[slot, k], sem.at[slot]).wait()
    g0 = route_ref[:, 2:3]
    g1 = route_ref[:, 3:4]
    for cc in range(ROW_CHUNKS):
        y0 = ybuf[slot, 0, pl.ds(cc, TMC, stride=ROW_CHUNKS), :]
        y1 = ybuf[slot, 1, pl.ds(cc, TMC, stride=ROW_CHUNKS), :]
        cols = slice(cc * LANES, (cc + 1) * LANES)
        out_ref[:, cols] = h2_ref[:, cols] + (g0 * y0 + g1 * y1)


def _t5_bucket(n):
    max_exact = N_BUCKETS // 2
    nf = jnp.maximum(n, 1).astype(_f32)
    large = max_exact + (jnp.log(nf / max_exact) / np.log(MAX_DISTANCE / max_exact)
                         * (N_BUCKETS - max_exact)).astype(jnp.int32)
    large = jnp.minimum(large, N_BUCKETS - 1)
    return jnp.where(n < max_exact, n, large)


def _bucket_maps():
    qi = jnp.arange(BLK)[:, None]
    sj = jnp.arange(2 * BLK)[None, :]
    dist = BLK + qi - sj
    band = (dist >= 0) & (dist < BLK)
    bucket = _t5_bucket(jnp.maximum(dist, 0))
    generic = jnp.where(band, bucket, -1)
    first = jnp.where(band & (sj >= PAD), bucket, -1)
    return jnp.stack([first, generic]).astype(jnp.int32)


def _const(shape):
    nd = len(shape)
    return pl.BlockSpec(shape, lambda *_: (0,) * nd)


def _block_diag_ones(n):
    idx = np.arange(n) // HEAD_DIM
    return jnp.asarray((idx[:, None] == idx[None, :]).astype(np.float32), dtype=_bf16)


def kernel(x, meta_tokens, rel_bias, mix_norm_g, w_in, q_norm_g, k_norm_g, attn_sinks, conv_w, attn_out_norm_g, conv_out_norm_g, w_out, ffn_norm_g, w_group_router, b_group_router, w_expert_router, b_expert_router, w_gate, w_up, w_down):
    bsz, seq, _ = x.shape
    assert seq % TM == 0 and (bsz * seq) % TMC == 0
    n_tok = bsz * seq
    nt = seq // TM

    win = w_in[0].astype(_bf16)
    wout = w_out[0].astype(_bf16)
    mixg = mix_norm_g[0].reshape(1, D_MODEL)
    fg = ffn_norm_g[0].reshape(1, D_MODEL)
    qg = (jnp.tile(q_norm_g[0], N_Q_HEADS) * (HEAD_DIM ** -0.5)).reshape(1, ATTN_DIM)
    kg = jnp.tile(k_norm_g[0], N_KV_HEADS).reshape(1, KV_DIM)
    ag = attn_out_norm_g[0].reshape(1, ATTN_DIM)
    cg = conv_out_norm_g[0].reshape(1, CONV_CH)
    convw = conv_w[0]
    sinks = attn_sinks[0]
    w_r = jnp.concatenate([w_group_router[0], w_expert_router[0].reshape(D_MODEL, N_EXPERTS)], axis=1)
    n_r = N_GROUPS + N_EXPERTS
    w_r_hi = w_r.astype(_bf16)
    w_r_lo = (w_r - w_r_hi.astype(_f32)).astype(_bf16)
    wr = jnp.zeros((D_MODEL, LANES), _bf16)
    wr = wr.at[:, :n_r].set(w_r_hi).at[:, HEAD_DIM:HEAD_DIM + n_r].set(w_r_lo)
    br = jnp.zeros((1, LANES), _f32).at[0, :n_r].set(
        jnp.concatenate([b_group_router[0], b_expert_router[0].reshape(N_EXPERTS)]))
    meta_blk = jnp.concatenate([jnp.zeros((PAD, D_MODEL), x.dtype), meta_tokens.astype(x.dtype)], axis=0)
    bdq = _block_diag_ones(ATTN_DIM)
    bdk = _block_diag_ones(KV_DIM)
    tri = jnp.asarray(np.tril(np.ones((TM, TM), np.float32), -1), dtype=_bf16)

    kv_sd = jax.ShapeDtypeStruct((BLK, KV_DIM), _bf16)
    k0m, k1m, v0m, v1m, utm, bias = pl.pallas_call(
        _prep_kernel,
        out_shape=(kv_sd, kv_sd, kv_sd, kv_sd,
                   jax.ShapeDtypeStruct((SUBLANES, CONV_CH), _f32),
                   jax.ShapeDtypeStruct((2, N_KV_HEADS, Q_GROUP * BLK, 2 * BLK), _f32)),
        in_specs=[pl.BlockSpec(memory_space=pltpu.SMEM)] + [pl.BlockSpec(memory_space=pltpu.VMEM)] * 6,
        out_specs=tuple(pl.BlockSpec(memory_space=pltpu.VMEM) for _ in range(6)),
        compiler_params=pltpu.CompilerParams(vmem_limit_bytes=VMEM_LIMIT),
        name="prep",
    )(rel_bias, meta_blk, mixg, win, kg, bdk, _bucket_maps())

    tile_idx = lambda b, t, *_: (b * nt + t, 0)
    grid_spec = pltpu.PrefetchScalarGridSpec(
        num_scalar_prefetch=1,
        grid=(bsz, nt),
        in_specs=[
            pl.BlockSpec((None, TM, D_MODEL), lambda b, t, *_: (b, t, 0)),
            _const((1, D_MODEL)), _const((D_MODEL, IN_PROJ)), _const((1, ATTN_DIM)), _const((1, KV_DIM)),
            _const((2, N_KV_HEADS, Q_GROUP * BLK, 2 * BLK)), _const((3, CONV_CH)),
            _const((1, ATTN_DIM)), _const((1, CONV_CH)), _const((D_MODEL, D_MODEL)), _const((1, D_MODEL)),
            _const((D_MODEL, LANES)), _const((1, LANES)),
            _const((BLK, KV_DIM)), _const((BLK, KV_DIM)), _const((BLK, KV_DIM)), _const((BLK, KV_DIM)),
            _const((SUBLANES, CONV_CH)),
            _const((ATTN_DIM, ATTN_DIM)), _const((KV_DIM, KV_DIM)), _const((TM, TM)),
        ],
        out_specs=[
            pl.BlockSpec((None, TM, D_MODEL), lambda b, t, *_: (b, t, 0)),
            pl.BlockSpec((TM * ROW_CHUNKS, LANES), tile_idx),
            pl.BlockSpec((TM, ROUTE_W), tile_idx),
            _const((1, LANES)),
        ],
        scratch_shapes=[
            pltpu.VMEM((BLK, KV_DIM), _bf16), pltpu.VMEM((BLK, KV_DIM), _bf16),
            pltpu.VMEM((BLK, KV_DIM), _bf16), pltpu.VMEM((BLK, KV_DIM), _bf16),
            pltpu.VMEM((TM + SUBLANES, CONV_CH), _f32),
            pltpu.VMEM((TM, ATTN_DIM), _f32),
            pltpu.VMEM((1, LANES), _f32),
        ],
    )
    h2, hn2, route, cnt = pl.pallas_call(
        _mixer_kernel,
        grid_spec=grid_spec,
        out_shape=(jax.ShapeDtypeStruct((bsz, seq, D_MODEL), _f32),
                   jax.ShapeDtypeStruct((n_tok * ROW_CHUNKS, LANES), _f32),
                   jax.ShapeDtypeStruct((n_tok, ROUTE_W), _f32),
                   jax.ShapeDtypeStruct((1, LANES), _f32)),
        compiler_params=pltpu.CompilerParams(dimension_semantics=("arbitrary", "arbitrary"),
                                             vmem_limit_bytes=VMEM_LIMIT),
        name="mixer",
    )(sinks, x, mixg, win, qg, kg, bias, convw, ag, cg, wout, fg, wr, br,
      k0m, k1m, v0m, v1m, utm, bdq, bdk, tri)

    n_tiles = (n_tok * 2) // TME + N_EXPERTS
    counts = cnt[0, :N_EXPERTS].astype(jnp.int32)
    ntile = (counts + TME - 1) // TME
    tile_end = jnp.cumsum(ntile)
    tile_start = tile_end - ntile
    eid = route[:, 0:2].astype(jnp.int32)
    rank = route[:, 4:6].astype(jnp.int32)
    pos = (tile_start[eid] * TME + rank).reshape(-1)
    tiles = jnp.arange(n_tiles, dtype=jnp.int32)
    t_valid = (tiles < tile_end[-1]).astype(jnp.int32)
    t_exp = jnp.minimum(jnp.searchsorted(tile_end, tiles, side="right"), N_EXPERTS - 1).astype(jnp.int32)
    t_exp = jnp.where(t_valid == 1, t_exp, t_exp[jnp.maximum(tile_end[-1] - 1, 0)])
    src_tok = jnp.zeros((n_tiles * TME,), jnp.int32).at[pos].set(
        jnp.repeat(jnp.arange(n_tok, dtype=jnp.int32), 2))

    y_sorted = pl.pallas_call(
        _experts_kernel,
        grid_spec=pltpu.PrefetchScalarGridSpec(
            num_scalar_prefetch=3,
            grid=(n_tiles,),
            in_specs=[
                pl.BlockSpec(memory_space=pl.ANY),
                pl.BlockSpec((None, D_MODEL, D_EXPERT), lambda i, te, tv, src: (te[i], 0, 0)),
                pl.BlockSpec((None, D_MODEL, D_EXPERT), lambda i, te, tv, src: (te[i], 0, 0)),
                pl.BlockSpec((None, D_EXPERT, D_MODEL), lambda i, te, tv, src: (te[i], 0, 0)),
            ],
            out_specs=pl.BlockSpec((TME * ROW_CHUNKS, LANES), lambda i, *_: (i, 0)),
            scratch_shapes=[
                pltpu.VMEM((2, TME * ROW_CHUNKS, LANES), _f32),
                pltpu.SemaphoreType.DMA((2,)),
                pltpu.VMEM((D_MODEL, D_EXPERT), _bf16),
                pltpu.VMEM((D_MODEL, D_EXPERT), _bf16),
                pltpu.VMEM((D_EXPERT, D_MODEL), _bf16),
            ],
        ),
        out_shape=jax.ShapeDtypeStruct((n_tiles * TME * ROW_CHUNKS, LANES), _f32),
        compiler_params=pltpu.CompilerParams(dimension_semantics=("arbitrary",),
                                             vmem_limit_bytes=VMEM_LIMIT),
        name="experts",
    )(t_exp, t_valid, src_tok, hn2, w_gate[0], w_up[0], w_down[0])

    out = pl.pallas_call(
        _combine_kernel,
        grid_spec=pltpu.PrefetchScalarGridSpec(
            num_scalar_prefetch=1,
            grid=(n_tok // TMC,),
            in_specs=[
                pl.BlockSpec(memory_space=pl.ANY),
                pl.BlockSpec((TMC, D_MODEL), lambda i, *_: (i, 0)),
                pl.BlockSpec((TMC, ROUTE_W), lambda i, *_: (i, 0)),
            ],
            out_specs=pl.BlockSpec((TMC, D_MODEL), lambda i, *_: (i, 0)),
            scratch_shapes=[
                pltpu.VMEM((2, 2, TMC * ROW_CHUNKS, LANES), _f32),
                pltpu.SemaphoreType.DMA((2,)),
            ],
        ),
        out_shape=jax.ShapeDtypeStruct((n_tok, D_MODEL), _f32),
        compiler_params=pltpu.CompilerParams(dimension_semantics=("arbitrary",),
                                             vmem_limit_bytes=VMEM_LIMIT),
        name="combine",
    )(pos, y_sorted, h2.reshape(n_tok, D_MODEL), route)
    return out.reshape(bsz, seq, D_MODEL)
```

```python
import functools

import numpy as np
import jax
import jax.numpy as jnp
from jax import lax
from jax.experimental import pallas as pl
from jax.experimental.pallas import tpu as pltpu

D_MODEL = 1024
N_META = 16
N_Q_HEADS = 8
N_KV_HEADS = 2
HEAD_DIM = 64
Q_GROUP = N_Q_HEADS // N_KV_HEADS
ATTN_DIM = N_Q_HEADS * HEAD_DIM
KV_DIM = N_KV_HEADS * HEAD_DIM
BLK = 128
PAD = BLK - N_META
N_BUCKETS = 32
MAX_DISTANCE = 128
CONV_CH = D_MODEL // 2
IN_PROJ = ATTN_DIM + 2 * KV_DIM + 3 * CONV_CH
N_GROUPS = 4
EXPERTS_PER_GROUP = 8
N_EXPERTS = N_GROUPS * EXPERTS_PER_GROUP
D_EXPERT = D_MODEL // 2
EPS = 1e-6
NEG_INF = -1e30

LANES = 128
SUBLANES = 8
ROW_CHUNKS = D_MODEL // LANES
TM = 512
TME = 256
TMC = 256
TMD = 256
DMA_UNROLL = 8
ROUTE_W = 8
VMEM_LIMIT = 56 * 1024 * 1024

Q_OFF, K_OFF, V_OFF = 0, ATTN_DIM, ATTN_DIM + KV_DIM
CB_OFF = ATTN_DIM + 2 * KV_DIM
CC_OFF = CB_OFF + CONV_CH
CH_OFF = CC_OFF + CONV_CH

_f32 = jnp.float32
_bf16 = jnp.bfloat16


def _rms(x, g):
    return x * lax.rsqrt(jnp.mean(x * x, axis=-1, keepdims=True) + EPS) * g


def _dot(a, b):
    return jnp.dot(a, b, preferred_element_type=_f32)


def _dup_halves(x):
    lane = lax.broadcasted_iota(jnp.int32, x.shape, 1)
    sw = pltpu.roll(x, HEAD_DIM, axis=1)
    lo = lane < HEAD_DIM
    return jnp.where(lo, x, sw).astype(_bf16), jnp.where(lo, sw, x).astype(_bf16)


def _kv_state(hn_bf, win_ref, kg_ref, bdk_ref):
    kv = _dot(hn_bf, win_ref[:, K_OFF:K_OFF + 2 * KV_DIM])
    k = kv[:, :KV_DIM]
    v = kv[:, KV_DIM:]
    ssk = _dot((k * k).astype(_bf16), bdk_ref[...])
    kn = k * lax.rsqrt(ssk * (1.0 / HEAD_DIM) + EPS) * kg_ref[...]
    return _dup_halves(kn) + _dup_halves(v)


def _prep_kernel(rb_ref, meta_ref, mixg_ref, win_ref, kg_ref, bdk_ref, bucket_ref,
                 k0_ref, k1_ref, v0_ref, v1_ref, ut_ref, bias_ref):
    hn = _rms(meta_ref[...], mixg_ref[...]).astype(_bf16)
    k0, k1, v0, v1 = _kv_state(hn, win_ref, kg_ref, bdk_ref)
    k0_ref[...] = k0
    k1_ref[...] = k1
    v0_ref[...] = v0
    v1_ref[...] = v1
    cch = _dot(hn, win_ref[:, CC_OFF:CC_OFF + 2 * CONV_CH])
    u = cch[:, :CONV_CH] * cch[:, CONV_CH:]
    ut_ref[...] = u[BLK - SUBLANES:, :]
    for f in range(2):
        bk = bucket_ref[f]
        for h in range(N_Q_HEADS):
            acc = jnp.full((BLK, 2 * BLK), NEG_INF, _f32)
            for b in range(N_BUCKETS):
                acc = jnp.where(bk == b, rb_ref[b, h], acc)
            bias_ref[f, h // Q_GROUP, (h % Q_GROUP) * BLK:(h % Q_GROUP + 1) * BLK, :] = acc


def _mixer_kernel(sinks_ref,
                  x_ref, mixg_ref, win_ref, qg_ref, kg_ref, bias_ref, convw_ref, ag_ref, cg_ref,
                  wout_ref, fg_ref, wr_ref, br_ref, k0m_ref, k1m_ref, v0m_ref, v1m_ref, utm_ref,
                  bdq_ref, bdk_ref, tri_ref,
                  h2_ref, hn2_ref, route_ref, cnt_ref,
                  kp0, kp1, vp0, vp1, ubuf, a_scr, cnt_acc):
    b = pl.program_id(0)
    t = pl.program_id(1)

    @pl.when(t == 0)
    def _():
        kp0[...] = k0m_ref[...]
        kp1[...] = k1m_ref[...]
        vp0[...] = v0m_ref[...]
        vp1[...] = v1m_ref[...]
        ubuf[0:SUBLANES, :] = utm_ref[...]

    @pl.when(jnp.logical_and(b == 0, t == 0))
    def _():
        cnt_acc[...] = jnp.zeros_like(cnt_acc)

    x = x_ref[...]
    hn = _rms(x, mixg_ref[...]).astype(_bf16)

    q = _dot(hn, win_ref[:, Q_OFF:Q_OFF + ATTN_DIM])
    ssq = _dot((q * q).astype(_bf16), bdq_ref[...])
    qn = (q * lax.rsqrt(ssq * (1.0 / HEAD_DIM) + EPS) * qg_ref[...]).astype(_bf16)
    kd0, kd1, vd0, vd1 = _kv_state(hn, win_ref, kg_ref, bdk_ref)
    kd = (kd0, kd1)
    vd = (vd0, vd1)
    kp = (kp0, kp1)
    vp = (vp0, vp1)

    lane_q = lax.broadcasted_iota(jnp.int32, (BLK, LANES), 1)
    lo_half = lane_q < HEAD_DIM
    row4 = lax.broadcasted_iota(jnp.int32, (Q_GROUP * BLK, 1), 0) // BLK
    first = jnp.where(t == 0, 0, 1)
    zero_bf = jnp.zeros((BLK, LANES), _bf16)

    for j in range(TM // BLK):
        rows = slice(j * BLK, (j + 1) * BLK)
        for g in range(N_KV_HEADS):
            if j == 0:
                kcat = jnp.concatenate([kp[g][...], kd[g][rows]], axis=0)
                vcat = jnp.concatenate([vp[g][...], vd[g][rows]], axis=0)
                bias = bias_ref[first, g]
            else:
                kcat = kd[g][(j - 1) * BLK:(j + 1) * BLK]
                vcat = vd[g][(j - 1) * BLK:(j + 1) * BLK]
                bias = bias_ref[1, g]
            qs = []
            for hh in range(Q_GROUP):
                h = g * Q_GROUP + hh
                qc = qn[rows, (h // 2) * LANES:(h // 2 + 1) * LANES]
                keep = lo_half if h % 2 == 0 else jnp.logical_not(lo_half)
                qs.append(jnp.where(keep, qc, zero_bf))
            q4 = jnp.concatenate(qs, axis=0)
            s = lax.dot_general(q4, kcat, (((1,), (1,)), ((), ())),
                                preferred_element_type=_f32) + bias
            sink = jnp.full((Q_GROUP * BLK, 1), sinks_ref[g * Q_GROUP], _f32)
            for hh in range(1, Q_GROUP):
                sink = jnp.where(row4 == hh, sinks_ref[g * Q_GROUP + hh], sink)
            m = jnp.maximum(jnp.max(s, axis=-1, keepdims=True), sink)
            p = jnp.exp(s - m)
            l = jnp.sum(p, axis=-1, keepdims=True) + jnp.exp(sink - m)
            o = _dot(p.astype(_bf16), vcat) / l
            for pp in range(Q_GROUP // 2):
                ev = o[(2 * pp) * BLK:(2 * pp + 1) * BLK]
                od = o[(2 * pp + 1) * BLK:(2 * pp + 2) * BLK]
                col = g * (Q_GROUP // 2) + pp
                a_scr[rows, col * LANES:(col + 1) * LANES] = jnp.where(lo_half, ev, od)

    last = slice(TM - BLK, TM)
    kp0[...] = kd0[last]
    kp1[...] = kd1[last]
    vp0[...] = vd0[last]
    vp1[...] = vd1[last]

    cb = _dot(hn, win_ref[:, CB_OFF:CB_OFF + CONV_CH])
    cch = _dot(hn, win_ref[:, CC_OFF:CC_OFF + 2 * CONV_CH])
    u = cch[:, :CONV_CH] * cch[:, CONV_CH:]
    ubuf[SUBLANES:, :] = u
    u1 = ubuf[SUBLANES - 1:SUBLANES - 1 + TM, :]
    u2 = ubuf[SUBLANES - 2:SUBLANES - 2 + TM, :]
    c = cb * (convw_ref[0:1, :] * u2 + convw_ref[1:2, :] * u1 + convw_ref[2:3, :] * u)
    ubuf[0:SUBLANES, :] = u[TM - SUBLANES:, :]

    an = _rms(a_scr[...], ag_ref[...]).astype(_bf16)
    cn = _rms(c, cg_ref[...]).astype(_bf16)
    h2 = x + _dot(an, wout_ref[0:ATTN_DIM, :]) + _dot(cn, wout_ref[ATTN_DIM:, :])
    h2_ref[...] = h2

    hn2 = _rms(h2, fg_ref[...])
    for cc in range(ROW_CHUNKS):
        hn2_ref[pl.ds(cc, TM, stride=ROW_CHUNKS), :] = hn2[:, cc * LANES:(cc + 1) * LANES]

    hi = hn2.astype(_bf16)
    lo = (hn2 - hi.astype(_f32)).astype(_bf16)
    r1 = _dot(hi, wr_ref[...])
    r2 = _dot(lo, wr_ref[...])
    lg = r1 + pltpu.roll(r1, HEAD_DIM, axis=1) + r2 + br_ref[...]
    lane = lax.broadcasted_iota(jnp.int32, (TM, LANES), 1)
    lanef = lane.astype(_f32)
    ninf = jnp.float32(-jnp.inf)
    big = jnp.float32(LANES)

    gl = jnp.where(lane < N_GROUPS, lg, ninf)
    gmax = jnp.max(gl, axis=-1, keepdims=True)
    gsum = jnp.sum(jnp.exp(gl - gmax), axis=-1, keepdims=True)
    g_p = 1.0 / gsum
    g_idx = jnp.min(jnp.where(gl == gmax, lanef, big), axis=-1, keepdims=True)
    e_lo = N_GROUPS + EXPERTS_PER_GROUP * g_idx
    el = jnp.where(jnp.logical_and(lanef >= e_lo, lanef < e_lo + EXPERTS_PER_GROUP), lg, ninf)
    m1 = jnp.max(el, axis=-1, keepdims=True)
    i1 = jnp.min(jnp.where(el == m1, lanef, big), axis=-1, keepdims=True)
    el2 = jnp.where(lanef == i1, ninf, el)
    m2 = jnp.max(el2, axis=-1, keepdims=True)
    i2 = jnp.min(jnp.where(el2 == m2, lanef, big), axis=-1, keepdims=True)
    ex = jnp.exp(m2 - m1)
    den = 1.0 / (1.0 + ex)
    gate0 = g_p * den
    gate1 = g_p * ex * den
    e0 = i1 - N_GROUPS
    e1 = i2 - N_GROUPS

    oh0 = lanef == e0
    oh1 = lanef == e1
    cmat = (jnp.where(oh0, 1.0, 0.0) + jnp.where(oh1, 1.0, 0.0))
    prefix = _dot(tri_ref[...], cmat.astype(_bf16)) + cnt_acc[...]
    rank0 = jnp.sum(jnp.where(oh0, prefix, 0.0), axis=-1, keepdims=True)
    rank1 = jnp.sum(jnp.where(oh1, prefix, 0.0), axis=-1, keepdims=True)
    cnt_new = cnt_acc[...] + jnp.sum(cmat, axis=0, keepdims=True)
    cnt_acc[...] = cnt_new
    cnt_ref[...] = cnt_new

    lane8 = lax.broadcasted_iota(jnp.int32, (TM, ROUTE_W), 1)
    rec = jnp.zeros((TM, ROUTE_W), _f32)
    for idx, val in enumerate((e0, e1, gate0, gate1, rank0, rank1)):
        rec = jnp.where(lane8 == idx, val, rec)
    route_ref[...] = rec


def _row_tile(ref, row):
    return ref.at[pl.ds(pl.multiple_of(row * ROW_CHUNKS, ROW_CHUNKS), ROW_CHUNKS), :]


def _dispatch_kernel(pos_ref, zf_ref, hn2_hbm, zeros_hbm, xs_hbm, sem):
    g = pl.program_id(0)
    n = pl.num_programs(0)

    @pl.when(g == 0)
    def _():
        def fill(tile):
            rows = TME * ROW_CHUNKS
            return pltpu.make_async_copy(
                zeros_hbm, xs_hbm.at[pl.ds(pl.multiple_of(tile * rows, rows), rows), :], sem.at[1])

        def start(i, carry):
            @pl.when(zf_ref[i] == 1)
            def _():
                fill(i).start()
            return carry

        def wait(i, carry):
            @pl.when(zf_ref[i] == 1)
            def _():
                fill(i).wait()
            return carry

        lax.fori_loop(0, zf_ref.shape[0], start, 0)
        lax.fori_loop(0, zf_ref.shape[0], wait, 0)

    def body(it, carry):
        for j in range(DMA_UNROLL):
            tok = g * TMD + it * DMA_UNROLL + j
            for k in range(2):
                pltpu.make_async_copy(_row_tile(hn2_hbm, tok), _row_tile(xs_hbm, pos_ref[2 * tok + k]),
                                      sem.at[0]).start()
        return carry

    lax.fori_loop(0, TMD // DMA_UNROLL, body, 0)

    def wait_group():
        rows = 2 * TMD * ROW_CHUNKS
        pltpu.make_async_copy(hn2_hbm.at[pl.ds(0, rows), :], xs_hbm.at[pl.ds(0, rows), :], sem.at[0]).wait()

    @pl.when(g > 0)
    def _():
        wait_group()

    @pl.when(g == n - 1)
    def _():
        wait_group()


def _experts_kernel(te_ref, tn_ref,
                    xs_ref, wg_ref, wu_ref, wd_ref,
                    y_ref,
                    wgb, wub, wdb):
    i = pl.program_id(0)
    changed = jnp.logical_or(i == 0, te_ref[i] != te_ref[jnp.maximum(i - 1, 0)])

    @pl.when(jnp.logical_and(changed, tn_ref[i] > 0))
    def _():
        wgb[...] = wg_ref[...].astype(_bf16)
        wub[...] = wu_ref[...].astype(_bf16)
        wdb[...] = wd_ref[...].astype(_bf16)

    @pl.when(tn_ref[i] > 0)
    def _():
        xs = [xs_ref[pl.ds(cc, TME, stride=ROW_CHUNKS), :] for cc in range(ROW_CHUNKS)]
        xb = jnp.concatenate(xs, axis=1).astype(_bf16)
        live = lax.broadcasted_iota(jnp.int32, (TME, 1), 0) < tn_ref[i]
        xb = jnp.where(live, xb, jnp.zeros_like(xb))
        gate = _dot(xb, wgb[...])
        up = _dot(xb, wub[...])
        act = (gate * jax.nn.sigmoid(gate) * up).astype(_bf16)
        y = _dot(act, wdb[...])
        for cc in range(ROW_CHUNKS):
            y_ref[pl.ds(cc, TME, stride=ROW_CHUNKS), :] = y[:, cc * LANES:(cc + 1) * LANES]

    @pl.when(tn_ref[i] == 0)
    def _():
        y_ref[...] = jnp.zeros_like(y_ref)


def _combine_kernel(pos_ref,
                    y_hbm, h2_ref, route_ref,
                    out_ref,
                    ybuf, sem):
    i = pl.program_id(0)
    n = pl.num_programs(0)
    slot = i % 2

    def issue(tile, sl):
        base = tile * (2 * TMC)

        def body(it, carry):
            for j in range(DMA_UNROLL):
                r = it * DMA_UNROLL + j
                for k in range(2):
                    pltpu.make_async_copy(_row_tile(y_hbm, pos_ref[base + 2 * r + k]),
                                          _row_tile(ybuf.at[sl, k], r), sem.at[sl]).start()
            return carry

        lax.fori_loop(0, TMC // DMA_UNROLL, body, 0)

    @pl.when(i == 0)
    def _():
        issue(0, 0)

    @pl.when(i + 1 < n)
    def _():
        issue(i + 1, 1 - slot)

    for k in range(2):
        pltpu.make_async_copy(y_hbm.at[pl.ds(0, TMC * ROW_CHUNKS), :], ybuf.at[slot, k], sem.at[slot]).wait()
    g0 = route_ref[:, 2:3]
    g1 = route_ref[:, 3:4]
    for cc in range(ROW_CHUNKS):
        y0 = ybuf[slot, 0, pl.ds(cc, TMC, stride=ROW_CHUNKS), :]
        y1 = ybuf[slot, 1, pl.ds(cc, TMC, stride=ROW_CHUNKS), :]
        cols = slice(cc * LANES, (cc + 1) * LANES)
        out_ref[:, cols] = h2_ref[:, cols] + (g0 * y0 + g1 * y1)


def _t5_bucket(n):
    max_exact = N_BUCKETS // 2
    nf = jnp.maximum(n, 1).astype(_f32)
    large = max_exact + (jnp.log(nf / max_exact) / np.log(MAX_DISTANCE / max_exact)
                         * (N_BUCKETS - max_exact)).astype(jnp.int32)
    large = jnp.minimum(large, N_BUCKETS - 1)
    return jnp.where(n < max_exact, n, large)


def _bucket_maps():
    qi = jnp.arange(BLK)[:, None]
    sj = jnp.arange(2 * BLK)[None, :]
    dist = BLK + qi - sj
    band = (dist >= 0) & (dist < BLK)
    bucket = _t5_bucket(jnp.maximum(dist, 0))
    generic = jnp.where(band, bucket, -1)
    first = jnp.where(band & (sj >= PAD), bucket, -1)
    return jnp.stack([first, generic]).astype(jnp.int32)


def _const(shape):
    nd = len(shape)
    return pl.BlockSpec(shape, lambda *_: (0,) * nd)


def _block_diag_ones(n):
    idx = np.arange(n) // HEAD_DIM
    return jnp.asarray((idx[:, None] == idx[None, :]).astype(np.float32), dtype=_bf16)


def kernel(x, meta_tokens, rel_bias, mix_norm_g, w_in, q_norm_g, k_norm_g, attn_sinks, conv_w, attn_out_norm_g, conv_out_norm_g, w_out, ffn_norm_g, w_group_router, b_group_router, w_expert_router, b_expert_router, w_gate, w_up, w_down):
    bsz, seq, _ = x.shape
    assert seq % TM == 0 and (bsz * seq) % TMC == 0
    n_tok = bsz * seq
    nt = seq // TM

    win = w_in[0].astype(_bf16)
    wout = w_out[0].astype(_bf16)
    mixg = mix_norm_g[0].reshape(1, D_MODEL)
    fg = ffn_norm_g[0].reshape(1, D_MODEL)
    qg = (jnp.tile(q_norm_g[0], N_Q_HEADS) * (HEAD_DIM ** -0.5)).reshape(1, ATTN_DIM)
    kg = jnp.tile(k_norm_g[0], N_KV_HEADS).reshape(1, KV_DIM)
    ag = attn_out_norm_g[0].reshape(1, ATTN_DIM)
    cg = conv_out_norm_g[0].reshape(1, CONV_CH)
    convw = conv_w[0]
    sinks = attn_sinks[0]
    w_r = jnp.concatenate([w_group_router[0], w_expert_router[0].reshape(D_MODEL, N_EXPERTS)], axis=1)
    n_r = N_GROUPS + N_EXPERTS
    w_r_hi = w_r.astype(_bf16)
    w_r_lo = (w_r - w_r_hi.astype(_f32)).astype(_bf16)
    wr = jnp.zeros((D_MODEL, LANES), _bf16)
    wr = wr.at[:, :n_r].set(w_r_hi).at[:, HEAD_DIM:HEAD_DIM + n_r].set(w_r_lo)
    br = jnp.zeros((1, LANES), _f32).at[0, :n_r].set(
        jnp.concatenate([b_group_router[0], b_expert_router[0].reshape(N_EXPERTS)]))
    meta_blk = jnp.concatenate([jnp.zeros((PAD, D_MODEL), x.dtype), meta_tokens.astype(x.dtype)], axis=0)
    bdq = _block_diag_ones(ATTN_DIM)
    bdk = _block_diag_ones(KV_DIM)
    tri = jnp.asarray(np.tril(np.ones((TM, TM), np.float32), -1), dtype=_bf16)

    kv_sd = jax.ShapeDtypeStruct((BLK, KV_DIM), _bf16)
    k0m, k1m, v0m, v1m, utm, bias = pl.pallas_call(
        _prep_kernel,
        out_shape=(kv_sd, kv_sd, kv_sd, kv_sd,
                   jax.ShapeDtypeStruct((SUBLANES, CONV_CH), _f32),
                   jax.ShapeDtypeStruct((2, N_KV_HEADS, Q_GROUP * BLK, 2 * BLK), _f32)),
        in_specs=[pl.BlockSpec(memory_space=pltpu.SMEM)] + [pl.BlockSpec(memory_space=pltpu.VMEM)] * 6,
        out_specs=tuple(pl.BlockSpec(memory_space=pltpu.VMEM) for _ in range(6)),
        compiler_params=pltpu.CompilerParams(vmem_limit_bytes=VMEM_LIMIT),
        name="prep",
    )(rel_bias, meta_blk, mixg, win, kg, bdk, _bucket_maps())

    tile_idx = lambda b, t, *_: (b * nt + t, 0)
    grid_spec = pltpu.PrefetchScalarGridSpec(
        num_scalar_prefetch=1,
        grid=(bsz, nt),
        in_specs=[
            pl.BlockSpec((None, TM, D_MODEL), lambda b, t, *_: (b, t, 0)),
            _const((1, D_MODEL)), _const((D_MODEL, IN_PROJ)), _const((1, ATTN_DIM)), _const((1, KV_DIM)),
            _const((2, N_KV_HEADS, Q_GROUP * BLK, 2 * BLK)), _const((3, CONV_CH)),
            _const((1, ATTN_DIM)), _const((1, CONV_CH)), _const((D_MODEL, D_MODEL)), _const((1, D_MODEL)),
            _const((D_MODEL, LANES)), _const((1, LANES)),
            _const((BLK, KV_DIM)), _const((BLK, KV_DIM)), _const((BLK, KV_DIM)), _const((BLK, KV_DIM)),
            _const((SUBLANES, CONV_CH)),
            _const((ATTN_DIM, ATTN_DIM)), _const((KV_DIM, KV_DIM)), _const((TM, TM)),
        ],
        out_specs=[
            pl.BlockSpec((None, TM, D_MODEL), lambda b, t, *_: (b, t, 0)),
            pl.BlockSpec((TM * ROW_CHUNKS, LANES), tile_idx),
            pl.BlockSpec((TM, ROUTE_W), tile_idx),
            _const((1, LANES)),
        ],
        scratch_shapes=[
            pltpu.VMEM((BLK, KV_DIM), _bf16), pltpu.VMEM((BLK, KV_DIM), _bf16),
            pltpu.VMEM((BLK, KV_DIM), _bf16), pltpu.VMEM((BLK, KV_DIM), _bf16),
            pltpu.VMEM((TM + SUBLANES, CONV_CH), _f32),
            pltpu.VMEM((TM, ATTN_DIM), _f32),
            pltpu.VMEM((1, LANES), _f32),
        ],
    )
    h2, hn2, route, cnt = pl.pallas_call(
        _mixer_kernel,
        grid_spec=grid_spec,
        out_shape=(jax.ShapeDtypeStruct((bsz, seq, D_MODEL), _f32),
                   jax.ShapeDtypeStruct((n_tok * ROW_CHUNKS, LANES), _f32),
                   jax.ShapeDtypeStruct((n_tok, ROUTE_W), _f32),
                   jax.ShapeDtypeStruct((1, LANES), _f32)),
        compiler_params=pltpu.CompilerParams(dimension_semantics=("arbitrary", "arbitrary"),
                                             vmem_limit_bytes=VMEM_LIMIT),
        name="mixer",
    )(sinks, x, mixg, win, qg, kg, bias, convw, ag, cg, wout, fg, wr, br,
      k0m, k1m, v0m, v1m, utm, bdq, bdk, tri)

    n_tiles = (n_tok * 2) // TME + N_EXPERTS
    counts = cnt[0, :N_EXPERTS].astype(jnp.int32)
    ntile = (counts + TME - 1) // TME
    tile_end = jnp.cumsum(ntile)
    tile_start = tile_end - ntile
    eid = route[:, 0:2].astype(jnp.int32)
    rank = route[:, 4:6].astype(jnp.int32)
    experts = jnp.arange(N_EXPERTS, dtype=jnp.int32)
    start_of = jnp.sum(jnp.where(eid[:, :, None] == experts, tile_start, 0), axis=-1)
    pos = (start_of * TME + rank).reshape(-1)
    tiles = jnp.arange(n_tiles, dtype=jnp.int32)
    n_used = tile_end[-1]
    t_exp = jnp.sum((jnp.minimum(tiles, n_used - 1)[:, None] >= tile_end[None, :]).astype(jnp.int32), axis=-1)
    t_exp = jnp.minimum(t_exp, N_EXPERTS - 1)
    own = t_exp[:, None] == experts
    t_rows = jnp.sum(jnp.where(own, counts - (tiles[:, None] - tile_start) * TME, 0), axis=-1)
    t_rows = jnp.where(tiles < n_used, jnp.clip(t_rows, 0, TME), 0).astype(jnp.int32)
    t_fill = (t_rows < TME).astype(jnp.int32)

    xs = pl.pallas_call(
        _dispatch_kernel,
        grid_spec=pltpu.PrefetchScalarGridSpec(
            num_scalar_prefetch=2,
            grid=(n_tok // TMD,),
            in_specs=[pl.BlockSpec(memory_space=pl.ANY), pl.BlockSpec(memory_space=pl.ANY)],
            out_specs=pl.BlockSpec(memory_space=pl.ANY),
            scratch_shapes=[pltpu.SemaphoreType.DMA((2,))],
        ),
        out_shape=jax.ShapeDtypeStruct((n_tiles * TME * ROW_CHUNKS, LANES), _f32),
        compiler_params=pltpu.CompilerParams(dimension_semantics=("arbitrary",)),
        name="dispatch",
    )(pos, t_fill, hn2, jnp.zeros((TME * ROW_CHUNKS, LANES), _f32))

    y_sorted = pl.pallas_call(
        _experts_kernel,
        grid_spec=pltpu.PrefetchScalarGridSpec(
            num_scalar_prefetch=2,
            grid=(n_tiles,),
            in_specs=[
                pl.BlockSpec((TME * ROW_CHUNKS, LANES), lambda i, *_: (i, 0)),
                pl.BlockSpec((None, D_MODEL, D_EXPERT), lambda i, te, tn: (te[i], 0, 0)),
                pl.BlockSpec((None, D_MODEL, D_EXPERT), lambda i, te, tn: (te[i], 0, 0)),
                pl.BlockSpec((None, D_EXPERT, D_MODEL), lambda i, te, tn: (te[i], 0, 0)),
            ],
            out_specs=pl.BlockSpec((TME * ROW_CHUNKS, LANES), lambda i, *_: (i, 0)),
            scratch_shapes=[
                pltpu.VMEM((D_MODEL, D_EXPERT), _bf16),
                pltpu.VMEM((D_MODEL, D_EXPERT), _bf16),
                pltpu.VMEM((D_EXPERT, D_MODEL), _bf16),
            ],
        ),
        out_shape=jax.ShapeDtypeStruct((n_tiles * TME * ROW_CHUNKS, LANES), _f32),
        compiler_params=pltpu.CompilerParams(dimension_semantics=("arbitrary",),
                                             vmem_limit_bytes=VMEM_LIMIT),
        name="experts",
    )(t_exp, t_rows, xs, w_gate[0], w_up[0], w_down[0])

    out = pl.pallas_call(
        _combine_kernel,
        grid_spec=pltpu.PrefetchScalarGridSpec(
            num_scalar_prefetch=1,
            grid=(n_tok // TMC,),
            in_specs=[
                pl.BlockSpec(memory_space=pl.ANY),
                pl.BlockSpec((TMC, D_MODEL), lambda i, *_: (i, 0)),
                pl.BlockSpec((TMC, ROUTE_W), lambda i, *_: (i, 0)),
            ],
            out_specs=pl.BlockSpec((TMC, D_MODEL), lambda i, *_: (i, 0)),
            scratch_shapes=[
                pltpu.VMEM((2, 2, TMC * ROW_CHUNKS, LANES), _f32),
                pltpu.SemaphoreType.DMA((2,)),
            ],
        ),
        out_shape=jax.ShapeDtypeStruct((n_tok, D_MODEL), _f32),
        compiler_params=pltpu.CompilerParams(dimension_semantics=("arbitrary",),
                                             vmem_limit_bytes=VMEM_LIMIT),
        name="combine",
    )(pos, y_sorted, h2.reshape(n_tok, D_MODEL), route)
    return out.reshape(bsz, seq, D_MODEL)
```

```python
import functools

import numpy as np
import jax
import jax.numpy as jnp
from jax import lax
from jax.experimental import pallas as pl
from jax.experimental.pallas import tpu as pltpu

D_MODEL = 1024
N_META = 16
N_Q_HEADS = 8
N_KV_HEADS = 2
HEAD_DIM = 64
Q_GROUP = N_Q_HEADS // N_KV_HEADS
ATTN_DIM = N_Q_HEADS * HEAD_DIM
KV_DIM = N_KV_HEADS * HEAD_DIM
BLK = 128
PAD = BLK - N_META
N_BUCKETS = 32
MAX_DISTANCE = 128
CONV_CH = D_MODEL // 2
IN_PROJ = ATTN_DIM + 2 * KV_DIM + 3 * CONV_CH
N_GROUPS = 4
EXPERTS_PER_GROUP = 8
N_EXPERTS = N_GROUPS * EXPERTS_PER_GROUP
D_EXPERT = D_MODEL // 2
EPS = 1e-6
NEG_INF = -1e30

LANES = 128
SUBLANES = 8
ROW_CHUNKS = D_MODEL // LANES
TM = 512
TME = 256
TMC = 256
TMD = 512
DMA_UNROLL = 8
ROUTE_W = 8
VMEM_LIMIT = 56 * 1024 * 1024

Q_OFF, K_OFF, V_OFF = 0, ATTN_DIM, ATTN_DIM + KV_DIM
CB_OFF = ATTN_DIM + 2 * KV_DIM
CC_OFF = CB_OFF + CONV_CH
CH_OFF = CC_OFF + CONV_CH

_f32 = jnp.float32
_bf16 = jnp.bfloat16


def _rms(x, g):
    return x * lax.rsqrt(jnp.mean(x * x, axis=-1, keepdims=True) + EPS) * g


def _dot(a, b):
    return jnp.dot(a, b, preferred_element_type=_f32)


def _dup_halves(x):
    lane = lax.broadcasted_iota(jnp.int32, x.shape, 1)
    sw = pltpu.roll(x, HEAD_DIM, axis=1)
    lo = lane < HEAD_DIM
    return jnp.where(lo, x, sw).astype(_bf16), jnp.where(lo, sw, x).astype(_bf16)


def _kv_state(hn_bf, win_ref, kg_ref, bdk_ref):
    kv = _dot(hn_bf, win_ref[:, K_OFF:K_OFF + 2 * KV_DIM])
    k = kv[:, :KV_DIM]
    v = kv[:, KV_DIM:]
    ssk = _dot((k * k).astype(_bf16), bdk_ref[...])
    kn = k * lax.rsqrt(ssk * (1.0 / HEAD_DIM) + EPS) * kg_ref[...]
    return _dup_halves(kn) + _dup_halves(v)


def _prep_kernel(rb_ref, meta_ref, mixg_ref, win_ref, kg_ref, bdk_ref, bucket_ref,
                 k0_ref, k1_ref, v0_ref, v1_ref, ut_ref, bias_ref):
    hn = _rms(meta_ref[...], mixg_ref[...]).astype(_bf16)
    k0, k1, v0, v1 = _kv_state(hn, win_ref, kg_ref, bdk_ref)
    k0_ref[...] = k0
    k1_ref[...] = k1
    v0_ref[...] = v0
    v1_ref[...] = v1
    cch = _dot(hn, win_ref[:, CC_OFF:CC_OFF + 2 * CONV_CH])
    u = cch[:, :CONV_CH] * cch[:, CONV_CH:]
    ut_ref[...] = u[BLK - SUBLANES:, :]
    for f in range(2):
        bk = bucket_ref[f]
        for h in range(N_Q_HEADS):
            acc = jnp.full((BLK, 2 * BLK), NEG_INF, _f32)
            for b in range(N_BUCKETS):
                acc = jnp.where(bk == b, rb_ref[b, h], acc)
            bias_ref[f, h // Q_GROUP, (h % Q_GROUP) * BLK:(h % Q_GROUP + 1) * BLK, :] = acc


def _mixer_kernel(sinks_ref,
                  x_ref, mixg_ref, win_ref, qg_ref, kg_ref, bias_ref, convw_ref, ag_ref, cg_ref,
                  wout_ref, fg_ref, wr_ref, br_ref, k0m_ref, k1m_ref, v0m_ref, v1m_ref, utm_ref,
                  bdq_ref, bdk_ref, tri_ref,
                  h2_ref, hn2_ref, route_ref, cnt_ref,
                  kp0, kp1, vp0, vp1, ubuf, a_scr, cnt_acc):
    b = pl.program_id(0)
    t = pl.program_id(1)

    @pl.when(t == 0)
    def _():
        kp0[...] = k0m_ref[...]
        kp1[...] = k1m_ref[...]
        vp0[...] = v0m_ref[...]
        vp1[...] = v1m_ref[...]
        ubuf[0:SUBLANES, :] = utm_ref[...]

    @pl.when(jnp.logical_and(b == 0, t == 0))
    def _():
        cnt_acc[...] = jnp.zeros_like(cnt_acc)

    x = x_ref[...]
    hn = _rms(x, mixg_ref[...]).astype(_bf16)

    q = _dot(hn, win_ref[:, Q_OFF:Q_OFF + ATTN_DIM])
    ssq = _dot((q * q).astype(_bf16), bdq_ref[...])
    qn = (q * lax.rsqrt(ssq * (1.0 / HEAD_DIM) + EPS) * qg_ref[...]).astype(_bf16)
    kd0, kd1, vd0, vd1 = _kv_state(hn, win_ref, kg_ref, bdk_ref)
    kd = (kd0, kd1)
    vd = (vd0, vd1)
    kp = (kp0, kp1)
    vp = (vp0, vp1)

    lane_q = lax.broadcasted_iota(jnp.int32, (BLK, LANES), 1)
    lo_half = lane_q < HEAD_DIM
    row4 = lax.broadcasted_iota(jnp.int32, (Q_GROUP * BLK, 1), 0) // BLK
    first = jnp.where(t == 0, 0, 1)
    zero_bf = jnp.zeros((BLK, LANES), _bf16)

    for j in range(TM // BLK):
        rows = slice(j * BLK, (j + 1) * BLK)
        for g in range(N_KV_HEADS):
            if j == 0:
                kcat = jnp.concatenate([kp[g][...], kd[g][rows]], axis=0)
                vcat = jnp.concatenate([vp[g][...], vd[g][rows]], axis=0)
                bias = bias_ref[first, g]
            else:
                kcat = kd[g][(j - 1) * BLK:(j + 1) * BLK]
                vcat = vd[g][(j - 1) * BLK:(j + 1) * BLK]
                bias = bias_ref[1, g]
            qs = []
            for hh in range(Q_GROUP):
                h = g * Q_GROUP + hh
                qc = qn[rows, (h // 2) * LANES:(h // 2 + 1) * LANES]
                keep = lo_half if h % 2 == 0 else jnp.logical_not(lo_half)
                qs.append(jnp.where(keep, qc, zero_bf))
            q4 = jnp.concatenate(qs, axis=0)
            s = lax.dot_general(q4, kcat, (((1,), (1,)), ((), ())),
                                preferred_element_type=_f32) + bias
            sink = jnp.full((Q_GROUP * BLK, 1), sinks_ref[g * Q_GROUP], _f32)
            for hh in range(1, Q_GROUP):
                sink = jnp.where(row4 == hh, sinks_ref[g * Q_GROUP + hh], sink)
            m = jnp.maximum(jnp.max(s, axis=-1, keepdims=True), sink)
            p = jnp.exp(s - m)
            l = jnp.sum(p, axis=-1, keepdims=True) + jnp.exp(sink - m)
            o = _dot(p.astype(_bf16), vcat) / l
            for pp in range(Q_GROUP // 2):
                ev = o[(2 * pp) * BLK:(2 * pp + 1) * BLK]
                od = o[(2 * pp + 1) * BLK:(2 * pp + 2) * BLK]
                col = g * (Q_GROUP // 2) + pp
                a_scr[rows, col * LANES:(col + 1) * LANES] = jnp.where(lo_half, ev, od)

    last = slice(TM - BLK, TM)
    kp0[...] = kd0[last]
    kp1[...] = kd1[last]
    vp0[...] = vd0[last]
    vp1[...] = vd1[last]

    cb = _dot(hn, win_ref[:, CB_OFF:CB_OFF + CONV_CH])
    cch = _dot(hn, win_ref[:, CC_OFF:CC_OFF + 2 * CONV_CH])
    u = cch[:, :CONV_CH] * cch[:, CONV_CH:]
    ubuf[SUBLANES:, :] = u
    u1 = ubuf[SUBLANES - 1:SUBLANES - 1 + TM, :]
    u2 = ubuf[SUBLANES - 2:SUBLANES - 2 + TM, :]
    c = cb * (convw_ref[0:1, :] * u2 + convw_ref[1:2, :] * u1 + convw_ref[2:3, :] * u)
    ubuf[0:SUBLANES, :] = u[TM - SUBLANES:, :]

    an = _rms(a_scr[...], ag_ref[...]).astype(_bf16)
    cn = _rms(c, cg_ref[...]).astype(_bf16)
    h2 = x + _dot(an, wout_ref[0:ATTN_DIM, :]) + _dot(cn, wout_ref[ATTN_DIM:, :])
    h2_ref[...] = h2

    hn2 = _rms(h2, fg_ref[...])
    for cc in range(ROW_CHUNKS):
        hn2_ref[pl.ds(cc, TM, stride=ROW_CHUNKS), :] = hn2[:, cc * LANES:(cc + 1) * LANES]

    hi = hn2.astype(_bf16)
    lo = (hn2 - hi.astype(_f32)).astype(_bf16)
    r1 = _dot(hi, wr_ref[...])
    r2 = _dot(lo, wr_ref[...])
    lg = r1 + pltpu.roll(r1, HEAD_DIM, axis=1) + r2 + br_ref[...]
    lane = lax.broadcasted_iota(jnp.int32, (TM, LANES), 1)
    lanef = lane.astype(_f32)
    ninf = jnp.float32(-jnp.inf)
    big = jnp.float32(LANES)

    gl = jnp.where(lane < N_GROUPS, lg, ninf)
    gmax = jnp.max(gl, axis=-1, keepdims=True)
    gsum = jnp.sum(jnp.exp(gl - gmax), axis=-1, keepdims=True)
    g_p = 1.0 / gsum
    g_idx = jnp.min(jnp.where(gl == gmax, lanef, big), axis=-1, keepdims=True)
    e_lo = N_GROUPS + EXPERTS_PER_GROUP * g_idx
    el = jnp.where(jnp.logical_and(lanef >= e_lo, lanef < e_lo + EXPERTS_PER_GROUP), lg, ninf)
    m1 = jnp.max(el, axis=-1, keepdims=True)
    i1 = jnp.min(jnp.where(el == m1, lanef, big), axis=-1, keepdims=True)
    el2 = jnp.where(lanef == i1, ninf, el)
    m2 = jnp.max(el2, axis=-1, keepdims=True)
    i2 = jnp.min(jnp.where(el2 == m2, lanef, big), axis=-1, keepdims=True)
    ex = jnp.exp(m2 - m1)
    den = 1.0 / (1.0 + ex)
    gate0 = g_p * den
    gate1 = g_p * ex * den
    e0 = i1 - N_GROUPS
    e1 = i2 - N_GROUPS

    oh0 = lanef == e0
    oh1 = lanef == e1
    cmat = (jnp.where(oh0, 1.0, 0.0) + jnp.where(oh1, 1.0, 0.0))
    prefix = _dot(tri_ref[...], cmat.astype(_bf16)) + cnt_acc[...]
    rank0 = jnp.sum(jnp.where(oh0, prefix, 0.0), axis=-1, keepdims=True)
    rank1 = jnp.sum(jnp.where(oh1, prefix, 0.0), axis=-1, keepdims=True)
    cnt_new = cnt_acc[...] + jnp.sum(cmat, axis=0, keepdims=True)
    cnt_acc[...] = cnt_new
    cnt_ref[...] = cnt_new

    lane8 = lax.broadcasted_iota(jnp.int32, (TM, ROUTE_W), 1)
    rec = jnp.zeros((TM, ROUTE_W), _f32)
    for idx, val in enumerate((e0, e1, gate0, gate1, rank0, rank1)):
        rec = jnp.where(lane8 == idx, val, rec)
    route_ref[...] = rec


def _row_tile(ref, row):
    return ref.at[pl.ds(pl.multiple_of(row * ROW_CHUNKS, ROW_CHUNKS), ROW_CHUNKS), :]


def _dispatch_kernel(pos_ref, zf_ref, hn2_ref, zeros_hbm, xs_hbm, sem):
    g = pl.program_id(0)

    @pl.when(g == 0)
    def _():
        def fill(tile):
            rows = TME * ROW_CHUNKS
            return pltpu.make_async_copy(
                zeros_hbm, xs_hbm.at[pl.ds(pl.multiple_of(tile * rows, rows), rows), :], sem.at[1])

        def start(i, carry):
            @pl.when(zf_ref[i] == 1)
            def _():
                fill(i).start()
            return carry

        def wait(i, carry):
            @pl.when(zf_ref[i] == 1)
            def _():
                fill(i).wait()
            return carry

        lax.fori_loop(0, zf_ref.shape[0], start, 0)
        lax.fori_loop(0, zf_ref.shape[0], wait, 0)

    def body(it, carry):
        for j in range(DMA_UNROLL):
            r = it * DMA_UNROLL + j
            for k in range(2):
                pltpu.make_async_copy(_row_tile(hn2_ref, r), _row_tile(xs_hbm, pos_ref[2 * (g * TMD + r) + k]),
                                      sem.at[0]).start()
        return carry

    lax.fori_loop(0, TMD // DMA_UNROLL, body, 0)
    rows = TMD * ROW_CHUNKS
    for k in range(2):
        pltpu.make_async_copy(hn2_ref, xs_hbm.at[pl.ds(0, rows), :], sem.at[0]).wait()


def _experts_kernel(te_ref, tn_ref,
                    xs_ref, wg_ref, wu_ref, wd_ref,
                    y_ref,
                    wgb, wub, wdb):
    i = pl.program_id(0)
    changed = jnp.logical_or(i == 0, te_ref[i] != te_ref[jnp.maximum(i - 1, 0)])

    @pl.when(jnp.logical_and(changed, tn_ref[i] > 0))
    def _():
        wgb[...] = wg_ref[...].astype(_bf16)
        wub[...] = wu_ref[...].astype(_bf16)
        wdb[...] = wd_ref[...].astype(_bf16)

    @pl.when(tn_ref[i] > 0)
    def _():
        xs = [xs_ref[pl.ds(cc, TME, stride=ROW_CHUNKS), :] for cc in range(ROW_CHUNKS)]
        xb = jnp.concatenate(xs, axis=1).astype(_bf16)
        live = lax.broadcasted_iota(jnp.int32, (TME, 1), 0) < tn_ref[i]
        xb = jnp.where(live, xb, jnp.zeros_like(xb))
        gate = _dot(xb, wgb[...])
        up = _dot(xb, wub[...])
        act = (gate * jax.nn.sigmoid(gate) * up).astype(_bf16)
        y = _dot(act, wdb[...])
        for cc in range(ROW_CHUNKS):
            y_ref[pl.ds(cc, TME, stride=ROW_CHUNKS), :] = y[:, cc * LANES:(cc + 1) * LANES]

    @pl.when(tn_ref[i] == 0)
    def _():
        y_ref[...] = jnp.zeros_like(y_ref)


def _combine_kernel(pos_ref,
                    y_hbm, h2_ref, route_ref,
                    out_ref,
                    ybuf, sem):
    i = pl.program_id(0)
    n = pl.num_programs(0)
    slot = i % 2

    def issue(tile, sl):
        base = tile * (2 * TMC)

        def body(it, carry):
            for j in range(DMA_UNROLL):
                r = it * DMA_UNROLL + j
                for k in range(2):
                    pltpu.make_async_copy(_row_tile(y_hbm, pos_ref[base + 2 * r + k]),
                                          _row_tile(ybuf.at[sl, k], r), sem.at[sl]).start()
            return carry

        lax.fori_loop(0, TMC // DMA_UNROLL, body, 0)

    @pl.when(i == 0)
    def _():
        issue(0, 0)

    @pl.when(i + 1 < n)
    def _():
        issue(i + 1, 1 - slot)

    for k in range(2):
        pltpu.make_async_copy(y_hbm.at[pl.ds(0, TMC * ROW_CHUNKS), :], ybuf.at[slot, k], sem.at[slot]).wait()
    g0 = route_ref[:, 2:3]
    g1 = route_ref[:, 3:4]
    for cc in range(ROW_CHUNKS):
        y0 = ybuf[slot, 0, pl.ds(cc, TMC, stride=ROW_CHUNKS), :]
        y1 = ybuf[slot, 1, pl.ds(cc, TMC, stride=ROW_CHUNKS), :]
        cols = slice(cc * LANES, (cc + 1) * LANES)
        out_ref[:, cols] = h2_ref[:, cols] + (g0 * y0 + g1 * y1)


def _t5_bucket(n):
    max_exact = N_BUCKETS // 2
    nf = jnp.maximum(n, 1).astype(_f32)
    large = max_exact + (jnp.log(nf / max_exact) / np.log(MAX_DISTANCE / max_exact)
                         * (N_BUCKETS - max_exact)).astype(jnp.int32)
    large = jnp.minimum(large, N_BUCKETS - 1)
    return jnp.where(n < max_exact, n, large)


def _bucket_maps():
    qi = jnp.arange(BLK)[:, None]
    sj = jnp.arange(2 * BLK)[None, :]
    dist = BLK + qi - sj
    band = (dist >= 0) & (dist < BLK)
    bucket = _t5_bucket(jnp.maximum(dist, 0))
    generic = jnp.where(band, bucket, -1)
    first = jnp.where(band & (sj >= PAD), bucket, -1)
    return jnp.stack([first, generic]).astype(jnp.int32)


def _const(shape):
    nd = len(shape)
    return pl.BlockSpec(shape, lambda *_: (0,) * nd)


def _block_diag_ones(n):
    idx = np.arange(n) // HEAD_DIM
    return jnp.asarray((idx[:, None] == idx[None, :]).astype(np.float32), dtype=_bf16)


def kernel(x, meta_tokens, rel_bias, mix_norm_g, w_in, q_norm_g, k_norm_g, attn_sinks, conv_w, attn_out_norm_g, conv_out_norm_g, w_out, ffn_norm_g, w_group_router, b_group_router, w_expert_router, b_expert_router, w_gate, w_up, w_down):
    bsz, seq, _ = x.shape
    assert seq % TM == 0 and (bsz * seq) % TMC == 0
    n_tok = bsz * seq
    nt = seq // TM

    win = w_in[0].astype(_bf16)
    wout = w_out[0].astype(_bf16)
    mixg = mix_norm_g[0].reshape(1, D_MODEL)
    fg = ffn_norm_g[0].reshape(1, D_MODEL)
    qg = (jnp.tile(q_norm_g[0], N_Q_HEADS) * (HEAD_DIM ** -0.5)).reshape(1, ATTN_DIM)
    kg = jnp.tile(k_norm_g[0], N_KV_HEADS).reshape(1, KV_DIM)
    ag = attn_out_norm_g[0].reshape(1, ATTN_DIM)
    cg = conv_out_norm_g[0].reshape(1, CONV_CH)
    convw = conv_w[0]
    sinks = attn_sinks[0]
    w_r = jnp.concatenate([w_group_router[0], w_expert_router[0].reshape(D_MODEL, N_EXPERTS)], axis=1)
    n_r = N_GROUPS + N_EXPERTS
    w_r_hi = w_r.astype(_bf16)
    w_r_lo = (w_r - w_r_hi.astype(_f32)).astype(_bf16)
    wr = jnp.zeros((D_MODEL, LANES), _bf16)
    wr = wr.at[:, :n_r].set(w_r_hi).at[:, HEAD_DIM:HEAD_DIM + n_r].set(w_r_lo)
    br = jnp.zeros((1, LANES), _f32).at[0, :n_r].set(
        jnp.concatenate([b_group_router[0], b_expert_router[0].reshape(N_EXPERTS)]))
    meta_blk = jnp.concatenate([jnp.zeros((PAD, D_MODEL), x.dtype), meta_tokens.astype(x.dtype)], axis=0)
    bdq = _block_diag_ones(ATTN_DIM)
    bdk = _block_diag_ones(KV_DIM)
    tri = jnp.asarray(np.tril(np.ones((TM, TM), np.float32), -1), dtype=_bf16)

    kv_sd = jax.ShapeDtypeStruct((BLK, KV_DIM), _bf16)
    k0m, k1m, v0m, v1m, utm, bias = pl.pallas_call(
        _prep_kernel,
        out_shape=(kv_sd, kv_sd, kv_sd, kv_sd,
                   jax.ShapeDtypeStruct((SUBLANES, CONV_CH), _f32),
                   jax.ShapeDtypeStruct((2, N_KV_HEADS, Q_GROUP * BLK, 2 * BLK), _f32)),
        in_specs=[pl.BlockSpec(memory_space=pltpu.SMEM)] + [pl.BlockSpec(memory_space=pltpu.VMEM)] * 6,
        out_specs=tuple(pl.BlockSpec(memory_space=pltpu.VMEM) for _ in range(6)),
        compiler_params=pltpu.CompilerParams(vmem_limit_bytes=VMEM_LIMIT),
        name="prep",
    )(rel_bias, meta_blk, mixg, win, kg, bdk, _bucket_maps())

    tile_idx = lambda b, t, *_: (b * nt + t, 0)
    grid_spec = pltpu.PrefetchScalarGridSpec(
        num_scalar_prefetch=1,
        grid=(bsz, nt),
        in_specs=[
            pl.BlockSpec((None, TM, D_MODEL), lambda b, t, *_: (b, t, 0)),
            _const((1, D_MODEL)), _const((D_MODEL, IN_PROJ)), _const((1, ATTN_DIM)), _const((1, KV_DIM)),
            _const((2, N_KV_HEADS, Q_GROUP * BLK, 2 * BLK)), _const((3, CONV_CH)),
            _const((1, ATTN_DIM)), _const((1, CONV_CH)), _const((D_MODEL, D_MODEL)), _const((1, D_MODEL)),
            _const((D_MODEL, LANES)), _const((1, LANES)),
            _const((BLK, KV_DIM)), _const((BLK, KV_DIM)), _const((BLK, KV_DIM)), _const((BLK, KV_DIM)),
            _const((SUBLANES, CONV_CH)),
            _const((ATTN_DIM, ATTN_DIM)), _const((KV_DIM, KV_DIM)), _const((TM, TM)),
        ],
        out_specs=[
            pl.BlockSpec((None, TM, D_MODEL), lambda b, t, *_: (b, t, 0)),
            pl.BlockSpec((TM * ROW_CHUNKS, LANES), tile_idx),
            pl.BlockSpec((TM, ROUTE_W), tile_idx),
            _const((1, LANES)),
        ],
        scratch_shapes=[
            pltpu.VMEM((BLK, KV_DIM), _bf16), pltpu.VMEM((BLK, KV_DIM), _bf16),
            pltpu.VMEM((BLK, KV_DIM), _bf16), pltpu.VMEM((BLK, KV_DIM), _bf16),
            pltpu.VMEM((TM + SUBLANES, CONV_CH), _f32),
            pltpu.VMEM((TM, ATTN_DIM), _f32),
            pltpu.VMEM((1, LANES), _f32),
        ],
    )
    h2, hn2, route, cnt = pl.pallas_call(
        _mixer_kernel,
        grid_spec=grid_spec,
        out_shape=(jax.ShapeDtypeStruct((bsz, seq, D_MODEL), _f32),
                   jax.ShapeDtypeStruct((n_tok * ROW_CHUNKS, LANES), _f32),
                   jax.ShapeDtypeStruct((n_tok, ROUTE_W), _f32),
                   jax.ShapeDtypeStruct((1, LANES), _f32)),
        compiler_params=pltpu.CompilerParams(dimension_semantics=("arbitrary", "arbitrary"),
                                             vmem_limit_bytes=VMEM_LIMIT),
        name="mixer",
    )(sinks, x, mixg, win, qg, kg, bias, convw, ag, cg, wout, fg, wr, br,
      k0m, k1m, v0m, v1m, utm, bdq, bdk, tri)

    n_tiles = (n_tok * 2) // TME + N_EXPERTS
    counts = cnt[0, :N_EXPERTS].astype(jnp.int32)
    ntile = (counts + TME - 1) // TME
    tile_end = jnp.cumsum(ntile)
    tile_start = tile_end - ntile
    eid = route[:, 0:2].astype(jnp.int32)
    rank = route[:, 4:6].astype(jnp.int32)
    experts = jnp.arange(N_EXPERTS, dtype=jnp.int32)
    start_of = jnp.sum(jnp.where(eid[:, :, None] == experts, tile_start, 0), axis=-1)
    pos = (start_of * TME + rank).reshape(-1)
    tiles = jnp.arange(n_tiles, dtype=jnp.int32)
    n_used = tile_end[-1]
    t_exp = jnp.sum((jnp.minimum(tiles, n_used - 1)[:, None] >= tile_end[None, :]).astype(jnp.int32), axis=-1)
    t_exp = jnp.minimum(t_exp, N_EXPERTS - 1)
    own = t_exp[:, None] == experts
    t_rows = jnp.sum(jnp.where(own, counts - (tiles[:, None] - tile_start) * TME, 0), axis=-1)
    t_rows = jnp.where(tiles < n_used, jnp.clip(t_rows, 0, TME), 0).astype(jnp.int32)
    t_fill = (t_rows < TME).astype(jnp.int32)

    xs = pl.pallas_call(
        _dispatch_kernel,
        grid_spec=pltpu.PrefetchScalarGridSpec(
            num_scalar_prefetch=2,
            grid=(n_tok // TMD,),
            in_specs=[pl.BlockSpec((TMD * ROW_CHUNKS, LANES), lambda g, *_: (g, 0)),
                      pl.BlockSpec(memory_space=pl.ANY)],
            out_specs=pl.BlockSpec(memory_space=pl.ANY),
            scratch_shapes=[pltpu.SemaphoreType.DMA((2,))],
        ),
        out_shape=jax.ShapeDtypeStruct((n_tiles * TME * ROW_CHUNKS, LANES), _f32),
        compiler_params=pltpu.CompilerParams(dimension_semantics=("arbitrary",)),
        name="dispatch",
    )(pos, t_fill, hn2, jnp.zeros((TME * ROW_CHUNKS, LANES), _f32))

    y_sorted = pl.pallas_call(
        _experts_kernel,
        grid_spec=pltpu.PrefetchScalarGridSpec(
            num_scalar_prefetch=2,
            grid=(n_tiles,),
            in_specs=[
                pl.BlockSpec((TME * ROW_CHUNKS, LANES), lambda i, *_: (i, 0)),
                pl.BlockSpec((None, D_MODEL, D_EXPERT), lambda i, te, tn: (te[i], 0, 0)),
                pl.BlockSpec((None, D_MODEL, D_EXPERT), lambda i, te, tn: (te[i], 0, 0)),
                pl.BlockSpec((None, D_EXPERT, D_MODEL), lambda i, te, tn: (te[i], 0, 0)),
            ],
            out_specs=pl.BlockSpec((TME * ROW_CHUNKS, LANES), lambda i, *_: (i, 0)),
            scratch_shapes=[
                pltpu.VMEM((D_MODEL, D_EXPERT), _bf16),
                pltpu.VMEM((D_MODEL, D_EXPERT), _bf16),
                pltpu.VMEM((D_EXPERT, D_MODEL), _bf16),
            ],
        ),
        out_shape=jax.ShapeDtypeStruct((n_tiles * TME * ROW_CHUNKS, LANES), _f32),
        compiler_params=pltpu.CompilerParams(dimension_semantics=("arbitrary",),
                                             vmem_limit_bytes=VMEM_LIMIT),
        name="experts",
    )(t_exp, t_rows, xs, w_gate[0], w_up[0], w_down[0])

    out = pl.pallas_call(
        _combine_kernel,
        grid_spec=pltpu.PrefetchScalarGridSpec(
            num_scalar_prefetch=1,
            grid=(n_tok // TMC,),
            in_specs=[
                pl.BlockSpec(memory_space=pl.ANY),
                pl.BlockSpec((TMC, D_MODEL), lambda i, *_: (i, 0)),
                pl.BlockSpec((TMC, ROUTE_W), lambda i, *_: (i, 0)),
            ],
            out_specs=pl.BlockSpec((TMC, D_MODEL), lambda i, *_: (i, 0)),
            scratch_shapes=[
                pltpu.VMEM((2, 2, TMC * ROW_CHUNKS, LANES), _f32),
                pltpu.SemaphoreType.DMA((2,)),
            ],
        ),
        out_shape=jax.ShapeDtypeStruct((n_tok, D_MODEL), _f32),
        compiler_params=pltpu.CompilerParams(dimension_semantics=("arbitrary",),
                                             vmem_limit_bytes=VMEM_LIMIT),
        name="combine",
    )(pos, y_sorted, h2.reshape(n_tok, D_MODEL), route)
    return out.reshape(bsz, seq, D_MODEL)
```

```python
import functools

import numpy as np
import jax
import jax.numpy as jnp
from jax import lax
from jax.experimental import pallas as pl
from jax.experimental.pallas import tpu as pltpu

D_MODEL = 1024
N_META = 16
N_Q_HEADS = 8
N_KV_HEADS = 2
HEAD_DIM = 64
Q_GROUP = N_Q_HEADS // N_KV_HEADS
ATTN_DIM = N_Q_HEADS * HEAD_DIM
KV_DIM = N_KV_HEADS * HEAD_DIM
BLK = 128
PAD = BLK - N_META
N_BUCKETS = 32
MAX_DISTANCE = 128
CONV_CH = D_MODEL // 2
IN_PROJ = ATTN_DIM + 2 * KV_DIM + 3 * CONV_CH
N_GROUPS = 4
EXPERTS_PER_GROUP = 8
N_EXPERTS = N_GROUPS * EXPERTS_PER_GROUP
D_EXPERT = D_MODEL // 2
EPS = 1e-6
NEG_INF = -1e30

LANES = 128
SUBLANES = 8
ROW_CHUNKS = D_MODEL // LANES
TM = 512
TME = 256
TMC = 256
INV_BLK = 8192
SCALAR_UNROLL = 16
DMA_UNROLL = 8
ROUTE_W = 8
VMEM_LIMIT = 56 * 1024 * 1024

Q_OFF, K_OFF, V_OFF = 0, ATTN_DIM, ATTN_DIM + KV_DIM
CB_OFF = ATTN_DIM + 2 * KV_DIM
CC_OFF = CB_OFF + CONV_CH
CH_OFF = CC_OFF + CONV_CH

_f32 = jnp.float32
_bf16 = jnp.bfloat16


def _rms(x, g):
    return x * lax.rsqrt(jnp.mean(x * x, axis=-1, keepdims=True) + EPS) * g


def _dot(a, b):
    return jnp.dot(a, b, preferred_element_type=_f32)


def _dup_halves(x):
    lane = lax.broadcasted_iota(jnp.int32, x.shape, 1)
    sw = pltpu.roll(x, HEAD_DIM, axis=1)
    lo = lane < HEAD_DIM
    return jnp.where(lo, x, sw).astype(_bf16), jnp.where(lo, sw, x).astype(_bf16)


def _kv_state(hn_bf, win_ref, kg_ref, bdk_ref):
    kv = _dot(hn_bf, win_ref[:, K_OFF:K_OFF + 2 * KV_DIM])
    k = kv[:, :KV_DIM]
    v = kv[:, KV_DIM:]
    ssk = _dot((k * k).astype(_bf16), bdk_ref[...])
    kn = k * lax.rsqrt(ssk * (1.0 / HEAD_DIM) + EPS) * kg_ref[...]
    return _dup_halves(kn) + _dup_halves(v)


def _prep_kernel(rb_ref, meta_ref, mixg_ref, win_ref, kg_ref, bdk_ref, bucket_ref,
                 k0_ref, k1_ref, v0_ref, v1_ref, ut_ref, bias_ref):
    hn = _rms(meta_ref[...], mixg_ref[...]).astype(_bf16)
    k0, k1, v0, v1 = _kv_state(hn, win_ref, kg_ref, bdk_ref)
    k0_ref[...] = k0
    k1_ref[...] = k1
    v0_ref[...] = v0
    v1_ref[...] = v1
    cch = _dot(hn, win_ref[:, CC_OFF:CC_OFF + 2 * CONV_CH])
    u = cch[:, :CONV_CH] * cch[:, CONV_CH:]
    ut_ref[...] = u[BLK - SUBLANES:, :]
    for f in range(2):
        bk = bucket_ref[f]
        for h in range(N_Q_HEADS):
            acc = jnp.full((BLK, 2 * BLK), NEG_INF, _f32)
            for b in range(N_BUCKETS):
                acc = jnp.where(bk == b, rb_ref[b, h], acc)
            bias_ref[f, h // Q_GROUP, (h % Q_GROUP) * BLK:(h % Q_GROUP + 1) * BLK, :] = acc


def _mixer_kernel(sinks_ref,
                  x_ref, mixg_ref, win_ref, qg_ref, kg_ref, bias_ref, convw_ref, ag_ref, cg_ref,
                  wout_ref, fg_ref, wr_ref, br_ref, k0m_ref, k1m_ref, v0m_ref, v1m_ref, utm_ref,
                  bdq_ref, bdk_ref, tri_ref,
                  h2_ref, hn2_ref, route_ref, cnt_ref,
                  kp0, kp1, vp0, vp1, ubuf, a_scr, cnt_acc):
    b = pl.program_id(0)
    t = pl.program_id(1)

    @pl.when(t == 0)
    def _():
        kp0[...] = k0m_ref[...]
        kp1[...] = k1m_ref[...]
        vp0[...] = v0m_ref[...]
        vp1[...] = v1m_ref[...]
        ubuf[0:SUBLANES, :] = utm_ref[...]

    @pl.when(jnp.logical_and(b == 0, t == 0))
    def _():
        cnt_acc[...] = jnp.zeros_like(cnt_acc)

    x = x_ref[...]
    hn = _rms(x, mixg_ref[...]).astype(_bf16)

    q = _dot(hn, win_ref[:, Q_OFF:Q_OFF + ATTN_DIM])
    ssq = _dot((q * q).astype(_bf16), bdq_ref[...])
    qn = (q * lax.rsqrt(ssq * (1.0 / HEAD_DIM) + EPS) * qg_ref[...]).astype(_bf16)
    kd0, kd1, vd0, vd1 = _kv_state(hn, win_ref, kg_ref, bdk_ref)
    kd = (kd0, kd1)
    vd = (vd0, vd1)
    kp = (kp0, kp1)
    vp = (vp0, vp1)

    lane_q = lax.broadcasted_iota(jnp.int32, (BLK, LANES), 1)
    lo_half = lane_q < HEAD_DIM
    row4 = lax.broadcasted_iota(jnp.int32, (Q_GROUP * BLK, 1), 0) // BLK
    first = jnp.where(t == 0, 0, 1)
    zero_bf = jnp.zeros((BLK, LANES), _bf16)

    for j in range(TM // BLK):
        rows = slice(j * BLK, (j + 1) * BLK)
        for g in range(N_KV_HEADS):
            if j == 0:
                kcat = jnp.concatenate([kp[g][...], kd[g][rows]], axis=0)
                vcat = jnp.concatenate([vp[g][...], vd[g][rows]], axis=0)
                bias = bias_ref[first, g]
            else:
                kcat = kd[g][(j - 1) * BLK:(j + 1) * BLK]
                vcat = vd[g][(j - 1) * BLK:(j + 1) * BLK]
                bias = bias_ref[1, g]
            qs = []
            for hh in range(Q_GROUP):
                h = g * Q_GROUP + hh
                qc = qn[rows, (h // 2) * LANES:(h // 2 + 1) * LANES]
                keep = lo_half if h % 2 == 0 else jnp.logical_not(lo_half)
                qs.append(jnp.where(keep, qc, zero_bf))
            q4 = jnp.concatenate(qs, axis=0)
            s = lax.dot_general(q4, kcat, (((1,), (1,)), ((), ())),
                                preferred_element_type=_f32) + bias
            sink = jnp.full((Q_GROUP * BLK, 1), sinks_ref[g * Q_GROUP], _f32)
            for hh in range(1, Q_GROUP):
                sink = jnp.where(row4 == hh, sinks_ref[g * Q_GROUP + hh], sink)
            m = jnp.maximum(jnp.max(s, axis=-1, keepdims=True), sink)
            p = jnp.exp(s - m)
            l = jnp.sum(p, axis=-1, keepdims=True) + jnp.exp(sink - m)
            o = _dot(p.astype(_bf16), vcat) / l
            for pp in range(Q_GROUP // 2):
                ev = o[(2 * pp) * BLK:(2 * pp + 1) * BLK]
                od = o[(2 * pp + 1) * BLK:(2 * pp + 2) * BLK]
                col = g * (Q_GROUP // 2) + pp
                a_scr[rows, col * LANES:(col + 1) * LANES] = jnp.where(lo_half, ev, od)

    last = slice(TM - BLK, TM)
    kp0[...] = kd0[last]
    kp1[...] = kd1[last]
    vp0[...] = vd0[last]
    vp1[...] = vd1[last]

    cb = _dot(hn, win_ref[:, CB_OFF:CB_OFF + CONV_CH])
    cch = _dot(hn, win_ref[:, CC_OFF:CC_OFF + 2 * CONV_CH])
    u = cch[:, :CONV_CH] * cch[:, CONV_CH:]
    ubuf[SUBLANES:, :] = u
    u1 = ubuf[SUBLANES - 1:SUBLANES - 1 + TM, :]
    u2 = ubuf[SUBLANES - 2:SUBLANES - 2 + TM, :]
    c = cb * (convw_ref[0:1, :] * u2 + convw_ref[1:2, :] * u1 + convw_ref[2:3, :] * u)
    ubuf[0:SUBLANES, :] = u[TM - SUBLANES:, :]

    an = _rms(a_scr[...], ag_ref[...]).astype(_bf16)
    cn = _rms(c, cg_ref[...]).astype(_bf16)
    h2 = x + _dot(an, wout_ref[0:ATTN_DIM, :]) + _dot(cn, wout_ref[ATTN_DIM:, :])
    h2_ref[...] = h2

    hn2 = _rms(h2, fg_ref[...])
    for cc in range(ROW_CHUNKS):
        hn2_ref[pl.ds(cc, TM, stride=ROW_CHUNKS), :] = hn2[:, cc * LANES:(cc + 1) * LANES]

    hi = hn2.astype(_bf16)
    lo = (hn2 - hi.astype(_f32)).astype(_bf16)
    r1 = _dot(hi, wr_ref[...])
    r2 = _dot(lo, wr_ref[...])
    lg = r1 + pltpu.roll(r1, HEAD_DIM, axis=1) + r2 + br_ref[...]
    lane = lax.broadcasted_iota(jnp.int32, (TM, LANES), 1)
    lanef = lane.astype(_f32)
    ninf = jnp.float32(-jnp.inf)
    big = jnp.float32(LANES)

    gl = jnp.where(lane < N_GROUPS, lg, ninf)
    gmax = jnp.max(gl, axis=-1, keepdims=True)
    gsum = jnp.sum(jnp.exp(gl - gmax), axis=-1, keepdims=True)
    g_p = 1.0 / gsum
    g_idx = jnp.min(jnp.where(gl == gmax, lanef, big), axis=-1, keepdims=True)
    e_lo = N_GROUPS + EXPERTS_PER_GROUP * g_idx
    el = jnp.where(jnp.logical_and(lanef >= e_lo, lanef < e_lo + EXPERTS_PER_GROUP), lg, ninf)
    m1 = jnp.max(el, axis=-1, keepdims=True)
    i1 = jnp.min(jnp.where(el == m1, lanef, big), axis=-1, keepdims=True)
    el2 = jnp.where(lanef == i1, ninf, el)
    m2 = jnp.max(el2, axis=-1, keepdims=True)
    i2 = jnp.min(jnp.where(el2 == m2, lanef, big), axis=-1, keepdims=True)
    ex = jnp.exp(m2 - m1)
    den = 1.0 / (1.0 + ex)
    gate0 = g_p * den
    gate1 = g_p * ex * den
    e0 = i1 - N_GROUPS
    e1 = i2 - N_GROUPS

    oh0 = lanef == e0
    oh1 = lanef == e1
    cmat = (jnp.where(oh0, 1.0, 0.0) + jnp.where(oh1, 1.0, 0.0))
    prefix = _dot(tri_ref[...], cmat.astype(_bf16)) + cnt_acc[...]
    rank0 = jnp.sum(jnp.where(oh0, prefix, 0.0), axis=-1, keepdims=True)
    rank1 = jnp.sum(jnp.where(oh1, prefix, 0.0), axis=-1, keepdims=True)
    cnt_new = cnt_acc[...] + jnp.sum(cmat, axis=0, keepdims=True)
    cnt_acc[...] = cnt_new
    cnt_ref[...] = cnt_new

    lane8 = lax.broadcasted_iota(jnp.int32, (TM, ROUTE_W), 1)
    rec = jnp.zeros((TM, ROUTE_W), _f32)
    for idx, val in enumerate((e0, e1, gate0, gate1, rank0, rank1)):
        rec = jnp.where(lane8 == idx, val, rec)
    route_ref[...] = rec


def _row_tile(ref, row):
    return ref.at[pl.ds(pl.multiple_of(row * ROW_CHUNKS, ROW_CHUNKS), ROW_CHUNKS), :]


def _invert_kernel(pos_ref, src_ref):
    g = pl.program_id(0)

    @pl.when(g == 0)
    def _():
        def init(it, carry):
            for j in range(SCALAR_UNROLL):
                src_ref[it * SCALAR_UNROLL + j] = 0
            return carry

        lax.fori_loop(0, src_ref.shape[0] // SCALAR_UNROLL, init, 0)

    def body(it, carry):
        for j in range(SCALAR_UNROLL):
            a = it * SCALAR_UNROLL + j
            src_ref[pos_ref[a]] = (g * INV_BLK + a) // 2
        return carry

    lax.fori_loop(0, INV_BLK // SCALAR_UNROLL, body, 0)


def _experts_kernel(te_ref, tn_ref, src_ref,
                    hn2_hbm, wg_ref, wu_ref, wd_ref,
                    y_ref,
                    xbuf, sem, wgb, wub, wdb):
    i = pl.program_id(0)
    n = pl.num_programs(0)
    slot = i % 2

    def issue(tile, sl):
        base = tile * TME

        def body(it, carry):
            for j in range(DMA_UNROLL):
                r = it * DMA_UNROLL + j
                pltpu.make_async_copy(_row_tile(hn2_hbm, src_ref[base + r]),
                                      _row_tile(xbuf.at[sl], r), sem.at[sl]).start()
            return carry

        lax.fori_loop(0, TME // DMA_UNROLL, body, 0)

    @pl.when(jnp.logical_and(i == 0, tn_ref[0] > 0))
    def _():
        issue(0, 0)

    @pl.when(jnp.logical_and(i + 1 < n, tn_ref[jnp.minimum(i + 1, n - 1)] > 0))
    def _():
        issue(i + 1, 1 - slot)

    changed = jnp.logical_or(i == 0, te_ref[i] != te_ref[jnp.maximum(i - 1, 0)])

    @pl.when(jnp.logical_and(changed, tn_ref[i] > 0))
    def _():
        wgb[...] = wg_ref[...].astype(_bf16)
        wub[...] = wu_ref[...].astype(_bf16)
        wdb[...] = wd_ref[...].astype(_bf16)

    @pl.when(tn_ref[i] > 0)
    def _():
        pltpu.make_async_copy(hn2_hbm.at[pl.ds(0, TME * ROW_CHUNKS), :], xbuf.at[slot], sem.at[slot]).wait()
        xs = [xbuf[slot, pl.ds(cc, TME, stride=ROW_CHUNKS), :] for cc in range(ROW_CHUNKS)]
        xb = jnp.concatenate(xs, axis=1).astype(_bf16)
        gate = _dot(xb, wgb[...])
        up = _dot(xb, wub[...])
        act = (gate * jax.nn.sigmoid(gate) * up).astype(_bf16)
        y = _dot(act, wdb[...])
        for cc in range(ROW_CHUNKS):
            y_ref[pl.ds(cc, TME, stride=ROW_CHUNKS), :] = y[:, cc * LANES:(cc + 1) * LANES]

    @pl.when(tn_ref[i] == 0)
    def _():
        y_ref[...] = jnp.zeros_like(y_ref)


def _combine_kernel(pos_ref,
                    y_hbm, h2_ref, route_ref,
                    out_ref,
                    ybuf, sem):
    i = pl.program_id(0)
    n = pl.num_programs(0)
    slot = i % 2

    def issue(tile, sl):
        base = tile * (2 * TMC)

        def body(it, carry):
            for j in range(DMA_UNROLL):
                r = it * DMA_UNROLL + j
                for k in range(2):
                    pltpu.make_async_copy(_row_tile(y_hbm, pos_ref[base + 2 * r + k]),
                                          _row_tile(ybuf.at[sl, k], r), sem.at[sl]).start()
            return carry

        lax.fori_loop(0, TMC // DMA_UNROLL, body, 0)

    @pl.when(i == 0)
    def _():
        issue(0, 0)

    @pl.when(i + 1 < n)
    def _():
        issue(i + 1, 1 - slot)

    for k in range(2):
        pltpu.make_async_copy(y_hbm.at[pl.ds(0, TMC * ROW_CHUNKS), :], ybuf.at[slot, k], sem.at[slot]).wait()
    g0 = route_ref[:, 2:3]
    g1 = route_ref[:, 3:4]
    for cc in range(ROW_CHUNKS):
        y0 = ybuf[slot, 0, pl.ds(cc, TMC, stride=ROW_CHUNKS), :]
        y1 = ybuf[slot, 1, pl.ds(cc, TMC, stride=ROW_CHUNKS), :]
        cols = slice(cc * LANES, (cc + 1) * LANES)
        out_ref[:, cols] = h2_ref[:, cols] + (g0 * y0 + g1 * y1)


def _t5_bucket(n):
    max_exact = N_BUCKETS // 2
    nf = jnp.maximum(n, 1).astype(_f32)
    large = max_exact + (jnp.log(nf / max_exact) / np.log(MAX_DISTANCE / max_exact)
                         * (N_BUCKETS - max_exact)).astype(jnp.int32)
    large = jnp.minimum(large, N_BUCKETS - 1)
    return jnp.where(n < max_exact, n, large)


def _bucket_maps():
    qi = jnp.arange(BLK)[:, None]
    sj = jnp.arange(2 * BLK)[None, :]
    dist = BLK + qi - sj
    band = (dist >= 0) & (dist < BLK)
    bucket = _t5_bucket(jnp.maximum(dist, 0))
    generic = jnp.where(band, bucket, -1)
    first = jnp.where(band & (sj >= PAD), bucket, -1)
    return jnp.stack([first, generic]).astype(jnp.int32)


def _const(shape):
    nd = len(shape)
    return pl.BlockSpec(shape, lambda *_: (0,) * nd)


def _block_diag_ones(n):
    idx = np.arange(n) // HEAD_DIM
    return jnp.asarray((idx[:, None] == idx[None, :]).astype(np.float32), dtype=_bf16)


def kernel(x, meta_tokens, rel_bias, mix_norm_g, w_in, q_norm_g, k_norm_g, attn_sinks, conv_w, attn_out_norm_g, conv_out_norm_g, w_out, ffn_norm_g, w_group_router, b_group_router, w_expert_router, b_expert_router, w_gate, w_up, w_down):
    bsz, seq, _ = x.shape
    assert seq % TM == 0 and (bsz * seq) % TMC == 0
    n_tok = bsz * seq
    nt = seq // TM

    win = w_in[0].astype(_bf16)
    wout = w_out[0].astype(_bf16)
    mixg = mix_norm_g[0].reshape(1, D_MODEL)
    fg = ffn_norm_g[0].reshape(1, D_MODEL)
    qg = (jnp.tile(q_norm_g[0], N_Q_HEADS) * (HEAD_DIM ** -0.5)).reshape(1, ATTN_DIM)
    kg = jnp.tile(k_norm_g[0], N_KV_HEADS).reshape(1, KV_DIM)
    ag = attn_out_norm_g[0].reshape(1, ATTN_DIM)
    cg = conv_out_norm_g[0].reshape(1, CONV_CH)
    convw = conv_w[0]
    sinks = attn_sinks[0]
    w_r = jnp.concatenate([w_group_router[0], w_expert_router[0].reshape(D_MODEL, N_EXPERTS)], axis=1)
    n_r = N_GROUPS + N_EXPERTS
    w_r_hi = w_r.astype(_bf16)
    w_r_lo = (w_r - w_r_hi.astype(_f32)).astype(_bf16)
    wr = jnp.zeros((D_MODEL, LANES), _bf16)
    wr = wr.at[:, :n_r].set(w_r_hi).at[:, HEAD_DIM:HEAD_DIM + n_r].set(w_r_lo)
    br = jnp.zeros((1, LANES), _f32).at[0, :n_r].set(
        jnp.concatenate([b_group_router[0], b_expert_router[0].reshape(N_EXPERTS)]))
    meta_blk = jnp.concatenate([jnp.zeros((PAD, D_MODEL), x.dtype), meta_tokens.astype(x.dtype)], axis=0)
    bdq = _block_diag_ones(ATTN_DIM)
    bdk = _block_diag_ones(KV_DIM)
    tri = jnp.asarray(np.tril(np.ones((TM, TM), np.float32), -1), dtype=_bf16)

    kv_sd = jax.ShapeDtypeStruct((BLK, KV_DIM), _bf16)
    k0m, k1m, v0m, v1m, utm, bias = pl.pallas_call(
        _prep_kernel,
        out_shape=(kv_sd, kv_sd, kv_sd, kv_sd,
                   jax.ShapeDtypeStruct((SUBLANES, CONV_CH), _f32),
                   jax.ShapeDtypeStruct((2, N_KV_HEADS, Q_GROUP * BLK, 2 * BLK), _f32)),
        in_specs=[pl.BlockSpec(memory_space=pltpu.SMEM)] + [pl.BlockSpec(memory_space=pltpu.VMEM)] * 6,
        out_specs=tuple(pl.BlockSpec(memory_space=pltpu.VMEM) for _ in range(6)),
        compiler_params=pltpu.CompilerParams(vmem_limit_bytes=VMEM_LIMIT),
        name="prep",
    )(rel_bias, meta_blk, mixg, win, kg, bdk, _bucket_maps())

    tile_idx = lambda b, t, *_: (b * nt + t, 0)
    grid_spec = pltpu.PrefetchScalarGridSpec(
        num_scalar_prefetch=1,
        grid=(bsz, nt),
        in_specs=[
            pl.BlockSpec((None, TM, D_MODEL), lambda b, t, *_: (b, t, 0)),
            _const((1, D_MODEL)), _const((D_MODEL, IN_PROJ)), _const((1, ATTN_DIM)), _const((1, KV_DIM)),
            _const((2, N_KV_HEADS, Q_GROUP * BLK, 2 * BLK)), _const((3, CONV_CH)),
            _const((1, ATTN_DIM)), _const((1, CONV_CH)), _const((D_MODEL, D_MODEL)), _const((1, D_MODEL)),
            _const((D_MODEL, LANES)), _const((1, LANES)),
            _const((BLK, KV_DIM)), _const((BLK, KV_DIM)), _const((BLK, KV_DIM)), _const((BLK, KV_DIM)),
            _const((SUBLANES, CONV_CH)),
            _const((ATTN_DIM, ATTN_DIM)), _const((KV_DIM, KV_DIM)), _const((TM, TM)),
        ],
        out_specs=[
            pl.BlockSpec((None, TM, D_MODEL), lambda b, t, *_: (b, t, 0)),
            pl.BlockSpec((TM * ROW_CHUNKS, LANES), tile_idx),
            pl.BlockSpec((TM, ROUTE_W), tile_idx),
            _const((1, LANES)),
        ],
        scratch_shapes=[
            pltpu.VMEM((BLK, KV_DIM), _bf16), pltpu.VMEM((BLK, KV_DIM), _bf16),
            pltpu.VMEM((BLK, KV_DIM), _bf16), pltpu.VMEM((BLK, KV_DIM), _bf16),
            pltpu.VMEM((TM + SUBLANES, CONV_CH), _f32),
            pltpu.VMEM((TM, ATTN_DIM), _f32),
            pltpu.VMEM((1, LANES), _f32),
        ],
    )
    h2, hn2, route, cnt = pl.pallas_call(
        _mixer_kernel,
        grid_spec=grid_spec,
        out_shape=(jax.ShapeDtypeStruct((bsz, seq, D_MODEL), _f32),
                   jax.ShapeDtypeStruct((n_tok * ROW_CHUNKS, LANES), _f32),
                   jax.ShapeDtypeStruct((n_tok, ROUTE_W), _f32),
                   jax.ShapeDtypeStruct((1, LANES), _f32)),
        compiler_params=pltpu.CompilerParams(dimension_semantics=("arbitrary", "arbitrary"),
                                             vmem_limit_bytes=VMEM_LIMIT),
        name="mixer",
    )(sinks, x, mixg, win, qg, kg, bias, convw, ag, cg, wout, fg, wr, br,
      k0m, k1m, v0m, v1m, utm, bdq, bdk, tri)

    n_tiles = (n_tok * 2) // TME + N_EXPERTS
    counts = cnt[0, :N_EXPERTS].astype(jnp.int32)
    ntile = (counts + TME - 1) // TME
    tile_end = jnp.cumsum(ntile)
    tile_start = tile_end - ntile
    eid = route[:, 0:2].astype(jnp.int32)
    rank = route[:, 4:6].astype(jnp.int32)
    experts = jnp.arange(N_EXPERTS, dtype=jnp.int32)
    start_of = jnp.sum(jnp.where(eid[:, :, None] == experts, tile_start, 0), axis=-1)
    pos = (start_of * TME + rank).reshape(-1)
    tiles = jnp.arange(n_tiles, dtype=jnp.int32)
    n_used = tile_end[-1]
    t_exp = jnp.sum((jnp.minimum(tiles, n_used - 1)[:, None] >= tile_end[None, :]).astype(jnp.int32), axis=-1)
    t_exp = jnp.minimum(t_exp, N_EXPERTS - 1)
    own = t_exp[:, None] == experts
    t_rows = jnp.sum(jnp.where(own, counts - (tiles[:, None] - tile_start) * TME, 0), axis=-1)
    t_rows = jnp.where(tiles < n_used, jnp.clip(t_rows, 0, TME), 0).astype(jnp.int32)

    src_tok = pl.pallas_call(
        _invert_kernel,
        grid=((2 * n_tok) // INV_BLK,),
        in_specs=[pl.BlockSpec((INV_BLK,), lambda g: (g,), memory_space=pltpu.SMEM)],
        out_specs=pl.BlockSpec(memory_space=pltpu.SMEM),
        out_shape=jax.ShapeDtypeStruct((n_tiles * TME,), jnp.int32),
        compiler_params=pltpu.CompilerParams(dimension_semantics=("arbitrary",)),
        name="invert",
    )(pos)

    y_sorted = pl.pallas_call(
        _experts_kernel,
        grid_spec=pltpu.PrefetchScalarGridSpec(
            num_scalar_prefetch=3,
            grid=(n_tiles,),
            in_specs=[
                pl.BlockSpec(memory_space=pl.ANY),
                pl.BlockSpec((None, D_MODEL, D_EXPERT), lambda i, te, tn, src: (te[i], 0, 0)),
                pl.BlockSpec((None, D_MODEL, D_EXPERT), lambda i, te, tn, src: (te[i], 0, 0)),
                pl.BlockSpec((None, D_EXPERT, D_MODEL), lambda i, te, tn, src: (te[i], 0, 0)),
            ],
            out_specs=pl.BlockSpec((TME * ROW_CHUNKS, LANES), lambda i, *_: (i, 0)),
            scratch_shapes=[
                pltpu.VMEM((2, TME * ROW_CHUNKS, LANES), _f32),
                pltpu.SemaphoreType.DMA((2,)),
                pltpu.VMEM((D_MODEL, D_EXPERT), _bf16),
                pltpu.VMEM((D_MODEL, D_EXPERT), _bf16),
                pltpu.VMEM((D_EXPERT, D_MODEL), _bf16),
            ],
        ),
        out_shape=jax.ShapeDtypeStruct((n_tiles * TME * ROW_CHUNKS, LANES), _f32),
        compiler_params=pltpu.CompilerParams(dimension_semantics=("arbitrary",),
                                             vmem_limit_bytes=VMEM_LIMIT),
        name="experts",
    )(t_exp, t_rows, src_tok, hn2, w_gate[0], w_up[0], w_down[0])

    out = pl.pallas_call(
        _combine_kernel,
        grid_spec=pltpu.PrefetchScalarGridSpec(
            num_scalar_prefetch=1,
            grid=(n_tok // TMC,),
            in_specs=[
                pl.BlockSpec(memory_space=pl.ANY),
                pl.BlockSpec((TMC, D_MODEL), lambda i, *_: (i, 0)),
                pl.BlockSpec((TMC, ROUTE_W), lambda i, *_: (i, 0)),
            ],
            out_specs=pl.BlockSpec((TMC, D_MODEL), lambda i, *_: (i, 0)),
            scratch_shapes=[
                pltpu.VMEM((2, 2, TMC * ROW_CHUNKS, LANES), _f32),
                pltpu.SemaphoreType.DMA((2,)),
            ],
        ),
        out_shape=jax.ShapeDtypeStruct((n_tok, D_MODEL), _f32),
        compiler_params=pltpu.CompilerParams(dimension_semantics=("arbitrary",),
                                             vmem_limit_bytes=VMEM_LIMIT),
        name="combine",
    )(pos, y_sorted, h2.reshape(n_tok, D_MODEL), route)
    return out.reshape(bsz, seq, D_MODEL)
```

```python
import functools

import numpy as np
import jax
import jax.numpy as jnp
from jax import lax
from jax.experimental import pallas as pl
from jax.experimental.pallas import tpu as pltpu

D_MODEL = 1024
N_META = 16
N_Q_HEADS = 8
N_KV_HEADS = 2
HEAD_DIM = 64
Q_GROUP = N_Q_HEADS // N_KV_HEADS
ATTN_DIM = N_Q_HEADS * HEAD_DIM
KV_DIM = N_KV_HEADS * HEAD_DIM
BLK = 128
PAD = BLK - N_META
N_BUCKETS = 32
MAX_DISTANCE = 128
CONV_CH = D_MODEL // 2
IN_PROJ = ATTN_DIM + 2 * KV_DIM + 3 * CONV_CH
N_GROUPS = 4
EXPERTS_PER_GROUP = 8
N_EXPERTS = N_GROUPS * EXPERTS_PER_GROUP
D_EXPERT = D_MODEL // 2
EPS = 1e-6
NEG_INF = -1e30

LANES = 128
SUBLANES = 8
ROW_CHUNKS = D_MODEL // LANES
TM = 512
TME = 256
TMC = 256
INV_BLK = 8192
SCALAR_UNROLL = 16
DMA_UNROLL = 8
ROUTE_W = 8
VMEM_LIMIT = 56 * 1024 * 1024

Q_OFF, K_OFF, V_OFF = 0, ATTN_DIM, ATTN_DIM + KV_DIM
CB_OFF = ATTN_DIM + 2 * KV_DIM
CC_OFF = CB_OFF + CONV_CH
CH_OFF = CC_OFF + CONV_CH

_f32 = jnp.float32
_bf16 = jnp.bfloat16


def _rms(x, g):
    return x * lax.rsqrt(jnp.mean(x * x, axis=-1, keepdims=True) + EPS) * g


def _dot(a, b):
    return jnp.dot(a, b, preferred_element_type=_f32)


def _dup_halves(x):
    lane = lax.broadcasted_iota(jnp.int32, x.shape, 1)
    sw = pltpu.roll(x, HEAD_DIM, axis=1)
    lo = lane < HEAD_DIM
    return jnp.where(lo, x, sw).astype(_bf16), jnp.where(lo, sw, x).astype(_bf16)


def _kv_state(hn_bf, win_ref, kg_ref, bdk_ref):
    kv = _dot(hn_bf, win_ref[:, K_OFF:K_OFF + 2 * KV_DIM])
    k = kv[:, :KV_DIM]
    v = kv[:, KV_DIM:]
    ssk = _dot((k * k).astype(_bf16), bdk_ref[...])
    kn = k * lax.rsqrt(ssk * (1.0 / HEAD_DIM) + EPS) * kg_ref[...]
    return _dup_halves(kn) + _dup_halves(v)


def _prep_kernel(rb_ref, meta_ref, mixg_ref, win_ref, kg_ref, bdk_ref, bucket_ref,
                 k0_ref, k1_ref, v0_ref, v1_ref, ut_ref, bias_ref):
    hn = _rms(meta_ref[...], mixg_ref[...]).astype(_bf16)
    k0, k1, v0, v1 = _kv_state(hn, win_ref, kg_ref, bdk_ref)
    k0_ref[...] = k0
    k1_ref[...] = k1
    v0_ref[...] = v0
    v1_ref[...] = v1
    cch = _dot(hn, win_ref[:, CC_OFF:CC_OFF + 2 * CONV_CH])
    u = cch[:, :CONV_CH] * cch[:, CONV_CH:]
    ut_ref[...] = u[BLK - SUBLANES:, :]
    for f in range(2):
        bk = bucket_ref[f]
        for h in range(N_Q_HEADS):
            acc = jnp.full((BLK, 2 * BLK), NEG_INF, _f32)
            for b in range(N_BUCKETS):
                acc = jnp.where(bk == b, rb_ref[b, h], acc)
            bias_ref[f, h // Q_GROUP, (h % Q_GROUP) * BLK:(h % Q_GROUP + 1) * BLK, :] = acc


def _mixer_kernel(sinks_ref,
                  x_ref, mixg_ref, win_ref, qg_ref, kg_ref, bias_ref, convw_ref, ag_ref, cg_ref,
                  wout_ref, fg_ref, wr_ref, br_ref, k0m_ref, k1m_ref, v0m_ref, v1m_ref, utm_ref,
                  bdq_ref, bdk_ref, tri_ref,
                  h2_ref, hn2_ref, route_ref, cnt_ref,
                  kp0, kp1, vp0, vp1, ubuf, a_scr, cnt_acc):
    b = pl.program_id(0)
    t = pl.program_id(1)

    @pl.when(t == 0)
    def _():
        kp0[...] = k0m_ref[...]
        kp1[...] = k1m_ref[...]
        vp0[...] = v0m_ref[...]
        vp1[...] = v1m_ref[...]
        ubuf[0:SUBLANES, :] = utm_ref[...]

    @pl.when(jnp.logical_and(b == 0, t == 0))
    def _():
        cnt_acc[...] = jnp.zeros_like(cnt_acc)

    x = x_ref[...]
    hn = _rms(x, mixg_ref[...]).astype(_bf16)

    q = _dot(hn, win_ref[:, Q_OFF:Q_OFF + ATTN_DIM])
    ssq = _dot((q * q).astype(_bf16), bdq_ref[...])
    qn = (q * lax.rsqrt(ssq * (1.0 / HEAD_DIM) + EPS) * qg_ref[...]).astype(_bf16)
    kd0, kd1, vd0, vd1 = _kv_state(hn, win_ref, kg_ref, bdk_ref)
    kd = (kd0, kd1)
    vd = (vd0, vd1)
    kp = (kp0, kp1)
    vp = (vp0, vp1)

    lane_q = lax.broadcasted_iota(jnp.int32, (BLK, LANES), 1)
    lo_half = lane_q < HEAD_DIM
    row4 = lax.broadcasted_iota(jnp.int32, (Q_GROUP * BLK, 1), 0) // BLK
    first = jnp.where(t == 0, 0, 1)
    zero_bf = jnp.zeros((BLK, LANES), _bf16)

    for j in range(TM // BLK):
        rows = slice(j * BLK, (j + 1) * BLK)
        for g in range(N_KV_HEADS):
            if j == 0:
                kcat = jnp.concatenate([kp[g][...], kd[g][rows]], axis=0)
                vcat = jnp.concatenate([vp[g][...], vd[g][rows]], axis=0)
                bias = bias_ref[first, g]
            else:
                kcat = kd[g][(j - 1) * BLK:(j + 1) * BLK]
                vcat = vd[g][(j - 1) * BLK:(j + 1) * BLK]
                bias = bias_ref[1, g]
            qs = []
            for hh in range(Q_GROUP):
                h = g * Q_GROUP + hh
                qc = qn[rows, (h // 2) * LANES:(h // 2 + 1) * LANES]
                keep = lo_half if h % 2 == 0 else jnp.logical_not(lo_half)
                qs.append(jnp.where(keep, qc, zero_bf))
            q4 = jnp.concatenate(qs, axis=0)
            s = lax.dot_general(q4, kcat, (((1,), (1,)), ((), ())),
                                preferred_element_type=_f32) + bias
            sink = jnp.full((Q_GROUP * BLK, 1), sinks_ref[g * Q_GROUP], _f32)
            for hh in range(1, Q_GROUP):
                sink = jnp.where(row4 == hh, sinks_ref[g * Q_GROUP + hh], sink)
            m = jnp.maximum(jnp.max(s, axis=-1, keepdims=True), sink)
            p = jnp.exp(s - m)
            l = jnp.sum(p, axis=-1, keepdims=True) + jnp.exp(sink - m)
            o = _dot(p.astype(_bf16), vcat) / l
            for pp in range(Q_GROUP // 2):
                ev = o[(2 * pp) * BLK:(2 * pp + 1) * BLK]
                od = o[(2 * pp + 1) * BLK:(2 * pp + 2) * BLK]
                col = g * (Q_GROUP // 2) + pp
                a_scr[rows, col * LANES:(col + 1) * LANES] = jnp.where(lo_half, ev, od)

    last = slice(TM - BLK, TM)
    kp0[...] = kd0[last]
    kp1[...] = kd1[last]
    vp0[...] = vd0[last]
    vp1[...] = vd1[last]

    cb = _dot(hn, win_ref[:, CB_OFF:CB_OFF + CONV_CH])
    cch = _dot(hn, win_ref[:, CC_OFF:CC_OFF + 2 * CONV_CH])
    u = cch[:, :CONV_CH] * cch[:, CONV_CH:]
    ubuf[SUBLANES:, :] = u
    u1 = ubuf[SUBLANES - 1:SUBLANES - 1 + TM, :]
    u2 = ubuf[SUBLANES - 2:SUBLANES - 2 + TM, :]
    c = cb * (convw_ref[0:1, :] * u2 + convw_ref[1:2, :] * u1 + convw_ref[2:3, :] * u)
    ubuf[0:SUBLANES, :] = u[TM - SUBLANES:, :]

    an = _rms(a_scr[...], ag_ref[...]).astype(_bf16)
    cn = _rms(c, cg_ref[...]).astype(_bf16)
    h2 = x + _dot(an, wout_ref[0:ATTN_DIM, :]) + _dot(cn, wout_ref[ATTN_DIM:, :])
    h2_ref[...] = h2

    hn2 = _rms(h2, fg_ref[...])
    for cc in range(ROW_CHUNKS):
        hn2_ref[pl.ds(cc, TM, stride=ROW_CHUNKS), :] = hn2[:, cc * LANES:(cc + 1) * LANES]

    hi = hn2.astype(_bf16)
    lo = (hn2 - hi.astype(_f32)).astype(_bf16)
    r1 = _dot(hi, wr_ref[...])
    r2 = _dot(lo, wr_ref[...])
    lg = r1 + pltpu.roll(r1, HEAD_DIM, axis=1) + r2 + br_ref[...]
    lane = lax.broadcasted_iota(jnp.int32, (TM, LANES), 1)
    lanef = lane.astype(_f32)
    ninf = jnp.float32(-jnp.inf)
    big = jnp.float32(LANES)

    gl = jnp.where(lane < N_GROUPS, lg, ninf)
    gmax = jnp.max(gl, axis=-1, keepdims=True)
    gsum = jnp.sum(jnp.exp(gl - gmax), axis=-1, keepdims=True)
    g_p = 1.0 / gsum
    g_idx = jnp.min(jnp.where(gl == gmax, lanef, big), axis=-1, keepdims=True)
    e_lo = N_GROUPS + EXPERTS_PER_GROUP * g_idx
    el = jnp.where(jnp.logical_and(lanef >= e_lo, lanef < e_lo + EXPERTS_PER_GROUP), lg, ninf)
    m1 = jnp.max(el, axis=-1, keepdims=True)
    i1 = jnp.min(jnp.where(el == m1, lanef, big), axis=-1, keepdims=True)
    el2 = jnp.where(lanef == i1, ninf, el)
    m2 = jnp.max(el2, axis=-1, keepdims=True)
    i2 = jnp.min(jnp.where(el2 == m2, lanef, big), axis=-1, keepdims=True)
    ex = jnp.exp(m2 - m1)
    den = 1.0 / (1.0 + ex)
    gate0 = g_p * den
    gate1 = g_p * ex * den
    e0 = i1 - N_GROUPS
    e1 = i2 - N_GROUPS

    oh0 = lanef == e0
    oh1 = lanef == e1
    cmat = (jnp.where(oh0, 1.0, 0.0) + jnp.where(oh1, 1.0, 0.0))
    prefix = _dot(tri_ref[...], cmat.astype(_bf16)) + cnt_acc[...]
    rank0 = jnp.sum(jnp.where(oh0, prefix, 0.0), axis=-1, keepdims=True)
    rank1 = jnp.sum(jnp.where(oh1, prefix, 0.0), axis=-1, keepdims=True)
    cnt_new = cnt_acc[...] + jnp.sum(cmat, axis=0, keepdims=True)
    cnt_acc[...] = cnt_new
    cnt_ref[...] = cnt_new

    lane8 = lax.broadcasted_iota(jnp.int32, (TM, ROUTE_W), 1)
    rec = jnp.zeros((TM, ROUTE_W), _f32)
    for idx, val in enumerate((e0, e1, gate0, gate1, rank0, rank1)):
        rec = jnp.where(lane8 == idx, val, rec)
    route_ref[...] = rec


def _row_tile(ref, row):
    return ref.at[pl.ds(pl.multiple_of(row * ROW_CHUNKS, ROW_CHUNKS), ROW_CHUNKS), :]


def _invert_kernel(pos_ref, zeros_hbm, src_ref, sem):
    g = pl.program_id(0)

    @pl.when(g == 0)
    def _():
        fill = pltpu.make_async_copy(zeros_hbm, src_ref, sem.at[0])
        fill.start()
        fill.wait()

    def body(it, carry):
        dst = [pos_ref[it * SCALAR_UNROLL + j] for j in range(SCALAR_UNROLL)]
        tok0 = g * (INV_BLK // 2) + it * (SCALAR_UNROLL // 2)
        for j in range(SCALAR_UNROLL):
            src_ref[dst[j]] = tok0 + j // 2
        return carry

    lax.fori_loop(0, INV_BLK // SCALAR_UNROLL, body, 0)


def _experts_kernel(te_ref, tn_ref, src_ref,
                    hn2_hbm, wg_ref, wu_ref, wd_ref,
                    y_ref,
                    xbuf0, xbuf1, sem, wgb, wub, wdb):
    i = pl.program_id(0)
    n = pl.num_programs(0)
    bufs = (xbuf0, xbuf1)

    def row_copy(tile, sl, r):
        return pltpu.make_async_copy(_row_tile(hn2_hbm, src_ref[tile * TME + r]),
                                     _row_tile(bufs[sl], r), sem.at[sl])

    def issue_loop(tile, sl):
        def body(it, carry):
            for j in range(DMA_UNROLL):
                row_copy(tile, sl, it * DMA_UNROLL + j).start()
            return carry

        lax.fori_loop(0, TME // DMA_UNROLL, body, 0)

    def issue_inline(tile, sl):
        for r in range(TME):
            row_copy(tile, sl, r).start()

    def compute(sl):
        xs = [bufs[sl][pl.ds(cc, TME, stride=ROW_CHUNKS), :] for cc in range(ROW_CHUNKS)]
        xb = jnp.concatenate(xs, axis=1).astype(_bf16)
        gate = _dot(xb, wgb[...])
        up = _dot(xb, wub[...])
        act = (gate * jax.nn.sigmoid(gate) * up).astype(_bf16)
        y = _dot(act, wdb[...])
        for cc in range(ROW_CHUNKS):
            y_ref[pl.ds(cc, TME, stride=ROW_CHUNKS), :] = y[:, cc * LANES:(cc + 1) * LANES]

    @pl.when(i == 0)
    def _():
        issue_loop(0, 0)

    valid = tn_ref[i] > 0
    more = i + 1 < n
    changed = jnp.logical_or(i == 0, te_ref[i] != te_ref[jnp.maximum(i - 1, 0)])

    @pl.when(jnp.logical_and(changed, valid))
    def _():
        wgb[...] = wg_ref[...].astype(_bf16)
        wub[...] = wu_ref[...].astype(_bf16)
        wdb[...] = wd_ref[...].astype(_bf16)

    for sl in range(2):
        mine = (i % 2) == sl

        @pl.when(mine)
        def _():
            pltpu.make_async_copy(hn2_hbm.at[pl.ds(0, TME * ROW_CHUNKS), :], bufs[sl], sem.at[sl]).wait()

        @pl.when(jnp.logical_and(mine, jnp.logical_and(valid, more)))
        def _():
            issue_inline(i + 1, 1 - sl)
            compute(sl)

        @pl.when(jnp.logical_and(mine, jnp.logical_and(valid, jnp.logical_not(more))))
        def _():
            compute(sl)

        @pl.when(jnp.logical_and(mine, jnp.logical_and(jnp.logical_not(valid), more)))
        def _():
            issue_loop(i + 1, 1 - sl)

    @pl.when(jnp.logical_not(valid))
    def _():
        y_ref[...] = jnp.zeros_like(y_ref)


def _combine_kernel(pos_ref,
                    y_hbm, h2_ref, route_ref,
                    out_ref,
                    ybuf, sem):
    i = pl.program_id(0)
    n = pl.num_programs(0)
    slot = i % 2

    def issue(tile, sl):
        base = tile * (2 * TMC)

        def body(it, carry):
            for j in range(DMA_UNROLL):
                r = it * DMA_UNROLL + j
                for k in range(2):
                    pltpu.make_async_copy(_row_tile(y_hbm, pos_ref[base + 2 * r + k]),
                                          _row_tile(ybuf.at[sl, k], r), sem.at[sl]).start()
            return carry

        lax.fori_loop(0, TMC // DMA_UNROLL, body, 0)

    @pl.when(i == 0)
    def _():
        issue(0, 0)

    @pl.when(i + 1 < n)
    def _():
        issue(i + 1, 1 - slot)

    for k in range(2):
        pltpu.make_async_copy(y_hbm.at[pl.ds(0, TMC * ROW_CHUNKS), :], ybuf.at[slot, k], sem.at[slot]).wait()
    g0 = route_ref[:, 2:3]
    g1 = route_ref[:, 3:4]
    for cc in range(ROW_CHUNKS):
        y0 = ybuf[slot, 0, pl.ds(cc, TMC, stride=ROW_CHUNKS), :]
        y1 = ybuf[slot, 1, pl.ds(cc, TMC, stride=ROW_CHUNKS), :]
        cols = slice(cc * LANES, (cc + 1) * LANES)
        out_ref[:, cols] = h2_ref[:, cols] + (g0 * y0 + g1 * y1)


def _t5_bucket(n):
    max_exact = N_BUCKETS // 2
    nf = jnp.maximum(n, 1).astype(_f32)
    large = max_exact + (jnp.log(nf / max_exact) / np.log(MAX_DISTANCE / max_exact)
                         * (N_BUCKETS - max_exact)).astype(jnp.int32)
    large = jnp.minimum(large, N_BUCKETS - 1)
    return jnp.where(n < max_exact, n, large)


def _bucket_maps():
    qi = jnp.arange(BLK)[:, None]
    sj = jnp.arange(2 * BLK)[None, :]
    dist = BLK + qi - sj
    band = (dist >= 0) & (dist < BLK)
    bucket = _t5_bucket(jnp.maximum(dist, 0))
    generic = jnp.where(band, bucket, -1)
    first = jnp.where(band & (sj >= PAD), bucket, -1)
    return jnp.stack([first, generic]).astype(jnp.int32)


def _const(shape):
    nd = len(shape)
    return pl.BlockSpec(shape, lambda *_: (0,) * nd)


def _block_diag_ones(n):
    idx = np.arange(n) // HEAD_DIM
    return jnp.asarray((idx[:, None] == idx[None, :]).astype(np.float32), dtype=_bf16)


def kernel(x, meta_tokens, rel_bias, mix_norm_g, w_in, q_norm_g, k_norm_g, attn_sinks, conv_w, attn_out_norm_g, conv_out_norm_g, w_out, ffn_norm_g, w_group_router, b_group_router, w_expert_router, b_expert_router, w_gate, w_up, w_down):
    bsz, seq, _ = x.shape
    assert seq % TM == 0 and (bsz * seq) % TMC == 0
    n_tok = bsz * seq
    nt = seq // TM

    win = w_in[0].astype(_bf16)
    wout = w_out[0].astype(_bf16)
    mixg = mix_norm_g[0].reshape(1, D_MODEL)
    fg = ffn_norm_g[0].reshape(1, D_MODEL)
    qg = (jnp.tile(q_norm_g[0], N_Q_HEADS) * (HEAD_DIM ** -0.5)).reshape(1, ATTN_DIM)
    kg = jnp.tile(k_norm_g[0], N_KV_HEADS).reshape(1, KV_DIM)
    ag = attn_out_norm_g[0].reshape(1, ATTN_DIM)
    cg = conv_out_norm_g[0].reshape(1, CONV_CH)
    convw = conv_w[0]
    sinks = attn_sinks[0]
    w_r = jnp.concatenate([w_group_router[0], w_expert_router[0].reshape(D_MODEL, N_EXPERTS)], axis=1)
    n_r = N_GROUPS + N_EXPERTS
    w_r_hi = w_r.astype(_bf16)
    w_r_lo = (w_r - w_r_hi.astype(_f32)).astype(_bf16)
    wr = jnp.zeros((D_MODEL, LANES), _bf16)
    wr = wr.at[:, :n_r].set(w_r_hi).at[:, HEAD_DIM:HEAD_DIM + n_r].set(w_r_lo)
    br = jnp.zeros((1, LANES), _f32).at[0, :n_r].set(
        jnp.concatenate([b_group_router[0], b_expert_router[0].reshape(N_EXPERTS)]))
    meta_blk = jnp.concatenate([jnp.zeros((PAD, D_MODEL), x.dtype), meta_tokens.astype(x.dtype)], axis=0)
    bdq = _block_diag_ones(ATTN_DIM)
    bdk = _block_diag_ones(KV_DIM)
    tri = jnp.asarray(np.tril(np.ones((TM, TM), np.float32), -1), dtype=_bf16)

    kv_sd = jax.ShapeDtypeStruct((BLK, KV_DIM), _bf16)
    k0m, k1m, v0m, v1m, utm, bias = pl.pallas_call(
        _prep_kernel,
        out_shape=(kv_sd, kv_sd, kv_sd, kv_sd,
                   jax.ShapeDtypeStruct((SUBLANES, CONV_CH), _f32),
                   jax.ShapeDtypeStruct((2, N_KV_HEADS, Q_GROUP * BLK, 2 * BLK), _f32)),
        in_specs=[pl.BlockSpec(memory_space=pltpu.SMEM)] + [pl.BlockSpec(memory_space=pltpu.VMEM)] * 6,
        out_specs=tuple(pl.BlockSpec(memory_space=pltpu.VMEM) for _ in range(6)),
        compiler_params=pltpu.CompilerParams(vmem_limit_bytes=VMEM_LIMIT),
        name="prep",
    )(rel_bias, meta_blk, mixg, win, kg, bdk, _bucket_maps())

    tile_idx = lambda b, t, *_: (b * nt + t, 0)
    grid_spec = pltpu.PrefetchScalarGridSpec(
        num_scalar_prefetch=1,
        grid=(bsz, nt),
        in_specs=[
            pl.BlockSpec((None, TM, D_MODEL), lambda b, t, *_: (b, t, 0)),
            _const((1, D_MODEL)), _const((D_MODEL, IN_PROJ)), _const((1, ATTN_DIM)), _const((1, KV_DIM)),
            _const((2, N_KV_HEADS, Q_GROUP * BLK, 2 * BLK)), _const((3, CONV_CH)),
            _const((1, ATTN_DIM)), _const((1, CONV_CH)), _const((D_MODEL, D_MODEL)), _const((1, D_MODEL)),
            _const((D_MODEL, LANES)), _const((1, LANES)),
            _const((BLK, KV_DIM)), _const((BLK, KV_DIM)), _const((BLK, KV_DIM)), _const((BLK, KV_DIM)),
            _const((SUBLANES, CONV_CH)),
            _const((ATTN_DIM, ATTN_DIM)), _const((KV_DIM, KV_DIM)), _const((TM, TM)),
        ],
        out_specs=[
            pl.BlockSpec((None, TM, D_MODEL), lambda b, t, *_: (b, t, 0)),
            pl.BlockSpec((TM * ROW_CHUNKS, LANES), tile_idx),
            pl.BlockSpec((TM, ROUTE_W), tile_idx),
            _const((1, LANES)),
        ],
        scratch_shapes=[
            pltpu.VMEM((BLK, KV_DIM), _bf16), pltpu.VMEM((BLK, KV_DIM), _bf16),
            pltpu.VMEM((BLK, KV_DIM), _bf16), pltpu.VMEM((BLK, KV_DIM), _bf16),
            pltpu.VMEM((TM + SUBLANES, CONV_CH), _f32),
            pltpu.VMEM((TM, ATTN_DIM), _f32),
            pltpu.VMEM((1, LANES), _f32),
        ],
    )
    h2, hn2, route, cnt = pl.pallas_call(
        _mixer_kernel,
        grid_spec=grid_spec,
        out_shape=(jax.ShapeDtypeStruct((bsz, seq, D_MODEL), _f32),
                   jax.ShapeDtypeStruct((n_tok * ROW_CHUNKS, LANES), _f32),
                   jax.ShapeDtypeStruct((n_tok, ROUTE_W), _f32),
                   jax.ShapeDtypeStruct((1, LANES), _f32)),
        compiler_params=pltpu.CompilerParams(dimension_semantics=("arbitrary", "arbitrary"),
                                             vmem_limit_bytes=VMEM_LIMIT),
        name="mixer",
    )(sinks, x, mixg, win, qg, kg, bias, convw, ag, cg, wout, fg, wr, br,
      k0m, k1m, v0m, v1m, utm, bdq, bdk, tri)

    n_tiles = (n_tok * 2) // TME + N_EXPERTS
    counts = cnt[0, :N_EXPERTS].astype(jnp.int32)
    ntile = (counts + TME - 1) // TME
    tile_end = jnp.cumsum(ntile)
    tile_start = tile_end - ntile
    eid = route[:, 0:2].astype(jnp.int32)
    rank = route[:, 4:6].astype(jnp.int32)
    experts = jnp.arange(N_EXPERTS, dtype=jnp.int32)
    start_of = jnp.sum(jnp.where(eid[:, :, None] == experts, tile_start, 0), axis=-1)
    pos = (start_of * TME + rank).reshape(-1)
    tiles = jnp.arange(n_tiles, dtype=jnp.int32)
    n_used = tile_end[-1]
    t_exp = jnp.sum((jnp.minimum(tiles, n_used - 1)[:, None] >= tile_end[None, :]).astype(jnp.int32), axis=-1)
    t_exp = jnp.minimum(t_exp, N_EXPERTS - 1)
    own = t_exp[:, None] == experts
    t_rows = jnp.sum(jnp.where(own, counts - (tiles[:, None] - tile_start) * TME, 0), axis=-1)
    t_rows = jnp.where(tiles < n_used, jnp.clip(t_rows, 0, TME), 0).astype(jnp.int32)

    src_tok = pl.pallas_call(
        _invert_kernel,
        grid=((2 * n_tok) // INV_BLK,),
        in_specs=[pl.BlockSpec((INV_BLK,), lambda g: (g,), memory_space=pltpu.SMEM),
                  pl.BlockSpec(memory_space=pl.ANY)],
        out_specs=pl.BlockSpec(memory_space=pltpu.SMEM),
        out_shape=jax.ShapeDtypeStruct((n_tiles * TME,), jnp.int32),
        scratch_shapes=[pltpu.SemaphoreType.DMA((1,))],
        compiler_params=pltpu.CompilerParams(dimension_semantics=("arbitrary",)),
        name="invert",
    )(pos, jnp.zeros((n_tiles * TME,), jnp.int32))

    y_sorted = pl.pallas_call(
        _experts_kernel,
        grid_spec=pltpu.PrefetchScalarGridSpec(
            num_scalar_prefetch=3,
            grid=(n_tiles,),
            in_specs=[
                pl.BlockSpec(memory_space=pl.ANY),
                pl.BlockSpec((None, D_MODEL, D_EXPERT), lambda i, te, tn, src: (te[i], 0, 0)),
                pl.BlockSpec((None, D_MODEL, D_EXPERT), lambda i, te, tn, src: (te[i], 0, 0)),
                pl.BlockSpec((None, D_EXPERT, D_MODEL), lambda i, te, tn, src: (te[i], 0, 0)),
            ],
            out_specs=pl.BlockSpec((TME * ROW_CHUNKS, LANES), lambda i, *_: (i, 0)),
            scratch_shapes=[
                pltpu.VMEM((TME * ROW_CHUNKS, LANES), _f32),
                pltpu.VMEM((TME * ROW_CHUNKS, LANES), _f32),
                pltpu.SemaphoreType.DMA((2,)),
                pltpu.VMEM((D_MODEL, D_EXPERT), _bf16),
                pltpu.VMEM((D_MODEL, D_EXPERT), _bf16),
                pltpu.VMEM((D_EXPERT, D_MODEL), _bf16),
            ],
        ),
        out_shape=jax.ShapeDtypeStruct((n_tiles * TME * ROW_CHUNKS, LANES), _f32),
        compiler_params=pltpu.CompilerParams(dimension_semantics=("arbitrary",),
                                             vmem_limit_bytes=VMEM_LIMIT),
        name="experts",
    )(t_exp, t_rows, src_tok, hn2, w_gate[0], w_up[0], w_down[0])

    out = pl.pallas_call(
        _combine_kernel,
        grid_spec=pltpu.PrefetchScalarGridSpec(
            num_scalar_prefetch=1,
            grid=(n_tok // TMC,),
            in_specs=[
                pl.BlockSpec(memory_space=pl.ANY),
                pl.BlockSpec((TMC, D_MODEL), lambda i, *_: (i, 0)),
                pl.BlockSpec((TMC, ROUTE_W), lambda i, *_: (i, 0)),
            ],
            out_specs=pl.BlockSpec((TMC, D_MODEL), lambda i, *_: (i, 0)),
            scratch_shapes=[
                pltpu.VMEM((2, 2, TMC * ROW_CHUNKS, LANES), _f32),
                pltpu.SemaphoreType.DMA((2,)),
            ],
        ),
        out_shape=jax.ShapeDtypeStruct((n_tok, D_MODEL), _f32),
        compiler_params=pltpu.CompilerParams(dimension_semantics=("arbitrary",),
                                             vmem_limit_bytes=VMEM_LIMIT),
        name="combine",
    )(pos, y_sorted, h2.reshape(n_tok, D_MODEL), route)
    return out.reshape(bsz, seq, D_MODEL)
```

```python
import functools

import numpy as np
import jax
import jax.numpy as jnp
from jax import lax
from jax.experimental import pallas as pl
from jax.experimental.pallas import tpu as pltpu

D_MODEL = 1024
N_META = 16
N_Q_HEADS = 8
N_KV_HEADS = 2
HEAD_DIM = 64
Q_GROUP = N_Q_HEADS // N_KV_HEADS
ATTN_DIM = N_Q_HEADS * HEAD_DIM
KV_DIM = N_KV_HEADS * HEAD_DIM
BLK = 128
PAD = BLK - N_META
N_BUCKETS = 32
MAX_DISTANCE = 128
CONV_CH = D_MODEL // 2
IN_PROJ = ATTN_DIM + 2 * KV_DIM + 3 * CONV_CH
N_GROUPS = 4
EXPERTS_PER_GROUP = 8
N_EXPERTS = N_GROUPS * EXPERTS_PER_GROUP
D_EXPERT = D_MODEL // 2
EPS = 1e-6
NEG_INF = -1e30

LANES = 128
SUBLANES = 8
ROW_CHUNKS = D_MODEL // LANES
TM = 512
TME = 256
TMC = 256
INV_BLK = 8192
SCALAR_UNROLL = 16
DMA_UNROLL = 8
ROUTE_W = 8
VMEM_LIMIT = 56 * 1024 * 1024

Q_OFF, K_OFF, V_OFF = 0, ATTN_DIM, ATTN_DIM + KV_DIM
CB_OFF = ATTN_DIM + 2 * KV_DIM
CC_OFF = CB_OFF + CONV_CH
CH_OFF = CC_OFF + CONV_CH

_f32 = jnp.float32
_bf16 = jnp.bfloat16


def _rms(x, g):
    return x * lax.rsqrt(jnp.mean(x * x, axis=-1, keepdims=True) + EPS) * g


def _dot(a, b):
    return jnp.dot(a, b, preferred_element_type=_f32)


def _dup_halves(x):
    lane = lax.broadcasted_iota(jnp.int32, x.shape, 1)
    sw = pltpu.roll(x, HEAD_DIM, axis=1)
    lo = lane < HEAD_DIM
    return jnp.where(lo, x, sw).astype(_bf16), jnp.where(lo, sw, x).astype(_bf16)


def _kv_state(hn_bf, win_ref, kg_ref, bdk_ref):
    kv = _dot(hn_bf, win_ref[:, K_OFF:K_OFF + 2 * KV_DIM])
    k = kv[:, :KV_DIM]
    v = kv[:, KV_DIM:]
    ssk = _dot((k * k).astype(_bf16), bdk_ref[...])
    kn = k * lax.rsqrt(ssk * (1.0 / HEAD_DIM) + EPS) * kg_ref[...]
    return _dup_halves(kn) + _dup_halves(v)


def _prep_kernel(rb_ref, meta_ref, mixg_ref, win_ref, kg_ref, bdk_ref, bucket_ref,
                 k0_ref, k1_ref, v0_ref, v1_ref, ut_ref, bias_ref):
    hn = _rms(meta_ref[...], mixg_ref[...]).astype(_bf16)
    k0, k1, v0, v1 = _kv_state(hn, win_ref, kg_ref, bdk_ref)
    k0_ref[...] = k0
    k1_ref[...] = k1
    v0_ref[...] = v0
    v1_ref[...] = v1
    cch = _dot(hn, win_ref[:, CC_OFF:CC_OFF + 2 * CONV_CH])
    u = cch[:, :CONV_CH] * cch[:, CONV_CH:]
    ut_ref[...] = u[BLK - SUBLANES:, :]
    for f in range(2):
        bk = bucket_ref[f]
        for h in range(N_Q_HEADS):
            acc = jnp.full((BLK, 2 * BLK), NEG_INF, _f32)
            for b in range(N_BUCKETS):
                acc = jnp.where(bk == b, rb_ref[b, h], acc)
            bias_ref[f, h // Q_GROUP, (h % Q_GROUP) * BLK:(h % Q_GROUP + 1) * BLK, :] = acc


def _mixer_kernel(sinks_ref,
                  x_ref, mixg_ref, win_ref, qg_ref, kg_ref, bias_ref, convw_ref, ag_ref, cg_ref,
                  wout_ref, fg_ref, wr_ref, br_ref, k0m_ref, k1m_ref, v0m_ref, v1m_ref, utm_ref,
                  bdq_ref, bdk_ref, tri_ref,
                  h2_ref, hn2_ref, route_ref, cnt_ref,
                  kp0, kp1, vp0, vp1, ubuf, a_scr, cnt_acc):
    b = pl.program_id(0)
    t = pl.program_id(1)

    @pl.when(t == 0)
    def _():
        kp0[...] = k0m_ref[...]
        kp1[...] = k1m_ref[...]
        vp0[...] = v0m_ref[...]
        vp1[...] = v1m_ref[...]
        ubuf[0:SUBLANES, :] = utm_ref[...]

    @pl.when(jnp.logical_and(b == 0, t == 0))
    def _():
        cnt_acc[...] = jnp.zeros_like(cnt_acc)

    x = x_ref[...]
    hn = _rms(x, mixg_ref[...]).astype(_bf16)

    q = _dot(hn, win_ref[:, Q_OFF:Q_OFF + ATTN_DIM])
    ssq = _dot((q * q).astype(_bf16), bdq_ref[...])
    qn = (q * lax.rsqrt(ssq * (1.0 / HEAD_DIM) + EPS) * qg_ref[...]).astype(_bf16)
    kd0, kd1, vd0, vd1 = _kv_state(hn, win_ref, kg_ref, bdk_ref)
    kd = (kd0, kd1)
    vd = (vd0, vd1)
    kp = (kp0, kp1)
    vp = (vp0, vp1)

    lane_q = lax.broadcasted_iota(jnp.int32, (BLK, LANES), 1)
    lo_half = lane_q < HEAD_DIM
    row4 = lax.broadcasted_iota(jnp.int32, (Q_GROUP * BLK, 1), 0) // BLK
    first = jnp.where(t == 0, 0, 1)
    zero_bf = jnp.zeros((BLK, LANES), _bf16)

    for j in range(TM // BLK):
        rows = slice(j * BLK, (j + 1) * BLK)
        for g in range(N_KV_HEADS):
            if j == 0:
                kcat = jnp.concatenate([kp[g][...], kd[g][rows]], axis=0)
                vcat = jnp.concatenate([vp[g][...], vd[g][rows]], axis=0)
                bias = bias_ref[first, g]
            else:
                kcat = kd[g][(j - 1) * BLK:(j + 1) * BLK]
                vcat = vd[g][(j - 1) * BLK:(j + 1) * BLK]
                bias = bias_ref[1, g]
            qs = []
            for hh in range(Q_GROUP):
                h = g * Q_GROUP + hh
                qc = qn[rows, (h // 2) * LANES:(h // 2 + 1) * LANES]
                keep = lo_half if h % 2 == 0 else jnp.logical_not(lo_half)
                qs.append(jnp.where(keep, qc, zero_bf))
            q4 = jnp.concatenate(qs, axis=0)
            s = lax.dot_general(q4, kcat, (((1,), (1,)), ((), ())),
                                preferred_element_type=_f32) + bias
            sink = jnp.full((Q_GROUP * BLK, 1), sinks_ref[g * Q_GROUP], _f32)
            for hh in range(1, Q_GROUP):
                sink = jnp.where(row4 == hh, sinks_ref[g * Q_GROUP + hh], sink)
            m = jnp.maximum(jnp.max(s, axis=-1, keepdims=True), sink)
            p = jnp.exp(s - m)
            l = jnp.sum(p, axis=-1, keepdims=True) + jnp.exp(sink - m)
            o = _dot(p.astype(_bf16), vcat) / l
            for pp in range(Q_GROUP // 2):
                ev = o[(2 * pp) * BLK:(2 * pp + 1) * BLK]
                od = o[(2 * pp + 1) * BLK:(2 * pp + 2) * BLK]
                col = g * (Q_GROUP // 2) + pp
                a_scr[rows, col * LANES:(col + 1) * LANES] = jnp.where(lo_half, ev, od)

    last = slice(TM - BLK, TM)
    kp0[...] = kd0[last]
    kp1[...] = kd1[last]
    vp0[...] = vd0[last]
    vp1[...] = vd1[last]

    cb = _dot(hn, win_ref[:, CB_OFF:CB_OFF + CONV_CH])
    cch = _dot(hn, win_ref[:, CC_OFF:CC_OFF + 2 * CONV_CH])
    u = cch[:, :CONV_CH] * cch[:, CONV_CH:]
    ubuf[SUBLANES:, :] = u
    u1 = ubuf[SUBLANES - 1:SUBLANES - 1 + TM, :]
    u2 = ubuf[SUBLANES - 2:SUBLANES - 2 + TM, :]
    c = cb * (convw_ref[0:1, :] * u2 + convw_ref[1:2, :] * u1 + convw_ref[2:3, :] * u)
    ubuf[0:SUBLANES, :] = u[TM - SUBLANES:, :]

    an = _rms(a_scr[...], ag_ref[...]).astype(_bf16)
    cn = _rms(c, cg_ref[...]).astype(_bf16)
    h2 = x + _dot(an, wout_ref[0:ATTN_DIM, :]) + _dot(cn, wout_ref[ATTN_DIM:, :])
    h2_ref[...] = h2

    hn2 = _rms(h2, fg_ref[...])
    for cc in range(ROW_CHUNKS):
        hn2_ref[pl.ds(cc, TM, stride=ROW_CHUNKS), :] = hn2[:, cc * LANES:(cc + 1) * LANES]

    hi = hn2.astype(_bf16)
    lo = (hn2 - hi.astype(_f32)).astype(_bf16)
    r1 = _dot(hi, wr_ref[...])
    r2 = _dot(lo, wr_ref[...])
    lg = r1 + pltpu.roll(r1, HEAD_DIM, axis=1) + r2 + br_ref[...]
    lane = lax.broadcasted_iota(jnp.int32, (TM, LANES), 1)
    lanef = lane.astype(_f32)
    ninf = jnp.float32(-jnp.inf)
    big = jnp.float32(LANES)

    gl = jnp.where(lane < N_GROUPS, lg, ninf)
    gmax = jnp.max(gl, axis=-1, keepdims=True)
    gsum = jnp.sum(jnp.exp(gl - gmax), axis=-1, keepdims=True)
    g_p = 1.0 / gsum
    g_idx = jnp.min(jnp.where(gl == gmax, lanef, big), axis=-1, keepdims=True)
    e_lo = N_GROUPS + EXPERTS_PER_GROUP * g_idx
    el = jnp.where(jnp.logical_and(lanef >= e_lo, lanef < e_lo + EXPERTS_PER_GROUP), lg, ninf)
    m1 = jnp.max(el, axis=-1, keepdims=True)
    i1 = jnp.min(jnp.where(el == m1, lanef, big), axis=-1, keepdims=True)
    el2 = jnp.where(lanef == i1, ninf, el)
    m2 = jnp.max(el2, axis=-1, keepdims=True)
    i2 = jnp.min(jnp.where(el2 == m2, lanef, big), axis=-1, keepdims=True)
    ex = jnp.exp(m2 - m1)
    den = 1.0 / (1.0 + ex)
    gate0 = g_p * den
    gate1 = g_p * ex * den
    e0 = i1 - N_GROUPS
    e1 = i2 - N_GROUPS

    oh0 = lanef == e0
    oh1 = lanef == e1
    cmat = (jnp.where(oh0, 1.0, 0.0) + jnp.where(oh1, 1.0, 0.0))
    prefix = _dot(tri_ref[...], cmat.astype(_bf16)) + cnt_acc[...]
    rank0 = jnp.sum(jnp.where(oh0, prefix, 0.0), axis=-1, keepdims=True)
    rank1 = jnp.sum(jnp.where(oh1, prefix, 0.0), axis=-1, keepdims=True)
    cnt_new = cnt_acc[...] + jnp.sum(cmat, axis=0, keepdims=True)
    cnt_acc[...] = cnt_new
    cnt_ref[...] = cnt_new

    lane8 = lax.broadcasted_iota(jnp.int32, (TM, ROUTE_W), 1)
    rec = jnp.zeros((TM, ROUTE_W), _f32)
    for idx, val in enumerate((e0, e1, gate0, gate1, rank0, rank1)):
        rec = jnp.where(lane8 == idx, val, rec)
    route_ref[...] = rec


def _row_tile(ref, row):
    return ref.at[pl.ds(pl.multiple_of(row * ROW_CHUNKS, ROW_CHUNKS), ROW_CHUNKS), :]


def _invert_kernel(pos_ref, zeros_hbm, src_ref, sem):
    g = pl.program_id(0)

    @pl.when(g == 0)
    def _():
        fill = pltpu.make_async_copy(zeros_hbm, src_ref, sem.at[0])
        fill.start()
        fill.wait()

    def body(it, carry):
        dst = [pos_ref[it * SCALAR_UNROLL + j] for j in range(SCALAR_UNROLL)]
        tok0 = g * (INV_BLK // 2) + it * (SCALAR_UNROLL // 2)
        for j in range(SCALAR_UNROLL):
            src_ref[dst[j]] = tok0 + j // 2
        return carry

    lax.fori_loop(0, INV_BLK // SCALAR_UNROLL, body, 0)


def _experts_kernel(te_ref, tn_ref, src_ref,
                    hn2_hbm, wg_ref, wu_ref, wd_ref,
                    y_ref,
                    xbuf0, xbuf1, sem, wgb, wub, wdb):
    i = pl.program_id(0)
    n = pl.num_programs(0)
    bufs = (xbuf0, xbuf1)

    def row_copy(tile, sl, r):
        return pltpu.make_async_copy(_row_tile(hn2_hbm, src_ref[tile * TME + r]),
                                     _row_tile(bufs[sl], r), sem.at[sl])

    def issue_loop(tile, sl):
        def body(it, carry):
            for j in range(DMA_UNROLL):
                row_copy(tile, sl, it * DMA_UNROLL + j).start(priority=1)
            return carry

        lax.fori_loop(0, TME // DMA_UNROLL, body, 0)

    def issue_inline(tile, sl):
        for r in range(TME):
            row_copy(tile, sl, r).start(priority=1)

    def compute(sl):
        xs = [bufs[sl][pl.ds(cc, TME, stride=ROW_CHUNKS), :] for cc in range(ROW_CHUNKS)]
        xb = jnp.concatenate(xs, axis=1).astype(_bf16)
        gate = _dot(xb, wgb[...])
        up = _dot(xb, wub[...])
        act = (gate * jax.nn.sigmoid(gate) * up).astype(_bf16)
        y = _dot(act, wdb[...])
        for cc in range(ROW_CHUNKS):
            y_ref[pl.ds(cc, TME, stride=ROW_CHUNKS), :] = y[:, cc * LANES:(cc + 1) * LANES]

    @pl.when(i == 0)
    def _():
        issue_loop(0, 0)

    valid = tn_ref[i] > 0
    more = i + 1 < n
    changed = jnp.logical_or(i == 0, te_ref[i] != te_ref[jnp.maximum(i - 1, 0)])

    @pl.when(jnp.logical_and(changed, valid))
    def _():
        wgb[...] = wg_ref[...].astype(_bf16)
        wub[...] = wu_ref[...].astype(_bf16)
        wdb[...] = wd_ref[...].astype(_bf16)

    for sl in range(2):
        mine = (i % 2) == sl

        @pl.when(mine)
        def _():
            pltpu.make_async_copy(hn2_hbm.at[pl.ds(0, TME * ROW_CHUNKS), :], bufs[sl], sem.at[sl]).wait()

        @pl.when(jnp.logical_and(mine, jnp.logical_and(valid, more)))
        def _():
            issue_inline(i + 1, 1 - sl)
            compute(sl)

        @pl.when(jnp.logical_and(mine, jnp.logical_and(valid, jnp.logical_not(more))))
        def _():
            compute(sl)

        @pl.when(jnp.logical_and(mine, jnp.logical_and(jnp.logical_not(valid), more)))
        def _():
            issue_loop(i + 1, 1 - sl)

    @pl.when(jnp.logical_not(valid))
    def _():
        y_ref[...] = jnp.zeros_like(y_ref)


def _combine_kernel(pos_ref,
                    y_hbm, h2_ref, route_ref,
                    out_ref,
                    ybuf, sem):
    i = pl.program_id(0)
    n = pl.num_programs(0)
    slot = i % 2

    def issue(tile, sl):
        base = tile * (2 * TMC)

        def body(it, carry):
            for j in range(DMA_UNROLL):
                r = it * DMA_UNROLL + j
                for k in range(2):
                    pltpu.make_async_copy(_row_tile(y_hbm, pos_ref[base + 2 * r + k]),
                                          _row_tile(ybuf.at[sl, k], r), sem.at[sl]).start(priority=k)
            return carry

        lax.fori_loop(0, TMC // DMA_UNROLL, body, 0)

    @pl.when(i == 0)
    def _():
        issue(0, 0)

    @pl.when(i + 1 < n)
    def _():
        issue(i + 1, 1 - slot)

    for k in range(2):
        pltpu.make_async_copy(y_hbm.at[pl.ds(0, TMC * ROW_CHUNKS), :], ybuf.at[slot, k], sem.at[slot]).wait()
    g0 = route_ref[:, 2:3]
    g1 = route_ref[:, 3:4]
    for cc in range(ROW_CHUNKS):
        y0 = ybuf[slot, 0, pl.ds(cc, TMC, stride=ROW_CHUNKS), :]
        y1 = ybuf[slot, 1, pl.ds(cc, TMC, stride=ROW_CHUNKS), :]
        cols = slice(cc * LANES, (cc + 1) * LANES)
        out_ref[:, cols] = h2_ref[:, cols] + (g0 * y0 + g1 * y1)


def _t5_bucket(n):
    max_exact = N_BUCKETS // 2
    nf = jnp.maximum(n, 1).astype(_f32)
    large = max_exact + (jnp.log(nf / max_exact) / np.log(MAX_DISTANCE / max_exact)
                         * (N_BUCKETS - max_exact)).astype(jnp.int32)
    large = jnp.minimum(large, N_BUCKETS - 1)
    return jnp.where(n < max_exact, n, large)


def _bucket_maps():
    qi = jnp.arange(BLK)[:, None]
    sj = jnp.arange(2 * BLK)[None, :]
    dist = BLK + qi - sj
    band = (dist >= 0) & (dist < BLK)
    bucket = _t5_bucket(jnp.maximum(dist, 0))
    generic = jnp.where(band, bucket, -1)
    first = jnp.where(band & (sj >= PAD), bucket, -1)
    return jnp.stack([first, generic]).astype(jnp.int32)


def _const(shape):
    nd = len(shape)
    return pl.BlockSpec(shape, lambda *_: (0,) * nd)


def _block_diag_ones(n):
    idx = np.arange(n) // HEAD_DIM
    return jnp.asarray((idx[:, None] == idx[None, :]).astype(np.float32), dtype=_bf16)


def kernel(x, meta_tokens, rel_bias, mix_norm_g, w_in, q_norm_g, k_norm_g, attn_sinks, conv_w, attn_out_norm_g, conv_out_norm_g, w_out, ffn_norm_g, w_group_router, b_group_router, w_expert_router, b_expert_router, w_gate, w_up, w_down):
    bsz, seq, _ = x.shape
    assert seq % TM == 0 and (bsz * seq) % TMC == 0
    n_tok = bsz * seq
    nt = seq // TM

    win = w_in[0].astype(_bf16)
    wout = w_out[0].astype(_bf16)
    mixg = mix_norm_g[0].reshape(1, D_MODEL)
    fg = ffn_norm_g[0].reshape(1, D_MODEL)
    qg = (jnp.tile(q_norm_g[0], N_Q_HEADS) * (HEAD_DIM ** -0.5)).reshape(1, ATTN_DIM)
    kg = jnp.tile(k_norm_g[0], N_KV_HEADS).reshape(1, KV_DIM)
    ag = attn_out_norm_g[0].reshape(1, ATTN_DIM)
    cg = conv_out_norm_g[0].reshape(1, CONV_CH)
    convw = conv_w[0]
    sinks = attn_sinks[0]
    w_r = jnp.concatenate([w_group_router[0], w_expert_router[0].reshape(D_MODEL, N_EXPERTS)], axis=1)
    n_r = N_GROUPS + N_EXPERTS
    w_r_hi = w_r.astype(_bf16)
    w_r_lo = (w_r - w_r_hi.astype(_f32)).astype(_bf16)
    wr = jnp.zeros((D_MODEL, LANES), _bf16)
    wr = wr.at[:, :n_r].set(w_r_hi).at[:, HEAD_DIM:HEAD_DIM + n_r].set(w_r_lo)
    br = jnp.zeros((1, LANES), _f32).at[0, :n_r].set(
        jnp.concatenate([b_group_router[0], b_expert_router[0].reshape(N_EXPERTS)]))
    meta_blk = jnp.concatenate([jnp.zeros((PAD, D_MODEL), x.dtype), meta_tokens.astype(x.dtype)], axis=0)
    bdq = _block_diag_ones(ATTN_DIM)
    bdk = _block_diag_ones(KV_DIM)
    tri = jnp.asarray(np.tril(np.ones((TM, TM), np.float32), -1), dtype=_bf16)

    kv_sd = jax.ShapeDtypeStruct((BLK, KV_DIM), _bf16)
    k0m, k1m, v0m, v1m, utm, bias = pl.pallas_call(
        _prep_kernel,
        out_shape=(kv_sd, kv_sd, kv_sd, kv_sd,
                   jax.ShapeDtypeStruct((SUBLANES, CONV_CH), _f32),
                   jax.ShapeDtypeStruct((2, N_KV_HEADS, Q_GROUP * BLK, 2 * BLK), _f32)),
        in_specs=[pl.BlockSpec(memory_space=pltpu.SMEM)] + [pl.BlockSpec(memory_space=pltpu.VMEM)] * 6,
        out_specs=tuple(pl.BlockSpec(memory_space=pltpu.VMEM) for _ in range(6)),
        compiler_params=pltpu.CompilerParams(vmem_limit_bytes=VMEM_LIMIT),
        name="prep",
    )(rel_bias, meta_blk, mixg, win, kg, bdk, _bucket_maps())

    tile_idx = lambda b, t, *_: (b * nt + t, 0)
    grid_spec = pltpu.PrefetchScalarGridSpec(
        num_scalar_prefetch=1,
        grid=(bsz, nt),
        in_specs=[
            pl.BlockSpec((None, TM, D_MODEL), lambda b, t, *_: (b, t, 0)),
            _const((1, D_MODEL)), _const((D_MODEL, IN_PROJ)), _const((1, ATTN_DIM)), _const((1, KV_DIM)),
            _const((2, N_KV_HEADS, Q_GROUP * BLK, 2 * BLK)), _const((3, CONV_CH)),
            _const((1, ATTN_DIM)), _const((1, CONV_CH)), _const((D_MODEL, D_MODEL)), _const((1, D_MODEL)),
            _const((D_MODEL, LANES)), _const((1, LANES)),
            _const((BLK, KV_DIM)), _const((BLK, KV_DIM)), _const((BLK, KV_DIM)), _const((BLK, KV_DIM)),
            _const((SUBLANES, CONV_CH)),
            _const((ATTN_DIM, ATTN_DIM)), _const((KV_DIM, KV_DIM)), _const((TM, TM)),
        ],
        out_specs=[
            pl.BlockSpec((None, TM, D_MODEL), lambda b, t, *_: (b, t, 0)),
            pl.BlockSpec((TM * ROW_CHUNKS, LANES), tile_idx),
            pl.BlockSpec((TM, ROUTE_W), tile_idx),
            _const((1, LANES)),
        ],
        scratch_shapes=[
            pltpu.VMEM((BLK, KV_DIM), _bf16), pltpu.VMEM((BLK, KV_DIM), _bf16),
            pltpu.VMEM((BLK, KV_DIM), _bf16), pltpu.VMEM((BLK, KV_DIM), _bf16),
            pltpu.VMEM((TM + SUBLANES, CONV_CH), _f32),
            pltpu.VMEM((TM, ATTN_DIM), _f32),
            pltpu.VMEM((1, LANES), _f32),
        ],
    )
    h2, hn2, route, cnt = pl.pallas_call(
        _mixer_kernel,
        grid_spec=grid_spec,
        out_shape=(jax.ShapeDtypeStruct((bsz, seq, D_MODEL), _f32),
                   jax.ShapeDtypeStruct((n_tok * ROW_CHUNKS, LANES), _f32),
                   jax.ShapeDtypeStruct((n_tok, ROUTE_W), _f32),
                   jax.ShapeDtypeStruct((1, LANES), _f32)),
        compiler_params=pltpu.CompilerParams(dimension_semantics=("arbitrary", "arbitrary"),
                                             vmem_limit_bytes=VMEM_LIMIT),
        name="mixer",
    )(sinks, x, mixg, win, qg, kg, bias, convw, ag, cg, wout, fg, wr, br,
      k0m, k1m, v0m, v1m, utm, bdq, bdk, tri)

    n_tiles = (n_tok * 2) // TME + N_EXPERTS
    counts = cnt[0, :N_EXPERTS].astype(jnp.int32)
    ntile = (counts + TME - 1) // TME
    tile_end = jnp.cumsum(ntile)
    tile_start = tile_end - ntile
    eid = route[:, 0:2].astype(jnp.int32)
    rank = route[:, 4:6].astype(jnp.int32)
    experts = jnp.arange(N_EXPERTS, dtype=jnp.int32)
    start_of = jnp.sum(jnp.where(eid[:, :, None] == experts, tile_start, 0), axis=-1)
    pos = (start_of * TME + rank).reshape(-1)
    tiles = jnp.arange(n_tiles, dtype=jnp.int32)
    n_used = tile_end[-1]
    t_exp = jnp.sum((jnp.minimum(tiles, n_used - 1)[:, None] >= tile_end[None, :]).astype(jnp.int32), axis=-1)
    t_exp = jnp.minimum(t_exp, N_EXPERTS - 1)
    own = t_exp[:, None] == experts
    t_rows = jnp.sum(jnp.where(own, counts - (tiles[:, None] - tile_start) * TME, 0), axis=-1)
    t_rows = jnp.where(tiles < n_used, jnp.clip(t_rows, 0, TME), 0).astype(jnp.int32)

    src_tok = pl.pallas_call(
        _invert_kernel,
        grid=((2 * n_tok) // INV_BLK,),
        in_specs=[pl.BlockSpec((INV_BLK,), lambda g: (g,), memory_space=pltpu.SMEM),
                  pl.BlockSpec(memory_space=pl.ANY)],
        out_specs=pl.BlockSpec(memory_space=pltpu.SMEM),
        out_shape=jax.ShapeDtypeStruct((n_tiles * TME,), jnp.int32),
        scratch_shapes=[pltpu.SemaphoreType.DMA((1,))],
        compiler_params=pltpu.CompilerParams(dimension_semantics=("arbitrary",)),
        name="invert",
    )(pos, jnp.zeros((n_tiles * TME,), jnp.int32))

    y_sorted = pl.pallas_call(
        _experts_kernel,
        grid_spec=pltpu.PrefetchScalarGridSpec(
            num_scalar_prefetch=3,
            grid=(n_tiles,),
            in_specs=[
                pl.BlockSpec(memory_space=pl.ANY),
                pl.BlockSpec((None, D_MODEL, D_EXPERT), lambda i, te, tn, src: (te[i], 0, 0)),
                pl.BlockSpec((None, D_MODEL, D_EXPERT), lambda i, te, tn, src: (te[i], 0, 0)),
                pl.BlockSpec((None, D_EXPERT, D_MODEL), lambda i, te, tn, src: (te[i], 0, 0)),
            ],
            out_specs=pl.BlockSpec((TME * ROW_CHUNKS, LANES), lambda i, *_: (i, 0)),
            scratch_shapes=[
                pltpu.VMEM((TME * ROW_CHUNKS, LANES), _f32),
                pltpu.VMEM((TME * ROW_CHUNKS, LANES), _f32),
                pltpu.SemaphoreType.DMA((2,)),
                pltpu.VMEM((D_MODEL, D_EXPERT), _bf16),
                pltpu.VMEM((D_MODEL, D_EXPERT), _bf16),
                pltpu.VMEM((D_EXPERT, D_MODEL), _bf16),
            ],
        ),
        out_shape=jax.ShapeDtypeStruct((n_tiles * TME * ROW_CHUNKS, LANES), _f32),
        compiler_params=pltpu.CompilerParams(dimension_semantics=("arbitrary",),
                                             vmem_limit_bytes=VMEM_LIMIT),
        name="experts",
    )(t_exp, t_rows, src_tok, hn2, w_gate[0], w_up[0], w_down[0])

    out = pl.pallas_call(
        _combine_kernel,
        grid_spec=pltpu.PrefetchScalarGridSpec(
            num_scalar_prefetch=1,
            grid=(n_tok // TMC,),
            in_specs=[
                pl.BlockSpec(memory_space=pl.ANY),
                pl.BlockSpec((TMC, D_MODEL), lambda i, *_: (i, 0)),
                pl.BlockSpec((TMC, ROUTE_W), lambda i, *_: (i, 0)),
            ],
            out_specs=pl.BlockSpec((TMC, D_MODEL), lambda i, *_: (i, 0)),
            scratch_shapes=[
                pltpu.VMEM((2, 2, TMC * ROW_CHUNKS, LANES), _f32),
                pltpu.SemaphoreType.DMA((2,)),
            ],
        ),
        out_shape=jax.ShapeDtypeStruct((n_tok, D_MODEL), _f32),
        compiler_params=pltpu.CompilerParams(dimension_semantics=("arbitrary",),
                                             vmem_limit_bytes=VMEM_LIMIT),
        name="combine",
    )(pos, y_sorted, h2.reshape(n_tok, D_MODEL), route)
    return out.reshape(bsz, seq, D_MODEL)
```

```python
import functools

import numpy as np
import jax
import jax.numpy as jnp
from jax import lax
from jax.experimental import pallas as pl
from jax.experimental.pallas import tpu as pltpu

D_MODEL = 1024
N_META = 16
N_Q_HEADS = 8
N_KV_HEADS = 2
HEAD_DIM = 64
Q_GROUP = N_Q_HEADS // N_KV_HEADS
ATTN_DIM = N_Q_HEADS * HEAD_DIM
KV_DIM = N_KV_HEADS * HEAD_DIM
BLK = 128
PAD = BLK - N_META
N_BUCKETS = 32
MAX_DISTANCE = 128
CONV_CH = D_MODEL // 2
IN_PROJ = ATTN_DIM + 2 * KV_DIM + 3 * CONV_CH
N_GROUPS = 4
EXPERTS_PER_GROUP = 8
N_EXPERTS = N_GROUPS * EXPERTS_PER_GROUP
D_EXPERT = D_MODEL // 2
EPS = 1e-6
NEG_INF = -1e30

LANES = 128
SUBLANES = 8
ROW_CHUNKS = D_MODEL // LANES
TM = 512
TME = 256
TMC = 256
INV_BLK = 8192
SCALAR_UNROLL = 16
DMA_UNROLL = 8
ROUTE_W = 8
VMEM_LIMIT = 56 * 1024 * 1024

Q_OFF, K_OFF, V_OFF = 0, ATTN_DIM, ATTN_DIM + KV_DIM
CB_OFF = ATTN_DIM + 2 * KV_DIM
CC_OFF = CB_OFF + CONV_CH
CH_OFF = CC_OFF + CONV_CH

_f32 = jnp.float32
_bf16 = jnp.bfloat16


def _rms(x, g):
    return x * lax.rsqrt(jnp.mean(x * x, axis=-1, keepdims=True) + EPS) * g


def _dot(a, b):
    return jnp.dot(a, b, preferred_element_type=_f32)


def _dup_halves(x):
    lane = lax.broadcasted_iota(jnp.int32, x.shape, 1)
    sw = pltpu.roll(x, HEAD_DIM, axis=1)
    lo = lane < HEAD_DIM
    return jnp.where(lo, x, sw).astype(_bf16), jnp.where(lo, sw, x).astype(_bf16)


def _kv_state(hn_bf, win_ref, kg_ref, bdk_ref):
    kv = _dot(hn_bf, win_ref[:, K_OFF:K_OFF + 2 * KV_DIM])
    k = kv[:, :KV_DIM]
    v = kv[:, KV_DIM:]
    ssk = _dot((k * k).astype(_bf16), bdk_ref[...])
    kn = k * lax.rsqrt(ssk * (1.0 / HEAD_DIM) + EPS) * kg_ref[...]
    return _dup_halves(kn) + _dup_halves(v)


def _prep_kernel(rb_ref, meta_ref, mixg_ref, win_ref, kg_ref, bdk_ref, bucket_ref,
                 k0_ref, k1_ref, v0_ref, v1_ref, ut_ref, bias_ref):
    hn = _rms(meta_ref[...], mixg_ref[...]).astype(_bf16)
    k0, k1, v0, v1 = _kv_state(hn, win_ref, kg_ref, bdk_ref)
    k0_ref[...] = k0
    k1_ref[...] = k1
    v0_ref[...] = v0
    v1_ref[...] = v1
    cch = _dot(hn, win_ref[:, CC_OFF:CC_OFF + 2 * CONV_CH])
    u = cch[:, :CONV_CH] * cch[:, CONV_CH:]
    ut_ref[...] = u[BLK - SUBLANES:, :]
    for f in range(2):
        bk = bucket_ref[f]
        for h in range(N_Q_HEADS):
            acc = jnp.full((BLK, 2 * BLK), NEG_INF, _f32)
            for b in range(N_BUCKETS):
                acc = jnp.where(bk == b, rb_ref[b, h], acc)
            bias_ref[f, h // Q_GROUP, (h % Q_GROUP) * BLK:(h % Q_GROUP + 1) * BLK, :] = acc


def _mixer_kernel(sinks_ref,
                  x_ref, mixg_ref, win_ref, qg_ref, kg_ref, bias_ref, convw_ref, ag_ref, cg_ref,
                  wout_ref, fg_ref, wr_ref, br_ref, k0m_ref, k1m_ref, v0m_ref, v1m_ref, utm_ref,
                  bdq_ref, bdk_ref, tri_ref,
                  h2_ref, hn2_ref, route_ref, cnt_ref,
                  kp0, kp1, vp0, vp1, ubuf, a_scr, cnt_acc):
    b = pl.program_id(0)
    t = pl.program_id(1)

    @pl.when(t == 0)
    def _():
        kp0[...] = k0m_ref[...]
        kp1[...] = k1m_ref[...]
        vp0[...] = v0m_ref[...]
        vp1[...] = v1m_ref[...]
        ubuf[0:SUBLANES, :] = utm_ref[...]

    @pl.when(jnp.logical_and(b == 0, t == 0))
    def _():
        cnt_acc[...] = jnp.zeros_like(cnt_acc)

    x = x_ref[...]
    hn = _rms(x, mixg_ref[...]).astype(_bf16)

    q = _dot(hn, win_ref[:, Q_OFF:Q_OFF + ATTN_DIM])
    ssq = _dot((q * q).astype(_bf16), bdq_ref[...])
    qn = (q * lax.rsqrt(ssq * (1.0 / HEAD_DIM) + EPS) * qg_ref[...]).astype(_bf16)
    kd0, kd1, vd0, vd1 = _kv_state(hn, win_ref, kg_ref, bdk_ref)
    kd = (kd0, kd1)
    vd = (vd0, vd1)
    kp = (kp0, kp1)
    vp = (vp0, vp1)

    lane_q = lax.broadcasted_iota(jnp.int32, (BLK, LANES), 1)
    lo_half = lane_q < HEAD_DIM
    row4 = lax.broadcasted_iota(jnp.int32, (Q_GROUP * BLK, 1), 0) // BLK
    first = jnp.where(t == 0, 0, 1)
    zero_bf = jnp.zeros((BLK, LANES), _bf16)

    for j in range(TM // BLK):
        rows = slice(j * BLK, (j + 1) * BLK)
        for g in range(N_KV_HEADS):
            if j == 0:
                kcat = jnp.concatenate([kp[g][...], kd[g][rows]], axis=0)
                vcat = jnp.concatenate([vp[g][...], vd[g][rows]], axis=0)
                bias = bias_ref[first, g]
            else:
                kcat = kd[g][(j - 1) * BLK:(j + 1) * BLK]
                vcat = vd[g][(j - 1) * BLK:(j + 1) * BLK]
                bias = bias_ref[1, g]
            qs = []
            for hh in range(Q_GROUP):
                h = g * Q_GROUP + hh
                qc = qn[rows, (h // 2) * LANES:(h // 2 + 1) * LANES]
                keep = lo_half if h % 2 == 0 else jnp.logical_not(lo_half)
                qs.append(jnp.where(keep, qc, zero_bf))
            q4 = jnp.concatenate(qs, axis=0)
            s = lax.dot_general(q4, kcat, (((1,), (1,)), ((), ())),
                                preferred_element_type=_f32) + bias
            sink = jnp.full((Q_GROUP * BLK, 1), sinks_ref[g * Q_GROUP], _f32)
            for hh in range(1, Q_GROUP):
                sink = jnp.where(row4 == hh, sinks_ref[g * Q_GROUP + hh], sink)
            m = jnp.maximum(jnp.max(s, axis=-1, keepdims=True), sink)
            p = jnp.exp(s - m)
            l = jnp.sum(p, axis=-1, keepdims=True) + jnp.exp(sink - m)
            o = _dot(p.astype(_bf16), vcat) / l
            for pp in range(Q_GROUP // 2):
                ev = o[(2 * pp) * BLK:(2 * pp + 1) * BLK]
                od = o[(2 * pp + 1) * BLK:(2 * pp + 2) * BLK]
                col = g * (Q_GROUP // 2) + pp
                a_scr[rows, col * LANES:(col + 1) * LANES] = jnp.where(lo_half, ev, od)

    last = slice(TM - BLK, TM)
    kp0[...] = kd0[last]
    kp1[...] = kd1[last]
    vp0[...] = vd0[last]
    vp1[...] = vd1[last]

    cb = _dot(hn, win_ref[:, CB_OFF:CB_OFF + CONV_CH])
    cch = _dot(hn, win_ref[:, CC_OFF:CC_OFF + 2 * CONV_CH])
    u = cch[:, :CONV_CH] * cch[:, CONV_CH:]
    ubuf[SUBLANES:, :] = u
    u1 = ubuf[SUBLANES - 1:SUBLANES - 1 + TM, :]
    u2 = ubuf[SUBLANES - 2:SUBLANES - 2 + TM, :]
    c = cb * (convw_ref[0:1, :] * u2 + convw_ref[1:2, :] * u1 + convw_ref[2:3, :] * u)
    ubuf[0:SUBLANES, :] = u[TM - SUBLANES:, :]

    an = _rms(a_scr[...], ag_ref[...]).astype(_bf16)
    cn = _rms(c, cg_ref[...]).astype(_bf16)
    h2 = x + _dot(an, wout_ref[0:ATTN_DIM, :]) + _dot(cn, wout_ref[ATTN_DIM:, :])
    h2_ref[...] = h2

    hn2 = _rms(h2, fg_ref[...])
    for cc in range(ROW_CHUNKS):
        hn2_ref[pl.ds(cc, TM, stride=ROW_CHUNKS), :] = hn2[:, cc * LANES:(cc + 1) * LANES]

    hi = hn2.astype(_bf16)
    lo = (hn2 - hi.astype(_f32)).astype(_bf16)
    r1 = _dot(hi, wr_ref[...])
    r2 = _dot(lo, wr_ref[...])
    lg = r1 + pltpu.roll(r1, HEAD_DIM, axis=1) + r2 + br_ref[...]
    lane = lax.broadcasted_iota(jnp.int32, (TM, LANES), 1)
    lanef = lane.astype(_f32)
    ninf = jnp.float32(-jnp.inf)
    big = jnp.float32(LANES)

    gl = jnp.where(lane < N_GROUPS, lg, ninf)
    gmax = jnp.max(gl, axis=-1, keepdims=True)
    gsum = jnp.sum(jnp.exp(gl - gmax), axis=-1, keepdims=True)
    g_p = 1.0 / gsum
    g_idx = jnp.min(jnp.where(gl == gmax, lanef, big), axis=-1, keepdims=True)
    e_lo = N_GROUPS + EXPERTS_PER_GROUP * g_idx
    el = jnp.where(jnp.logical_and(lanef >= e_lo, lanef < e_lo + EXPERTS_PER_GROUP), lg, ninf)
    m1 = jnp.max(el, axis=-1, keepdims=True)
    i1 = jnp.min(jnp.where(el == m1, lanef, big), axis=-1, keepdims=True)
    el2 = jnp.where(lanef == i1, ninf, el)
    m2 = jnp.max(el2, axis=-1, keepdims=True)
    i2 = jnp.min(jnp.where(el2 == m2, lanef, big), axis=-1, keepdims=True)
    ex = jnp.exp(m2 - m1)
    den = 1.0 / (1.0 + ex)
    gate0 = g_p * den
    gate1 = g_p * ex * den
    e0 = i1 - N_GROUPS
    e1 = i2 - N_GROUPS

    oh0 = lanef == e0
    oh1 = lanef == e1
    cmat = (jnp.where(oh0, 1.0, 0.0) + jnp.where(oh1, 1.0, 0.0))
    prefix = _dot(tri_ref[...], cmat.astype(_bf16)) + cnt_acc[...]
    rank0 = jnp.sum(jnp.where(oh0, prefix, 0.0), axis=-1, keepdims=True)
    rank1 = jnp.sum(jnp.where(oh1, prefix, 0.0), axis=-1, keepdims=True)
    cnt_new = cnt_acc[...] + jnp.sum(cmat, axis=0, keepdims=True)
    cnt_acc[...] = cnt_new
    cnt_ref[...] = cnt_new

    lane8 = lax.broadcasted_iota(jnp.int32, (TM, ROUTE_W), 1)
    rec = jnp.zeros((TM, ROUTE_W), _f32)
    for idx, val in enumerate((e0, e1, gate0, gate1, rank0, rank1)):
        rec = jnp.where(lane8 == idx, val, rec)
    route_ref[...] = rec


def _row_tile(ref, row):
    return ref.at[pl.ds(pl.multiple_of(row * ROW_CHUNKS, ROW_CHUNKS), ROW_CHUNKS), :]


def _invert_kernel(pos_ref, zeros_hbm, src_ref, sem):
    g = pl.program_id(0)

    @pl.when(g == 0)
    def _():
        fill = pltpu.make_async_copy(zeros_hbm, src_ref, sem.at[0])
        fill.start()
        fill.wait()

    def body(it, carry):
        dst = [pos_ref[it * SCALAR_UNROLL + j] for j in range(SCALAR_UNROLL)]
        tok0 = g * (INV_BLK // 2) + it * (SCALAR_UNROLL // 2)
        for j in range(SCALAR_UNROLL):
            src_ref[dst[j]] = tok0 + j // 2
        return carry

    lax.fori_loop(0, INV_BLK // SCALAR_UNROLL, body, 0)


def _experts_kernel(te_ref, tn_ref, src_ref,
                    hn2_hbm, wg_ref, wu_ref, wd_ref,
                    y_ref,
                    xbuf, sem, wgb, wub, wdb):
    i = pl.program_id(0)
    n = pl.num_programs(0)
    slot = i % 2
    nxt = jnp.minimum(i + 1, n - 1)

    def row_copy(tile, sl, r):
        return pltpu.make_async_copy(_row_tile(hn2_hbm, src_ref[tile * TME + r]),
                                     _row_tile(xbuf.at[sl], r), sem.at[sl])

    def tile_wait(sl):
        pltpu.make_async_copy(hn2_hbm.at[pl.ds(0, TME * ROW_CHUNKS), :], xbuf.at[sl], sem.at[sl]).wait()

    def issue_loop(tile, sl):
        def body(it, carry):
            for j in range(DMA_UNROLL):
                row_copy(tile, sl, it * DMA_UNROLL + j).start(priority=j % 2)
            return carry

        lax.fori_loop(0, TME // DMA_UNROLL, body, 0)

    @pl.when(i == 0)
    def _():
        issue_loop(0, 0)

    valid = tn_ref[i] > 0
    changed = jnp.logical_or(i == 0, te_ref[i] != te_ref[jnp.maximum(i - 1, 0)])

    @pl.when(jnp.logical_and(changed, valid))
    def _():
        wgb[...] = wg_ref[...].astype(_bf16)
        wub[...] = wu_ref[...].astype(_bf16)
        wdb[...] = wd_ref[...].astype(_bf16)

    tile_wait(slot)

    @pl.when(valid)
    def _():
        for r in range(TME):
            row_copy(nxt, 1 - slot, r).start(priority=r % 2)
        xs = [xbuf[slot, pl.ds(cc, TME, stride=ROW_CHUNKS), :] for cc in range(ROW_CHUNKS)]
        xb = jnp.concatenate(xs, axis=1).astype(_bf16)
        gate = _dot(xb, wgb[...])
        up = _dot(xb, wub[...])
        act = (gate * jax.nn.sigmoid(gate) * up).astype(_bf16)
        y = _dot(act, wdb[...])
        for cc in range(ROW_CHUNKS):
            y_ref[pl.ds(cc, TME, stride=ROW_CHUNKS), :] = y[:, cc * LANES:(cc + 1) * LANES]

    @pl.when(jnp.logical_not(valid))
    def _():
        issue_loop(nxt, 1 - slot)
        y_ref[...] = jnp.zeros_like(y_ref)

    @pl.when(i == n - 1)
    def _():
        tile_wait(1 - slot)


def _combine_kernel(pos_ref,
                    y_hbm, h2_ref, route_ref,
                    out_ref,
                    ybuf, sem):
    i = pl.program_id(0)
    n = pl.num_programs(0)
    slot = i % 2
    nxt = jnp.minimum(i + 1, n - 1)

    def row_copy(tile, sl, r, k):
        return pltpu.make_async_copy(_row_tile(y_hbm, pos_ref[tile * (2 * TMC) + 2 * r + k]),
                                     _row_tile(ybuf.at[sl, k], r), sem.at[sl])

    def tile_wait(sl):
        for k in range(2):
            pltpu.make_async_copy(y_hbm.at[pl.ds(0, TMC * ROW_CHUNKS), :], ybuf.at[sl, k], sem.at[sl]).wait()

    @pl.when(i == 0)
    def _():
        def body(it, carry):
            for j in range(DMA_UNROLL):
                for k in range(2):
                    row_copy(0, 0, it * DMA_UNROLL + j, k).start(priority=k)
            return carry

        lax.fori_loop(0, TMC // DMA_UNROLL, body, 0)

    tile_wait(slot)
    for r in range(TMC):
        for k in range(2):
            row_copy(nxt, 1 - slot, r, k).start(priority=k)
    g0 = route_ref[:, 2:3]
    g1 = route_ref[:, 3:4]
    for cc in range(ROW_CHUNKS):
        y0 = ybuf[slot, 0, pl.ds(cc, TMC, stride=ROW_CHUNKS), :]
        y1 = ybuf[slot, 1, pl.ds(cc, TMC, stride=ROW_CHUNKS), :]
        cols = slice(cc * LANES, (cc + 1) * LANES)
        out_ref[:, cols] = h2_ref[:, cols] + (g0 * y0 + g1 * y1)

    @pl.when(i == n - 1)
    def _():
        tile_wait(1 - slot)


def _t5_bucket(n):
    max_exact = N_BUCKETS // 2
    nf = jnp.maximum(n, 1).astype(_f32)
    large = max_exact + (jnp.log(nf / max_exact) / np.log(MAX_DISTANCE / max_exact)
                         * (N_BUCKETS - max_exact)).astype(jnp.int32)
    large = jnp.minimum(large, N_BUCKETS - 1)
    return jnp.where(n < max_exact, n, large)


def _bucket_maps():
    qi = jnp.arange(BLK)[:, None]
    sj = jnp.arange(2 * BLK)[None, :]
    dist = BLK + qi - sj
    band = (dist >= 0) & (dist < BLK)
    bucket = _t5_bucket(jnp.maximum(dist, 0))
    generic = jnp.where(band, bucket, -1)
    first = jnp.where(band & (sj >= PAD), bucket, -1)
    return jnp.stack([first, generic]).astype(jnp.int32)


def _const(shape):
    nd = len(shape)
    return pl.BlockSpec(shape, lambda *_: (0,) * nd)


def _block_diag_ones(n):
    idx = np.arange(n) // HEAD_DIM
    return jnp.asarray((idx[:, None] == idx[None, :]).astype(np.float32), dtype=_bf16)


def kernel(x, meta_tokens, rel_bias, mix_norm_g, w_in, q_norm_g, k_norm_g, attn_sinks, conv_w, attn_out_norm_g, conv_out_norm_g, w_out, ffn_norm_g, w_group_router, b_group_router, w_expert_router, b_expert_router, w_gate, w_up, w_down):
    bsz, seq, _ = x.shape
    assert seq % TM == 0 and (bsz * seq) % TMC == 0
    n_tok = bsz * seq
    nt = seq // TM

    win = w_in[0].astype(_bf16)
    wout = w_out[0].astype(_bf16)
    mixg = mix_norm_g[0].reshape(1, D_MODEL)
    fg = ffn_norm_g[0].reshape(1, D_MODEL)
    qg = (jnp.tile(q_norm_g[0], N_Q_HEADS) * (HEAD_DIM ** -0.5)).reshape(1, ATTN_DIM)
    kg = jnp.tile(k_norm_g[0], N_KV_HEADS).reshape(1, KV_DIM)
    ag = attn_out_norm_g[0].reshape(1, ATTN_DIM)
    cg = conv_out_norm_g[0].reshape(1, CONV_CH)
    convw = conv_w[0]
    sinks = attn_sinks[0]
    w_r = jnp.concatenate([w_group_router[0], w_expert_router[0].reshape(D_MODEL, N_EXPERTS)], axis=1)
    n_r = N_GROUPS + N_EXPERTS
    w_r_hi = w_r.astype(_bf16)
    w_r_lo = (w_r - w_r_hi.astype(_f32)).astype(_bf16)
    wr = jnp.zeros((D_MODEL, LANES), _bf16)
    wr = wr.at[:, :n_r].set(w_r_hi).at[:, HEAD_DIM:HEAD_DIM + n_r].set(w_r_lo)
    br = jnp.zeros((1, LANES), _f32).at[0, :n_r].set(
        jnp.concatenate([b_group_router[0], b_expert_router[0].reshape(N_EXPERTS)]))
    meta_blk = jnp.concatenate([jnp.zeros((PAD, D_MODEL), x.dtype), meta_tokens.astype(x.dtype)], axis=0)
    bdq = _block_diag_ones(ATTN_DIM)
    bdk = _block_diag_ones(KV_DIM)
    tri = jnp.asarray(np.tril(np.ones((TM, TM), np.float32), -1), dtype=_bf16)

    kv_sd = jax.ShapeDtypeStruct((BLK, KV_DIM), _bf16)
    k0m, k1m, v0m, v1m, utm, bias = pl.pallas_call(
        _prep_kernel,
        out_shape=(kv_sd, kv_sd, kv_sd, kv_sd,
                   jax.ShapeDtypeStruct((SUBLANES, CONV_CH), _f32),
                   jax.ShapeDtypeStruct((2, N_KV_HEADS, Q_GROUP * BLK, 2 * BLK), _f32)),
        in_specs=[pl.BlockSpec(memory_space=pltpu.SMEM)] + [pl.BlockSpec(memory_space=pltpu.VMEM)] * 6,
        out_specs=tuple(pl.BlockSpec(memory_space=pltpu.VMEM) for _ in range(6)),
        compiler_params=pltpu.CompilerParams(vmem_limit_bytes=VMEM_LIMIT),
        name="prep",
    )(rel_bias, meta_blk, mixg, win, kg, bdk, _bucket_maps())

    tile_idx = lambda b, t, *_: (b * nt + t, 0)
    grid_spec = pltpu.PrefetchScalarGridSpec(
        num_scalar_prefetch=1,
        grid=(bsz, nt),
        in_specs=[
            pl.BlockSpec((None, TM, D_MODEL), lambda b, t, *_: (b, t, 0)),
            _const((1, D_MODEL)), _const((D_MODEL, IN_PROJ)), _const((1, ATTN_DIM)), _const((1, KV_DIM)),
            _const((2, N_KV_HEADS, Q_GROUP * BLK, 2 * BLK)), _const((3, CONV_CH)),
            _const((1, ATTN_DIM)), _const((1, CONV_CH)), _const((D_MODEL, D_MODEL)), _const((1, D_MODEL)),
            _const((D_MODEL, LANES)), _const((1, LANES)),
            _const((BLK, KV_DIM)), _const((BLK, KV_DIM)), _const((BLK, KV_DIM)), _const((BLK, KV_DIM)),
            _const((SUBLANES, CONV_CH)),
            _const((ATTN_DIM, ATTN_DIM)), _const((KV_DIM, KV_DIM)), _const((TM, TM)),
        ],
        out_specs=[
            pl.BlockSpec((None, TM, D_MODEL), lambda b, t, *_: (b, t, 0)),
            pl.BlockSpec((TM * ROW_CHUNKS, LANES), tile_idx),
            pl.BlockSpec((TM, ROUTE_W), tile_idx),
            _const((1, LANES)),
        ],
        scratch_shapes=[
            pltpu.VMEM((BLK, KV_DIM), _bf16), pltpu.VMEM((BLK, KV_DIM), _bf16),
            pltpu.VMEM((BLK, KV_DIM), _bf16), pltpu.VMEM((BLK, KV_DIM), _bf16),
            pltpu.VMEM((TM + SUBLANES, CONV_CH), _f32),
            pltpu.VMEM((TM, ATTN_DIM), _f32),
            pltpu.VMEM((1, LANES), _f32),
        ],
    )
    h2, hn2, route, cnt = pl.pallas_call(
        _mixer_kernel,
        grid_spec=grid_spec,
        out_shape=(jax.ShapeDtypeStruct((bsz, seq, D_MODEL), _f32),
                   jax.ShapeDtypeStruct((n_tok * ROW_CHUNKS, LANES), _f32),
                   jax.ShapeDtypeStruct((n_tok, ROUTE_W), _f32),
                   jax.ShapeDtypeStruct((1, LANES), _f32)),
        compiler_params=pltpu.CompilerParams(dimension_semantics=("arbitrary", "arbitrary"),
                                             vmem_limit_bytes=VMEM_LIMIT),
        name="mixer",
    )(sinks, x, mixg, win, qg, kg, bias, convw, ag, cg, wout, fg, wr, br,
      k0m, k1m, v0m, v1m, utm, bdq, bdk, tri)

    n_tiles = (n_tok * 2) // TME + N_EXPERTS
    counts = cnt[0, :N_EXPERTS].astype(jnp.int32)
    ntile = (counts + TME - 1) // TME
    tile_end = jnp.cumsum(ntile)
    tile_start = tile_end - ntile
    eid = route[:, 0:2].astype(jnp.int32)
    rank = route[:, 4:6].astype(jnp.int32)
    experts = jnp.arange(N_EXPERTS, dtype=jnp.int32)
    start_of = jnp.sum(jnp.where(eid[:, :, None] == experts, tile_start, 0), axis=-1)
    pos = (start_of * TME + rank).reshape(-1)
    tiles = jnp.arange(n_tiles, dtype=jnp.int32)
    n_used = tile_end[-1]
    t_exp = jnp.sum((jnp.minimum(tiles, n_used - 1)[:, None] >= tile_end[None, :]).astype(jnp.int32), axis=-1)
    t_exp = jnp.minimum(t_exp, N_EXPERTS - 1)
    own = t_exp[:, None] == experts
    t_rows = jnp.sum(jnp.where(own, counts - (tiles[:, None] - tile_start) * TME, 0), axis=-1)
    t_rows = jnp.where(tiles < n_used, jnp.clip(t_rows, 0, TME), 0).astype(jnp.int32)

    src_tok = pl.pallas_call(
        _invert_kernel,
        grid=((2 * n_tok) // INV_BLK,),
        in_specs=[pl.BlockSpec((INV_BLK,), lambda g: (g,), memory_space=pltpu.SMEM),
                  pl.BlockSpec(memory_space=pl.ANY)],
        out_specs=pl.BlockSpec(memory_space=pltpu.SMEM),
        out_shape=jax.ShapeDtypeStruct((n_tiles * TME,), jnp.int32),
        scratch_shapes=[pltpu.SemaphoreType.DMA((1,))],
        compiler_params=pltpu.CompilerParams(dimension_semantics=("arbitrary",)),
        name="invert",
    )(pos, jnp.zeros((n_tiles * TME,), jnp.int32))

    y_sorted = pl.pallas_call(
        _experts_kernel,
        grid_spec=pltpu.PrefetchScalarGridSpec(
            num_scalar_prefetch=3,
            grid=(n_tiles,),
            in_specs=[
                pl.BlockSpec(memory_space=pl.ANY),
                pl.BlockSpec((None, D_MODEL, D_EXPERT), lambda i, te, tn, src: (te[i], 0, 0)),
                pl.BlockSpec((None, D_MODEL, D_EXPERT), lambda i, te, tn, src: (te[i], 0, 0)),
                pl.BlockSpec((None, D_EXPERT, D_MODEL), lambda i, te, tn, src: (te[i], 0, 0)),
            ],
            out_specs=pl.BlockSpec((TME * ROW_CHUNKS, LANES), lambda i, *_: (i, 0)),
            scratch_shapes=[
                pltpu.VMEM((2, TME * ROW_CHUNKS, LANES), _f32),
                pltpu.SemaphoreType.DMA((2,)),
                pltpu.VMEM((D_MODEL, D_EXPERT), _bf16),
                pltpu.VMEM((D_MODEL, D_EXPERT), _bf16),
                pltpu.VMEM((D_EXPERT, D_MODEL), _bf16),
            ],
        ),
        out_shape=jax.ShapeDtypeStruct((n_tiles * TME * ROW_CHUNKS, LANES), _f32),
        compiler_params=pltpu.CompilerParams(dimension_semantics=("arbitrary",),
                                             vmem_limit_bytes=VMEM_LIMIT),
        name="experts",
    )(t_exp, t_rows, src_tok, hn2, w_gate[0], w_up[0], w_down[0])

    out = pl.pallas_call(
        _combine_kernel,
        grid_spec=pltpu.PrefetchScalarGridSpec(
            num_scalar_prefetch=1,
            grid=(n_tok // TMC,),
            in_specs=[
                pl.BlockSpec(memory_space=pl.ANY),
                pl.BlockSpec((TMC, D_MODEL), lambda i, *_: (i, 0)),
                pl.BlockSpec((TMC, ROUTE_W), lambda i, *_: (i, 0)),
            ],
            out_specs=pl.BlockSpec((TMC, D_MODEL), lambda i, *_: (i, 0)),
            scratch_shapes=[
                pltpu.VMEM((2, 2, TMC * ROW_CHUNKS, LANES), _f32),
                pltpu.SemaphoreType.DMA((2,)),
            ],
        ),
        out_shape=jax.ShapeDtypeStruct((n_tok, D_MODEL), _f32),
        compiler_params=pltpu.CompilerParams(dimension_semantics=("arbitrary",),
                                             vmem_limit_bytes=VMEM_LIMIT),
        name="combine",
    )(pos, y_sorted, h2.reshape(n_tok, D_MODEL), route)
    return out.reshape(bsz, seq, D_MODEL)
```

```python
import functools

import numpy as np
import jax
import jax.numpy as jnp
from jax import lax
from jax.experimental import pallas as pl
from jax.experimental.pallas import tpu as pltpu
from jax.experimental.pallas import tpu_sc as plsc

D_MODEL = 1024
N_META = 16
N_Q_HEADS = 8
N_KV_HEADS = 2
HEAD_DIM = 64
Q_GROUP = N_Q_HEADS // N_KV_HEADS
ATTN_DIM = N_Q_HEADS * HEAD_DIM
KV_DIM = N_KV_HEADS * HEAD_DIM
BLK = 128
PAD = BLK - N_META
N_BUCKETS = 32
MAX_DISTANCE = 128
CONV_CH = D_MODEL // 2
IN_PROJ = ATTN_DIM + 2 * KV_DIM + 3 * CONV_CH
N_GROUPS = 4
EXPERTS_PER_GROUP = 8
N_EXPERTS = N_GROUPS * EXPERTS_PER_GROUP
D_EXPERT = D_MODEL // 2
EPS = 1e-6
NEG_INF = -1e30

LANES = 128
SUBLANES = 8
ROW_CHUNKS = D_MODEL // LANES
TM = 512
TME = 256
TMC = 256
SC_WINDOW = 32
DMA_UNROLL = 8
ROUTE_W = 8
VMEM_LIMIT = 56 * 1024 * 1024

Q_OFF, K_OFF, V_OFF = 0, ATTN_DIM, ATTN_DIM + KV_DIM
CB_OFF = ATTN_DIM + 2 * KV_DIM
CC_OFF = CB_OFF + CONV_CH
CH_OFF = CC_OFF + CONV_CH

_f32 = jnp.float32
_bf16 = jnp.bfloat16


def _rms(x, g):
    return x * lax.rsqrt(jnp.mean(x * x, axis=-1, keepdims=True) + EPS) * g


def _dot(a, b):
    return jnp.dot(a, b, preferred_element_type=_f32)


def _dup_halves(x):
    lane = lax.broadcasted_iota(jnp.int32, x.shape, 1)
    sw = pltpu.roll(x, HEAD_DIM, axis=1)
    lo = lane < HEAD_DIM
    return jnp.where(lo, x, sw).astype(_bf16), jnp.where(lo, sw, x).astype(_bf16)


def _kv_state(hn_bf, win_ref, kg_ref, bdk_ref):
    kv = _dot(hn_bf, win_ref[:, K_OFF:K_OFF + 2 * KV_DIM])
    k = kv[:, :KV_DIM]
    v = kv[:, KV_DIM:]
    ssk = _dot((k * k).astype(_bf16), bdk_ref[...])
    kn = k * lax.rsqrt(ssk * (1.0 / HEAD_DIM) + EPS) * kg_ref[...]
    return _dup_halves(kn) + _dup_halves(v)


def _prep_kernel(rb_ref, meta_ref, mixg_ref, win_ref, kg_ref, bdk_ref, bucket_ref,
                 k0_ref, k1_ref, v0_ref, v1_ref, ut_ref, bias_ref):
    hn = _rms(meta_ref[...], mixg_ref[...]).astype(_bf16)
    k0, k1, v0, v1 = _kv_state(hn, win_ref, kg_ref, bdk_ref)
    k0_ref[...] = k0
    k1_ref[...] = k1
    v0_ref[...] = v0
    v1_ref[...] = v1
    cch = _dot(hn, win_ref[:, CC_OFF:CC_OFF + 2 * CONV_CH])
    u = cch[:, :CONV_CH] * cch[:, CONV_CH:]
    ut_ref[...] = u[BLK - SUBLANES:, :]
    for f in range(2):
        bk = bucket_ref[f]
        for h in range(N_Q_HEADS):
            acc = jnp.full((BLK, 2 * BLK), NEG_INF, _f32)
            for b in range(N_BUCKETS):
                acc = jnp.where(bk == b, rb_ref[b, h], acc)
            bias_ref[f, h // Q_GROUP, (h % Q_GROUP) * BLK:(h % Q_GROUP + 1) * BLK, :] = acc


def _mixer_kernel(sinks_ref,
                  x_ref, mixg_ref, win_ref, qg_ref, kg_ref, bias_ref, convw_ref, ag_ref, cg_ref,
                  wout_ref, fg_ref, wr_ref, br_ref, k0m_ref, k1m_ref, v0m_ref, v1m_ref, utm_ref,
                  bdq_ref, bdk_ref, tri_ref,
                  h2_ref, hn2_ref, route_ref, cnt_ref,
                  kp0, kp1, vp0, vp1, ubuf, a_scr, cnt_acc):
    b = pl.program_id(0)
    t = pl.program_id(1)

    @pl.when(t == 0)
    def _():
        kp0[...] = k0m_ref[...]
        kp1[...] = k1m_ref[...]
        vp0[...] = v0m_ref[...]
        vp1[...] = v1m_ref[...]
        ubuf[0:SUBLANES, :] = utm_ref[...]

    @pl.when(jnp.logical_and(b == 0, t == 0))
    def _():
        cnt_acc[...] = jnp.zeros_like(cnt_acc)

    x = x_ref[...]
    hn = _rms(x, mixg_ref[...]).astype(_bf16)

    q = _dot(hn, win_ref[:, Q_OFF:Q_OFF + ATTN_DIM])
    ssq = _dot((q * q).astype(_bf16), bdq_ref[...])
    qn = (q * lax.rsqrt(ssq * (1.0 / HEAD_DIM) + EPS) * qg_ref[...]).astype(_bf16)
    kd0, kd1, vd0, vd1 = _kv_state(hn, win_ref, kg_ref, bdk_ref)
    kd = (kd0, kd1)
    vd = (vd0, vd1)
    kp = (kp0, kp1)
    vp = (vp0, vp1)

    lane_q = lax.broadcasted_iota(jnp.int32, (BLK, LANES), 1)
    lo_half = lane_q < HEAD_DIM
    row4 = lax.broadcasted_iota(jnp.int32, (Q_GROUP * BLK, 1), 0) // BLK
    first = jnp.where(t == 0, 0, 1)
    zero_bf = jnp.zeros((BLK, LANES), _bf16)

    for j in range(TM // BLK):
        rows = slice(j * BLK, (j + 1) * BLK)
        for g in range(N_KV_HEADS):
            if j == 0:
                kcat = jnp.concatenate([kp[g][...], kd[g][rows]], axis=0)
                vcat = jnp.concatenate([vp[g][...], vd[g][rows]], axis=0)
                bias = bias_ref[first, g]
            else:
                kcat = kd[g][(j - 1) * BLK:(j + 1) * BLK]
                vcat = vd[g][(j - 1) * BLK:(j + 1) * BLK]
                bias = bias_ref[1, g]
            qs = []
            for hh in range(Q_GROUP):
                h = g * Q_GROUP + hh
                qc = qn[rows, (h // 2) * LANES:(h // 2 + 1) * LANES]
                keep = lo_half if h % 2 == 0 else jnp.logical_not(lo_half)
                qs.append(jnp.where(keep, qc, zero_bf))
            q4 = jnp.concatenate(qs, axis=0)
            s = lax.dot_general(q4, kcat, (((1,), (1,)), ((), ())),
                                preferred_element_type=_f32) + bias
            sink = jnp.full((Q_GROUP * BLK, 1), sinks_ref[g * Q_GROUP], _f32)
            for hh in range(1, Q_GROUP):
                sink = jnp.where(row4 == hh, sinks_ref[g * Q_GROUP + hh], sink)
            m = jnp.maximum(jnp.max(s, axis=-1, keepdims=True), sink)
            p = jnp.exp(s - m)
            l = jnp.sum(p, axis=-1, keepdims=True) + jnp.exp(sink - m)
            o = _dot(p.astype(_bf16), vcat) / l
            for pp in range(Q_GROUP // 2):
                ev = o[(2 * pp) * BLK:(2 * pp + 1) * BLK]
                od = o[(2 * pp + 1) * BLK:(2 * pp + 2) * BLK]
                col = g * (Q_GROUP // 2) + pp
                a_scr[rows, col * LANES:(col + 1) * LANES] = jnp.where(lo_half, ev, od)

    last = slice(TM - BLK, TM)
    kp0[...] = kd0[last]
    kp1[...] = kd1[last]
    vp0[...] = vd0[last]
    vp1[...] = vd1[last]

    cb = _dot(hn, win_ref[:, CB_OFF:CB_OFF + CONV_CH])
    cch = _dot(hn, win_ref[:, CC_OFF:CC_OFF + 2 * CONV_CH])
    u = cch[:, :CONV_CH] * cch[:, CONV_CH:]
    ubuf[SUBLANES:, :] = u
    u1 = ubuf[SUBLANES - 1:SUBLANES - 1 + TM, :]
    u2 = ubuf[SUBLANES - 2:SUBLANES - 2 + TM, :]
    c = cb * (convw_ref[0:1, :] * u2 + convw_ref[1:2, :] * u1 + convw_ref[2:3, :] * u)
    ubuf[0:SUBLANES, :] = u[TM - SUBLANES:, :]

    an = _rms(a_scr[...], ag_ref[...]).astype(_bf16)
    cn = _rms(c, cg_ref[...]).astype(_bf16)
    h2 = x + _dot(an, wout_ref[0:ATTN_DIM, :]) + _dot(cn, wout_ref[ATTN_DIM:, :])
    h2_ref[...] = h2

    hn2 = _rms(h2, fg_ref[...])
    for cc in range(ROW_CHUNKS):
        hn2_ref[pl.ds(cc, TM, stride=ROW_CHUNKS), :] = hn2[:, cc * LANES:(cc + 1) * LANES]

    hi = hn2.astype(_bf16)
    lo = (hn2 - hi.astype(_f32)).astype(_bf16)
    r1 = _dot(hi, wr_ref[...])
    r2 = _dot(lo, wr_ref[...])
    lg = r1 + pltpu.roll(r1, HEAD_DIM, axis=1) + r2 + br_ref[...]
    lane = lax.broadcasted_iota(jnp.int32, (TM, LANES), 1)
    lanef = lane.astype(_f32)
    ninf = jnp.float32(-jnp.inf)
    big = jnp.float32(LANES)

    gl = jnp.where(lane < N_GROUPS, lg, ninf)
    gmax = jnp.max(gl, axis=-1, keepdims=True)
    gsum = jnp.sum(jnp.exp(gl - gmax), axis=-1, keepdims=True)
    g_p = 1.0 / gsum
    g_idx = jnp.min(jnp.where(gl == gmax, lanef, big), axis=-1, keepdims=True)
    e_lo = N_GROUPS + EXPERTS_PER_GROUP * g_idx
    el = jnp.where(jnp.logical_and(lanef >= e_lo, lanef < e_lo + EXPERTS_PER_GROUP), lg, ninf)
    m1 = jnp.max(el, axis=-1, keepdims=True)
    i1 = jnp.min(jnp.where(el == m1, lanef, big), axis=-1, keepdims=True)
    el2 = jnp.where(lanef == i1, ninf, el)
    m2 = jnp.max(el2, axis=-1, keepdims=True)
    i2 = jnp.min(jnp.where(el2 == m2, lanef, big), axis=-1, keepdims=True)
    ex = jnp.exp(m2 - m1)
    den = 1.0 / (1.0 + ex)
    gate0 = g_p * den
    gate1 = g_p * ex * den
    e0 = i1 - N_GROUPS
    e1 = i2 - N_GROUPS

    oh0 = lanef == e0
    oh1 = lanef == e1
    cmat = (jnp.where(oh0, 1.0, 0.0) + jnp.where(oh1, 1.0, 0.0))
    prefix = _dot(tri_ref[...], cmat.astype(_bf16)) + cnt_acc[...]
    rank0 = jnp.sum(jnp.where(oh0, prefix, 0.0), axis=-1, keepdims=True)
    rank1 = jnp.sum(jnp.where(oh1, prefix, 0.0), axis=-1, keepdims=True)
    cnt_new = cnt_acc[...] + jnp.sum(cmat, axis=0, keepdims=True)
    cnt_acc[...] = cnt_new
    cnt_ref[...] = cnt_new

    lane8 = lax.broadcasted_iota(jnp.int32, (TM, ROUTE_W), 1)
    rec = jnp.zeros((TM, ROUTE_W), _f32)
    for idx, val in enumerate((e0, e1, gate0, gate1, rank0, rank1)):
        rec = jnp.where(lane8 == idx, val, rec)
    route_ref[...] = rec


def _row_tile(ref, row):
    return ref.at[pl.ds(pl.multiple_of(row * ROW_CHUNKS, ROW_CHUNKS), ROW_CHUNKS), :]


def _dispatch_rows(hn2_rows, pos0, pos1, n_rows):
    n_tok = hn2_rows.shape[0]
    info = plsc.get_sparse_core_info()
    n_workers = info.num_cores * info.num_subcores
    per_worker = n_tok // n_workers
    assert per_worker * n_workers == n_tok and per_worker % SC_WINDOW == 0
    mesh = plsc.VectorSubcoreMesh(core_axis_name="core", subcore_axis_name="subcore")

    @functools.partial(
        pl.kernel,
        out_type=jax.ShapeDtypeStruct((n_rows, ROW_CHUNKS, LANES), _f32),
        mesh=mesh,
        scratch_types=[pltpu.VMEM((SC_WINDOW,), jnp.int32), pltpu.VMEM((SC_WINDOW,), jnp.int32),
                       pltpu.VMEM((SC_WINDOW, ROW_CHUNKS, LANES), _f32)],
        compiler_params=pltpu.CompilerParams(use_tc_tiling_on_sc=True),
        name="dispatch",
    )
    def dispatch(x_hbm, i0_hbm, i1_hbm, o_hbm, i0_v, i1_v, rows_v):
        wid = lax.axis_index("subcore") * info.num_cores + lax.axis_index("core")
        base = wid * per_worker

        @pl.loop(0, per_worker // SC_WINDOW)
        def _(c):
            off = pl.multiple_of(base + c * SC_WINDOW, SC_WINDOW)
            pltpu.sync_copy(i0_hbm.at[pl.ds(off, SC_WINDOW)], i0_v)
            pltpu.sync_copy(i1_hbm.at[pl.ds(off, SC_WINDOW)], i1_v)
            pltpu.sync_copy(x_hbm.at[pl.ds(off, SC_WINDOW)], rows_v)
            pltpu.sync_copy(rows_v, o_hbm.at[i0_v])
            pltpu.sync_copy(rows_v, o_hbm.at[i1_v])

    return dispatch(hn2_rows, pos0, pos1)


def _experts_kernel(te_ref, tn_ref,
                    xs_ref, wg_ref, wu_ref, wd_ref,
                    y_ref,
                    wgb, wub, wdb):
    i = pl.program_id(0)
    valid = tn_ref[i] > 0
    changed = jnp.logical_or(i == 0, te_ref[i] != te_ref[jnp.maximum(i - 1, 0)])

    @pl.when(jnp.logical_and(changed, valid))
    def _():
        wgb[...] = wg_ref[...].astype(_bf16)
        wub[...] = wu_ref[...].astype(_bf16)
        wdb[...] = wd_ref[...].astype(_bf16)

    @pl.when(valid)
    def _():
        xs = [xs_ref[pl.ds(cc, TME, stride=ROW_CHUNKS), :] for cc in range(ROW_CHUNKS)]
        xb = jnp.concatenate(xs, axis=1).astype(_bf16)
        live = lax.broadcasted_iota(jnp.int32, (TME, 1), 0) < tn_ref[i]
        xb = jnp.where(live, xb, jnp.zeros_like(xb))
        gate = _dot(xb, wgb[...])
        up = _dot(xb, wub[...])
        act = (gate * jax.nn.sigmoid(gate) * up).astype(_bf16)
        y = _dot(act, wdb[...])
        for cc in range(ROW_CHUNKS):
            y_ref[pl.ds(cc, TME, stride=ROW_CHUNKS), :] = y[:, cc * LANES:(cc + 1) * LANES]

    @pl.when(jnp.logical_not(valid))
    def _():
        y_ref[...] = jnp.zeros_like(y_ref)


def _combine_kernel(pos_ref,
                    y_hbm, h2_ref, route_ref,
                    out_ref,
                    ybuf, sem):
    i = pl.program_id(0)
    n = pl.num_programs(0)
    slot = i % 2

    def issue(tile, sl):
        base = tile * (2 * TMC)

        def body(it, carry):
            for j in range(DMA_UNROLL):
                r = it * DMA_UNROLL + j
                for k in range(2):
                    pltpu.make_async_copy(_row_tile(y_hbm, pos_ref[base + 2 * r + k]),
                                          _row_tile(ybuf.at[sl, k], r), sem.at[sl]).start(priority=k)
            return carry

        lax.fori_loop(0, TMC // DMA_UNROLL, body, 0)

    @pl.when(i == 0)
    def _():
        issue(0, 0)

    @pl.when(i + 1 < n)
    def _():
        issue(i + 1, 1 - slot)

    for k in range(2):
        pltpu.make_async_copy(y_hbm.at[pl.ds(0, TMC * ROW_CHUNKS), :], ybuf.at[slot, k], sem.at[slot]).wait()
    g0 = route_ref[:, 2:3]
    g1 = route_ref[:, 3:4]
    for cc in range(ROW_CHUNKS):
        y0 = ybuf[slot, 0, pl.ds(cc, TMC, stride=ROW_CHUNKS), :]
        y1 = ybuf[slot, 1, pl.ds(cc, TMC, stride=ROW_CHUNKS), :]
        cols = slice(cc * LANES, (cc + 1) * LANES)
        out_ref[:, cols] = h2_ref[:, cols] + (g0 * y0 + g1 * y1)


def _t5_bucket(n):
    max_exact = N_BUCKETS // 2
    nf = jnp.maximum(n, 1).astype(_f32)
    large = max_exact + (jnp.log(nf / max_exact) / np.log(MAX_DISTANCE / max_exact)
                         * (N_BUCKETS - max_exact)).astype(jnp.int32)
    large = jnp.minimum(large, N_BUCKETS - 1)
    return jnp.where(n < max_exact, n, large)


def _bucket_maps():
    qi = jnp.arange(BLK)[:, None]
    sj = jnp.arange(2 * BLK)[None, :]
    dist = BLK + qi - sj
    band = (dist >= 0) & (dist < BLK)
    bucket = _t5_bucket(jnp.maximum(dist, 0))
    generic = jnp.where(band, bucket, -1)
    first = jnp.where(band & (sj >= PAD), bucket, -1)
    return jnp.stack([first, generic]).astype(jnp.int32)


def _const(shape):
    nd = len(shape)
    return pl.BlockSpec(shape, lambda *_: (0,) * nd)


def _block_diag_ones(n):
    idx = np.arange(n) // HEAD_DIM
    return jnp.asarray((idx[:, None] == idx[None, :]).astype(np.float32), dtype=_bf16)


def kernel(x, meta_tokens, rel_bias, mix_norm_g, w_in, q_norm_g, k_norm_g, attn_sinks, conv_w, attn_out_norm_g, conv_out_norm_g, w_out, ffn_norm_g, w_group_router, b_group_router, w_expert_router, b_expert_router, w_gate, w_up, w_down):
    bsz, seq, _ = x.shape
    assert seq % TM == 0 and (bsz * seq) % TMC == 0
    n_tok = bsz * seq
    nt = seq // TM

    win = w_in[0].astype(_bf16)
    wout = w_out[0].astype(_bf16)
    mixg = mix_norm_g[0].reshape(1, D_MODEL)
    fg = ffn_norm_g[0].reshape(1, D_MODEL)
    qg = (jnp.tile(q_norm_g[0], N_Q_HEADS) * (HEAD_DIM ** -0.5)).reshape(1, ATTN_DIM)
    kg = jnp.tile(k_norm_g[0], N_KV_HEADS).reshape(1, KV_DIM)
    ag = attn_out_norm_g[0].reshape(1, ATTN_DIM)
    cg = conv_out_norm_g[0].reshape(1, CONV_CH)
    convw = conv_w[0]
    sinks = attn_sinks[0]
    w_r = jnp.concatenate([w_group_router[0], w_expert_router[0].reshape(D_MODEL, N_EXPERTS)], axis=1)
    n_r = N_GROUPS + N_EXPERTS
    w_r_hi = w_r.astype(_bf16)
    w_r_lo = (w_r - w_r_hi.astype(_f32)).astype(_bf16)
    wr = jnp.zeros((D_MODEL, LANES), _bf16)
    wr = wr.at[:, :n_r].set(w_r_hi).at[:, HEAD_DIM:HEAD_DIM + n_r].set(w_r_lo)
    br = jnp.zeros((1, LANES), _f32).at[0, :n_r].set(
        jnp.concatenate([b_group_router[0], b_expert_router[0].reshape(N_EXPERTS)]))
    meta_blk = jnp.concatenate([jnp.zeros((PAD, D_MODEL), x.dtype), meta_tokens.astype(x.dtype)], axis=0)
    bdq = _block_diag_ones(ATTN_DIM)
    bdk = _block_diag_ones(KV_DIM)
    tri = jnp.asarray(np.tril(np.ones((TM, TM), np.float32), -1), dtype=_bf16)

    kv_sd = jax.ShapeDtypeStruct((BLK, KV_DIM), _bf16)
    k0m, k1m, v0m, v1m, utm, bias = pl.pallas_call(
        _prep_kernel,
        out_shape=(kv_sd, kv_sd, kv_sd, kv_sd,
                   jax.ShapeDtypeStruct((SUBLANES, CONV_CH), _f32),
                   jax.ShapeDtypeStruct((2, N_KV_HEADS, Q_GROUP * BLK, 2 * BLK), _f32)),
        in_specs=[pl.BlockSpec(memory_space=pltpu.SMEM)] + [pl.BlockSpec(memory_space=pltpu.VMEM)] * 6,
        out_specs=tuple(pl.BlockSpec(memory_space=pltpu.VMEM) for _ in range(6)),
        compiler_params=pltpu.CompilerParams(vmem_limit_bytes=VMEM_LIMIT),
        name="prep",
    )(rel_bias, meta_blk, mixg, win, kg, bdk, _bucket_maps())

    tile_idx = lambda b, t, *_: (b * nt + t, 0)
    grid_spec = pltpu.PrefetchScalarGridSpec(
        num_scalar_prefetch=1,
        grid=(bsz, nt),
        in_specs=[
            pl.BlockSpec((None, TM, D_MODEL), lambda b, t, *_: (b, t, 0)),
            _const((1, D_MODEL)), _const((D_MODEL, IN_PROJ)), _const((1, ATTN_DIM)), _const((1, KV_DIM)),
            _const((2, N_KV_HEADS, Q_GROUP * BLK, 2 * BLK)), _const((3, CONV_CH)),
            _const((1, ATTN_DIM)), _const((1, CONV_CH)), _const((D_MODEL, D_MODEL)), _const((1, D_MODEL)),
            _const((D_MODEL, LANES)), _const((1, LANES)),
            _const((BLK, KV_DIM)), _const((BLK, KV_DIM)), _const((BLK, KV_DIM)), _const((BLK, KV_DIM)),
            _const((SUBLANES, CONV_CH)),
            _const((ATTN_DIM, ATTN_DIM)), _const((KV_DIM, KV_DIM)), _const((TM, TM)),
        ],
        out_specs=[
            pl.BlockSpec((None, TM, D_MODEL), lambda b, t, *_: (b, t, 0)),
            pl.BlockSpec((TM * ROW_CHUNKS, LANES), tile_idx),
            pl.BlockSpec((TM, ROUTE_W), tile_idx),
            _const((1, LANES)),
        ],
        scratch_shapes=[
            pltpu.VMEM((BLK, KV_DIM), _bf16), pltpu.VMEM((BLK, KV_DIM), _bf16),
            pltpu.VMEM((BLK, KV_DIM), _bf16), pltpu.VMEM((BLK, KV_DIM), _bf16),
            pltpu.VMEM((TM + SUBLANES, CONV_CH), _f32),
            pltpu.VMEM((TM, ATTN_DIM), _f32),
            pltpu.VMEM((1, LANES), _f32),
        ],
    )
    h2, hn2, route, cnt = pl.pallas_call(
        _mixer_kernel,
        grid_spec=grid_spec,
        out_shape=(jax.ShapeDtypeStruct((bsz, seq, D_MODEL), _f32),
                   jax.ShapeDtypeStruct((n_tok * ROW_CHUNKS, LANES), _f32),
                   jax.ShapeDtypeStruct((n_tok, ROUTE_W), _f32),
                   jax.ShapeDtypeStruct((1, LANES), _f32)),
        compiler_params=pltpu.CompilerParams(dimension_semantics=("arbitrary", "arbitrary"),
                                             vmem_limit_bytes=VMEM_LIMIT),
        name="mixer",
    )(sinks, x, mixg, win, qg, kg, bias, convw, ag, cg, wout, fg, wr, br,
      k0m, k1m, v0m, v1m, utm, bdq, bdk, tri)

    n_tiles = (n_tok * 2) // TME + N_EXPERTS
    counts = cnt[0, :N_EXPERTS].astype(jnp.int32)
    ntile = (counts + TME - 1) // TME
    tile_end = jnp.cumsum(ntile)
    tile_start = tile_end - ntile
    eid = route[:, 0:2].astype(jnp.int32)
    rank = route[:, 4:6].astype(jnp.int32)
    experts = jnp.arange(N_EXPERTS, dtype=jnp.int32)
    start_of = jnp.sum(jnp.where(eid[:, :, None] == experts, tile_start, 0), axis=-1)
    pos2 = start_of * TME + rank
    pos = pos2.reshape(-1)
    tiles = jnp.arange(n_tiles, dtype=jnp.int32)
    n_used = tile_end[-1]
    t_exp = jnp.sum((jnp.minimum(tiles, n_used - 1)[:, None] >= tile_end[None, :]).astype(jnp.int32), axis=-1)
    t_exp = jnp.minimum(t_exp, N_EXPERTS - 1)
    own = t_exp[:, None] == experts
    t_rows = jnp.sum(jnp.where(own, counts - (tiles[:, None] - tile_start) * TME, 0), axis=-1)
    t_rows = jnp.where(tiles < n_used, jnp.clip(t_rows, 0, TME), 0).astype(jnp.int32)

    xs = _dispatch_rows(hn2.reshape(n_tok, ROW_CHUNKS, LANES), pos2[:, 0], pos2[:, 1], n_tiles * TME)
    xs = xs.reshape(n_tiles * TME * ROW_CHUNKS, LANES)

    y_sorted = pl.pallas_call(
        _experts_kernel,
        grid_spec=pltpu.PrefetchScalarGridSpec(
            num_scalar_prefetch=2,
            grid=(n_tiles,),
            in_specs=[
                pl.BlockSpec((TME * ROW_CHUNKS, LANES), lambda i, *_: (i, 0)),
                pl.BlockSpec((None, D_MODEL, D_EXPERT), lambda i, te, tn: (te[i], 0, 0)),
                pl.BlockSpec((None, D_MODEL, D_EXPERT), lambda i, te, tn: (te[i], 0, 0)),
                pl.BlockSpec((None, D_EXPERT, D_MODEL), lambda i, te, tn: (te[i], 0, 0)),
            ],
            out_specs=pl.BlockSpec((TME * ROW_CHUNKS, LANES), lambda i, *_: (i, 0)),
            scratch_shapes=[
                pltpu.VMEM((D_MODEL, D_EXPERT), _bf16),
                pltpu.VMEM((D_MODEL, D_EXPERT), _bf16),
                pltpu.VMEM((D_EXPERT, D_MODEL), _bf16),
            ],
        ),
        out_shape=jax.ShapeDtypeStruct((n_tiles * TME * ROW_CHUNKS, LANES), _f32),
        compiler_params=pltpu.CompilerParams(dimension_semantics=("arbitrary",),
                                             vmem_limit_bytes=VMEM_LIMIT),
        name="experts",
    )(t_exp, t_rows, xs, w_gate[0], w_up[0], w_down[0])

    out = pl.pallas_call(
        _combine_kernel,
        grid_spec=pltpu.PrefetchScalarGridSpec(
            num_scalar_prefetch=1,
            grid=(n_tok // TMC,),
            in_specs=[
                pl.BlockSpec(memory_space=pl.ANY),
                pl.BlockSpec((TMC, D_MODEL), lambda i, *_: (i, 0)),
                pl.BlockSpec((TMC, ROUTE_W), lambda i, *_: (i, 0)),
            ],
            out_specs=pl.BlockSpec((TMC, D_MODEL), lambda i, *_: (i, 0)),
            scratch_shapes=[
                pltpu.VMEM((2, 2, TMC * ROW_CHUNKS, LANES), _f32),
                pltpu.SemaphoreType.DMA((2,)),
            ],
        ),
        out_shape=jax.ShapeDtypeStruct((n_tok, D_MODEL), _f32),
        compiler_params=pltpu.CompilerParams(dimension_semantics=("arbitrary",),
                                             vmem_limit_bytes=VMEM_LIMIT),
        name="combine",
    )(pos, y_sorted, h2.reshape(n_tok, D_MODEL), route)
    return out.reshape(bsz, seq, D_MODEL)
```

```python
import functools

import numpy as np
import jax
import jax.numpy as jnp
from jax import lax
from jax.experimental import pallas as pl
from jax.experimental.pallas import tpu as pltpu
from jax.experimental.pallas import tpu_sc as plsc

D_MODEL = 1024
N_META = 16
N_Q_HEADS = 8
N_KV_HEADS = 2
HEAD_DIM = 64
Q_GROUP = N_Q_HEADS // N_KV_HEADS
ATTN_DIM = N_Q_HEADS * HEAD_DIM
KV_DIM = N_KV_HEADS * HEAD_DIM
BLK = 128
PAD = BLK - N_META
N_BUCKETS = 32
MAX_DISTANCE = 128
CONV_CH = D_MODEL // 2
IN_PROJ = ATTN_DIM + 2 * KV_DIM + 3 * CONV_CH
N_GROUPS = 4
EXPERTS_PER_GROUP = 8
N_EXPERTS = N_GROUPS * EXPERTS_PER_GROUP
D_EXPERT = D_MODEL // 2
EPS = 1e-6
NEG_INF = -1e30

LANES = 128
SUBLANES = 8
ROW_CHUNKS = D_MODEL // LANES
TM = 512
TME = 256
TMC = 256
SC_WINDOW = 32
DMA_UNROLL = 8
ROUTE_W = 8
VMEM_LIMIT = 56 * 1024 * 1024

Q_OFF, K_OFF, V_OFF = 0, ATTN_DIM, ATTN_DIM + KV_DIM
CB_OFF = ATTN_DIM + 2 * KV_DIM
CC_OFF = CB_OFF + CONV_CH
CH_OFF = CC_OFF + CONV_CH

_f32 = jnp.float32
_bf16 = jnp.bfloat16


def _rms(x, g):
    return x * lax.rsqrt(jnp.mean(x * x, axis=-1, keepdims=True) + EPS) * g


def _dot(a, b):
    return jnp.dot(a, b, preferred_element_type=_f32)


def _dup_halves(x):
    lane = lax.broadcasted_iota(jnp.int32, x.shape, 1)
    sw = pltpu.roll(x, HEAD_DIM, axis=1)
    lo = lane < HEAD_DIM
    return jnp.where(lo, x, sw).astype(_bf16), jnp.where(lo, sw, x).astype(_bf16)


def _kv_state(hn_bf, win_ref, kg_ref, bdk_ref):
    kv = _dot(hn_bf, win_ref[:, K_OFF:K_OFF + 2 * KV_DIM])
    k = kv[:, :KV_DIM]
    v = kv[:, KV_DIM:]
    ssk = _dot((k * k).astype(_bf16), bdk_ref[...])
    kn = k * lax.rsqrt(ssk * (1.0 / HEAD_DIM) + EPS) * kg_ref[...]
    return _dup_halves(kn) + _dup_halves(v)


def _prep_kernel(rb_ref, meta_ref, mixg_ref, win_ref, kg_ref, bdk_ref, bucket_ref,
                 k0_ref, k1_ref, v0_ref, v1_ref, ut_ref, bias_ref):
    hn = _rms(meta_ref[...], mixg_ref[...]).astype(_bf16)
    k0, k1, v0, v1 = _kv_state(hn, win_ref, kg_ref, bdk_ref)
    k0_ref[...] = k0
    k1_ref[...] = k1
    v0_ref[...] = v0
    v1_ref[...] = v1
    cch = _dot(hn, win_ref[:, CC_OFF:CC_OFF + 2 * CONV_CH])
    u = cch[:, :CONV_CH] * cch[:, CONV_CH:]
    ut_ref[...] = u[BLK - SUBLANES:, :]
    for f in range(2):
        bk = bucket_ref[f]
        for h in range(N_Q_HEADS):
            acc = jnp.full((BLK, 2 * BLK), NEG_INF, _f32)
            for b in range(N_BUCKETS):
                acc = jnp.where(bk == b, rb_ref[b, h], acc)
            bias_ref[f, h // Q_GROUP, (h % Q_GROUP) * BLK:(h % Q_GROUP + 1) * BLK, :] = acc


def _mixer_kernel(sinks_ref,
                  x_ref, mixg_ref, win_ref, qg_ref, kg_ref, bias_ref, convw_ref, ag_ref, cg_ref,
                  wout_ref, fg_ref, wr_ref, br_ref, k0m_ref, k1m_ref, v0m_ref, v1m_ref, utm_ref,
                  bdq_ref, bdk_ref, tri_ref,
                  h2_ref, hn2_ref, route_ref, cnt_ref,
                  kp0, kp1, vp0, vp1, ubuf, a_scr, cnt_acc):
    b = pl.program_id(0)
    t = pl.program_id(1)

    @pl.when(t == 0)
    def _():
        kp0[...] = k0m_ref[...]
        kp1[...] = k1m_ref[...]
        vp0[...] = v0m_ref[...]
        vp1[...] = v1m_ref[...]
        ubuf[0:SUBLANES, :] = utm_ref[...]

    @pl.when(jnp.logical_and(b == 0, t == 0))
    def _():
        cnt_acc[...] = jnp.zeros_like(cnt_acc)

    x = x_ref[...]
    hn = _rms(x, mixg_ref[...]).astype(_bf16)

    q = _dot(hn, win_ref[:, Q_OFF:Q_OFF + ATTN_DIM])
    ssq = _dot((q * q).astype(_bf16), bdq_ref[...])
    qn = (q * lax.rsqrt(ssq * (1.0 / HEAD_DIM) + EPS) * qg_ref[...]).astype(_bf16)
    kd0, kd1, vd0, vd1 = _kv_state(hn, win_ref, kg_ref, bdk_ref)
    kd = (kd0, kd1)
    vd = (vd0, vd1)
    kp = (kp0, kp1)
    vp = (vp0, vp1)

    lane_q = lax.broadcasted_iota(jnp.int32, (BLK, LANES), 1)
    lo_half = lane_q < HEAD_DIM
    row4 = lax.broadcasted_iota(jnp.int32, (Q_GROUP * BLK, 1), 0) // BLK
    first = jnp.where(t == 0, 0, 1)
    zero_bf = jnp.zeros((BLK, LANES), _bf16)

    for j in range(TM // BLK):
        rows = slice(j * BLK, (j + 1) * BLK)
        for g in range(N_KV_HEADS):
            if j == 0:
                kcat = jnp.concatenate([kp[g][...], kd[g][rows]], axis=0)
                vcat = jnp.concatenate([vp[g][...], vd[g][rows]], axis=0)
                bias = bias_ref[first, g]
            else:
                kcat = kd[g][(j - 1) * BLK:(j + 1) * BLK]
                vcat = vd[g][(j - 1) * BLK:(j + 1) * BLK]
                bias = bias_ref[1, g]
            qs = []
            for hh in range(Q_GROUP):
                h = g * Q_GROUP + hh
                qc = qn[rows, (h // 2) * LANES:(h // 2 + 1) * LANES]
                keep = lo_half if h % 2 == 0 else jnp.logical_not(lo_half)
                qs.append(jnp.where(keep, qc, zero_bf))
            q4 = jnp.concatenate(qs, axis=0)
            s = lax.dot_general(q4, kcat, (((1,), (1,)), ((), ())),
                                preferred_element_type=_f32) + bias
            sink = jnp.full((Q_GROUP * BLK, 1), sinks_ref[g * Q_GROUP], _f32)
            for hh in range(1, Q_GROUP):
                sink = jnp.where(row4 == hh, sinks_ref[g * Q_GROUP + hh], sink)
            m = jnp.maximum(jnp.max(s, axis=-1, keepdims=True), sink)
            p = jnp.exp(s - m)
            l = jnp.sum(p, axis=-1, keepdims=True) + jnp.exp(sink - m)
            o = _dot(p.astype(_bf16), vcat) / l
            for pp in range(Q_GROUP // 2):
                ev = o[(2 * pp) * BLK:(2 * pp + 1) * BLK]
                od = o[(2 * pp + 1) * BLK:(2 * pp + 2) * BLK]
                col = g * (Q_GROUP // 2) + pp
                a_scr[rows, col * LANES:(col + 1) * LANES] = jnp.where(lo_half, ev, od)

    last = slice(TM - BLK, TM)
    kp0[...] = kd0[last]
    kp1[...] = kd1[last]
    vp0[...] = vd0[last]
    vp1[...] = vd1[last]

    cb = _dot(hn, win_ref[:, CB_OFF:CB_OFF + CONV_CH])
    cch = _dot(hn, win_ref[:, CC_OFF:CC_OFF + 2 * CONV_CH])
    u = cch[:, :CONV_CH] * cch[:, CONV_CH:]
    ubuf[SUBLANES:, :] = u
    u1 = ubuf[SUBLANES - 1:SUBLANES - 1 + TM, :]
    u2 = ubuf[SUBLANES - 2:SUBLANES - 2 + TM, :]
    c = cb * (convw_ref[0:1, :] * u2 + convw_ref[1:2, :] * u1 + convw_ref[2:3, :] * u)
    ubuf[0:SUBLANES, :] = u[TM - SUBLANES:, :]

    an = _rms(a_scr[...], ag_ref[...]).astype(_bf16)
    cn = _rms(c, cg_ref[...]).astype(_bf16)
    h2 = x + _dot(an, wout_ref[0:ATTN_DIM, :]) + _dot(cn, wout_ref[ATTN_DIM:, :])
    h2_ref[...] = h2

    hn2 = _rms(h2, fg_ref[...])
    hn2_ref[...] = hn2

    hi = hn2.astype(_bf16)
    lo = (hn2 - hi.astype(_f32)).astype(_bf16)
    r1 = _dot(hi, wr_ref[...])
    r2 = _dot(lo, wr_ref[...])
    lg = r1 + pltpu.roll(r1, HEAD_DIM, axis=1) + r2 + br_ref[...]
    lane = lax.broadcasted_iota(jnp.int32, (TM, LANES), 1)
    lanef = lane.astype(_f32)
    ninf = jnp.float32(-jnp.inf)
    big = jnp.float32(LANES)

    gl = jnp.where(lane < N_GROUPS, lg, ninf)
    gmax = jnp.max(gl, axis=-1, keepdims=True)
    gsum = jnp.sum(jnp.exp(gl - gmax), axis=-1, keepdims=True)
    g_p = 1.0 / gsum
    g_idx = jnp.min(jnp.where(gl == gmax, lanef, big), axis=-1, keepdims=True)
    e_lo = N_GROUPS + EXPERTS_PER_GROUP * g_idx
    el = jnp.where(jnp.logical_and(lanef >= e_lo, lanef < e_lo + EXPERTS_PER_GROUP), lg, ninf)
    m1 = jnp.max(el, axis=-1, keepdims=True)
    i1 = jnp.min(jnp.where(el == m1, lanef, big), axis=-1, keepdims=True)
    el2 = jnp.where(lanef == i1, ninf, el)
    m2 = jnp.max(el2, axis=-1, keepdims=True)
    i2 = jnp.min(jnp.where(el2 == m2, lanef, big), axis=-1, keepdims=True)
    ex = jnp.exp(m2 - m1)
    den = 1.0 / (1.0 + ex)
    gate0 = g_p * den
    gate1 = g_p * ex * den
    e0 = i1 - N_GROUPS
    e1 = i2 - N_GROUPS

    oh0 = lanef == e0
    oh1 = lanef == e1
    cmat = (jnp.where(oh0, 1.0, 0.0) + jnp.where(oh1, 1.0, 0.0))
    prefix = _dot(tri_ref[...], cmat.astype(_bf16)) + cnt_acc[...]
    rank0 = jnp.sum(jnp.where(oh0, prefix, 0.0), axis=-1, keepdims=True)
    rank1 = jnp.sum(jnp.where(oh1, prefix, 0.0), axis=-1, keepdims=True)
    cnt_new = cnt_acc[...] + jnp.sum(cmat, axis=0, keepdims=True)
    cnt_acc[...] = cnt_new
    cnt_ref[...] = cnt_new

    lane8 = lax.broadcasted_iota(jnp.int32, (TM, ROUTE_W), 1)
    rec = jnp.zeros((TM, ROUTE_W), _f32)
    for idx, val in enumerate((e0, e1, gate0, gate1, rank0, rank1)):
        rec = jnp.where(lane8 == idx, val, rec)
    route_ref[...] = rec


def _row_tile(ref, row):
    return ref.at[pl.ds(pl.multiple_of(row * ROW_CHUNKS, ROW_CHUNKS), ROW_CHUNKS), :]


def _dispatch_rows(hn2_rows, pos0, pos1, n_rows):
    n_tok = hn2_rows.shape[0]
    info = plsc.get_sparse_core_info()
    n_workers = info.num_cores * info.num_subcores
    per_worker = n_tok // n_workers
    assert per_worker * n_workers == n_tok and per_worker % SC_WINDOW == 0
    mesh = plsc.VectorSubcoreMesh(core_axis_name="core", subcore_axis_name="subcore")

    @functools.partial(
        pl.kernel,
        out_type=jax.ShapeDtypeStruct((n_rows, D_MODEL), _f32),
        mesh=mesh,
        scratch_types=[pltpu.VMEM((SC_WINDOW,), jnp.int32), pltpu.VMEM((SC_WINDOW,), jnp.int32),
                       pltpu.VMEM((SC_WINDOW, D_MODEL), _f32)],
        compiler_params=pltpu.CompilerParams(use_tc_tiling_on_sc=True),
        name="dispatch",
    )
    def dispatch(x_hbm, i0_hbm, i1_hbm, o_hbm, i0_v, i1_v, rows_v):
        wid = lax.axis_index("subcore") * info.num_cores + lax.axis_index("core")
        base = wid * per_worker

        @pl.loop(0, per_worker // SC_WINDOW)
        def _(c):
            off = pl.multiple_of(base + c * SC_WINDOW, SC_WINDOW)
            pltpu.sync_copy(i0_hbm.at[pl.ds(off, SC_WINDOW)], i0_v)
            pltpu.sync_copy(i1_hbm.at[pl.ds(off, SC_WINDOW)], i1_v)
            pltpu.sync_copy(x_hbm.at[pl.ds(off, SC_WINDOW)], rows_v)
            pltpu.sync_copy(rows_v, o_hbm.at[i0_v])
            pltpu.sync_copy(rows_v, o_hbm.at[i1_v])

    return dispatch(hn2_rows, pos0, pos1)


def _experts_kernel(te_ref, tn_ref,
                    xs_ref, wg_ref, wu_ref, wd_ref,
                    y_ref,
                    wgb, wub, wdb):
    i = pl.program_id(0)
    valid = tn_ref[i] > 0
    changed = jnp.logical_or(i == 0, te_ref[i] != te_ref[jnp.maximum(i - 1, 0)])

    @pl.when(jnp.logical_and(changed, valid))
    def _():
        wgb[...] = wg_ref[...].astype(_bf16)
        wub[...] = wu_ref[...].astype(_bf16)
        wdb[...] = wd_ref[...].astype(_bf16)

    @pl.when(valid)
    def _():
        xb = xs_ref[...].astype(_bf16)
        live = lax.broadcasted_iota(jnp.int32, (TME, 1), 0) < tn_ref[i]
        xb = jnp.where(live, xb, jnp.zeros_like(xb))
        gate = _dot(xb, wgb[...])
        up = _dot(xb, wub[...])
        act = (gate * jax.nn.sigmoid(gate) * up).astype(_bf16)
        y = _dot(act, wdb[...])
        for cc in range(ROW_CHUNKS):
            y_ref[pl.ds(cc, TME, stride=ROW_CHUNKS), :] = y[:, cc * LANES:(cc + 1) * LANES]

    @pl.when(jnp.logical_not(valid))
    def _():
        y_ref[...] = jnp.zeros_like(y_ref)


def _combine_kernel(pos_ref,
                    y_hbm, h2_ref, route_ref,
                    out_ref,
                    ybuf, sem):
    i = pl.program_id(0)
    n = pl.num_programs(0)
    slot = i % 2

    def issue(tile, sl):
        base = tile * (2 * TMC)

        def body(it, carry):
            for j in range(DMA_UNROLL):
                r = it * DMA_UNROLL + j
                for k in range(2):
                    pltpu.make_async_copy(_row_tile(y_hbm, pos_ref[base + 2 * r + k]),
                                          _row_tile(ybuf.at[sl, k], r), sem.at[sl]).start(priority=k)
            return carry

        lax.fori_loop(0, TMC // DMA_UNROLL, body, 0)

    @pl.when(i == 0)
    def _():
        issue(0, 0)

    @pl.when(i + 1 < n)
    def _():
        issue(i + 1, 1 - slot)

    for k in range(2):
        pltpu.make_async_copy(y_hbm.at[pl.ds(0, TMC * ROW_CHUNKS), :], ybuf.at[slot, k], sem.at[slot]).wait()
    g0 = route_ref[:, 2:3]
    g1 = route_ref[:, 3:4]
    for cc in range(ROW_CHUNKS):
        y0 = ybuf[slot, 0, pl.ds(cc, TMC, stride=ROW_CHUNKS), :]
        y1 = ybuf[slot, 1, pl.ds(cc, TMC, stride=ROW_CHUNKS), :]
        cols = slice(cc * LANES, (cc + 1) * LANES)
        out_ref[:, cols] = h2_ref[:, cols] + (g0 * y0 + g1 * y1)


def _t5_bucket(n):
    max_exact = N_BUCKETS // 2
    nf = jnp.maximum(n, 1).astype(_f32)
    large = max_exact + (jnp.log(nf / max_exact) / np.log(MAX_DISTANCE / max_exact)
                         * (N_BUCKETS - max_exact)).astype(jnp.int32)
    large = jnp.minimum(large, N_BUCKETS - 1)
    return jnp.where(n < max_exact, n, large)


def _bucket_maps():
    qi = jnp.arange(BLK)[:, None]
    sj = jnp.arange(2 * BLK)[None, :]
    dist = BLK + qi - sj
    band = (dist >= 0) & (dist < BLK)
    bucket = _t5_bucket(jnp.maximum(dist, 0))
    generic = jnp.where(band, bucket, -1)
    first = jnp.where(band & (sj >= PAD), bucket, -1)
    return jnp.stack([first, generic]).astype(jnp.int32)


def _const(shape):
    nd = len(shape)
    return pl.BlockSpec(shape, lambda *_: (0,) * nd)


def _block_diag_ones(n):
    idx = np.arange(n) // HEAD_DIM
    return jnp.asarray((idx[:, None] == idx[None, :]).astype(np.float32), dtype=_bf16)


def kernel(x, meta_tokens, rel_bias, mix_norm_g, w_in, q_norm_g, k_norm_g, attn_sinks, conv_w, attn_out_norm_g, conv_out_norm_g, w_out, ffn_norm_g, w_group_router, b_group_router, w_expert_router, b_expert_router, w_gate, w_up, w_down):
    bsz, seq, _ = x.shape
    assert seq % TM == 0 and (bsz * seq) % TMC == 0
    n_tok = bsz * seq
    nt = seq // TM

    win = w_in[0].astype(_bf16)
    wout = w_out[0].astype(_bf16)
    mixg = mix_norm_g[0].reshape(1, D_MODEL)
    fg = ffn_norm_g[0].reshape(1, D_MODEL)
    qg = (jnp.tile(q_norm_g[0], N_Q_HEADS) * (HEAD_DIM ** -0.5)).reshape(1, ATTN_DIM)
    kg = jnp.tile(k_norm_g[0], N_KV_HEADS).reshape(1, KV_DIM)
    ag = attn_out_norm_g[0].reshape(1, ATTN_DIM)
    cg = conv_out_norm_g[0].reshape(1, CONV_CH)
    convw = conv_w[0]
    sinks = attn_sinks[0]
    w_r = jnp.concatenate([w_group_router[0], w_expert_router[0].reshape(D_MODEL, N_EXPERTS)], axis=1)
    n_r = N_GROUPS + N_EXPERTS
    w_r_hi = w_r.astype(_bf16)
    w_r_lo = (w_r - w_r_hi.astype(_f32)).astype(_bf16)
    wr = jnp.zeros((D_MODEL, LANES), _bf16)
    wr = wr.at[:, :n_r].set(w_r_hi).at[:, HEAD_DIM:HEAD_DIM + n_r].set(w_r_lo)
    br = jnp.zeros((1, LANES), _f32).at[0, :n_r].set(
        jnp.concatenate([b_group_router[0], b_expert_router[0].reshape(N_EXPERTS)]))
    meta_blk = jnp.concatenate([jnp.zeros((PAD, D_MODEL), x.dtype), meta_tokens.astype(x.dtype)], axis=0)
    bdq = _block_diag_ones(ATTN_DIM)
    bdk = _block_diag_ones(KV_DIM)
    tri = jnp.asarray(np.tril(np.ones((TM, TM), np.float32), -1), dtype=_bf16)

    kv_sd = jax.ShapeDtypeStruct((BLK, KV_DIM), _bf16)
    k0m, k1m, v0m, v1m, utm, bias = pl.pallas_call(
        _prep_kernel,
        out_shape=(kv_sd, kv_sd, kv_sd, kv_sd,
                   jax.ShapeDtypeStruct((SUBLANES, CONV_CH), _f32),
                   jax.ShapeDtypeStruct((2, N_KV_HEADS, Q_GROUP * BLK, 2 * BLK), _f32)),
        in_specs=[pl.BlockSpec(memory_space=pltpu.SMEM)] + [pl.BlockSpec(memory_space=pltpu.VMEM)] * 6,
        out_specs=tuple(pl.BlockSpec(memory_space=pltpu.VMEM) for _ in range(6)),
        compiler_params=pltpu.CompilerParams(vmem_limit_bytes=VMEM_LIMIT),
        name="prep",
    )(rel_bias, meta_blk, mixg, win, kg, bdk, _bucket_maps())

    tile_idx = lambda b, t, *_: (b * nt + t, 0)
    grid_spec = pltpu.PrefetchScalarGridSpec(
        num_scalar_prefetch=1,
        grid=(bsz, nt),
        in_specs=[
            pl.BlockSpec((None, TM, D_MODEL), lambda b, t, *_: (b, t, 0)),
            _const((1, D_MODEL)), _const((D_MODEL, IN_PROJ)), _const((1, ATTN_DIM)), _const((1, KV_DIM)),
            _const((2, N_KV_HEADS, Q_GROUP * BLK, 2 * BLK)), _const((3, CONV_CH)),
            _const((1, ATTN_DIM)), _const((1, CONV_CH)), _const((D_MODEL, D_MODEL)), _const((1, D_MODEL)),
            _const((D_MODEL, LANES)), _const((1, LANES)),
            _const((BLK, KV_DIM)), _const((BLK, KV_DIM)), _const((BLK, KV_DIM)), _const((BLK, KV_DIM)),
            _const((SUBLANES, CONV_CH)),
            _const((ATTN_DIM, ATTN_DIM)), _const((KV_DIM, KV_DIM)), _const((TM, TM)),
        ],
        out_specs=[
            pl.BlockSpec((None, TM, D_MODEL), lambda b, t, *_: (b, t, 0)),
            pl.BlockSpec((TM, D_MODEL), tile_idx),
            pl.BlockSpec((TM, ROUTE_W), tile_idx),
            _const((1, LANES)),
        ],
        scratch_shapes=[
            pltpu.VMEM((BLK, KV_DIM), _bf16), pltpu.VMEM((BLK, KV_DIM), _bf16),
            pltpu.VMEM((BLK, KV_DIM), _bf16), pltpu.VMEM((BLK, KV_DIM), _bf16),
            pltpu.VMEM((TM + SUBLANES, CONV_CH), _f32),
            pltpu.VMEM((TM, ATTN_DIM), _f32),
            pltpu.VMEM((1, LANES), _f32),
        ],
    )
    h2, hn2, route, cnt = pl.pallas_call(
        _mixer_kernel,
        grid_spec=grid_spec,
        out_shape=(jax.ShapeDtypeStruct((bsz, seq, D_MODEL), _f32),
                   jax.ShapeDtypeStruct((n_tok, D_MODEL), _f32),
                   jax.ShapeDtypeStruct((n_tok, ROUTE_W), _f32),
                   jax.ShapeDtypeStruct((1, LANES), _f32)),
        compiler_params=pltpu.CompilerParams(dimension_semantics=("arbitrary", "arbitrary"),
                                             vmem_limit_bytes=VMEM_LIMIT),
        name="mixer",
    )(sinks, x, mixg, win, qg, kg, bias, convw, ag, cg, wout, fg, wr, br,
      k0m, k1m, v0m, v1m, utm, bdq, bdk, tri)

    n_tiles = (n_tok * 2) // TME + N_EXPERTS
    counts = cnt[0, :N_EXPERTS].astype(jnp.int32)
    ntile = (counts + TME - 1) // TME
    tile_end = jnp.cumsum(ntile)
    tile_start = tile_end - ntile
    eid = route[:, 0:2].astype(jnp.int32)
    rank = route[:, 4:6].astype(jnp.int32)
    experts = jnp.arange(N_EXPERTS, dtype=jnp.int32)
    start_of = jnp.sum(jnp.where(eid[:, :, None] == experts, tile_start, 0), axis=-1)
    pos2 = start_of * TME + rank
    pos = pos2.reshape(-1)
    tiles = jnp.arange(n_tiles, dtype=jnp.int32)
    n_used = tile_end[-1]
    t_exp = jnp.sum((jnp.minimum(tiles, n_used - 1)[:, None] >= tile_end[None, :]).astype(jnp.int32), axis=-1)
    t_exp = jnp.minimum(t_exp, N_EXPERTS - 1)
    own = t_exp[:, None] == experts
    t_rows = jnp.sum(jnp.where(own, counts - (tiles[:, None] - tile_start) * TME, 0), axis=-1)
    t_rows = jnp.where(tiles < n_used, jnp.clip(t_rows, 0, TME), 0).astype(jnp.int32)

    xs = _dispatch_rows(hn2, pos2[:, 0], pos2[:, 1], n_tiles * TME)

    y_sorted = pl.pallas_call(
        _experts_kernel,
        grid_spec=pltpu.PrefetchScalarGridSpec(
            num_scalar_prefetch=2,
            grid=(n_tiles,),
            in_specs=[
                pl.BlockSpec((TME, D_MODEL), lambda i, *_: (i, 0)),
                pl.BlockSpec((None, D_MODEL, D_EXPERT), lambda i, te, tn: (te[i], 0, 0)),
                pl.BlockSpec((None, D_MODEL, D_EXPERT), lambda i, te, tn: (te[i], 0, 0)),
                pl.BlockSpec((None, D_EXPERT, D_MODEL), lambda i, te, tn: (te[i], 0, 0)),
            ],
            out_specs=pl.BlockSpec((TME * ROW_CHUNKS, LANES), lambda i, *_: (i, 0)),
            scratch_shapes=[
                pltpu.VMEM((D_MODEL, D_EXPERT), _bf16),
                pltpu.VMEM((D_MODEL, D_EXPERT), _bf16),
                pltpu.VMEM((D_EXPERT, D_MODEL), _bf16),
            ],
        ),
        out_shape=jax.ShapeDtypeStruct((n_tiles * TME * ROW_CHUNKS, LANES), _f32),
        compiler_params=pltpu.CompilerParams(dimension_semantics=("arbitrary",),
                                             vmem_limit_bytes=VMEM_LIMIT),
        name="experts",
    )(t_exp, t_rows, xs, w_gate[0], w_up[0], w_down[0])

    out = pl.pallas_call(
        _combine_kernel,
        grid_spec=pltpu.PrefetchScalarGridSpec(
            num_scalar_prefetch=1,
            grid=(n_tok // TMC,),
            in_specs=[
                pl.BlockSpec(memory_space=pl.ANY),
                pl.BlockSpec((TMC, D_MODEL), lambda i, *_: (i, 0)),
                pl.BlockSpec((TMC, ROUTE_W), lambda i, *_: (i, 0)),
            ],
            out_specs=pl.BlockSpec((TMC, D_MODEL), lambda i, *_: (i, 0)),
            scratch_shapes=[
                pltpu.VMEM((2, 2, TMC * ROW_CHUNKS, LANES), _f32),
                pltpu.SemaphoreType.DMA((2,)),
            ],
        ),
        out_shape=jax.ShapeDtypeStruct((n_tok, D_MODEL), _f32),
        compiler_params=pltpu.CompilerParams(dimension_semantics=("arbitrary",),
                                             vmem_limit_bytes=VMEM_LIMIT),
        name="combine",
    )(pos, y_sorted, h2.reshape(n_tok, D_MODEL), route)
    return out.reshape(bsz, seq, D_MODEL)
```

```python
import functools

import numpy as np
import jax
import jax.numpy as jnp
from jax import lax
from jax.experimental import pallas as pl
from jax.experimental.pallas import tpu as pltpu
from jax.experimental.pallas import tpu_sc as plsc

D_MODEL = 1024
N_META = 16
N_Q_HEADS = 8
N_KV_HEADS = 2
HEAD_DIM = 64
Q_GROUP = N_Q_HEADS // N_KV_HEADS
ATTN_DIM = N_Q_HEADS * HEAD_DIM
KV_DIM = N_KV_HEADS * HEAD_DIM
BLK = 128
PAD = BLK - N_META
N_BUCKETS = 32
MAX_DISTANCE = 128
CONV_CH = D_MODEL // 2
IN_PROJ = ATTN_DIM + 2 * KV_DIM + 3 * CONV_CH
N_GROUPS = 4
EXPERTS_PER_GROUP = 8
N_EXPERTS = N_GROUPS * EXPERTS_PER_GROUP
D_EXPERT = D_MODEL // 2
EPS = 1e-6
NEG_INF = -1e30

LANES = 128
SUBLANES = 8
ROW_CHUNKS = D_MODEL // LANES
TM = 512
TME = 256
TMC = 256
SC_WINDOW = 32
DMA_UNROLL = 8
ROUTE_W = 8
VMEM_LIMIT = 56 * 1024 * 1024

Q_OFF, K_OFF, V_OFF = 0, ATTN_DIM, ATTN_DIM + KV_DIM
CB_OFF = ATTN_DIM + 2 * KV_DIM
CC_OFF = CB_OFF + CONV_CH
CH_OFF = CC_OFF + CONV_CH

_f32 = jnp.float32
_bf16 = jnp.bfloat16


def _rms(x, g):
    return x * lax.rsqrt(jnp.mean(x * x, axis=-1, keepdims=True) + EPS) * g


def _dot(a, b):
    return jnp.dot(a, b, preferred_element_type=_f32)


def _dup_halves(x):
    lane = lax.broadcasted_iota(jnp.int32, x.shape, 1)
    sw = pltpu.roll(x, HEAD_DIM, axis=1)
    lo = lane < HEAD_DIM
    return jnp.where(lo, x, sw).astype(_bf16), jnp.where(lo, sw, x).astype(_bf16)


def _kv_state(hn_bf, win_ref, kg_ref, bdk_ref):
    kv = _dot(hn_bf, win_ref[:, K_OFF:K_OFF + 2 * KV_DIM])
    k = kv[:, :KV_DIM]
    v = kv[:, KV_DIM:]
    ssk = _dot((k * k).astype(_bf16), bdk_ref[...])
    kn = k * lax.rsqrt(ssk * (1.0 / HEAD_DIM) + EPS) * kg_ref[...]
    return _dup_halves(kn) + _dup_halves(v)


def _prep_kernel(rb_ref, meta_ref, mixg_ref, win_ref, kg_ref, bdk_ref, bucket_ref,
                 k0_ref, k1_ref, v0_ref, v1_ref, ut_ref, bias_ref):
    hn = _rms(meta_ref[...], mixg_ref[...]).astype(_bf16)
    k0, k1, v0, v1 = _kv_state(hn, win_ref, kg_ref, bdk_ref)
    k0_ref[...] = k0
    k1_ref[...] = k1
    v0_ref[...] = v0
    v1_ref[...] = v1
    cch = _dot(hn, win_ref[:, CC_OFF:CC_OFF + 2 * CONV_CH])
    u = cch[:, :CONV_CH] * cch[:, CONV_CH:]
    ut_ref[...] = u[BLK - SUBLANES:, :]
    for f in range(2):
        bk = bucket_ref[f]
        for h in range(N_Q_HEADS):
            acc = jnp.full((BLK, 2 * BLK), NEG_INF, _f32)
            for b in range(N_BUCKETS):
                acc = jnp.where(bk == b, rb_ref[b, h], acc)
            bias_ref[f, h // Q_GROUP, (h % Q_GROUP) * BLK:(h % Q_GROUP + 1) * BLK, :] = acc


def _mixer_kernel(sinks_ref,
                  x_ref, mixg_ref, win_ref, qg_ref, kg_ref, bias_ref, convw_ref, ag_ref, cg_ref,
                  wout_ref, fg_ref, wr_ref, br_ref, k0m_ref, k1m_ref, v0m_ref, v1m_ref, utm_ref,
                  bdq_ref, bdk_ref, tri_ref,
                  h2_ref, hn2_ref, route_ref, route_t_ref, cnt_ref,
                  kp0, kp1, vp0, vp1, ubuf, a_scr, cnt_acc):
    b = pl.program_id(0)
    t = pl.program_id(1)

    @pl.when(t == 0)
    def _():
        kp0[...] = k0m_ref[...]
        kp1[...] = k1m_ref[...]
        vp0[...] = v0m_ref[...]
        vp1[...] = v1m_ref[...]
        ubuf[0:SUBLANES, :] = utm_ref[...]

    @pl.when(jnp.logical_and(b == 0, t == 0))
    def _():
        cnt_acc[...] = jnp.zeros_like(cnt_acc)

    x = x_ref[...]
    hn = _rms(x, mixg_ref[...]).astype(_bf16)

    q = _dot(hn, win_ref[:, Q_OFF:Q_OFF + ATTN_DIM])
    ssq = _dot((q * q).astype(_bf16), bdq_ref[...])
    qn = (q * lax.rsqrt(ssq * (1.0 / HEAD_DIM) + EPS) * qg_ref[...]).astype(_bf16)
    kd0, kd1, vd0, vd1 = _kv_state(hn, win_ref, kg_ref, bdk_ref)
    kd = (kd0, kd1)
    vd = (vd0, vd1)
    kp = (kp0, kp1)
    vp = (vp0, vp1)

    lane_q = lax.broadcasted_iota(jnp.int32, (BLK, LANES), 1)
    lo_half = lane_q < HEAD_DIM
    row4 = lax.broadcasted_iota(jnp.int32, (Q_GROUP * BLK, 1), 0) // BLK
    first = jnp.where(t == 0, 0, 1)
    zero_bf = jnp.zeros((BLK, LANES), _bf16)

    for j in range(TM // BLK):
        rows = slice(j * BLK, (j + 1) * BLK)
        for g in range(N_KV_HEADS):
            if j == 0:
                kcat = jnp.concatenate([kp[g][...], kd[g][rows]], axis=0)
                vcat = jnp.concatenate([vp[g][...], vd[g][rows]], axis=0)
                bias = bias_ref[first, g]
            else:
                kcat = kd[g][(j - 1) * BLK:(j + 1) * BLK]
                vcat = vd[g][(j - 1) * BLK:(j + 1) * BLK]
                bias = bias_ref[1, g]
            qs = []
            for hh in range(Q_GROUP):
                h = g * Q_GROUP + hh
                qc = qn[rows, (h // 2) * LANES:(h // 2 + 1) * LANES]
                keep = lo_half if h % 2 == 0 else jnp.logical_not(lo_half)
                qs.append(jnp.where(keep, qc, zero_bf))
            q4 = jnp.concatenate(qs, axis=0)
            s = lax.dot_general(q4, kcat, (((1,), (1,)), ((), ())),
                                preferred_element_type=_f32) + bias
            sink = jnp.full((Q_GROUP * BLK, 1), sinks_ref[g * Q_GROUP], _f32)
            for hh in range(1, Q_GROUP):
                sink = jnp.where(row4 == hh, sinks_ref[g * Q_GROUP + hh], sink)
            m = jnp.maximum(jnp.max(s, axis=-1, keepdims=True), sink)
            p = jnp.exp(s - m)
            l = jnp.sum(p, axis=-1, keepdims=True) + jnp.exp(sink - m)
            o = _dot(p.astype(_bf16), vcat) / l
            for pp in range(Q_GROUP // 2):
                ev = o[(2 * pp) * BLK:(2 * pp + 1) * BLK]
                od = o[(2 * pp + 1) * BLK:(2 * pp + 2) * BLK]
                col = g * (Q_GROUP // 2) + pp
                a_scr[rows, col * LANES:(col + 1) * LANES] = jnp.where(lo_half, ev, od)

    last = slice(TM - BLK, TM)
    kp0[...] = kd0[last]
    kp1[...] = kd1[last]
    vp0[...] = vd0[last]
    vp1[...] = vd1[last]

    cb = _dot(hn, win_ref[:, CB_OFF:CB_OFF + CONV_CH])
    cch = _dot(hn, win_ref[:, CC_OFF:CC_OFF + 2 * CONV_CH])
    u = cch[:, :CONV_CH] * cch[:, CONV_CH:]
    ubuf[SUBLANES:, :] = u
    u1 = ubuf[SUBLANES - 1:SUBLANES - 1 + TM, :]
    u2 = ubuf[SUBLANES - 2:SUBLANES - 2 + TM, :]
    c = cb * (convw_ref[0:1, :] * u2 + convw_ref[1:2, :] * u1 + convw_ref[2:3, :] * u)
    ubuf[0:SUBLANES, :] = u[TM - SUBLANES:, :]

    an = _rms(a_scr[...], ag_ref[...]).astype(_bf16)
    cn = _rms(c, cg_ref[...]).astype(_bf16)
    h2 = x + _dot(an, wout_ref[0:ATTN_DIM, :]) + _dot(cn, wout_ref[ATTN_DIM:, :])
    h2_ref[...] = h2

    hn2 = _rms(h2, fg_ref[...])
    hn2_ref[...] = hn2

    hi = hn2.astype(_bf16)
    lo = (hn2 - hi.astype(_f32)).astype(_bf16)
    r1 = _dot(hi, wr_ref[...])
    r2 = _dot(lo, wr_ref[...])
    lg = r1 + pltpu.roll(r1, HEAD_DIM, axis=1) + r2 + br_ref[...]
    lane = lax.broadcasted_iota(jnp.int32, (TM, LANES), 1)
    lanef = lane.astype(_f32)
    ninf = jnp.float32(-jnp.inf)
    big = jnp.float32(LANES)

    gl = jnp.where(lane < N_GROUPS, lg, ninf)
    gmax = jnp.max(gl, axis=-1, keepdims=True)
    gsum = jnp.sum(jnp.exp(gl - gmax), axis=-1, keepdims=True)
    g_p = 1.0 / gsum
    g_idx = jnp.min(jnp.where(gl == gmax, lanef, big), axis=-1, keepdims=True)
    e_lo = N_GROUPS + EXPERTS_PER_GROUP * g_idx
    el = jnp.where(jnp.logical_and(lanef >= e_lo, lanef < e_lo + EXPERTS_PER_GROUP), lg, ninf)
    m1 = jnp.max(el, axis=-1, keepdims=True)
    i1 = jnp.min(jnp.where(el == m1, lanef, big), axis=-1, keepdims=True)
    el2 = jnp.where(lanef == i1, ninf, el)
    m2 = jnp.max(el2, axis=-1, keepdims=True)
    i2 = jnp.min(jnp.where(el2 == m2, lanef, big), axis=-1, keepdims=True)
    ex = jnp.exp(m2 - m1)
    den = 1.0 / (1.0 + ex)
    gate0 = g_p * den
    gate1 = g_p * ex * den
    e0 = i1 - N_GROUPS
    e1 = i2 - N_GROUPS

    oh0 = lanef == e0
    oh1 = lanef == e1
    cmat = (jnp.where(oh0, 1.0, 0.0) + jnp.where(oh1, 1.0, 0.0))
    prefix = _dot(tri_ref[...], cmat.astype(_bf16)) + cnt_acc[...]
    rank0 = jnp.sum(jnp.where(oh0, prefix, 0.0), axis=-1, keepdims=True)
    rank1 = jnp.sum(jnp.where(oh1, prefix, 0.0), axis=-1, keepdims=True)
    cnt_new = cnt_acc[...] + jnp.sum(cmat, axis=0, keepdims=True)
    cnt_acc[...] = cnt_new
    cnt_ref[...] = cnt_new

    lane8 = lax.broadcasted_iota(jnp.int32, (TM, ROUTE_W), 1)
    rec = jnp.zeros((TM, ROUTE_W), _f32)
    for idx, val in enumerate((e0, e1, gate0, gate1, rank0, rank1)):
        rec = jnp.where(lane8 == idx, val, rec)
    route_ref[...] = rec
    lane_r = lax.broadcasted_iota(jnp.int32, (TM, LANES), 1)
    wide = jnp.zeros((TM, LANES), _f32)
    for idx, val in enumerate((e0, e1, rank0, rank1)):
        wide = jnp.where(lane_r == idx, val, wide)
    route_t_ref[...] = wide.T[:SUBLANES, :]


def _row_tile(ref, row):
    return ref.at[pl.ds(pl.multiple_of(row * ROW_CHUNKS, ROW_CHUNKS), ROW_CHUNKS), :]


def _dispatch_rows(hn2_rows, pos0, pos1, n_rows):
    n_tok = hn2_rows.shape[0]
    info = plsc.get_sparse_core_info()
    n_workers = info.num_cores * info.num_subcores
    per_worker = n_tok // n_workers
    n_chunks = per_worker // SC_WINDOW
    assert per_worker * n_workers == n_tok and n_chunks * SC_WINDOW == per_worker and n_chunks % 2 == 0
    mesh = plsc.VectorSubcoreMesh(core_axis_name="core", subcore_axis_name="subcore")

    @functools.partial(
        pl.kernel,
        out_type=jax.ShapeDtypeStruct((n_rows, D_MODEL), _f32),
        mesh=mesh,
        scratch_types=[pltpu.VMEM((SC_WINDOW,), jnp.int32), pltpu.VMEM((SC_WINDOW,), jnp.int32),
                       pltpu.VMEM((SC_WINDOW,), jnp.int32), pltpu.VMEM((SC_WINDOW,), jnp.int32),
                       pltpu.VMEM((SC_WINDOW, D_MODEL), _f32), pltpu.VMEM((SC_WINDOW, D_MODEL), _f32),
                       pltpu.SemaphoreType.DMA((2,)), pltpu.SemaphoreType.DMA((2,))],
        compiler_params=pltpu.CompilerParams(use_tc_tiling_on_sc=True),
        name="dispatch",
    )
    def dispatch(x_hbm, i0_hbm, i1_hbm, o_hbm, i0_a, i0_b, i1_a, i1_b, rows_a, rows_b, sem_ld, sem_st):
        wid = lax.axis_index("subcore") * info.num_cores + lax.axis_index("core")
        base = wid * per_worker
        i0_v, i1_v, rows_v = (i0_a, i0_b), (i1_a, i1_b), (rows_a, rows_b)

        def loads(chunk, b):
            off = pl.multiple_of(base + chunk * SC_WINDOW, SC_WINDOW)
            return (pltpu.make_async_copy(i0_hbm.at[pl.ds(off, SC_WINDOW)], i0_v[b], sem_ld.at[b]),
                    pltpu.make_async_copy(i1_hbm.at[pl.ds(off, SC_WINDOW)], i1_v[b], sem_ld.at[b]),
                    pltpu.make_async_copy(x_hbm.at[pl.ds(off, SC_WINDOW)], rows_v[b], sem_ld.at[b]))

        def stores(b):
            return (pltpu.make_async_copy(rows_v[b], o_hbm.at[i0_v[b]], sem_st.at[b]),
                    pltpu.make_async_copy(rows_v[b], o_hbm.at[i1_v[b]], sem_st.at[b]))

        for d in loads(0, 0):
            d.start()

        @pl.loop(0, n_chunks, step=2)
        def _(c):
            for b in range(2):
                chunk = c + b
                for d in loads(chunk, b):
                    d.wait()
                for d in stores(b):
                    d.start()

                @pl.when(chunk >= 1)
                def _():
                    for d in stores(1 - b):
                        d.wait()

                @pl.when(chunk + 1 < n_chunks)
                def _():
                    for d in loads(chunk + 1, 1 - b):
                        d.start()

        for d in stores(1):
            d.wait()

    return dispatch(hn2_rows, pos0, pos1)


def _experts_kernel(te_ref, tn_ref,
                    xs_ref, wg_ref, wu_ref, wd_ref,
                    y_ref,
                    wgb, wub, wdb):
    i = pl.program_id(0)
    valid = tn_ref[i] > 0
    changed = jnp.logical_or(i == 0, te_ref[i] != te_ref[jnp.maximum(i - 1, 0)])

    @pl.when(jnp.logical_and(changed, valid))
    def _():
        wgb[...] = wg_ref[...].astype(_bf16)
        wub[...] = wu_ref[...].astype(_bf16)
        wdb[...] = wd_ref[...].astype(_bf16)

    @pl.when(valid)
    def _():
        xb = xs_ref[...].astype(_bf16)
        live = lax.broadcasted_iota(jnp.int32, (TME, 1), 0) < tn_ref[i]
        xb = jnp.where(live, xb, jnp.zeros_like(xb))
        gate = _dot(xb, wgb[...])
        up = _dot(xb, wub[...])
        act = (gate * jax.nn.sigmoid(gate) * up).astype(_bf16)
        y = _dot(act, wdb[...])
        for cc in range(ROW_CHUNKS):
            y_ref[pl.ds(cc, TME, stride=ROW_CHUNKS), :] = y[:, cc * LANES:(cc + 1) * LANES]

    @pl.when(jnp.logical_not(valid))
    def _():
        y_ref[...] = jnp.zeros_like(y_ref)


def _combine_kernel(pos0_ref, pos1_ref,
                    y_hbm, h2_ref, route_ref,
                    out_ref,
                    ybuf, sem):
    i = pl.program_id(0)
    n = pl.num_programs(0)
    slot = i % 2

    def issue(tile, sl):
        base = tile * TMC
        pos_refs = (pos0_ref, pos1_ref)

        def body(it, carry):
            for j in range(DMA_UNROLL):
                r = it * DMA_UNROLL + j
                for k in range(2):
                    pltpu.make_async_copy(_row_tile(y_hbm, pos_refs[k][base + r]),
                                          _row_tile(ybuf.at[sl, k], r), sem.at[sl]).start(priority=k)
            return carry

        lax.fori_loop(0, TMC // DMA_UNROLL, body, 0)

    @pl.when(i == 0)
    def _():
        issue(0, 0)

    @pl.when(i + 1 < n)
    def _():
        issue(i + 1, 1 - slot)

    for k in range(2):
        pltpu.make_async_copy(y_hbm.at[pl.ds(0, TMC * ROW_CHUNKS), :], ybuf.at[slot, k], sem.at[slot]).wait()
    g0 = route_ref[:, 2:3]
    g1 = route_ref[:, 3:4]
    for cc in range(ROW_CHUNKS):
        y0 = ybuf[slot, 0, pl.ds(cc, TMC, stride=ROW_CHUNKS), :]
        y1 = ybuf[slot, 1, pl.ds(cc, TMC, stride=ROW_CHUNKS), :]
        cols = slice(cc * LANES, (cc + 1) * LANES)
        out_ref[:, cols] = h2_ref[:, cols] + (g0 * y0 + g1 * y1)


def _t5_bucket(n):
    max_exact = N_BUCKETS // 2
    nf = jnp.maximum(n, 1).astype(_f32)
    large = max_exact + (jnp.log(nf / max_exact) / np.log(MAX_DISTANCE / max_exact)
                         * (N_BUCKETS - max_exact)).astype(jnp.int32)
    large = jnp.minimum(large, N_BUCKETS - 1)
    return jnp.where(n < max_exact, n, large)


def _bucket_maps():
    qi = jnp.arange(BLK)[:, None]
    sj = jnp.arange(2 * BLK)[None, :]
    dist = BLK + qi - sj
    band = (dist >= 0) & (dist < BLK)
    bucket = _t5_bucket(jnp.maximum(dist, 0))
    generic = jnp.where(band, bucket, -1)
    first = jnp.where(band & (sj >= PAD), bucket, -1)
    return jnp.stack([first, generic]).astype(jnp.int32)


def _const(shape):
    nd = len(shape)
    return pl.BlockSpec(shape, lambda *_: (0,) * nd)


def _block_diag_ones(n):
    idx = np.arange(n) // HEAD_DIM
    return jnp.asarray((idx[:, None] == idx[None, :]).astype(np.float32), dtype=_bf16)


def kernel(x, meta_tokens, rel_bias, mix_norm_g, w_in, q_norm_g, k_norm_g, attn_sinks, conv_w, attn_out_norm_g, conv_out_norm_g, w_out, ffn_norm_g, w_group_router, b_group_router, w_expert_router, b_expert_router, w_gate, w_up, w_down):
    bsz, seq, _ = x.shape
    assert seq % TM == 0 and (bsz * seq) % TMC == 0
    n_tok = bsz * seq
    nt = seq // TM

    win = w_in[0].astype(_bf16)
    wout = w_out[0].astype(_bf16)
    mixg = mix_norm_g[0].reshape(1, D_MODEL)
    fg = ffn_norm_g[0].reshape(1, D_MODEL)
    qg = (jnp.tile(q_norm_g[0], N_Q_HEADS) * (HEAD_DIM ** -0.5)).reshape(1, ATTN_DIM)
    kg = jnp.tile(k_norm_g[0], N_KV_HEADS).reshape(1, KV_DIM)
    ag = attn_out_norm_g[0].reshape(1, ATTN_DIM)
    cg = conv_out_norm_g[0].reshape(1, CONV_CH)
    convw = conv_w[0]
    sinks = attn_sinks[0]
    w_r = jnp.concatenate([w_group_router[0], w_expert_router[0].reshape(D_MODEL, N_EXPERTS)], axis=1)
    n_r = N_GROUPS + N_EXPERTS
    w_r_hi = w_r.astype(_bf16)
    w_r_lo = (w_r - w_r_hi.astype(_f32)).astype(_bf16)
    wr = jnp.zeros((D_MODEL, LANES), _bf16)
    wr = wr.at[:, :n_r].set(w_r_hi).at[:, HEAD_DIM:HEAD_DIM + n_r].set(w_r_lo)
    br = jnp.zeros((1, LANES), _f32).at[0, :n_r].set(
        jnp.concatenate([b_group_router[0], b_expert_router[0].reshape(N_EXPERTS)]))
    meta_blk = jnp.concatenate([jnp.zeros((PAD, D_MODEL), x.dtype), meta_tokens.astype(x.dtype)], axis=0)
    bdq = _block_diag_ones(ATTN_DIM)
    bdk = _block_diag_ones(KV_DIM)
    tri = jnp.asarray(np.tril(np.ones((TM, TM), np.float32), -1), dtype=_bf16)

    kv_sd = jax.ShapeDtypeStruct((BLK, KV_DIM), _bf16)
    k0m, k1m, v0m, v1m, utm, bias = pl.pallas_call(
        _prep_kernel,
        out_shape=(kv_sd, kv_sd, kv_sd, kv_sd,
                   jax.ShapeDtypeStruct((SUBLANES, CONV_CH), _f32),
                   jax.ShapeDtypeStruct((2, N_KV_HEADS, Q_GROUP * BLK, 2 * BLK), _f32)),
        in_specs=[pl.BlockSpec(memory_space=pltpu.SMEM)] + [pl.BlockSpec(memory_space=pltpu.VMEM)] * 6,
        out_specs=tuple(pl.BlockSpec(memory_space=pltpu.VMEM) for _ in range(6)),
        compiler_params=pltpu.CompilerParams(vmem_limit_bytes=VMEM_LIMIT),
        name="prep",
    )(rel_bias, meta_blk, mixg, win, kg, bdk, _bucket_maps())

    tile_idx = lambda b, t, *_: (b * nt + t, 0)
    grid_spec = pltpu.PrefetchScalarGridSpec(
        num_scalar_prefetch=1,
        grid=(bsz, nt),
        in_specs=[
            pl.BlockSpec((None, TM, D_MODEL), lambda b, t, *_: (b, t, 0)),
            _const((1, D_MODEL)), _const((D_MODEL, IN_PROJ)), _const((1, ATTN_DIM)), _const((1, KV_DIM)),
            _const((2, N_KV_HEADS, Q_GROUP * BLK, 2 * BLK)), _const((3, CONV_CH)),
            _const((1, ATTN_DIM)), _const((1, CONV_CH)), _const((D_MODEL, D_MODEL)), _const((1, D_MODEL)),
            _const((D_MODEL, LANES)), _const((1, LANES)),
            _const((BLK, KV_DIM)), _const((BLK, KV_DIM)), _const((BLK, KV_DIM)), _const((BLK, KV_DIM)),
            _const((SUBLANES, CONV_CH)),
            _const((ATTN_DIM, ATTN_DIM)), _const((KV_DIM, KV_DIM)), _const((TM, TM)),
        ],
        out_specs=[
            pl.BlockSpec((None, TM, D_MODEL), lambda b, t, *_: (b, t, 0)),
            pl.BlockSpec((TM, D_MODEL), tile_idx),
            pl.BlockSpec((TM, ROUTE_W), tile_idx),
            pl.BlockSpec((SUBLANES, TM), lambda b, t, *_: (0, b * nt + t)),
            _const((1, LANES)),
        ],
        scratch_shapes=[
            pltpu.VMEM((BLK, KV_DIM), _bf16), pltpu.VMEM((BLK, KV_DIM), _bf16),
            pltpu.VMEM((BLK, KV_DIM), _bf16), pltpu.VMEM((BLK, KV_DIM), _bf16),
            pltpu.VMEM((TM + SUBLANES, CONV_CH), _f32),
            pltpu.VMEM((TM, ATTN_DIM), _f32),
            pltpu.VMEM((1, LANES), _f32),
        ],
    )
    h2, hn2, route, route_t, cnt = pl.pallas_call(
        _mixer_kernel,
        grid_spec=grid_spec,
        out_shape=(jax.ShapeDtypeStruct((bsz, seq, D_MODEL), _f32),
                   jax.ShapeDtypeStruct((n_tok, D_MODEL), _f32),
                   jax.ShapeDtypeStruct((n_tok, ROUTE_W), _f32),
                   jax.ShapeDtypeStruct((SUBLANES, n_tok), _f32),
                   jax.ShapeDtypeStruct((1, LANES), _f32)),
        compiler_params=pltpu.CompilerParams(dimension_semantics=("arbitrary", "arbitrary"),
                                             vmem_limit_bytes=VMEM_LIMIT),
        name="mixer",
    )(sinks, x, mixg, win, qg, kg, bias, convw, ag, cg, wout, fg, wr, br,
      k0m, k1m, v0m, v1m, utm, bdq, bdk, tri)

    n_tiles = (n_tok * 2) // TME + N_EXPERTS
    counts = cnt[0, :N_EXPERTS].astype(jnp.int32)
    ntile = (counts + TME - 1) // TME
    tile_end = jnp.cumsum(ntile)
    tile_start = tile_end - ntile
    eid = route_t[0:2].astype(jnp.int32)
    rank = route_t[2:4].astype(jnp.int32)
    experts = jnp.arange(N_EXPERTS, dtype=jnp.int32)
    start_of = jnp.sum(jnp.where(eid[None] == experts[:, None, None], tile_start[:, None, None], 0), axis=0)
    pos2 = start_of * TME + rank
    tiles = jnp.arange(n_tiles, dtype=jnp.int32)
    n_used = tile_end[-1]
    t_exp = jnp.sum((jnp.minimum(tiles, n_used - 1)[:, None] >= tile_end[None, :]).astype(jnp.int32), axis=-1)
    t_exp = jnp.minimum(t_exp, N_EXPERTS - 1)
    own = t_exp[:, None] == experts
    t_rows = jnp.sum(jnp.where(own, counts - (tiles[:, None] - tile_start) * TME, 0), axis=-1)
    t_rows = jnp.where(tiles < n_used, jnp.clip(t_rows, 0, TME), 0).astype(jnp.int32)

    xs = _dispatch_rows(hn2, pos2[0], pos2[1], n_tiles * TME)

    y_sorted = pl.pallas_call(
        _experts_kernel,
        grid_spec=pltpu.PrefetchScalarGridSpec(
            num_scalar_prefetch=2,
            grid=(n_tiles,),
            in_specs=[
                pl.BlockSpec((TME, D_MODEL), lambda i, *_: (i, 0)),
                pl.BlockSpec((None, D_MODEL, D_EXPERT), lambda i, te, tn: (te[i], 0, 0)),
                pl.BlockSpec((None, D_MODEL, D_EXPERT), lambda i, te, tn: (te[i], 0, 0)),
                pl.BlockSpec((None, D_EXPERT, D_MODEL), lambda i, te, tn: (te[i], 0, 0)),
            ],
            out_specs=pl.BlockSpec((TME * ROW_CHUNKS, LANES), lambda i, *_: (i, 0)),
            scratch_shapes=[
                pltpu.VMEM((D_MODEL, D_EXPERT), _bf16),
                pltpu.VMEM((D_MODEL, D_EXPERT), _bf16),
                pltpu.VMEM((D_EXPERT, D_MODEL), _bf16),
            ],
        ),
        out_shape=jax.ShapeDtypeStruct((n_tiles * TME * ROW_CHUNKS, LANES), _f32),
        compiler_params=pltpu.CompilerParams(dimension_semantics=("arbitrary",),
                                             vmem_limit_bytes=VMEM_LIMIT),
        name="experts",
    )(t_exp, t_rows, xs, w_gate[0], w_up[0], w_down[0])

    out = pl.pallas_call(
        _combine_kernel,
        grid_spec=pltpu.PrefetchScalarGridSpec(
            num_scalar_prefetch=2,
            grid=(n_tok // TMC,),
            in_specs=[
                pl.BlockSpec(memory_space=pl.ANY),
                pl.BlockSpec((TMC, D_MODEL), lambda i, *_: (i, 0)),
                pl.BlockSpec((TMC, ROUTE_W), lambda i, *_: (i, 0)),
            ],
            out_specs=pl.BlockSpec((TMC, D_MODEL), lambda i, *_: (i, 0)),
            scratch_shapes=[
                pltpu.VMEM((2, 2, TMC * ROW_CHUNKS, LANES), _f32),
                pltpu.SemaphoreType.DMA((2,)),
            ],
        ),
        out_shape=jax.ShapeDtypeStruct((n_tok, D_MODEL), _f32),
        compiler_params=pltpu.CompilerParams(dimension_semantics=("arbitrary",),
                                             vmem_limit_bytes=VMEM_LIMIT),
        name="combine",
    )(pos2[0], pos2[1], y_sorted, h2.reshape(n_tok, D_MODEL), route)
    return out.reshape(bsz, seq, D_MODEL)
```

```python
import functools

import numpy as np
import jax
import jax.numpy as jnp
from jax import lax
from jax.experimental import pallas as pl
from jax.experimental.pallas import tpu as pltpu
from jax.experimental.pallas import tpu_sc as plsc

D_MODEL = 1024
N_META = 16
N_Q_HEADS = 8
N_KV_HEADS = 2
HEAD_DIM = 64
Q_GROUP = N_Q_HEADS // N_KV_HEADS
ATTN_DIM = N_Q_HEADS * HEAD_DIM
KV_DIM = N_KV_HEADS * HEAD_DIM
BLK = 128
PAD = BLK - N_META
N_BUCKETS = 32
MAX_DISTANCE = 128
CONV_CH = D_MODEL // 2
IN_PROJ = ATTN_DIM + 2 * KV_DIM + 3 * CONV_CH
N_GROUPS = 4
EXPERTS_PER_GROUP = 8
N_EXPERTS = N_GROUPS * EXPERTS_PER_GROUP
D_EXPERT = D_MODEL // 2
EPS = 1e-6
NEG_INF = -1e30

LANES = 128
SUBLANES = 8
ROW_CHUNKS = D_MODEL // LANES
TM = 512
TME = 512
TMC = 256
SC_WINDOW = 32
DMA_UNROLL = 8
ROUTE_W = 8
VMEM_LIMIT = 56 * 1024 * 1024

Q_OFF, K_OFF, V_OFF = 0, ATTN_DIM, ATTN_DIM + KV_DIM
CB_OFF = ATTN_DIM + 2 * KV_DIM
CC_OFF = CB_OFF + CONV_CH
CH_OFF = CC_OFF + CONV_CH

_f32 = jnp.float32
_bf16 = jnp.bfloat16


def _rms(x, g):
    return x * lax.rsqrt(jnp.mean(x * x, axis=-1, keepdims=True) + EPS) * g


def _dot(a, b):
    return jnp.dot(a, b, preferred_element_type=_f32)


def _dup_halves(x):
    lane = lax.broadcasted_iota(jnp.int32, x.shape, 1)
    sw = pltpu.roll(x, HEAD_DIM, axis=1)
    lo = lane < HEAD_DIM
    return jnp.where(lo, x, sw).astype(_bf16), jnp.where(lo, sw, x).astype(_bf16)


def _kv_state(hn_bf, win_ref, kg_ref, bdk_ref):
    kv = _dot(hn_bf, win_ref[:, K_OFF:K_OFF + 2 * KV_DIM])
    k = kv[:, :KV_DIM]
    v = kv[:, KV_DIM:]
    ssk = _dot((k * k).astype(_bf16), bdk_ref[...])
    kn = k * lax.rsqrt(ssk * (1.0 / HEAD_DIM) + EPS) * kg_ref[...]
    return _dup_halves(kn) + _dup_halves(v)


def _prep_kernel(rb_ref, meta_ref, mixg_ref, win_ref, kg_ref, bdk_ref, bucket_ref,
                 k0_ref, k1_ref, v0_ref, v1_ref, ut_ref, bias_ref):
    hn = _rms(meta_ref[...], mixg_ref[...]).astype(_bf16)
    k0, k1, v0, v1 = _kv_state(hn, win_ref, kg_ref, bdk_ref)
    k0_ref[...] = k0
    k1_ref[...] = k1
    v0_ref[...] = v0
    v1_ref[...] = v1
    cch = _dot(hn, win_ref[:, CC_OFF:CC_OFF + 2 * CONV_CH])
    u = cch[:, :CONV_CH] * cch[:, CONV_CH:]
    ut_ref[...] = u[BLK - SUBLANES:, :]
    for f in range(2):
        bk = bucket_ref[f]
        for h in range(N_Q_HEADS):
            acc = jnp.full((BLK, 2 * BLK), NEG_INF, _f32)
            for b in range(N_BUCKETS):
                acc = jnp.where(bk == b, rb_ref[b, h], acc)
            bias_ref[f, h // Q_GROUP, (h % Q_GROUP) * BLK:(h % Q_GROUP + 1) * BLK, :] = acc


def _mixer_kernel(sinks_ref,
                  x_ref, mixg_ref, win_ref, qg_ref, kg_ref, bias_ref, convw_ref, ag_ref, cg_ref,
                  wout_ref, fg_ref, wr_ref, br_ref, k0m_ref, k1m_ref, v0m_ref, v1m_ref, utm_ref,
                  bdq_ref, bdk_ref, tri_ref,
                  h2_ref, hn2_ref, route_ref, route_t_ref, cnt_ref,
                  kp0, kp1, vp0, vp1, ubuf, a_scr, cnt_acc):
    b = pl.program_id(0)
    t = pl.program_id(1)

    @pl.when(t == 0)
    def _():
        kp0[...] = k0m_ref[...]
        kp1[...] = k1m_ref[...]
        vp0[...] = v0m_ref[...]
        vp1[...] = v1m_ref[...]
        ubuf[0:SUBLANES, :] = utm_ref[...]

    @pl.when(jnp.logical_and(b == 0, t == 0))
    def _():
        cnt_acc[...] = jnp.zeros_like(cnt_acc)

    x = x_ref[...]
    hn = _rms(x, mixg_ref[...]).astype(_bf16)

    q = _dot(hn, win_ref[:, Q_OFF:Q_OFF + ATTN_DIM])
    ssq = _dot((q * q).astype(_bf16), bdq_ref[...])
    qn = (q * lax.rsqrt(ssq * (1.0 / HEAD_DIM) + EPS) * qg_ref[...]).astype(_bf16)
    kd0, kd1, vd0, vd1 = _kv_state(hn, win_ref, kg_ref, bdk_ref)
    kd = (kd0, kd1)
    vd = (vd0, vd1)
    kp = (kp0, kp1)
    vp = (vp0, vp1)

    lane_q = lax.broadcasted_iota(jnp.int32, (BLK, LANES), 1)
    lo_half = lane_q < HEAD_DIM
    row4 = lax.broadcasted_iota(jnp.int32, (Q_GROUP * BLK, 1), 0) // BLK
    first = jnp.where(t == 0, 0, 1)
    zero_bf = jnp.zeros((BLK, LANES), _bf16)

    for j in range(TM // BLK):
        rows = slice(j * BLK, (j + 1) * BLK)
        for g in range(N_KV_HEADS):
            if j == 0:
                kcat = jnp.concatenate([kp[g][...], kd[g][rows]], axis=0)
                vcat = jnp.concatenate([vp[g][...], vd[g][rows]], axis=0)
                bias = bias_ref[first, g]
            else:
                kcat = kd[g][(j - 1) * BLK:(j + 1) * BLK]
                vcat = vd[g][(j - 1) * BLK:(j + 1) * BLK]
                bias = bias_ref[1, g]
            qs = []
            for hh in range(Q_GROUP):
                h = g * Q_GROUP + hh
                qc = qn[rows, (h // 2) * LANES:(h // 2 + 1) * LANES]
                keep = lo_half if h % 2 == 0 else jnp.logical_not(lo_half)
                qs.append(jnp.where(keep, qc, zero_bf))
            q4 = jnp.concatenate(qs, axis=0)
            s = lax.dot_general(q4, kcat, (((1,), (1,)), ((), ())),
                                preferred_element_type=_f32) + bias
            sink = jnp.full((Q_GROUP * BLK, 1), sinks_ref[g * Q_GROUP], _f32)
            for hh in range(1, Q_GROUP):
                sink = jnp.where(row4 == hh, sinks_ref[g * Q_GROUP + hh], sink)
            m = jnp.maximum(jnp.max(s, axis=-1, keepdims=True), sink)
            p = jnp.exp(s - m)
            l = jnp.sum(p, axis=-1, keepdims=True) + jnp.exp(sink - m)
            o = _dot(p.astype(_bf16), vcat) / l
            for pp in range(Q_GROUP // 2):
                ev = o[(2 * pp) * BLK:(2 * pp + 1) * BLK]
                od = o[(2 * pp + 1) * BLK:(2 * pp + 2) * BLK]
                col = g * (Q_GROUP // 2) + pp
                a_scr[rows, col * LANES:(col + 1) * LANES] = jnp.where(lo_half, ev, od)

    last = slice(TM - BLK, TM)
    kp0[...] = kd0[last]
    kp1[...] = kd1[last]
    vp0[...] = vd0[last]
    vp1[...] = vd1[last]

    cb = _dot(hn, win_ref[:, CB_OFF:CB_OFF + CONV_CH])
    cch = _dot(hn, win_ref[:, CC_OFF:CC_OFF + 2 * CONV_CH])
    u = cch[:, :CONV_CH] * cch[:, CONV_CH:]
    ubuf[SUBLANES:, :] = u
    u1 = ubuf[SUBLANES - 1:SUBLANES - 1 + TM, :]
    u2 = ubuf[SUBLANES - 2:SUBLANES - 2 + TM, :]
    c = cb * (convw_ref[0:1, :] * u2 + convw_ref[1:2, :] * u1 + convw_ref[2:3, :] * u)
    ubuf[0:SUBLANES, :] = u[TM - SUBLANES:, :]

    an = _rms(a_scr[...], ag_ref[...]).astype(_bf16)
    cn = _rms(c, cg_ref[...]).astype(_bf16)
    h2 = x + _dot(an, wout_ref[0:ATTN_DIM, :]) + _dot(cn, wout_ref[ATTN_DIM:, :])
    h2_ref[...] = h2

    hn2 = _rms(h2, fg_ref[...])
    hn2_ref[...] = hn2

    hi = hn2.astype(_bf16)
    lo = (hn2 - hi.astype(_f32)).astype(_bf16)
    r1 = _dot(hi, wr_ref[...])
    r2 = _dot(lo, wr_ref[...])
    lg = r1 + pltpu.roll(r1, HEAD_DIM, axis=1) + r2 + br_ref[...]
    lane = lax.broadcasted_iota(jnp.int32, (TM, LANES), 1)
    lanef = lane.astype(_f32)
    ninf = jnp.float32(-jnp.inf)
    big = jnp.float32(LANES)

    gl = jnp.where(lane < N_GROUPS, lg, ninf)
    gmax = jnp.max(gl, axis=-1, keepdims=True)
    gsum = jnp.sum(jnp.exp(gl - gmax), axis=-1, keepdims=True)
    g_p = 1.0 / gsum
    g_idx = jnp.min(jnp.where(gl == gmax, lanef, big), axis=-1, keepdims=True)
    e_lo = N_GROUPS + EXPERTS_PER_GROUP * g_idx
    el = jnp.where(jnp.logical_and(lanef >= e_lo, lanef < e_lo + EXPERTS_PER_GROUP), lg, ninf)
    m1 = jnp.max(el, axis=-1, keepdims=True)
    i1 = jnp.min(jnp.where(el == m1, lanef, big), axis=-1, keepdims=True)
    el2 = jnp.where(lanef == i1, ninf, el)
    m2 = jnp.max(el2, axis=-1, keepdims=True)
    i2 = jnp.min(jnp.where(el2 == m2, lanef, big), axis=-1, keepdims=True)
    ex = jnp.exp(m2 - m1)
    den = 1.0 / (1.0 + ex)
    gate0 = g_p * den
    gate1 = g_p * ex * den
    e0 = i1 - N_GROUPS
    e1 = i2 - N_GROUPS

    oh0 = lanef == e0
    oh1 = lanef == e1
    cmat = (jnp.where(oh0, 1.0, 0.0) + jnp.where(oh1, 1.0, 0.0))
    prefix = _dot(tri_ref[...], cmat.astype(_bf16)) + cnt_acc[...]
    rank0 = jnp.sum(jnp.where(oh0, prefix, 0.0), axis=-1, keepdims=True)
    rank1 = jnp.sum(jnp.where(oh1, prefix, 0.0), axis=-1, keepdims=True)
    cnt_new = cnt_acc[...] + jnp.sum(cmat, axis=0, keepdims=True)
    cnt_acc[...] = cnt_new
    cnt_ref[...] = cnt_new

    lane8 = lax.broadcasted_iota(jnp.int32, (TM, ROUTE_W), 1)
    rec = jnp.zeros((TM, ROUTE_W), _f32)
    for idx, val in enumerate((e0, e1, gate0, gate1, rank0, rank1)):
        rec = jnp.where(lane8 == idx, val, rec)
    route_ref[...] = rec
    lane_r = lax.broadcasted_iota(jnp.int32, (TM, LANES), 1)
    wide = jnp.zeros((TM, LANES), _f32)
    for idx, val in enumerate((e0, e1, rank0, rank1)):
        wide = jnp.where(lane_r == idx, val, wide)
    route_t_ref[...] = wide.T[:SUBLANES, :]


def _row_tile(ref, row):
    return ref.at[pl.ds(pl.multiple_of(row * ROW_CHUNKS, ROW_CHUNKS), ROW_CHUNKS), :]


def _dispatch_rows(hn2_rows, pos0, pos1, n_rows):
    n_tok = hn2_rows.shape[0]
    info = plsc.get_sparse_core_info()
    n_workers = info.num_cores * info.num_subcores
    per_worker = n_tok // n_workers
    n_chunks = per_worker // SC_WINDOW
    assert per_worker * n_workers == n_tok and n_chunks * SC_WINDOW == per_worker and n_chunks % 2 == 0
    mesh = plsc.VectorSubcoreMesh(core_axis_name="core", subcore_axis_name="subcore")

    @functools.partial(
        pl.kernel,
        out_type=jax.ShapeDtypeStruct((n_rows, D_MODEL), _f32),
        mesh=mesh,
        scratch_types=[pltpu.VMEM((SC_WINDOW,), jnp.int32), pltpu.VMEM((SC_WINDOW,), jnp.int32),
                       pltpu.VMEM((SC_WINDOW,), jnp.int32), pltpu.VMEM((SC_WINDOW,), jnp.int32),
                       pltpu.VMEM((SC_WINDOW, D_MODEL), _f32), pltpu.VMEM((SC_WINDOW, D_MODEL), _f32),
                       pltpu.SemaphoreType.DMA((2,)), pltpu.SemaphoreType.DMA((2,))],
        compiler_params=pltpu.CompilerParams(use_tc_tiling_on_sc=True),
        name="dispatch",
    )
    def dispatch(x_hbm, i0_hbm, i1_hbm, o_hbm, i0_a, i0_b, i1_a, i1_b, rows_a, rows_b, sem_ld, sem_st):
        wid = lax.axis_index("subcore") * info.num_cores + lax.axis_index("core")
        base = wid * per_worker
        i0_v, i1_v, rows_v = (i0_a, i0_b), (i1_a, i1_b), (rows_a, rows_b)

        def loads(chunk, b):
            off = pl.multiple_of(base + chunk * SC_WINDOW, SC_WINDOW)
            return (pltpu.make_async_copy(i0_hbm.at[pl.ds(off, SC_WINDOW)], i0_v[b], sem_ld.at[b]),
                    pltpu.make_async_copy(i1_hbm.at[pl.ds(off, SC_WINDOW)], i1_v[b], sem_ld.at[b]),
                    pltpu.make_async_copy(x_hbm.at[pl.ds(off, SC_WINDOW)], rows_v[b], sem_ld.at[b]))

        def stores(b):
            return (pltpu.make_async_copy(rows_v[b], o_hbm.at[i0_v[b]], sem_st.at[b]),
                    pltpu.make_async_copy(rows_v[b], o_hbm.at[i1_v[b]], sem_st.at[b]))

        for d in loads(0, 0):
            d.start()

        @pl.loop(0, n_chunks, step=2)
        def _(c):
            for b in range(2):
                chunk = c + b
                for d in loads(chunk, b):
                    d.wait()
                for d in stores(b):
                    d.start()

                @pl.when(chunk >= 1)
                def _():
                    for d in stores(1 - b):
                        d.wait()

                @pl.when(chunk + 1 < n_chunks)
                def _():
                    for d in loads(chunk + 1, 1 - b):
                        d.start()

        for d in stores(1):
            d.wait()

    return dispatch(hn2_rows, pos0, pos1)


def _experts_kernel(te_ref, tn_ref,
                    xs_ref, wg_ref, wu_ref, wd_ref,
                    y_ref,
                    wgb, wub, wdb):
    i = pl.program_id(0)
    valid = tn_ref[i] > 0
    changed = jnp.logical_or(i == 0, te_ref[i] != te_ref[jnp.maximum(i - 1, 0)])

    @pl.when(jnp.logical_and(changed, valid))
    def _():
        wgb[...] = wg_ref[...].astype(_bf16)
        wub[...] = wu_ref[...].astype(_bf16)
        wdb[...] = wd_ref[...].astype(_bf16)

    @pl.when(valid)
    def _():
        xb = xs_ref[...].astype(_bf16)
        live = lax.broadcasted_iota(jnp.int32, (TME, 1), 0) < tn_ref[i]
        xb = jnp.where(live, xb, jnp.zeros_like(xb))
        gate = _dot(xb, wgb[...])
        up = _dot(xb, wub[...])
        act = (gate * jax.nn.sigmoid(gate) * up).astype(_bf16)
        y = _dot(act, wdb[...])
        for cc in range(ROW_CHUNKS):
            y_ref[pl.ds(cc, TME, stride=ROW_CHUNKS), :] = y[:, cc * LANES:(cc + 1) * LANES]

    @pl.when(jnp.logical_not(valid))
    def _():
        y_ref[...] = jnp.zeros_like(y_ref)


def _combine_kernel(has_prev, pos0_ref, pos1_ref, y_hbm, h2_ref, route_ref, *rest):
    out_ref, ybuf, sem = rest[1:] if has_prev else rest
    i = pl.program_id(0)
    n = pl.num_programs(0)
    slot = i % 2

    def issue(tile, sl):
        base = tile * TMC
        pos_refs = (pos0_ref, pos1_ref)

        def body(it, carry):
            for j in range(DMA_UNROLL):
                r = it * DMA_UNROLL + j
                for k in range(2):
                    pltpu.make_async_copy(_row_tile(y_hbm, pos_refs[k][base + r]),
                                          _row_tile(ybuf.at[sl, k], r), sem.at[sl]).start(priority=k)
            return carry

        lax.fori_loop(0, TMC // DMA_UNROLL, body, 0)

    @pl.when(i == 0)
    def _():
        issue(0, 0)

    @pl.when(i + 1 < n)
    def _():
        issue(i + 1, 1 - slot)

    for k in range(2):
        pltpu.make_async_copy(y_hbm.at[pl.ds(0, TMC * ROW_CHUNKS), :], ybuf.at[slot, k], sem.at[slot]).wait()
    g0 = route_ref[:, 2:3]
    g1 = route_ref[:, 3:4]
    for cc in range(ROW_CHUNKS):
        y0 = ybuf[slot, 0, pl.ds(cc, TMC, stride=ROW_CHUNKS), :]
        y1 = ybuf[slot, 1, pl.ds(cc, TMC, stride=ROW_CHUNKS), :]
        cols = slice(cc * LANES, (cc + 1) * LANES)
        out_ref[:, cols] = h2_ref[:, cols] + (g0 * y0 + g1 * y1)


def _t5_bucket(n):
    max_exact = N_BUCKETS // 2
    nf = jnp.maximum(n, 1).astype(_f32)
    large = max_exact + (jnp.log(nf / max_exact) / np.log(MAX_DISTANCE / max_exact)
                         * (N_BUCKETS - max_exact)).astype(jnp.int32)
    large = jnp.minimum(large, N_BUCKETS - 1)
    return jnp.where(n < max_exact, n, large)


def _bucket_maps():
    qi = jnp.arange(BLK)[:, None]
    sj = jnp.arange(2 * BLK)[None, :]
    dist = BLK + qi - sj
    band = (dist >= 0) & (dist < BLK)
    bucket = _t5_bucket(jnp.maximum(dist, 0))
    generic = jnp.where(band, bucket, -1)
    first = jnp.where(band & (sj >= PAD), bucket, -1)
    return jnp.stack([first, generic]).astype(jnp.int32)


def _const(shape):
    nd = len(shape)
    return pl.BlockSpec(shape, lambda *_: (0,) * nd)


def _block_diag_ones(n):
    idx = np.arange(n) // HEAD_DIM
    return jnp.asarray((idx[:, None] == idx[None, :]).astype(np.float32), dtype=_bf16)


def kernel(x, meta_tokens, rel_bias, mix_norm_g, w_in, q_norm_g, k_norm_g, attn_sinks, conv_w, attn_out_norm_g, conv_out_norm_g, w_out, ffn_norm_g, w_group_router, b_group_router, w_expert_router, b_expert_router, w_gate, w_up, w_down):
    bsz, seq, _ = x.shape
    assert seq % TM == 0 and (bsz * seq) % TMC == 0
    n_tok = bsz * seq
    nt = seq // TM

    win = w_in[0].astype(_bf16)
    wout = w_out[0].astype(_bf16)
    mixg = mix_norm_g[0].reshape(1, D_MODEL)
    fg = ffn_norm_g[0].reshape(1, D_MODEL)
    qg = (jnp.tile(q_norm_g[0], N_Q_HEADS) * (HEAD_DIM ** -0.5)).reshape(1, ATTN_DIM)
    kg = jnp.tile(k_norm_g[0], N_KV_HEADS).reshape(1, KV_DIM)
    ag = attn_out_norm_g[0].reshape(1, ATTN_DIM)
    cg = conv_out_norm_g[0].reshape(1, CONV_CH)
    convw = conv_w[0]
    sinks = attn_sinks[0]
    w_r = jnp.concatenate([w_group_router[0], w_expert_router[0].reshape(D_MODEL, N_EXPERTS)], axis=1)
    n_r = N_GROUPS + N_EXPERTS
    w_r_hi = w_r.astype(_bf16)
    w_r_lo = (w_r - w_r_hi.astype(_f32)).astype(_bf16)
    wr = jnp.zeros((D_MODEL, LANES), _bf16)
    wr = wr.at[:, :n_r].set(w_r_hi).at[:, HEAD_DIM:HEAD_DIM + n_r].set(w_r_lo)
    br = jnp.zeros((1, LANES), _f32).at[0, :n_r].set(
        jnp.concatenate([b_group_router[0], b_expert_router[0].reshape(N_EXPERTS)]))
    meta_blk = jnp.concatenate([jnp.zeros((PAD, D_MODEL), x.dtype), meta_tokens.astype(x.dtype)], axis=0)
    bdq = _block_diag_ones(ATTN_DIM)
    bdk = _block_diag_ones(KV_DIM)
    tri = jnp.asarray(np.tril(np.ones((TM, TM), np.float32), -1), dtype=_bf16)

    kv_sd = jax.ShapeDtypeStruct((BLK, KV_DIM), _bf16)
    k0m, k1m, v0m, v1m, utm, bias = pl.pallas_call(
        _prep_kernel,
        out_shape=(kv_sd, kv_sd, kv_sd, kv_sd,
                   jax.ShapeDtypeStruct((SUBLANES, CONV_CH), _f32),
                   jax.ShapeDtypeStruct((2, N_KV_HEADS, Q_GROUP * BLK, 2 * BLK), _f32)),
        in_specs=[pl.BlockSpec(memory_space=pltpu.SMEM)] + [pl.BlockSpec(memory_space=pltpu.VMEM)] * 6,
        out_specs=tuple(pl.BlockSpec(memory_space=pltpu.VMEM) for _ in range(6)),
        compiler_params=pltpu.CompilerParams(vmem_limit_bytes=VMEM_LIMIT),
        name="prep",
    )(rel_bias, meta_blk, mixg, win, kg, bdk, _bucket_maps())

    n_parts = 2 if bsz % 2 == 0 else 1
    part_tok = n_tok // n_parts
    part_tiles = part_tok // TM
    x_rows = x.reshape(n_tok, D_MODEL)
    consts = (mixg, win, qg, kg, bias, convw, ag, cg, wout, fg, wr, br, k0m, k1m, v0m, v1m, utm, bdq, bdk, tri)

    mixed = [_mixer_call(sinks, x_rows, consts, part * part_tiles, part_tok, nt) for part in range(n_parts)]
    plans = [_plan(m[3], m[4], part_tok) for m in mixed]
    xs = [_dispatch_rows(m[1], p[0][0], p[0][1], p[3] * TME) for m, p in zip(mixed, plans)]
    ys = [_experts_call(p[1], p[2], xr, w_gate[0], w_up[0], w_down[0], p[3]) for p, xr in zip(plans, xs)]
    out = None
    for part in range(n_parts):
        h2, _, route, _, _ = mixed[part]
        out = _combine_call(plans[part][0], ys[part], h2, route, out, part * (part_tok // TMC), n_tok)
    return out.reshape(bsz, seq, D_MODEL)


def _mixer_call(sinks, x_rows, consts, tile0, part_tok, nt):
    tile_idx = lambda b, t, *_: (b * nt + t, 0)
    grid_spec = pltpu.PrefetchScalarGridSpec(
        num_scalar_prefetch=1,
        grid=(part_tok // (nt * TM), nt),
        in_specs=[
            pl.BlockSpec((TM, D_MODEL), lambda b, t, *_: (tile0 + b * nt + t, 0)),
            _const((1, D_MODEL)), _const((D_MODEL, IN_PROJ)), _const((1, ATTN_DIM)), _const((1, KV_DIM)),
            _const((2, N_KV_HEADS, Q_GROUP * BLK, 2 * BLK)), _const((3, CONV_CH)),
            _const((1, ATTN_DIM)), _const((1, CONV_CH)), _const((D_MODEL, D_MODEL)), _const((1, D_MODEL)),
            _const((D_MODEL, LANES)), _const((1, LANES)),
            _const((BLK, KV_DIM)), _const((BLK, KV_DIM)), _const((BLK, KV_DIM)), _const((BLK, KV_DIM)),
            _const((SUBLANES, CONV_CH)),
            _const((ATTN_DIM, ATTN_DIM)), _const((KV_DIM, KV_DIM)), _const((TM, TM)),
        ],
        out_specs=[
            pl.BlockSpec((TM, D_MODEL), tile_idx),
            pl.BlockSpec((TM, D_MODEL), tile_idx),
            pl.BlockSpec((TM, ROUTE_W), tile_idx),
            pl.BlockSpec((SUBLANES, TM), lambda b, t, *_: (0, b * nt + t)),
            _const((1, LANES)),
        ],
        scratch_shapes=[
            pltpu.VMEM((BLK, KV_DIM), _bf16), pltpu.VMEM((BLK, KV_DIM), _bf16),
            pltpu.VMEM((BLK, KV_DIM), _bf16), pltpu.VMEM((BLK, KV_DIM), _bf16),
            pltpu.VMEM((TM + SUBLANES, CONV_CH), _f32),
            pltpu.VMEM((TM, ATTN_DIM), _f32),
            pltpu.VMEM((1, LANES), _f32),
        ],
    )
    return pl.pallas_call(
        _mixer_kernel,
        grid_spec=grid_spec,
        out_shape=(jax.ShapeDtypeStruct((part_tok, D_MODEL), _f32),
                   jax.ShapeDtypeStruct((part_tok, D_MODEL), _f32),
                   jax.ShapeDtypeStruct((part_tok, ROUTE_W), _f32),
                   jax.ShapeDtypeStruct((SUBLANES, part_tok), _f32),
                   jax.ShapeDtypeStruct((1, LANES), _f32)),
        compiler_params=pltpu.CompilerParams(dimension_semantics=("arbitrary", "arbitrary"),
                                             vmem_limit_bytes=VMEM_LIMIT),
        name="mixer",
    )(sinks, x_rows, *consts)


def _plan(route_t, cnt, part_tok):
    n_tiles = (part_tok * 2) // TME + N_EXPERTS
    counts = cnt[0, :N_EXPERTS].astype(jnp.int32)
    ntile = (counts + TME - 1) // TME
    tile_end = jnp.cumsum(ntile)
    tile_start = tile_end - ntile
    eid = route_t[0:2].astype(jnp.int32)
    rank = route_t[2:4].astype(jnp.int32)
    experts = jnp.arange(N_EXPERTS, dtype=jnp.int32)
    start_of = jnp.sum(jnp.where(eid[None] == experts[:, None, None], tile_start[:, None, None], 0), axis=0)
    pos2 = start_of * TME + rank
    tiles = jnp.arange(n_tiles, dtype=jnp.int32)
    n_used = tile_end[-1]
    t_exp = jnp.sum((jnp.minimum(tiles, n_used - 1)[:, None] >= tile_end[None, :]).astype(jnp.int32), axis=-1)
    t_exp = jnp.minimum(t_exp, N_EXPERTS - 1)
    own = t_exp[:, None] == experts
    t_rows = jnp.sum(jnp.where(own, counts - (tiles[:, None] - tile_start) * TME, 0), axis=-1)
    t_rows = jnp.where(tiles < n_used, jnp.clip(t_rows, 0, TME), 0).astype(jnp.int32)
    return pos2, t_exp, t_rows, n_tiles


def _experts_call(t_exp, t_rows, xs, w_gate, w_up, w_down, n_tiles):
    return pl.pallas_call(
        _experts_kernel,
        grid_spec=pltpu.PrefetchScalarGridSpec(
            num_scalar_prefetch=2,
            grid=(n_tiles,),
            in_specs=[
                pl.BlockSpec((TME, D_MODEL), lambda i, *_: (i, 0)),
                pl.BlockSpec((None, D_MODEL, D_EXPERT), lambda i, te, tn: (te[i], 0, 0)),
                pl.BlockSpec((None, D_MODEL, D_EXPERT), lambda i, te, tn: (te[i], 0, 0)),
                pl.BlockSpec((None, D_EXPERT, D_MODEL), lambda i, te, tn: (te[i], 0, 0)),
            ],
            out_specs=pl.BlockSpec((TME * ROW_CHUNKS, LANES), lambda i, *_: (i, 0)),
            scratch_shapes=[
                pltpu.VMEM((D_MODEL, D_EXPERT), _bf16),
                pltpu.VMEM((D_MODEL, D_EXPERT), _bf16),
                pltpu.VMEM((D_EXPERT, D_MODEL), _bf16),
            ],
        ),
        out_shape=jax.ShapeDtypeStruct((n_tiles * TME * ROW_CHUNKS, LANES), _f32),
        compiler_params=pltpu.CompilerParams(dimension_semantics=("arbitrary",),
                                             vmem_limit_bytes=VMEM_LIMIT),
        name="experts",
    )(t_exp, t_rows, xs, w_gate, w_up, w_down)


def _combine_call(pos2, y_sorted, h2, route, out_prev, tile0, n_tok):
    part_tok = h2.shape[0]
    in_specs = [
        pl.BlockSpec(memory_space=pl.ANY),
        pl.BlockSpec((TMC, D_MODEL), lambda i, *_: (i, 0)),
        pl.BlockSpec((TMC, ROUTE_W), lambda i, *_: (i, 0)),
    ]
    args = [pos2[0], pos2[1], y_sorted, h2, route]
    aliases = {}
    if out_prev is not None:
        in_specs.append(pl.BlockSpec(memory_space=pl.ANY))
        args.append(out_prev)
        aliases = {len(args) - 1: 0}
    return pl.pallas_call(
        functools.partial(_combine_kernel, out_prev is not None),
        grid_spec=pltpu.PrefetchScalarGridSpec(
            num_scalar_prefetch=2,
            grid=(part_tok // TMC,),
            in_specs=in_specs,
            out_specs=pl.BlockSpec((TMC, D_MODEL), lambda i, *_: (tile0 + i, 0)),
            scratch_shapes=[
                pltpu.VMEM((2, 2, TMC * ROW_CHUNKS, LANES), _f32),
                pltpu.SemaphoreType.DMA((2,)),
            ],
        ),
        out_shape=jax.ShapeDtypeStruct((n_tok, D_MODEL), _f32),
        input_output_aliases=aliases,
        compiler_params=pltpu.CompilerParams(dimension_semantics=("arbitrary",),
                                             vmem_limit_bytes=VMEM_LIMIT),
        name="combine",
    )(*args)
```

```python
import functools

import numpy as np
import jax
import jax.numpy as jnp
from jax import lax
from jax.experimental import pallas as pl
from jax.experimental.pallas import tpu as pltpu
from jax.experimental.pallas import tpu_sc as plsc

D_MODEL = 1024
N_META = 16
N_Q_HEADS = 8
N_KV_HEADS = 2
HEAD_DIM = 64
Q_GROUP = N_Q_HEADS // N_KV_HEADS
ATTN_DIM = N_Q_HEADS * HEAD_DIM
KV_DIM = N_KV_HEADS * HEAD_DIM
BLK = 128
PAD = BLK - N_META
N_BUCKETS = 32
MAX_DISTANCE = 128
CONV_CH = D_MODEL // 2
IN_PROJ = ATTN_DIM + 2 * KV_DIM + 3 * CONV_CH
N_GROUPS = 4
EXPERTS_PER_GROUP = 8
N_EXPERTS = N_GROUPS * EXPERTS_PER_GROUP
D_EXPERT = D_MODEL // 2
EPS = 1e-6
NEG_INF = -1e30

LANES = 128
SUBLANES = 8
ROW_CHUNKS = D_MODEL // LANES
TM = 512
TME = 512
TMC = 256
XS_SLOTS = 3
SC_WINDOW = 32
DMA_UNROLL = 8
ROUTE_W = 8
VMEM_LIMIT = 56 * 1024 * 1024

Q_OFF, K_OFF, V_OFF = 0, ATTN_DIM, ATTN_DIM + KV_DIM
CB_OFF = ATTN_DIM + 2 * KV_DIM
CC_OFF = CB_OFF + CONV_CH
CH_OFF = CC_OFF + CONV_CH

_f32 = jnp.float32
_bf16 = jnp.bfloat16


def _rms(x, g):
    return x * lax.rsqrt(jnp.mean(x * x, axis=-1, keepdims=True) + EPS) * g


def _dot(a, b):
    return jnp.dot(a, b, preferred_element_type=_f32)


def _dup_halves(x):
    lane = lax.broadcasted_iota(jnp.int32, x.shape, 1)
    sw = pltpu.roll(x, HEAD_DIM, axis=1)
    lo = lane < HEAD_DIM
    return jnp.where(lo, x, sw).astype(_bf16), jnp.where(lo, sw, x).astype(_bf16)


def _kv_state(hn_bf, win_ref, kg_ref, bdk_ref):
    kv = _dot(hn_bf, win_ref[:, K_OFF:K_OFF + 2 * KV_DIM])
    k = kv[:, :KV_DIM]
    v = kv[:, KV_DIM:]
    ssk = _dot((k * k).astype(_bf16), bdk_ref[...])
    kn = k * lax.rsqrt(ssk * (1.0 / HEAD_DIM) + EPS) * kg_ref[...]
    return _dup_halves(kn) + _dup_halves(v)


def _prep_kernel(rb_ref, meta_ref, mixg_ref, win_ref, kg_ref, bdk_ref, bucket_ref,
                 k0_ref, k1_ref, v0_ref, v1_ref, ut_ref, bias_ref):
    hn = _rms(meta_ref[...], mixg_ref[...]).astype(_bf16)
    k0, k1, v0, v1 = _kv_state(hn, win_ref, kg_ref, bdk_ref)
    k0_ref[...] = k0
    k1_ref[...] = k1
    v0_ref[...] = v0
    v1_ref[...] = v1
    cch = _dot(hn, win_ref[:, CC_OFF:CC_OFF + 2 * CONV_CH])
    u = cch[:, :CONV_CH] * cch[:, CONV_CH:]
    ut_ref[...] = u[BLK - SUBLANES:, :]
    for f in range(2):
        bk = bucket_ref[f]
        for h in range(N_Q_HEADS):
            acc = jnp.full((BLK, 2 * BLK), NEG_INF, _f32)
            for b in range(N_BUCKETS):
                acc = jnp.where(bk == b, rb_ref[b, h], acc)
            bias_ref[f, h // Q_GROUP, (h % Q_GROUP) * BLK:(h % Q_GROUP + 1) * BLK, :] = acc


def _mixer_kernel(sinks_ref,
                  x_ref, mixg_ref, win_ref, qg_ref, kg_ref, bias_ref, convw_ref, ag_ref, cg_ref,
                  wout_ref, fg_ref, wr_ref, br_ref, k0m_ref, k1m_ref, v0m_ref, v1m_ref, utm_ref,
                  bdq_ref, bdk_ref, tri_ref,
                  h2_ref, hn2_ref, route_ref, route_t_ref, cnt_ref,
                  kp0, kp1, vp0, vp1, ubuf, a_scr, cnt_acc):
    b = pl.program_id(0)
    t = pl.program_id(1)

    @pl.when(t == 0)
    def _():
        kp0[...] = k0m_ref[...]
        kp1[...] = k1m_ref[...]
        vp0[...] = v0m_ref[...]
        vp1[...] = v1m_ref[...]
        ubuf[0:SUBLANES, :] = utm_ref[...]

    @pl.when(jnp.logical_and(b == 0, t == 0))
    def _():
        cnt_acc[...] = jnp.zeros_like(cnt_acc)

    x = x_ref[...]
    hn = _rms(x, mixg_ref[...]).astype(_bf16)

    q = _dot(hn, win_ref[:, Q_OFF:Q_OFF + ATTN_DIM])
    ssq = _dot((q * q).astype(_bf16), bdq_ref[...])
    qn = (q * lax.rsqrt(ssq * (1.0 / HEAD_DIM) + EPS) * qg_ref[...]).astype(_bf16)
    kd0, kd1, vd0, vd1 = _kv_state(hn, win_ref, kg_ref, bdk_ref)
    kd = (kd0, kd1)
    vd = (vd0, vd1)
    kp = (kp0, kp1)
    vp = (vp0, vp1)

    lane_q = lax.broadcasted_iota(jnp.int32, (BLK, LANES), 1)
    lo_half = lane_q < HEAD_DIM
    row4 = lax.broadcasted_iota(jnp.int32, (Q_GROUP * BLK, 1), 0) // BLK
    first = jnp.where(t == 0, 0, 1)
    zero_bf = jnp.zeros((BLK, LANES), _bf16)

    for j in range(TM // BLK):
        rows = slice(j * BLK, (j + 1) * BLK)
        for g in range(N_KV_HEADS):
            if j == 0:
                kcat = jnp.concatenate([kp[g][...], kd[g][rows]], axis=0)
                vcat = jnp.concatenate([vp[g][...], vd[g][rows]], axis=0)
                bias = bias_ref[first, g]
            else:
                kcat = kd[g][(j - 1) * BLK:(j + 1) * BLK]
                vcat = vd[g][(j - 1) * BLK:(j + 1) * BLK]
                bias = bias_ref[1, g]
            qs = []
            for hh in range(Q_GROUP):
                h = g * Q_GROUP + hh
                qc = qn[rows, (h // 2) * LANES:(h // 2 + 1) * LANES]
                keep = lo_half if h % 2 == 0 else jnp.logical_not(lo_half)
                qs.append(jnp.where(keep, qc, zero_bf))
            q4 = jnp.concatenate(qs, axis=0)
            s = lax.dot_general(q4, kcat, (((1,), (1,)), ((), ())),
                                preferred_element_type=_f32) + bias
            sink = jnp.full((Q_GROUP * BLK, 1), sinks_ref[g * Q_GROUP], _f32)
            for hh in range(1, Q_GROUP):
                sink = jnp.where(row4 == hh, sinks_ref[g * Q_GROUP + hh], sink)
            m = jnp.maximum(jnp.max(s, axis=-1, keepdims=True), sink)
            p = jnp.exp(s - m)
            l = jnp.sum(p, axis=-1, keepdims=True) + jnp.exp(sink - m)
            o = _dot(p.astype(_bf16), vcat) / l
            for pp in range(Q_GROUP // 2):
                ev = o[(2 * pp) * BLK:(2 * pp + 1) * BLK]
                od = o[(2 * pp + 1) * BLK:(2 * pp + 2) * BLK]
                col = g * (Q_GROUP // 2) + pp
                a_scr[rows, col * LANES:(col + 1) * LANES] = jnp.where(lo_half, ev, od)

    last = slice(TM - BLK, TM)
    kp0[...] = kd0[last]
    kp1[...] = kd1[last]
    vp0[...] = vd0[last]
    vp1[...] = vd1[last]

    cb = _dot(hn, win_ref[:, CB_OFF:CB_OFF + CONV_CH])
    cch = _dot(hn, win_ref[:, CC_OFF:CC_OFF + 2 * CONV_CH])
    u = cch[:, :CONV_CH] * cch[:, CONV_CH:]
    ubuf[SUBLANES:, :] = u
    u1 = ubuf[SUBLANES - 1:SUBLANES - 1 + TM, :]
    u2 = ubuf[SUBLANES - 2:SUBLANES - 2 + TM, :]
    c = cb * (convw_ref[0:1, :] * u2 + convw_ref[1:2, :] * u1 + convw_ref[2:3, :] * u)
    ubuf[0:SUBLANES, :] = u[TM - SUBLANES:, :]

    an = _rms(a_scr[...], ag_ref[...]).astype(_bf16)
    cn = _rms(c, cg_ref[...]).astype(_bf16)
    h2 = x + _dot(an, wout_ref[0:ATTN_DIM, :]) + _dot(cn, wout_ref[ATTN_DIM:, :])
    h2_ref[...] = h2

    hn2 = _rms(h2, fg_ref[...])
    hn2_ref[...] = hn2

    hi = hn2.astype(_bf16)
    lo = (hn2 - hi.astype(_f32)).astype(_bf16)
    r1 = _dot(hi, wr_ref[...])
    r2 = _dot(lo, wr_ref[...])
    lg = r1 + pltpu.roll(r1, HEAD_DIM, axis=1) + r2 + br_ref[...]
    lane = lax.broadcasted_iota(jnp.int32, (TM, LANES), 1)
    lanef = lane.astype(_f32)
    ninf = jnp.float32(-jnp.inf)
    big = jnp.float32(LANES)

    gl = jnp.where(lane < N_GROUPS, lg, ninf)
    gmax = jnp.max(gl, axis=-1, keepdims=True)
    gsum = jnp.sum(jnp.exp(gl - gmax), axis=-1, keepdims=True)
    g_p = 1.0 / gsum
    g_idx = jnp.min(jnp.where(gl == gmax, lanef, big), axis=-1, keepdims=True)
    e_lo = N_GROUPS + EXPERTS_PER_GROUP * g_idx
    el = jnp.where(jnp.logical_and(lanef >= e_lo, lanef < e_lo + EXPERTS_PER_GROUP), lg, ninf)
    m1 = jnp.max(el, axis=-1, keepdims=True)
    i1 = jnp.min(jnp.where(el == m1, lanef, big), axis=-1, keepdims=True)
    el2 = jnp.where(lanef == i1, ninf, el)
    m2 = jnp.max(el2, axis=-1, keepdims=True)
    i2 = jnp.min(jnp.where(el2 == m2, lanef, big), axis=-1, keepdims=True)
    ex = jnp.exp(m2 - m1)
    den = 1.0 / (1.0 + ex)
    gate0 = g_p * den
    gate1 = g_p * ex * den
    e0 = i1 - N_GROUPS
    e1 = i2 - N_GROUPS

    oh0 = lanef == e0
    oh1 = lanef == e1
    cmat = (jnp.where(oh0, 1.0, 0.0) + jnp.where(oh1, 1.0, 0.0))
    prefix = _dot(tri_ref[...], cmat.astype(_bf16)) + cnt_acc[...]
    rank0 = jnp.sum(jnp.where(oh0, prefix, 0.0), axis=-1, keepdims=True)
    rank1 = jnp.sum(jnp.where(oh1, prefix, 0.0), axis=-1, keepdims=True)
    cnt_new = cnt_acc[...] + jnp.sum(cmat, axis=0, keepdims=True)
    cnt_acc[...] = cnt_new
    cnt_ref[...] = cnt_new

    lane8 = lax.broadcasted_iota(jnp.int32, (TM, ROUTE_W), 1)
    rec = jnp.zeros((TM, ROUTE_W), _f32)
    for idx, val in enumerate((e0, e1, gate0, gate1, rank0, rank1)):
        rec = jnp.where(lane8 == idx, val, rec)
    route_ref[...] = rec
    lane_r = lax.broadcasted_iota(jnp.int32, (TM, LANES), 1)
    wide = jnp.zeros((TM, LANES), _f32)
    for idx, val in enumerate((e0, e1, rank0, rank1)):
        wide = jnp.where(lane_r == idx, val, wide)
    route_t_ref[...] = wide.T[:SUBLANES, :]


def _row_tile(ref, row):
    return ref.at[pl.ds(pl.multiple_of(row * ROW_CHUNKS, ROW_CHUNKS), ROW_CHUNKS), :]


def _dispatch_rows(hn2_rows, pos0, pos1, n_rows):
    n_tok = hn2_rows.shape[0]
    info = plsc.get_sparse_core_info()
    n_workers = info.num_cores * info.num_subcores
    per_worker = n_tok // n_workers
    n_chunks = per_worker // SC_WINDOW
    assert per_worker * n_workers == n_tok and n_chunks * SC_WINDOW == per_worker and n_chunks % 2 == 0
    mesh = plsc.VectorSubcoreMesh(core_axis_name="core", subcore_axis_name="subcore")

    @functools.partial(
        pl.kernel,
        out_type=jax.ShapeDtypeStruct((n_rows, D_MODEL), _f32),
        mesh=mesh,
        scratch_types=[pltpu.VMEM((SC_WINDOW,), jnp.int32), pltpu.VMEM((SC_WINDOW,), jnp.int32),
                       pltpu.VMEM((SC_WINDOW,), jnp.int32), pltpu.VMEM((SC_WINDOW,), jnp.int32),
                       pltpu.VMEM((SC_WINDOW, D_MODEL), _f32), pltpu.VMEM((SC_WINDOW, D_MODEL), _f32),
                       pltpu.SemaphoreType.DMA((2,)), pltpu.SemaphoreType.DMA((2,))],
        compiler_params=pltpu.CompilerParams(use_tc_tiling_on_sc=True),
        name="dispatch",
    )
    def dispatch(x_hbm, i0_hbm, i1_hbm, o_hbm, i0_a, i0_b, i1_a, i1_b, rows_a, rows_b, sem_ld, sem_st):
        wid = lax.axis_index("subcore") * info.num_cores + lax.axis_index("core")
        base = wid * per_worker
        i0_v, i1_v, rows_v = (i0_a, i0_b), (i1_a, i1_b), (rows_a, rows_b)

        def loads(chunk, b):
            off = pl.multiple_of(base + chunk * SC_WINDOW, SC_WINDOW)
            return (pltpu.make_async_copy(i0_hbm.at[pl.ds(off, SC_WINDOW)], i0_v[b], sem_ld.at[b]),
                    pltpu.make_async_copy(i1_hbm.at[pl.ds(off, SC_WINDOW)], i1_v[b], sem_ld.at[b]),
                    pltpu.make_async_copy(x_hbm.at[pl.ds(off, SC_WINDOW)], rows_v[b], sem_ld.at[b]))

        def stores(b):
            return (pltpu.make_async_copy(rows_v[b], o_hbm.at[i0_v[b]], sem_st.at[b]),
                    pltpu.make_async_copy(rows_v[b], o_hbm.at[i1_v[b]], sem_st.at[b]))

        for d in loads(0, 0):
            d.start()

        @pl.loop(0, n_chunks, step=2)
        def _(c):
            for b in range(2):
                chunk = c + b
                for d in loads(chunk, b):
                    d.wait()
                for d in stores(b):
                    d.start()

                @pl.when(chunk >= 1)
                def _():
                    for d in stores(1 - b):
                        d.wait()

                @pl.when(chunk + 1 < n_chunks)
                def _():
                    for d in loads(chunk + 1, 1 - b):
                        d.start()

        for d in stores(1):
            d.wait()

    return dispatch(hn2_rows, pos0, pos1)


def _experts_kernel(te_ref, tn_ref,
                    xs_hbm, wg_ref, wu_ref, wd_ref,
                    y_ref,
                    xbuf, sem, wgb, wub, wdb):
    i = pl.program_id(0)
    n = pl.num_programs(0)

    def fetch(tile):
        sl = tile % XS_SLOTS
        return pltpu.make_async_copy(xs_hbm.at[pl.ds(pl.multiple_of(tile * TME, TME), TME), :],
                                     xbuf.at[sl], sem.at[sl])

    @pl.when(i == 0)
    def _():
        for tile in range(XS_SLOTS - 1):
            fetch(tile).start()

    @pl.when(i + XS_SLOTS - 1 < n)
    def _():
        fetch(i + XS_SLOTS - 1).start()

    valid = tn_ref[i] > 0
    changed = jnp.logical_or(i == 0, te_ref[i] != te_ref[jnp.maximum(i - 1, 0)])

    @pl.when(jnp.logical_and(changed, valid))
    def _():
        wgb[...] = wg_ref[...].astype(_bf16)
        wub[...] = wu_ref[...].astype(_bf16)
        wdb[...] = wd_ref[...].astype(_bf16)

    fetch(i).wait()

    @pl.when(valid)
    def _():
        xb = xbuf[i % XS_SLOTS].astype(_bf16)
        live = lax.broadcasted_iota(jnp.int32, (TME, 1), 0) < tn_ref[i]
        xb = jnp.where(live, xb, jnp.zeros_like(xb))
        gate = _dot(xb, wgb[...])
        up = _dot(xb, wub[...])
        act = (gate * jax.nn.sigmoid(gate) * up).astype(_bf16)
        y = _dot(act, wdb[...])
        for cc in range(ROW_CHUNKS):
            y_ref[pl.ds(cc, TME, stride=ROW_CHUNKS), :] = y[:, cc * LANES:(cc + 1) * LANES]

    @pl.when(jnp.logical_not(valid))
    def _():
        y_ref[...] = jnp.zeros_like(y_ref)


def _combine_kernel(pos0_ref, pos1_ref,
                    y_hbm, h2_ref, route_ref,
                    out_ref,
                    ybuf, sem):
    i = pl.program_id(0)
    n = pl.num_programs(0)
    slot = i % 2

    def issue(tile, sl):
        base = tile * TMC
        pos_refs = (pos0_ref, pos1_ref)

        def body(it, carry):
            for j in range(DMA_UNROLL):
                r = it * DMA_UNROLL + j
                for k in range(2):
                    pltpu.make_async_copy(_row_tile(y_hbm, pos_refs[k][base + r]),
                                          _row_tile(ybuf.at[sl, k], r), sem.at[sl]).start(priority=k)
            return carry

        lax.fori_loop(0, TMC // DMA_UNROLL, body, 0)

    @pl.when(i == 0)
    def _():
        issue(0, 0)

    @pl.when(i + 1 < n)
    def _():
        issue(i + 1, 1 - slot)

    for k in range(2):
        pltpu.make_async_copy(y_hbm.at[pl.ds(0, TMC * ROW_CHUNKS), :], ybuf.at[slot, k], sem.at[slot]).wait()
    g0 = route_ref[:, 2:3]
    g1 = route_ref[:, 3:4]
    for cc in range(ROW_CHUNKS):
        y0 = ybuf[slot, 0, pl.ds(cc, TMC, stride=ROW_CHUNKS), :]
        y1 = ybuf[slot, 1, pl.ds(cc, TMC, stride=ROW_CHUNKS), :]
        cols = slice(cc * LANES, (cc + 1) * LANES)
        out_ref[:, cols] = h2_ref[:, cols] + (g0 * y0 + g1 * y1)


def _t5_bucket(n):
    max_exact = N_BUCKETS // 2
    nf = jnp.maximum(n, 1).astype(_f32)
    large = max_exact + (jnp.log(nf / max_exact) / np.log(MAX_DISTANCE / max_exact)
                         * (N_BUCKETS - max_exact)).astype(jnp.int32)
    large = jnp.minimum(large, N_BUCKETS - 1)
    return jnp.where(n < max_exact, n, large)


def _bucket_maps():
    qi = jnp.arange(BLK)[:, None]
    sj = jnp.arange(2 * BLK)[None, :]
    dist = BLK + qi - sj
    band = (dist >= 0) & (dist < BLK)
    bucket = _t5_bucket(jnp.maximum(dist, 0))
    generic = jnp.where(band, bucket, -1)
    first = jnp.where(band & (sj >= PAD), bucket, -1)
    return jnp.stack([first, generic]).astype(jnp.int32)


def _const(shape):
    nd = len(shape)
    return pl.BlockSpec(shape, lambda *_: (0,) * nd)


def _block_diag_ones(n):
    idx = np.arange(n) // HEAD_DIM
    return jnp.asarray((idx[:, None] == idx[None, :]).astype(np.float32), dtype=_bf16)


def kernel(x, meta_tokens, rel_bias, mix_norm_g, w_in, q_norm_g, k_norm_g, attn_sinks, conv_w, attn_out_norm_g, conv_out_norm_g, w_out, ffn_norm_g, w_group_router, b_group_router, w_expert_router, b_expert_router, w_gate, w_up, w_down):
    bsz, seq, _ = x.shape
    assert seq % TM == 0 and (bsz * seq) % TMC == 0
    n_tok = bsz * seq
    nt = seq // TM

    win = w_in[0].astype(_bf16)
    wout = w_out[0].astype(_bf16)
    mixg = mix_norm_g[0].reshape(1, D_MODEL)
    fg = ffn_norm_g[0].reshape(1, D_MODEL)
    qg = (jnp.tile(q_norm_g[0], N_Q_HEADS) * (HEAD_DIM ** -0.5)).reshape(1, ATTN_DIM)
    kg = jnp.tile(k_norm_g[0], N_KV_HEADS).reshape(1, KV_DIM)
    ag = attn_out_norm_g[0].reshape(1, ATTN_DIM)
    cg = conv_out_norm_g[0].reshape(1, CONV_CH)
    convw = conv_w[0]
    sinks = attn_sinks[0]
    w_r = jnp.concatenate([w_group_router[0], w_expert_router[0].reshape(D_MODEL, N_EXPERTS)], axis=1)
    n_r = N_GROUPS + N_EXPERTS
    w_r_hi = w_r.astype(_bf16)
    w_r_lo = (w_r - w_r_hi.astype(_f32)).astype(_bf16)
    wr = jnp.zeros((D_MODEL, LANES), _bf16)
    wr = wr.at[:, :n_r].set(w_r_hi).at[:, HEAD_DIM:HEAD_DIM + n_r].set(w_r_lo)
    br = jnp.zeros((1, LANES), _f32).at[0, :n_r].set(
        jnp.concatenate([b_group_router[0], b_expert_router[0].reshape(N_EXPERTS)]))
    meta_blk = jnp.concatenate([jnp.zeros((PAD, D_MODEL), x.dtype), meta_tokens.astype(x.dtype)], axis=0)
    bdq = _block_diag_ones(ATTN_DIM)
    bdk = _block_diag_ones(KV_DIM)
    tri = jnp.asarray(np.tril(np.ones((TM, TM), np.float32), -1), dtype=_bf16)

    kv_sd = jax.ShapeDtypeStruct((BLK, KV_DIM), _bf16)
    k0m, k1m, v0m, v1m, utm, bias = pl.pallas_call(
        _prep_kernel,
        out_shape=(kv_sd, kv_sd, kv_sd, kv_sd,
                   jax.ShapeDtypeStruct((SUBLANES, CONV_CH), _f32),
                   jax.ShapeDtypeStruct((2, N_KV_HEADS, Q_GROUP * BLK, 2 * BLK), _f32)),
        in_specs=[pl.BlockSpec(memory_space=pltpu.SMEM)] + [pl.BlockSpec(memory_space=pltpu.VMEM)] * 6,
        out_specs=tuple(pl.BlockSpec(memory_space=pltpu.VMEM) for _ in range(6)),
        compiler_params=pltpu.CompilerParams(vmem_limit_bytes=VMEM_LIMIT),
        name="prep",
    )(rel_bias, meta_blk, mixg, win, kg, bdk, _bucket_maps())

    consts = (mixg, win, qg, kg, bias, convw, ag, cg, wout, fg, wr, br, k0m, k1m, v0m, v1m, utm, bdq, bdk, tri)
    h2, hn2, route, route_t, cnt = _mixer_call(sinks, x.reshape(n_tok, D_MODEL), consts, nt)
    pos2, t_exp, t_rows, n_tiles = _plan(route_t, cnt, n_tok)
    xs = _dispatch_rows(hn2, pos2[0], pos2[1], n_tiles * TME)
    y_sorted = _experts_call(t_exp, t_rows, xs, w_gate[0], w_up[0], w_down[0], n_tiles)
    out = _combine_call(pos2, y_sorted, h2, route)
    return out.reshape(bsz, seq, D_MODEL)


def _mixer_call(sinks, x_rows, consts, nt):
    n_tok = x_rows.shape[0]
    tile_idx = lambda b, t, *_: (b * nt + t, 0)
    grid_spec = pltpu.PrefetchScalarGridSpec(
        num_scalar_prefetch=1,
        grid=(n_tok // (nt * TM), nt),
        in_specs=[
            pl.BlockSpec((TM, D_MODEL), tile_idx),
            _const((1, D_MODEL)), _const((D_MODEL, IN_PROJ)), _const((1, ATTN_DIM)), _const((1, KV_DIM)),
            _const((2, N_KV_HEADS, Q_GROUP * BLK, 2 * BLK)), _const((3, CONV_CH)),
            _const((1, ATTN_DIM)), _const((1, CONV_CH)), _const((D_MODEL, D_MODEL)), _const((1, D_MODEL)),
            _const((D_MODEL, LANES)), _const((1, LANES)),
            _const((BLK, KV_DIM)), _const((BLK, KV_DIM)), _const((BLK, KV_DIM)), _const((BLK, KV_DIM)),
            _const((SUBLANES, CONV_CH)),
            _const((ATTN_DIM, ATTN_DIM)), _const((KV_DIM, KV_DIM)), _const((TM, TM)),
        ],
        out_specs=[
            pl.BlockSpec((TM, D_MODEL), tile_idx),
            pl.BlockSpec((TM, D_MODEL), tile_idx),
            pl.BlockSpec((TM, ROUTE_W), tile_idx),
            pl.BlockSpec((SUBLANES, TM), lambda b, t, *_: (0, b * nt + t)),
            _const((1, LANES)),
        ],
        scratch_shapes=[
            pltpu.VMEM((BLK, KV_DIM), _bf16), pltpu.VMEM((BLK, KV_DIM), _bf16),
            pltpu.VMEM((BLK, KV_DIM), _bf16), pltpu.VMEM((BLK, KV_DIM), _bf16),
            pltpu.VMEM((TM + SUBLANES, CONV_CH), _f32),
            pltpu.VMEM((TM, ATTN_DIM), _f32),
            pltpu.VMEM((1, LANES), _f32),
        ],
    )
    return pl.pallas_call(
        _mixer_kernel,
        grid_spec=grid_spec,
        out_shape=(jax.ShapeDtypeStruct((n_tok, D_MODEL), _f32),
                   jax.ShapeDtypeStruct((n_tok, D_MODEL), _f32),
                   jax.ShapeDtypeStruct((n_tok, ROUTE_W), _f32),
                   jax.ShapeDtypeStruct((SUBLANES, n_tok), _f32),
                   jax.ShapeDtypeStruct((1, LANES), _f32)),
        compiler_params=pltpu.CompilerParams(dimension_semantics=("arbitrary", "arbitrary"),
                                             vmem_limit_bytes=VMEM_LIMIT),
        name="mixer",
    )(sinks, x_rows, *consts)


def _plan(route_t, cnt, n_tok):
    n_tiles = (n_tok * 2) // TME + N_EXPERTS
    counts = cnt[0, :N_EXPERTS].astype(jnp.int32)
    ntile = (counts + TME - 1) // TME
    tile_end = jnp.cumsum(ntile)
    tile_start = tile_end - ntile
    eid = route_t[0:2].astype(jnp.int32)
    rank = route_t[2:4].astype(jnp.int32)
    experts = jnp.arange(N_EXPERTS, dtype=jnp.int32)
    start_of = jnp.sum(jnp.where(eid[None] == experts[:, None, None], tile_start[:, None, None], 0), axis=0)
    pos2 = start_of * TME + rank
    tiles = jnp.arange(n_tiles, dtype=jnp.int32)
    n_used = tile_end[-1]
    t_exp = jnp.sum((jnp.minimum(tiles, n_used - 1)[:, None] >= tile_end[None, :]).astype(jnp.int32), axis=-1)
    t_exp = jnp.minimum(t_exp, N_EXPERTS - 1)
    own = t_exp[:, None] == experts
    t_rows = jnp.sum(jnp.where(own, counts - (tiles[:, None] - tile_start) * TME, 0), axis=-1)
    t_rows = jnp.where(tiles < n_used, jnp.clip(t_rows, 0, TME), 0).astype(jnp.int32)
    return pos2, t_exp, t_rows, n_tiles


def _experts_call(t_exp, t_rows, xs, w_gate, w_up, w_down, n_tiles):
    return pl.pallas_call(
        _experts_kernel,
        grid_spec=pltpu.PrefetchScalarGridSpec(
            num_scalar_prefetch=2,
            grid=(n_tiles,),
            in_specs=[
                pl.BlockSpec(memory_space=pl.ANY),
                pl.BlockSpec((None, D_MODEL, D_EXPERT), lambda i, te, tn: (te[i], 0, 0)),
                pl.BlockSpec((None, D_MODEL, D_EXPERT), lambda i, te, tn: (te[i], 0, 0)),
                pl.BlockSpec((None, D_EXPERT, D_MODEL), lambda i, te, tn: (te[i], 0, 0)),
            ],
            out_specs=pl.BlockSpec((TME * ROW_CHUNKS, LANES), lambda i, *_: (i, 0)),
            scratch_shapes=[
                pltpu.VMEM((XS_SLOTS, TME, D_MODEL), _f32),
                pltpu.SemaphoreType.DMA((XS_SLOTS,)),
                pltpu.VMEM((D_MODEL, D_EXPERT), _bf16),
                pltpu.VMEM((D_MODEL, D_EXPERT), _bf16),
                pltpu.VMEM((D_EXPERT, D_MODEL), _bf16),
            ],
        ),
        out_shape=jax.ShapeDtypeStruct((n_tiles * TME * ROW_CHUNKS, LANES), _f32),
        compiler_params=pltpu.CompilerParams(dimension_semantics=("arbitrary",),
                                             vmem_limit_bytes=VMEM_LIMIT),
        name="experts",
    )(t_exp, t_rows, xs, w_gate, w_up, w_down)


def _combine_call(pos2, y_sorted, h2, route):
    n_tok = h2.shape[0]
    return pl.pallas_call(
        _combine_kernel,
        grid_spec=pltpu.PrefetchScalarGridSpec(
            num_scalar_prefetch=2,
            grid=(n_tok // TMC,),
            in_specs=[
                pl.BlockSpec(memory_space=pl.ANY),
                pl.BlockSpec((TMC, D_MODEL), lambda i, *_: (i, 0)),
                pl.BlockSpec((TMC, ROUTE_W), lambda i, *_: (i, 0)),
            ],
            out_specs=pl.BlockSpec((TMC, D_MODEL), lambda i, *_: (i, 0)),
            scratch_shapes=[
                pltpu.VMEM((2, 2, TMC * ROW_CHUNKS, LANES), _f32),
                pltpu.SemaphoreType.DMA((2,)),
            ],
        ),
        out_shape=jax.ShapeDtypeStruct((n_tok, D_MODEL), _f32),
        compiler_params=pltpu.CompilerParams(dimension_semantics=("arbitrary",),
                                             vmem_limit_bytes=VMEM_LIMIT),
        name="combine",
    )(pos2[0], pos2[1], y_sorted, h2, route)
```

```python
import functools

import numpy as np
import jax
import jax.numpy as jnp
from jax import lax
from jax.experimental import pallas as pl
from jax.experimental.pallas import tpu as pltpu
from jax.experimental.pallas import tpu_sc as plsc

D_MODEL = 1024
N_META = 16
N_Q_HEADS = 8
N_KV_HEADS = 2
HEAD_DIM = 64
Q_GROUP = N_Q_HEADS // N_KV_HEADS
ATTN_DIM = N_Q_HEADS * HEAD_DIM
KV_DIM = N_KV_HEADS * HEAD_DIM
BLK = 128
PAD = BLK - N_META
N_BUCKETS = 32
MAX_DISTANCE = 128
CONV_CH = D_MODEL // 2
IN_PROJ = ATTN_DIM + 2 * KV_DIM + 3 * CONV_CH
N_GROUPS = 4
EXPERTS_PER_GROUP = 8
N_EXPERTS = N_GROUPS * EXPERTS_PER_GROUP
D_EXPERT = D_MODEL // 2
EPS = 1e-6
NEG_INF = -1e30

LANES = 128
SUBLANES = 8
ROW_CHUNKS = D_MODEL // LANES
TM = 512
TME = 512
TMC = 256
XS_SLOTS = 3
PACKED_D = D_MODEL // 2
SC_WINDOW = 32
DMA_UNROLL = 8
ROUTE_W = 8
VMEM_LIMIT = 56 * 1024 * 1024

Q_OFF, K_OFF, V_OFF = 0, ATTN_DIM, ATTN_DIM + KV_DIM
CB_OFF = ATTN_DIM + 2 * KV_DIM
CC_OFF = CB_OFF + CONV_CH
CH_OFF = CC_OFF + CONV_CH

_f32 = jnp.float32
_bf16 = jnp.bfloat16


def _rms(x, g):
    return x * lax.rsqrt(jnp.mean(x * x, axis=-1, keepdims=True) + EPS) * g


def _dot(a, b):
    return jnp.dot(a, b, preferred_element_type=_f32)


def _dup_halves(x):
    lane = lax.broadcasted_iota(jnp.int32, x.shape, 1)
    sw = pltpu.roll(x, HEAD_DIM, axis=1)
    lo = lane < HEAD_DIM
    return jnp.where(lo, x, sw).astype(_bf16), jnp.where(lo, sw, x).astype(_bf16)


def _kv_state(hn_bf, win_ref, kg_ref, bdk_ref):
    kv = _dot(hn_bf, win_ref[:, K_OFF:K_OFF + 2 * KV_DIM])
    k = kv[:, :KV_DIM]
    v = kv[:, KV_DIM:]
    ssk = _dot((k * k).astype(_bf16), bdk_ref[...])
    kn = k * lax.rsqrt(ssk * (1.0 / HEAD_DIM) + EPS) * kg_ref[...]
    return _dup_halves(kn) + _dup_halves(v)


def _prep_kernel(rb_ref, meta_ref, mixg_ref, win_ref, kg_ref, bdk_ref, bucket_ref,
                 k0_ref, k1_ref, v0_ref, v1_ref, ut_ref, bias_ref):
    hn = _rms(meta_ref[...], mixg_ref[...]).astype(_bf16)
    k0, k1, v0, v1 = _kv_state(hn, win_ref, kg_ref, bdk_ref)
    k0_ref[...] = k0
    k1_ref[...] = k1
    v0_ref[...] = v0
    v1_ref[...] = v1
    cch = _dot(hn, win_ref[:, CC_OFF:CC_OFF + 2 * CONV_CH])
    u = cch[:, :CONV_CH] * cch[:, CONV_CH:]
    ut_ref[...] = u[BLK - SUBLANES:, :]
    for f in range(2):
        bk = bucket_ref[f]
        for h in range(N_Q_HEADS):
            acc = jnp.full((BLK, 2 * BLK), NEG_INF, _f32)
            for b in range(N_BUCKETS):
                acc = jnp.where(bk == b, rb_ref[b, h], acc)
            bias_ref[f, h // Q_GROUP, (h % Q_GROUP) * BLK:(h % Q_GROUP + 1) * BLK, :] = acc


def _mixer_kernel(sinks_ref,
                  x_ref, mixg_ref, win_ref, qg_ref, kg_ref, bias_ref, convw_ref, ag_ref, cg_ref,
                  wout_ref, fg_ref, wr_ref, br_ref, k0m_ref, k1m_ref, v0m_ref, v1m_ref, utm_ref,
                  bdq_ref, bdk_ref, tri_ref,
                  h2_ref, hn2_ref, route_ref, route_t_ref, cnt_ref,
                  kp0, kp1, vp0, vp1, ubuf, a_scr, cnt_acc):
    b = pl.program_id(0)
    t = pl.program_id(1)

    @pl.when(t == 0)
    def _():
        kp0[...] = k0m_ref[...]
        kp1[...] = k1m_ref[...]
        vp0[...] = v0m_ref[...]
        vp1[...] = v1m_ref[...]
        ubuf[0:SUBLANES, :] = utm_ref[...]

    @pl.when(jnp.logical_and(b == 0, t == 0))
    def _():
        cnt_acc[...] = jnp.zeros_like(cnt_acc)

    x = x_ref[...]
    hn = _rms(x, mixg_ref[...]).astype(_bf16)

    q = _dot(hn, win_ref[:, Q_OFF:Q_OFF + ATTN_DIM])
    ssq = _dot((q * q).astype(_bf16), bdq_ref[...])
    qn = (q * lax.rsqrt(ssq * (1.0 / HEAD_DIM) + EPS) * qg_ref[...]).astype(_bf16)
    kd0, kd1, vd0, vd1 = _kv_state(hn, win_ref, kg_ref, bdk_ref)
    kd = (kd0, kd1)
    vd = (vd0, vd1)
    kp = (kp0, kp1)
    vp = (vp0, vp1)

    lane_q = lax.broadcasted_iota(jnp.int32, (BLK, LANES), 1)
    lo_half = lane_q < HEAD_DIM
    row4 = lax.broadcasted_iota(jnp.int32, (Q_GROUP * BLK, 1), 0) // BLK
    first = jnp.where(t == 0, 0, 1)
    zero_bf = jnp.zeros((BLK, LANES), _bf16)

    for j in range(TM // BLK):
        rows = slice(j * BLK, (j + 1) * BLK)
        for g in range(N_KV_HEADS):
            if j == 0:
                kcat = jnp.concatenate([kp[g][...], kd[g][rows]], axis=0)
                vcat = jnp.concatenate([vp[g][...], vd[g][rows]], axis=0)
                bias = bias_ref[first, g]
            else:
                kcat = kd[g][(j - 1) * BLK:(j + 1) * BLK]
                vcat = vd[g][(j - 1) * BLK:(j + 1) * BLK]
                bias = bias_ref[1, g]
            qs = []
            for hh in range(Q_GROUP):
                h = g * Q_GROUP + hh
                qc = qn[rows, (h // 2) * LANES:(h // 2 + 1) * LANES]
                keep = lo_half if h % 2 == 0 else jnp.logical_not(lo_half)
                qs.append(jnp.where(keep, qc, zero_bf))
            q4 = jnp.concatenate(qs, axis=0)
            s = lax.dot_general(q4, kcat, (((1,), (1,)), ((), ())),
                                preferred_element_type=_f32) + bias
            sink = jnp.full((Q_GROUP * BLK, 1), sinks_ref[g * Q_GROUP], _f32)
            for hh in range(1, Q_GROUP):
                sink = jnp.where(row4 == hh, sinks_ref[g * Q_GROUP + hh], sink)
            m = jnp.maximum(jnp.max(s, axis=-1, keepdims=True), sink)
            p = jnp.exp(s - m)
            l = jnp.sum(p, axis=-1, keepdims=True) + jnp.exp(sink - m)
            o = _dot(p.astype(_bf16), vcat) / l
            for pp in range(Q_GROUP // 2):
                ev = o[(2 * pp) * BLK:(2 * pp + 1) * BLK]
                od = o[(2 * pp + 1) * BLK:(2 * pp + 2) * BLK]
                col = g * (Q_GROUP // 2) + pp
                a_scr[rows, col * LANES:(col + 1) * LANES] = jnp.where(lo_half, ev, od)

    last = slice(TM - BLK, TM)
    kp0[...] = kd0[last]
    kp1[...] = kd1[last]
    vp0[...] = vd0[last]
    vp1[...] = vd1[last]

    cb = _dot(hn, win_ref[:, CB_OFF:CB_OFF + CONV_CH])
    cch = _dot(hn, win_ref[:, CC_OFF:CC_OFF + 2 * CONV_CH])
    u = cch[:, :CONV_CH] * cch[:, CONV_CH:]
    ubuf[SUBLANES:, :] = u
    u1 = ubuf[SUBLANES - 1:SUBLANES - 1 + TM, :]
    u2 = ubuf[SUBLANES - 2:SUBLANES - 2 + TM, :]
    c = cb * (convw_ref[0:1, :] * u2 + convw_ref[1:2, :] * u1 + convw_ref[2:3, :] * u)
    ubuf[0:SUBLANES, :] = u[TM - SUBLANES:, :]

    an = _rms(a_scr[...], ag_ref[...]).astype(_bf16)
    cn = _rms(c, cg_ref[...]).astype(_bf16)
    h2 = x + _dot(an, wout_ref[0:ATTN_DIM, :]) + _dot(cn, wout_ref[ATTN_DIM:, :])
    h2_ref[...] = h2

    hn2 = _rms(h2, fg_ref[...])
    hn2_ref[...] = pltpu.pack_elementwise([hn2[:, :PACKED_D], hn2[:, PACKED_D:]], packed_dtype=_bf16)

    hi = hn2.astype(_bf16)
    lo = (hn2 - hi.astype(_f32)).astype(_bf16)
    r1 = _dot(hi, wr_ref[...])
    r2 = _dot(lo, wr_ref[...])
    lg = r1 + pltpu.roll(r1, HEAD_DIM, axis=1) + r2 + br_ref[...]
    lane = lax.broadcasted_iota(jnp.int32, (TM, LANES), 1)
    lanef = lane.astype(_f32)
    ninf = jnp.float32(-jnp.inf)
    big = jnp.float32(LANES)

    gl = jnp.where(lane < N_GROUPS, lg, ninf)
    gmax = jnp.max(gl, axis=-1, keepdims=True)
    gsum = jnp.sum(jnp.exp(gl - gmax), axis=-1, keepdims=True)
    g_p = 1.0 / gsum
    g_idx = jnp.min(jnp.where(gl == gmax, lanef, big), axis=-1, keepdims=True)
    e_lo = N_GROUPS + EXPERTS_PER_GROUP * g_idx
    el = jnp.where(jnp.logical_and(lanef >= e_lo, lanef < e_lo + EXPERTS_PER_GROUP), lg, ninf)
    m1 = jnp.max(el, axis=-1, keepdims=True)
    i1 = jnp.min(jnp.where(el == m1, lanef, big), axis=-1, keepdims=True)
    el2 = jnp.where(lanef == i1, ninf, el)
    m2 = jnp.max(el2, axis=-1, keepdims=True)
    i2 = jnp.min(jnp.where(el2 == m2, lanef, big), axis=-1, keepdims=True)
    ex = jnp.exp(m2 - m1)
    den = 1.0 / (1.0 + ex)
    gate0 = g_p * den
    gate1 = g_p * ex * den
    e0 = i1 - N_GROUPS
    e1 = i2 - N_GROUPS

    oh0 = lanef == e0
    oh1 = lanef == e1
    cmat = (jnp.where(oh0, 1.0, 0.0) + jnp.where(oh1, 1.0, 0.0))
    prefix = _dot(tri_ref[...], cmat.astype(_bf16)) + cnt_acc[...]
    rank0 = jnp.sum(jnp.where(oh0, prefix, 0.0), axis=-1, keepdims=True)
    rank1 = jnp.sum(jnp.where(oh1, prefix, 0.0), axis=-1, keepdims=True)
    cnt_new = cnt_acc[...] + jnp.sum(cmat, axis=0, keepdims=True)
    cnt_acc[...] = cnt_new
    cnt_ref[...] = cnt_new

    lane8 = lax.broadcasted_iota(jnp.int32, (TM, ROUTE_W), 1)
    rec = jnp.zeros((TM, ROUTE_W), _f32)
    for idx, val in enumerate((e0, e1, gate0, gate1, rank0, rank1)):
        rec = jnp.where(lane8 == idx, val, rec)
    route_ref[...] = rec
    lane_r = lax.broadcasted_iota(jnp.int32, (TM, LANES), 1)
    wide = jnp.zeros((TM, LANES), _f32)
    for idx, val in enumerate((e0, e1, rank0, rank1)):
        wide = jnp.where(lane_r == idx, val, wide)
    route_t_ref[...] = wide.T[:SUBLANES, :]


def _row_tile(ref, row):
    return ref.at[pl.ds(pl.multiple_of(row * ROW_CHUNKS, ROW_CHUNKS), ROW_CHUNKS), :]


def _dispatch_rows(hn2_rows, pos0, pos1, n_rows):
    n_tok = hn2_rows.shape[0]
    info = plsc.get_sparse_core_info()
    n_workers = info.num_cores * info.num_subcores
    per_worker = n_tok // n_workers
    n_chunks = per_worker // SC_WINDOW
    assert per_worker * n_workers == n_tok and n_chunks * SC_WINDOW == per_worker and n_chunks % 2 == 0
    mesh = plsc.VectorSubcoreMesh(core_axis_name="core", subcore_axis_name="subcore")

    @functools.partial(
        pl.kernel,
        out_type=jax.ShapeDtypeStruct((n_rows, PACKED_D), jnp.int32),
        mesh=mesh,
        scratch_types=[pltpu.VMEM((SC_WINDOW,), jnp.int32), pltpu.VMEM((SC_WINDOW,), jnp.int32),
                       pltpu.VMEM((SC_WINDOW,), jnp.int32), pltpu.VMEM((SC_WINDOW,), jnp.int32),
                       pltpu.VMEM((SC_WINDOW, PACKED_D), jnp.int32), pltpu.VMEM((SC_WINDOW, PACKED_D), jnp.int32),
                       pltpu.SemaphoreType.DMA((2,)), pltpu.SemaphoreType.DMA((2,))],
        compiler_params=pltpu.CompilerParams(use_tc_tiling_on_sc=True),
        name="dispatch",
    )
    def dispatch(x_hbm, i0_hbm, i1_hbm, o_hbm, i0_a, i0_b, i1_a, i1_b, rows_a, rows_b, sem_ld, sem_st):
        wid = lax.axis_index("subcore") * info.num_cores + lax.axis_index("core")
        base = wid * per_worker
        i0_v, i1_v, rows_v = (i0_a, i0_b), (i1_a, i1_b), (rows_a, rows_b)

        def loads(chunk, b):
            off = pl.multiple_of(base + chunk * SC_WINDOW, SC_WINDOW)
            return (pltpu.make_async_copy(i0_hbm.at[pl.ds(off, SC_WINDOW)], i0_v[b], sem_ld.at[b]),
                    pltpu.make_async_copy(i1_hbm.at[pl.ds(off, SC_WINDOW)], i1_v[b], sem_ld.at[b]),
                    pltpu.make_async_copy(x_hbm.at[pl.ds(off, SC_WINDOW)], rows_v[b], sem_ld.at[b]))

        def stores(b):
            return (pltpu.make_async_copy(rows_v[b], o_hbm.at[i0_v[b]], sem_st.at[b]),
                    pltpu.make_async_copy(rows_v[b], o_hbm.at[i1_v[b]], sem_st.at[b]))

        for d in loads(0, 0):
            d.start()

        @pl.loop(0, n_chunks, step=2)
        def _(c):
            for b in range(2):
                chunk = c + b
                for d in loads(chunk, b):
                    d.wait()
                for d in stores(b):
                    d.start()

                @pl.when(chunk >= 1)
                def _():
                    for d in stores(1 - b):
                        d.wait()

                @pl.when(chunk + 1 < n_chunks)
                def _():
                    for d in loads(chunk + 1, 1 - b):
                        d.start()

        for d in stores(1):
            d.wait()

    return dispatch(hn2_rows, pos0, pos1)


def _experts_kernel(te_ref, tn_ref,
                    xs_hbm, wg_ref, wu_ref, wd_ref,
                    y_ref,
                    xbuf, sem, wgb, wub, wdb):
    i = pl.program_id(0)
    n = pl.num_programs(0)

    def fetch(tile):
        sl = tile % XS_SLOTS
        return pltpu.make_async_copy(xs_hbm.at[pl.ds(pl.multiple_of(tile * TME, TME), TME), :],
                                     xbuf.at[sl], sem.at[sl])

    @pl.when(i == 0)
    def _():
        for tile in range(XS_SLOTS - 1):
            fetch(tile).start()

    @pl.when(i + XS_SLOTS - 1 < n)
    def _():
        fetch(i + XS_SLOTS - 1).start()

    valid = tn_ref[i] > 0
    changed = jnp.logical_or(i == 0, te_ref[i] != te_ref[jnp.maximum(i - 1, 0)])

    @pl.when(jnp.logical_and(changed, valid))
    def _():
        wgb[...] = wg_ref[...].astype(_bf16)
        wub[...] = wu_ref[...].astype(_bf16)
        wdb[...] = wd_ref[...].astype(_bf16)

    fetch(i).wait()

    @pl.when(valid)
    def _():
        packed = xbuf[i % XS_SLOTS]
        halves = [pltpu.unpack_elementwise(packed, index=k, packed_dtype=_bf16, unpacked_dtype=_f32)
                  for k in range(2)]
        xb = jnp.concatenate(halves, axis=1).astype(_bf16)
        live = lax.broadcasted_iota(jnp.int32, (TME, 1), 0) < tn_ref[i]
        xb = jnp.where(live, xb, jnp.zeros_like(xb))
        gate = _dot(xb, wgb[...])
        up = _dot(xb, wub[...])
        act = (gate * jax.nn.sigmoid(gate) * up).astype(_bf16)
        y = _dot(act, wdb[...])
        for cc in range(ROW_CHUNKS):
            y_ref[pl.ds(cc, TME, stride=ROW_CHUNKS), :] = y[:, cc * LANES:(cc + 1) * LANES]

    @pl.when(jnp.logical_not(valid))
    def _():
        y_ref[...] = jnp.zeros_like(y_ref)


def _combine_kernel(pos0_ref, pos1_ref,
                    y_hbm, h2_ref, route_ref,
                    out_ref,
                    ybuf, sem):
    i = pl.program_id(0)
    n = pl.num_programs(0)
    slot = i % 2

    def issue(tile, sl):
        base = tile * TMC
        pos_refs = (pos0_ref, pos1_ref)

        def body(it, carry):
            for j in range(DMA_UNROLL):
                r = it * DMA_UNROLL + j
                for k in range(2):
                    pltpu.make_async_copy(_row_tile(y_hbm, pos_refs[k][base + r]),
                                          _row_tile(ybuf.at[sl, k], r), sem.at[sl]).start(priority=k)
            return carry

        lax.fori_loop(0, TMC // DMA_UNROLL, body, 0)

    @pl.when(i == 0)
    def _():
        issue(0, 0)

    @pl.when(i + 1 < n)
    def _():
        issue(i + 1, 1 - slot)

    for k in range(2):
        pltpu.make_async_copy(y_hbm.at[pl.ds(0, TMC * ROW_CHUNKS), :], ybuf.at[slot, k], sem.at[slot]).wait()
    g0 = route_ref[:, 2:3]
    g1 = route_ref[:, 3:4]
    for cc in range(ROW_CHUNKS):
        y0 = ybuf[slot, 0, pl.ds(cc, TMC, stride=ROW_CHUNKS), :]
        y1 = ybuf[slot, 1, pl.ds(cc, TMC, stride=ROW_CHUNKS), :]
        cols = slice(cc * LANES, (cc + 1) * LANES)
        out_ref[:, cols] = h2_ref[:, cols] + (g0 * y0 + g1 * y1)


def _t5_bucket(n):
    max_exact = N_BUCKETS // 2
    nf = jnp.maximum(n, 1).astype(_f32)
    large = max_exact + (jnp.log(nf / max_exact) / np.log(MAX_DISTANCE / max_exact)
                         * (N_BUCKETS - max_exact)).astype(jnp.int32)
    large = jnp.minimum(large, N_BUCKETS - 1)
    return jnp.where(n < max_exact, n, large)


def _bucket_maps():
    qi = jnp.arange(BLK)[:, None]
    sj = jnp.arange(2 * BLK)[None, :]
    dist = BLK + qi - sj
    band = (dist >= 0) & (dist < BLK)
    bucket = _t5_bucket(jnp.maximum(dist, 0))
    generic = jnp.where(band, bucket, -1)
    first = jnp.where(band & (sj >= PAD), bucket, -1)
    return jnp.stack([first, generic]).astype(jnp.int32)


def _const(shape):
    nd = len(shape)
    return pl.BlockSpec(shape, lambda *_: (0,) * nd)


def _block_diag_ones(n):
    idx = np.arange(n) // HEAD_DIM
    return jnp.asarray((idx[:, None] == idx[None, :]).astype(np.float32), dtype=_bf16)


def kernel(x, meta_tokens, rel_bias, mix_norm_g, w_in, q_norm_g, k_norm_g, attn_sinks, conv_w, attn_out_norm_g, conv_out_norm_g, w_out, ffn_norm_g, w_group_router, b_group_router, w_expert_router, b_expert_router, w_gate, w_up, w_down):
    bsz, seq, _ = x.shape
    assert seq % TM == 0 and (bsz * seq) % TMC == 0
    n_tok = bsz * seq
    nt = seq // TM

    win = w_in[0].astype(_bf16)
    wout = w_out[0].astype(_bf16)
    mixg = mix_norm_g[0].reshape(1, D_MODEL)
    fg = ffn_norm_g[0].reshape(1, D_MODEL)
    qg = (jnp.tile(q_norm_g[0], N_Q_HEADS) * (HEAD_DIM ** -0.5)).reshape(1, ATTN_DIM)
    kg = jnp.tile(k_norm_g[0], N_KV_HEADS).reshape(1, KV_DIM)
    ag = attn_out_norm_g[0].reshape(1, ATTN_DIM)
    cg = conv_out_norm_g[0].reshape(1, CONV_CH)
    convw = conv_w[0]
    sinks = attn_sinks[0]
    w_r = jnp.concatenate([w_group_router[0], w_expert_router[0].reshape(D_MODEL, N_EXPERTS)], axis=1)
    n_r = N_GROUPS + N_EXPERTS
    w_r_hi = w_r.astype(_bf16)
    w_r_lo = (w_r - w_r_hi.astype(_f32)).astype(_bf16)
    wr = jnp.zeros((D_MODEL, LANES), _bf16)
    wr = wr.at[:, :n_r].set(w_r_hi).at[:, HEAD_DIM:HEAD_DIM + n_r].set(w_r_lo)
    br = jnp.zeros((1, LANES), _f32).at[0, :n_r].set(
        jnp.concatenate([b_group_router[0], b_expert_router[0].reshape(N_EXPERTS)]))
    meta_blk = jnp.concatenate([jnp.zeros((PAD, D_MODEL), x.dtype), meta_tokens.astype(x.dtype)], axis=0)
    bdq = _block_diag_ones(ATTN_DIM)
    bdk = _block_diag_ones(KV_DIM)
    tri = jnp.asarray(np.tril(np.ones((TM, TM), np.float32), -1), dtype=_bf16)

    kv_sd = jax.ShapeDtypeStruct((BLK, KV_DIM), _bf16)
    k0m, k1m, v0m, v1m, utm, bias = pl.pallas_call(
        _prep_kernel,
        out_shape=(kv_sd, kv_sd, kv_sd, kv_sd,
                   jax.ShapeDtypeStruct((SUBLANES, CONV_CH), _f32),
                   jax.ShapeDtypeStruct((2, N_KV_HEADS, Q_GROUP * BLK, 2 * BLK), _f32)),
        in_specs=[pl.BlockSpec(memory_space=pltpu.SMEM)] + [pl.BlockSpec(memory_space=pltpu.VMEM)] * 6,
        out_specs=tuple(pl.BlockSpec(memory_space=pltpu.VMEM) for _ in range(6)),
        compiler_params=pltpu.CompilerParams(vmem_limit_bytes=VMEM_LIMIT),
        name="prep",
    )(rel_bias, meta_blk, mixg, win, kg, bdk, _bucket_maps())

    consts = (mixg, win, qg, kg, bias, convw, ag, cg, wout, fg, wr, br, k0m, k1m, v0m, v1m, utm, bdq, bdk, tri)
    h2, hn2, route, route_t, cnt = _mixer_call(sinks, x.reshape(n_tok, D_MODEL), consts, nt)
    pos2, t_exp, t_rows, n_tiles = _plan(route_t, cnt, n_tok)
    xs = _dispatch_rows(hn2, pos2[0], pos2[1], n_tiles * TME)
    y_sorted = _experts_call(t_exp, t_rows, xs, w_gate[0], w_up[0], w_down[0], n_tiles)
    out = _combine_call(pos2, y_sorted, h2, route)
    return out.reshape(bsz, seq, D_MODEL)


def _mixer_call(sinks, x_rows, consts, nt):
    n_tok = x_rows.shape[0]
    tile_idx = lambda b, t, *_: (b * nt + t, 0)
    grid_spec = pltpu.PrefetchScalarGridSpec(
        num_scalar_prefetch=1,
        grid=(n_tok // (nt * TM), nt),
        in_specs=[
            pl.BlockSpec((TM, D_MODEL), tile_idx),
            _const((1, D_MODEL)), _const((D_MODEL, IN_PROJ)), _const((1, ATTN_DIM)), _const((1, KV_DIM)),
            _const((2, N_KV_HEADS, Q_GROUP * BLK, 2 * BLK)), _const((3, CONV_CH)),
            _const((1, ATTN_DIM)), _const((1, CONV_CH)), _const((D_MODEL, D_MODEL)), _const((1, D_MODEL)),
            _const((D_MODEL, LANES)), _const((1, LANES)),
            _const((BLK, KV_DIM)), _const((BLK, KV_DIM)), _const((BLK, KV_DIM)), _const((BLK, KV_DIM)),
            _const((SUBLANES, CONV_CH)),
            _const((ATTN_DIM, ATTN_DIM)), _const((KV_DIM, KV_DIM)), _const((TM, TM)),
        ],
        out_specs=[
            pl.BlockSpec((TM, D_MODEL), tile_idx),
            pl.BlockSpec((TM, PACKED_D), tile_idx),
            pl.BlockSpec((TM, ROUTE_W), tile_idx),
            pl.BlockSpec((SUBLANES, TM), lambda b, t, *_: (0, b * nt + t)),
            _const((1, LANES)),
        ],
        scratch_shapes=[
            pltpu.VMEM((BLK, KV_DIM), _bf16), pltpu.VMEM((BLK, KV_DIM), _bf16),
            pltpu.VMEM((BLK, KV_DIM), _bf16), pltpu.VMEM((BLK, KV_DIM), _bf16),
            pltpu.VMEM((TM + SUBLANES, CONV_CH), _f32),
            pltpu.VMEM((TM, ATTN_DIM), _f32),
            pltpu.VMEM((1, LANES), _f32),
        ],
    )
    return pl.pallas_call(
        _mixer_kernel,
        grid_spec=grid_spec,
        out_shape=(jax.ShapeDtypeStruct((n_tok, D_MODEL), _f32),
                   jax.ShapeDtypeStruct((n_tok, PACKED_D), jnp.int32),
                   jax.ShapeDtypeStruct((n_tok, ROUTE_W), _f32),
                   jax.ShapeDtypeStruct((SUBLANES, n_tok), _f32),
                   jax.ShapeDtypeStruct((1, LANES), _f32)),
        compiler_params=pltpu.CompilerParams(dimension_semantics=("arbitrary", "arbitrary"),
                                             vmem_limit_bytes=VMEM_LIMIT),
        name="mixer",
    )(sinks, x_rows, *consts)


def _plan(route_t, cnt, n_tok):
    n_tiles = (n_tok * 2) // TME + N_EXPERTS
    counts = cnt[0, :N_EXPERTS].astype(jnp.int32)
    ntile = (counts + TME - 1) // TME
    tile_end = jnp.cumsum(ntile)
    tile_start = tile_end - ntile
    eid = route_t[0:2].astype(jnp.int32)
    rank = route_t[2:4].astype(jnp.int32)
    experts = jnp.arange(N_EXPERTS, dtype=jnp.int32)
    start_of = jnp.sum(jnp.where(eid[None] == experts[:, None, None], tile_start[:, None, None], 0), axis=0)
    pos2 = start_of * TME + rank
    tiles = jnp.arange(n_tiles, dtype=jnp.int32)
    n_used = tile_end[-1]
    t_exp = jnp.sum((jnp.minimum(tiles, n_used - 1)[:, None] >= tile_end[None, :]).astype(jnp.int32), axis=-1)
    t_exp = jnp.minimum(t_exp, N_EXPERTS - 1)
    own = t_exp[:, None] == experts
    t_rows = jnp.sum(jnp.where(own, counts - (tiles[:, None] - tile_start) * TME, 0), axis=-1)
    t_rows = jnp.where(tiles < n_used, jnp.clip(t_rows, 0, TME), 0).astype(jnp.int32)
    return pos2, t_exp, t_rows, n_tiles


def _experts_call(t_exp, t_rows, xs, w_gate, w_up, w_down, n_tiles):
    return pl.pallas_call(
        _experts_kernel,
        grid_spec=pltpu.PrefetchScalarGridSpec(
            num_scalar_prefetch=2,
            grid=(n_tiles,),
            in_specs=[
                pl.BlockSpec(memory_space=pl.ANY),
                pl.BlockSpec((None, D_MODEL, D_EXPERT), lambda i, te, tn: (te[i], 0, 0)),
                pl.BlockSpec((None, D_MODEL, D_EXPERT), lambda i, te, tn: (te[i], 0, 0)),
                pl.BlockSpec((None, D_EXPERT, D_MODEL), lambda i, te, tn: (te[i], 0, 0)),
            ],
            out_specs=pl.BlockSpec((TME * ROW_CHUNKS, LANES), lambda i, *_: (i, 0)),
            scratch_shapes=[
                pltpu.VMEM((XS_SLOTS, TME, PACKED_D), jnp.int32),
                pltpu.SemaphoreType.DMA((XS_SLOTS,)),
                pltpu.VMEM((D_MODEL, D_EXPERT), _bf16),
                pltpu.VMEM((D_MODEL, D_EXPERT), _bf16),
                pltpu.VMEM((D_EXPERT, D_MODEL), _bf16),
            ],
        ),
        out_shape=jax.ShapeDtypeStruct((n_tiles * TME * ROW_CHUNKS, LANES), _f32),
        compiler_params=pltpu.CompilerParams(dimension_semantics=("arbitrary",),
                                             vmem_limit_bytes=VMEM_LIMIT),
        name="experts",
    )(t_exp, t_rows, xs, w_gate, w_up, w_down)


def _combine_call(pos2, y_sorted, h2, route):
    n_tok = h2.shape[0]
    return pl.pallas_call(
        _combine_kernel,
        grid_spec=pltpu.PrefetchScalarGridSpec(
            num_scalar_prefetch=2,
            grid=(n_tok // TMC,),
            in_specs=[
                pl.BlockSpec(memory_space=pl.ANY),
                pl.BlockSpec((TMC, D_MODEL), lambda i, *_: (i, 0)),
                pl.BlockSpec((TMC, ROUTE_W), lambda i, *_: (i, 0)),
            ],
            out_specs=pl.BlockSpec((TMC, D_MODEL), lambda i, *_: (i, 0)),
            scratch_shapes=[
                pltpu.VMEM((2, 2, TMC * ROW_CHUNKS, LANES), _f32),
                pltpu.SemaphoreType.DMA((2,)),
            ],
        ),
        out_shape=jax.ShapeDtypeStruct((n_tok, D_MODEL), _f32),
        compiler_params=pltpu.CompilerParams(dimension_semantics=("arbitrary",),
                                             vmem_limit_bytes=VMEM_LIMIT),
        name="combine",
    )(pos2[0], pos2[1], y_sorted, h2, route)
```

```python
import functools

import numpy as np
import jax
import jax.numpy as jnp
from jax import lax
from jax.experimental import pallas as pl
from jax.experimental.pallas import tpu as pltpu
from jax.experimental.pallas import tpu_sc as plsc

D_MODEL = 1024
N_META = 16
N_Q_HEADS = 8
N_KV_HEADS = 2
HEAD_DIM = 64
Q_GROUP = N_Q_HEADS // N_KV_HEADS
ATTN_DIM = N_Q_HEADS * HEAD_DIM
KV_DIM = N_KV_HEADS * HEAD_DIM
BLK = 128
PAD = BLK - N_META
N_BUCKETS = 32
MAX_DISTANCE = 128
CONV_CH = D_MODEL // 2
IN_PROJ = ATTN_DIM + 2 * KV_DIM + 3 * CONV_CH
N_GROUPS = 4
EXPERTS_PER_GROUP = 8
N_EXPERTS = N_GROUPS * EXPERTS_PER_GROUP
D_EXPERT = D_MODEL // 2
EPS = 1e-6
NEG_INF = -1e30
LOG2E = float(np.log2(np.e))

LANES = 128
SUBLANES = 8
ROW_CHUNKS = D_MODEL // LANES
TM = 512
TME = 512
TMC = 512
XS_SLOTS = 3
PACKED_D = D_MODEL // 2
SC_WINDOW = 32
DMA_UNROLL = 16
ROUTE_W = 8
VMEM_LIMIT = 56 * 1024 * 1024

Q_OFF, K_OFF, V_OFF = 0, ATTN_DIM, ATTN_DIM + KV_DIM
CB_OFF = ATTN_DIM + 2 * KV_DIM
CC_OFF = CB_OFF + CONV_CH
CH_OFF = CC_OFF + CONV_CH

_f32 = jnp.float32
_bf16 = jnp.bfloat16


def _rms(x, g):
    return x * lax.rsqrt(jnp.mean(x * x, axis=-1, keepdims=True) + EPS) * g


def _dot(a, b):
    return jnp.dot(a, b, preferred_element_type=_f32)


def _dup_halves(x):
    lane = lax.broadcasted_iota(jnp.int32, x.shape, 1)
    sw = pltpu.roll(x, HEAD_DIM, axis=1)
    lo = lane < HEAD_DIM
    return jnp.where(lo, x, sw).astype(_bf16), jnp.where(lo, sw, x).astype(_bf16)


def _kv_state(hn_bf, win_ref, kg_ref, bdk_ref):
    kv = _dot(hn_bf, win_ref[:, K_OFF:K_OFF + 2 * KV_DIM])
    k = kv[:, :KV_DIM]
    v = kv[:, KV_DIM:]
    ssk = _dot((k * k).astype(_bf16), bdk_ref[...])
    kn = k * lax.rsqrt(ssk + EPS) * kg_ref[...]
    return _dup_halves(kn) + _dup_halves(v)


def _prep_kernel(rb_ref, meta_ref, mixg_ref, win_ref, kg_ref, bdk_ref, bucket_ref,
                 k0_ref, k1_ref, v0_ref, v1_ref, ut_ref, bias_ref):
    hn = _rms(meta_ref[...], mixg_ref[...]).astype(_bf16)
    k0, k1, v0, v1 = _kv_state(hn, win_ref, kg_ref, bdk_ref)
    k0_ref[...] = k0
    k1_ref[...] = k1
    v0_ref[...] = v0
    v1_ref[...] = v1
    cch = _dot(hn, win_ref[:, CC_OFF:CC_OFF + 2 * CONV_CH])
    u = cch[:, :CONV_CH] * cch[:, CONV_CH:]
    ut_ref[...] = u[BLK - SUBLANES:, :]
    for f in range(2):
        bk = bucket_ref[f]
        for h in range(N_Q_HEADS):
            acc = jnp.full((BLK, 2 * BLK), NEG_INF, _f32)
            for b in range(N_BUCKETS):
                acc = jnp.where(bk == b, rb_ref[b, h] * LOG2E, acc)
            bias_ref[f, h // Q_GROUP, (h % Q_GROUP) * BLK:(h % Q_GROUP + 1) * BLK, :] = acc


def _mixer_kernel(sinks_ref,
                  x_ref, mixg_ref, win_ref, qg_ref, kg_ref, bias_ref, convw_ref, ag_ref, cg_ref,
                  wout_ref, fg_ref, wr_ref, br_ref, k0m_ref, k1m_ref, v0m_ref, v1m_ref, utm_ref,
                  bdq_ref, bdk_ref, tri_ref,
                  h2_ref, hn2_ref, route_ref, route_t_ref, cnt_ref,
                  kp0, kp1, vp0, vp1, ubuf, a_scr, cnt_acc):
    b = pl.program_id(0)
    t = pl.program_id(1)

    @pl.when(t == 0)
    def _():
        kp0[...] = k0m_ref[...]
        kp1[...] = k1m_ref[...]
        vp0[...] = v0m_ref[...]
        vp1[...] = v1m_ref[...]
        ubuf[0:SUBLANES, :] = utm_ref[...]

    @pl.when(jnp.logical_and(b == 0, t == 0))
    def _():
        cnt_acc[...] = jnp.zeros_like(cnt_acc)

    x = x_ref[...]
    hn = _rms(x, mixg_ref[...]).astype(_bf16)

    q = _dot(hn, win_ref[:, Q_OFF:Q_OFF + ATTN_DIM])
    ssq = _dot((q * q).astype(_bf16), bdq_ref[...])
    qn = (q * lax.rsqrt(ssq + EPS) * qg_ref[...]).astype(_bf16)
    kd0, kd1, vd0, vd1 = _kv_state(hn, win_ref, kg_ref, bdk_ref)
    kd = (kd0, kd1)
    vd = (vd0, vd1)
    kp = (kp0, kp1)
    vp = (vp0, vp1)

    lane_q = lax.broadcasted_iota(jnp.int32, (BLK, LANES), 1)
    lo_half = lane_q < HEAD_DIM
    row4 = lax.broadcasted_iota(jnp.int32, (Q_GROUP * BLK, 1), 0) // BLK
    first = jnp.where(t == 0, 0, 1)
    zero_bf = jnp.zeros((BLK, LANES), _bf16)

    for j in range(TM // BLK):
        rows = slice(j * BLK, (j + 1) * BLK)
        for g in range(N_KV_HEADS):
            if j == 0:
                kcat = jnp.concatenate([kp[g][...], kd[g][rows]], axis=0)
                vcat = jnp.concatenate([vp[g][...], vd[g][rows]], axis=0)
                bias = bias_ref[first, g]
            else:
                kcat = kd[g][(j - 1) * BLK:(j + 1) * BLK]
                vcat = vd[g][(j - 1) * BLK:(j + 1) * BLK]
                bias = bias_ref[1, g]
            qs = []
            for hh in range(Q_GROUP):
                h = g * Q_GROUP + hh
                qc = qn[rows, (h // 2) * LANES:(h // 2 + 1) * LANES]
                keep = lo_half if h % 2 == 0 else jnp.logical_not(lo_half)
                qs.append(jnp.where(keep, qc, zero_bf))
            q4 = jnp.concatenate(qs, axis=0)
            s = lax.dot_general(q4, kcat, (((1,), (1,)), ((), ())),
                                preferred_element_type=_f32) + bias
            sink = jnp.full((Q_GROUP * BLK, 1), sinks_ref[g * Q_GROUP], _f32)
            for hh in range(1, Q_GROUP):
                sink = jnp.where(row4 == hh, sinks_ref[g * Q_GROUP + hh], sink)
            m = jnp.maximum(jnp.max(s, axis=-1, keepdims=True), sink)
            p = jnp.exp2(s - m)
            l = jnp.sum(p, axis=-1, keepdims=True) + jnp.exp2(sink - m)
            o = _dot(p.astype(_bf16), vcat) / l
            for pp in range(Q_GROUP // 2):
                ev = o[(2 * pp) * BLK:(2 * pp + 1) * BLK]
                od = o[(2 * pp + 1) * BLK:(2 * pp + 2) * BLK]
                col = g * (Q_GROUP // 2) + pp
                a_scr[rows, col * LANES:(col + 1) * LANES] = jnp.where(lo_half, ev, od)

    last = slice(TM - BLK, TM)
    kp0[...] = kd0[last]
    kp1[...] = kd1[last]
    vp0[...] = vd0[last]
    vp1[...] = vd1[last]

    cb = _dot(hn, win_ref[:, CB_OFF:CB_OFF + CONV_CH])
    cch = _dot(hn, win_ref[:, CC_OFF:CC_OFF + 2 * CONV_CH])
    u = cch[:, :CONV_CH] * cch[:, CONV_CH:]
    ubuf[SUBLANES:, :] = u
    u1 = ubuf[SUBLANES - 1:SUBLANES - 1 + TM, :]
    u2 = ubuf[SUBLANES - 2:SUBLANES - 2 + TM, :]
    c = cb * (convw_ref[0:1, :] * u2 + convw_ref[1:2, :] * u1 + convw_ref[2:3, :] * u)
    ubuf[0:SUBLANES, :] = u[TM - SUBLANES:, :]

    an = _rms(a_scr[...], ag_ref[...]).astype(_bf16)
    cn = _rms(c, cg_ref[...]).astype(_bf16)
    h2 = x + _dot(an, wout_ref[0:ATTN_DIM, :]) + _dot(cn, wout_ref[ATTN_DIM:, :])
    h2_ref[...] = h2

    hn2 = _rms(h2, fg_ref[...])
    hn2_ref[...] = pltpu.pack_elementwise([hn2[:, :PACKED_D], hn2[:, PACKED_D:]], packed_dtype=_bf16)

    r1 = _dot(hn2.astype(_bf16), wr_ref[...])
    lg = r1 + pltpu.roll(r1, HEAD_DIM, axis=1) + br_ref[...]
    lane = lax.broadcasted_iota(jnp.int32, (TM, LANES), 1)
    lanef = lane.astype(_f32)
    ninf = jnp.float32(-jnp.inf)
    big = jnp.float32(LANES)

    gl = jnp.where(lane < N_GROUPS, lg, ninf)
    gmax = jnp.max(gl, axis=-1, keepdims=True)
    gsum = jnp.sum(jnp.exp(gl - gmax), axis=-1, keepdims=True)
    g_p = 1.0 / gsum
    g_idx = jnp.min(jnp.where(gl == gmax, lanef, big), axis=-1, keepdims=True)
    e_lo = N_GROUPS + EXPERTS_PER_GROUP * g_idx
    el = jnp.where(jnp.logical_and(lanef >= e_lo, lanef < e_lo + EXPERTS_PER_GROUP), lg, ninf)
    m1 = jnp.max(el, axis=-1, keepdims=True)
    i1 = jnp.min(jnp.where(el == m1, lanef, big), axis=-1, keepdims=True)
    el2 = jnp.where(lanef == i1, ninf, el)
    m2 = jnp.max(el2, axis=-1, keepdims=True)
    i2 = jnp.min(jnp.where(el2 == m2, lanef, big), axis=-1, keepdims=True)
    ex = jnp.exp(m2 - m1)
    den = 1.0 / (1.0 + ex)
    gate0 = g_p * den
    gate1 = g_p * ex * den
    e0 = i1 - N_GROUPS
    e1 = i2 - N_GROUPS

    oh0 = lanef == e0
    oh1 = lanef == e1
    cmat = (jnp.where(oh0, 1.0, 0.0) + jnp.where(oh1, 1.0, 0.0))
    prefix = _dot(tri_ref[...], cmat.astype(_bf16)) + cnt_acc[...]
    rank0 = jnp.sum(jnp.where(oh0, prefix, 0.0), axis=-1, keepdims=True)
    rank1 = jnp.sum(jnp.where(oh1, prefix, 0.0), axis=-1, keepdims=True)
    cnt_new = cnt_acc[...] + jnp.sum(cmat, axis=0, keepdims=True)
    cnt_acc[...] = cnt_new
    cnt_ref[...] = cnt_new

    lane8 = lax.broadcasted_iota(jnp.int32, (TM, ROUTE_W), 1)
    rec = jnp.zeros((TM, ROUTE_W), _f32)
    for idx, val in enumerate((e0, e1, gate0, gate1, rank0, rank1)):
        rec = jnp.where(lane8 == idx, val, rec)
    route_ref[...] = rec
    lane_r = lax.broadcasted_iota(jnp.int32, (TM, LANES), 1)
    wide = jnp.zeros((TM, LANES), _f32)
    for idx, val in enumerate((e0, e1, rank0, rank1)):
        wide = jnp.where(lane_r == idx, val, wide)
    route_t_ref[...] = wide.T[:SUBLANES, :]


def _row_tile(ref, row):
    return ref.at[pl.ds(pl.multiple_of(row * ROW_CHUNKS, ROW_CHUNKS), ROW_CHUNKS), :]


def _dispatch_rows(hn2_rows, pos0, pos1, n_rows):
    n_tok = hn2_rows.shape[0]
    info = plsc.get_sparse_core_info()
    n_workers = info.num_cores * info.num_subcores
    per_worker = n_tok // n_workers
    n_chunks = per_worker // SC_WINDOW
    assert per_worker * n_workers == n_tok and n_chunks * SC_WINDOW == per_worker and n_chunks % 2 == 0
    mesh = plsc.VectorSubcoreMesh(core_axis_name="core", subcore_axis_name="subcore")

    @functools.partial(
        pl.kernel,
        out_type=jax.ShapeDtypeStruct((n_rows, PACKED_D), jnp.int32),
        mesh=mesh,
        scratch_types=[pltpu.VMEM((SC_WINDOW,), jnp.int32), pltpu.VMEM((SC_WINDOW,), jnp.int32),
                       pltpu.VMEM((SC_WINDOW,), jnp.int32), pltpu.VMEM((SC_WINDOW,), jnp.int32),
                       pltpu.VMEM((SC_WINDOW, PACKED_D), jnp.int32), pltpu.VMEM((SC_WINDOW, PACKED_D), jnp.int32),
                       pltpu.SemaphoreType.DMA((2,)), pltpu.SemaphoreType.DMA((2,))],
        compiler_params=pltpu.CompilerParams(use_tc_tiling_on_sc=True),
        name="dispatch",
    )
    def dispatch(x_hbm, i0_hbm, i1_hbm, o_hbm, i0_a, i0_b, i1_a, i1_b, rows_a, rows_b, sem_ld, sem_st):
        wid = lax.axis_index("subcore") * info.num_cores + lax.axis_index("core")
        base = wid * per_worker
        i0_v, i1_v, rows_v = (i0_a, i0_b), (i1_a, i1_b), (rows_a, rows_b)

        def loads(chunk, b):
            off = pl.multiple_of(base + chunk * SC_WINDOW, SC_WINDOW)
            return (pltpu.make_async_copy(i0_hbm.at[pl.ds(off, SC_WINDOW)], i0_v[b], sem_ld.at[b]),
                    pltpu.make_async_copy(i1_hbm.at[pl.ds(off, SC_WINDOW)], i1_v[b], sem_ld.at[b]),
                    pltpu.make_async_copy(x_hbm.at[pl.ds(off, SC_WINDOW)], rows_v[b], sem_ld.at[b]))

        def stores(b):
            return (pltpu.make_async_copy(rows_v[b], o_hbm.at[i0_v[b]], sem_st.at[b]),
                    pltpu.make_async_copy(rows_v[b], o_hbm.at[i1_v[b]], sem_st.at[b]))

        for d in loads(0, 0):
            d.start()

        @pl.loop(0, n_chunks, step=2)
        def _(c):
            for b in range(2):
                chunk = c + b
                for d in loads(chunk, b):
                    d.wait()
                for d in stores(b):
                    d.start()

                @pl.when(chunk >= 1)
                def _():
                    for d in stores(1 - b):
                        d.wait()

                @pl.when(chunk + 1 < n_chunks)
                def _():
                    for d in loads(chunk + 1, 1 - b):
                        d.start()

        for d in stores(1):
            d.wait()

    return dispatch(hn2_rows, pos0, pos1)


def _experts_kernel(te_ref, tn_ref,
                    xs_hbm, wg_ref, wu_ref, wd_ref,
                    y_ref,
                    xbuf, sem, wgb, wub, wdb):
    i = pl.program_id(0)
    n = pl.num_programs(0)

    def fetch(tile):
        sl = tile % XS_SLOTS
        return pltpu.make_async_copy(xs_hbm.at[pl.ds(pl.multiple_of(tile * TME, TME), TME), :],
                                     xbuf.at[sl], sem.at[sl])

    @pl.when(i == 0)
    def _():
        for tile in range(XS_SLOTS - 1):
            fetch(tile).start()

    @pl.when(i + XS_SLOTS - 1 < n)
    def _():
        fetch(i + XS_SLOTS - 1).start()

    valid = tn_ref[i] > 0
    changed = jnp.logical_or(i == 0, te_ref[i] != te_ref[jnp.maximum(i - 1, 0)])

    @pl.when(jnp.logical_and(changed, valid))
    def _():
        wgb[...] = wg_ref[...].astype(_bf16)
        wub[...] = wu_ref[...].astype(_bf16)
        wdb[...] = wd_ref[...].astype(_bf16)

    fetch(i).wait()

    @pl.when(valid)
    def _():
        packed = xbuf[i % XS_SLOTS]
        halves = [pltpu.unpack_elementwise(packed, index=k, packed_dtype=_bf16, unpacked_dtype=_f32)
                  for k in range(2)]
        xb = jnp.concatenate(halves, axis=1).astype(_bf16)
        live = lax.broadcasted_iota(jnp.int32, (TME, 1), 0) < tn_ref[i]
        xb = jnp.where(live, xb, jnp.zeros_like(xb))
        gate = _dot(xb, wgb[...])
        up = _dot(xb, wub[...])
        act = (gate * jax.nn.sigmoid(gate) * up).astype(_bf16)
        y = _dot(act, wdb[...])
        for cc in range(ROW_CHUNKS):
            y_ref[pl.ds(cc, TME, stride=ROW_CHUNKS), :] = y[:, cc * LANES:(cc + 1) * LANES]

    @pl.when(jnp.logical_not(valid))
    def _():
        y_ref[...] = jnp.zeros_like(y_ref)


def _combine_kernel(pos0_ref, pos1_ref,
                    y_hbm, h2_ref, route_ref,
                    out_ref,
                    ybuf, sem):
    i = pl.program_id(0)
    n = pl.num_programs(0)
    slot = i % 2

    def issue(tile, sl):
        base = tile * TMC
        pos_refs = (pos0_ref, pos1_ref)

        def body(it, carry):
            for j in range(DMA_UNROLL):
                r = it * DMA_UNROLL + j
                for k in range(2):
                    pltpu.make_async_copy(_row_tile(y_hbm, pos_refs[k][base + r]),
                                          _row_tile(ybuf.at[sl, k], r), sem.at[sl]).start(priority=k)
            return carry

        lax.fori_loop(0, TMC // DMA_UNROLL, body, 0)

    @pl.when(i == 0)
    def _():
        issue(0, 0)

    @pl.when(i + 1 < n)
    def _():
        issue(i + 1, 1 - slot)

    for k in range(2):
        pltpu.make_async_copy(y_hbm.at[pl.ds(0, TMC * ROW_CHUNKS), :], ybuf.at[slot, k], sem.at[slot]).wait()
    g0 = route_ref[:, 2:3]
    g1 = route_ref[:, 3:4]
    for cc in range(ROW_CHUNKS):
        y0 = ybuf[slot, 0, pl.ds(cc, TMC, stride=ROW_CHUNKS), :]
        y1 = ybuf[slot, 1, pl.ds(cc, TMC, stride=ROW_CHUNKS), :]
        cols = slice(cc * LANES, (cc + 1) * LANES)
        out_ref[:, cols] = h2_ref[:, cols] + (g0 * y0 + g1 * y1)


def _t5_bucket(n):
    max_exact = N_BUCKETS // 2
    nf = jnp.maximum(n, 1).astype(_f32)
    large = max_exact + (jnp.log(nf / max_exact) / np.log(MAX_DISTANCE / max_exact)
                         * (N_BUCKETS - max_exact)).astype(jnp.int32)
    large = jnp.minimum(large, N_BUCKETS - 1)
    return jnp.where(n < max_exact, n, large)


def _bucket_maps():
    qi = jnp.arange(BLK)[:, None]
    sj = jnp.arange(2 * BLK)[None, :]
    dist = BLK + qi - sj
    band = (dist >= 0) & (dist < BLK)
    bucket = _t5_bucket(jnp.maximum(dist, 0))
    generic = jnp.where(band, bucket, -1)
    first = jnp.where(band & (sj >= PAD), bucket, -1)
    return jnp.stack([first, generic]).astype(jnp.int32)


def _const(shape):
    nd = len(shape)
    return pl.BlockSpec(shape, lambda *_: (0,) * nd)


def _block_diag_mean(n):
    idx = np.arange(n) // HEAD_DIM
    return jnp.asarray((idx[:, None] == idx[None, :]).astype(np.float32) / HEAD_DIM, dtype=_bf16)


def kernel(x, meta_tokens, rel_bias, mix_norm_g, w_in, q_norm_g, k_norm_g, attn_sinks, conv_w, attn_out_norm_g, conv_out_norm_g, w_out, ffn_norm_g, w_group_router, b_group_router, w_expert_router, b_expert_router, w_gate, w_up, w_down):
    bsz, seq, _ = x.shape
    assert seq % TM == 0 and (bsz * seq) % TMC == 0
    n_tok = bsz * seq
    nt = seq // TM

    win = w_in[0].astype(_bf16)
    wout = w_out[0].astype(_bf16)
    mixg = mix_norm_g[0].reshape(1, D_MODEL)
    fg = ffn_norm_g[0].reshape(1, D_MODEL)
    qg = (jnp.tile(q_norm_g[0], N_Q_HEADS) * (HEAD_DIM ** -0.5 * LOG2E)).reshape(1, ATTN_DIM)
    kg = jnp.tile(k_norm_g[0], N_KV_HEADS).reshape(1, KV_DIM)
    ag = attn_out_norm_g[0].reshape(1, ATTN_DIM)
    cg = conv_out_norm_g[0].reshape(1, CONV_CH)
    convw = conv_w[0]
    sinks = attn_sinks[0] * LOG2E
    w_r = jnp.concatenate([w_group_router[0], w_expert_router[0].reshape(D_MODEL, N_EXPERTS)], axis=1)
    n_r = N_GROUPS + N_EXPERTS
    w_r_hi = w_r.astype(_bf16)
    w_r_lo = (w_r - w_r_hi.astype(_f32)).astype(_bf16)
    wr = jnp.zeros((D_MODEL, LANES), _bf16)
    wr = wr.at[:, :n_r].set(w_r_hi).at[:, HEAD_DIM:HEAD_DIM + n_r].set(w_r_lo)
    br = jnp.zeros((1, LANES), _f32).at[0, :n_r].set(
        jnp.concatenate([b_group_router[0], b_expert_router[0].reshape(N_EXPERTS)]))
    meta_blk = jnp.concatenate([jnp.zeros((PAD, D_MODEL), x.dtype), meta_tokens.astype(x.dtype)], axis=0)
    bdq = _block_diag_mean(ATTN_DIM)
    bdk = _block_diag_mean(KV_DIM)
    tri = jnp.asarray(np.tril(np.ones((TM, TM), np.float32), -1), dtype=_bf16)

    kv_sd = jax.ShapeDtypeStruct((BLK, KV_DIM), _bf16)
    k0m, k1m, v0m, v1m, utm, bias = pl.pallas_call(
        _prep_kernel,
        out_shape=(kv_sd, kv_sd, kv_sd, kv_sd,
                   jax.ShapeDtypeStruct((SUBLANES, CONV_CH), _f32),
                   jax.ShapeDtypeStruct((2, N_KV_HEADS, Q_GROUP * BLK, 2 * BLK), _f32)),
        in_specs=[pl.BlockSpec(memory_space=pltpu.SMEM)] + [pl.BlockSpec(memory_space=pltpu.VMEM)] * 6,
        out_specs=tuple(pl.BlockSpec(memory_space=pltpu.VMEM) for _ in range(6)),
        compiler_params=pltpu.CompilerParams(vmem_limit_bytes=VMEM_LIMIT),
        name="prep",
    )(rel_bias, meta_blk, mixg, win, kg, bdk, _bucket_maps())

    consts = (mixg, win, qg, kg, bias, convw, ag, cg, wout, fg, wr, br, k0m, k1m, v0m, v1m, utm, bdq, bdk, tri)
    h2, hn2, route, route_t, cnt = _mixer_call(sinks, x.reshape(n_tok, D_MODEL), consts, nt)
    pos2, t_exp, t_rows, n_tiles = _plan(route_t, cnt, n_tok)
    xs = _dispatch_rows(hn2, pos2[0], pos2[1], n_tiles * TME)
    y_sorted = _experts_call(t_exp, t_rows, xs, w_gate[0], w_up[0], w_down[0], n_tiles)
    out = _combine_call(pos2, y_sorted, h2, route)
    return out.reshape(bsz, seq, D_MODEL)


def _mixer_call(sinks, x_rows, consts, nt):
    n_tok = x_rows.shape[0]
    tile_idx = lambda b, t, *_: (b * nt + t, 0)
    grid_spec = pltpu.PrefetchScalarGridSpec(
        num_scalar_prefetch=1,
        grid=(n_tok // (nt * TM), nt),
        in_specs=[
            pl.BlockSpec((TM, D_MODEL), tile_idx),
            _const((1, D_MODEL)), _const((D_MODEL, IN_PROJ)), _const((1, ATTN_DIM)), _const((1, KV_DIM)),
            _const((2, N_KV_HEADS, Q_GROUP * BLK, 2 * BLK)), _const((3, CONV_CH)),
            _const((1, ATTN_DIM)), _const((1, CONV_CH)), _const((D_MODEL, D_MODEL)), _const((1, D_MODEL)),
            _const((D_MODEL, LANES)), _const((1, LANES)),
            _const((BLK, KV_DIM)), _const((BLK, KV_DIM)), _const((BLK, KV_DIM)), _const((BLK, KV_DIM)),
            _const((SUBLANES, CONV_CH)),
            _const((ATTN_DIM, ATTN_DIM)), _const((KV_DIM, KV_DIM)), _const((TM, TM)),
        ],
        out_specs=[
            pl.BlockSpec((TM, D_MODEL), tile_idx),
            pl.BlockSpec((TM, PACKED_D), tile_idx),
            pl.BlockSpec((TM, ROUTE_W), tile_idx),
            pl.BlockSpec((SUBLANES, TM), lambda b, t, *_: (0, b * nt + t)),
            _const((1, LANES)),
        ],
        scratch_shapes=[
            pltpu.VMEM((BLK, KV_DIM), _bf16), pltpu.VMEM((BLK, KV_DIM), _bf16),
            pltpu.VMEM((BLK, KV_DIM), _bf16), pltpu.VMEM((BLK, KV_DIM), _bf16),
            pltpu.VMEM((TM + SUBLANES, CONV_CH), _f32),
            pltpu.VMEM((TM, ATTN_DIM), _f32),
            pltpu.VMEM((1, LANES), _f32),
        ],
    )
    return pl.pallas_call(
        _mixer_kernel,
        grid_spec=grid_spec,
        out_shape=(jax.ShapeDtypeStruct((n_tok, D_MODEL), _f32),
                   jax.ShapeDtypeStruct((n_tok, PACKED_D), jnp.int32),
                   jax.ShapeDtypeStruct((n_tok, ROUTE_W), _f32),
                   jax.ShapeDtypeStruct((SUBLANES, n_tok), _f32),
                   jax.ShapeDtypeStruct((1, LANES), _f32)),
        compiler_params=pltpu.CompilerParams(dimension_semantics=("arbitrary", "arbitrary"),
                                             vmem_limit_bytes=VMEM_LIMIT),
        name="mixer",
    )(sinks, x_rows, *consts)


def _plan(route_t, cnt, n_tok):
    n_tiles = (n_tok * 2) // TME + N_EXPERTS
    counts = cnt[0, :N_EXPERTS].astype(jnp.int32)
    ntile = (counts + TME - 1) // TME
    tile_end = jnp.cumsum(ntile)
    tile_start = tile_end - ntile
    eid = route_t[0:2].astype(jnp.int32)
    rank = route_t[2:4].astype(jnp.int32)
    experts = jnp.arange(N_EXPERTS, dtype=jnp.int32)
    start_of = jnp.sum(jnp.where(eid[None] == experts[:, None, None], tile_start[:, None, None], 0), axis=0)
    pos2 = start_of * TME + rank
    tiles = jnp.arange(n_tiles, dtype=jnp.int32)
    n_used = tile_end[-1]
    t_exp = jnp.sum((jnp.minimum(tiles, n_used - 1)[:, None] >= tile_end[None, :]).astype(jnp.int32), axis=-1)
    t_exp = jnp.minimum(t_exp, N_EXPERTS - 1)
    own = t_exp[:, None] == experts
    t_rows = jnp.sum(jnp.where(own, counts - (tiles[:, None] - tile_start) * TME, 0), axis=-1)
    t_rows = jnp.where(tiles < n_used, jnp.clip(t_rows, 0, TME), 0).astype(jnp.int32)
    return pos2, t_exp, t_rows, n_tiles


def _experts_call(t_exp, t_rows, xs, w_gate, w_up, w_down, n_tiles):
    return pl.pallas_call(
        _experts_kernel,
        grid_spec=pltpu.PrefetchScalarGridSpec(
            num_scalar_prefetch=2,
            grid=(n_tiles,),
            in_specs=[
                pl.BlockSpec(memory_space=pl.ANY),
                pl.BlockSpec((None, D_MODEL, D_EXPERT), lambda i, te, tn: (te[i], 0, 0)),
                pl.BlockSpec((None, D_MODEL, D_EXPERT), lambda i, te, tn: (te[i], 0, 0)),
                pl.BlockSpec((None, D_EXPERT, D_MODEL), lambda i, te, tn: (te[i], 0, 0)),
            ],
            out_specs=pl.BlockSpec((TME * ROW_CHUNKS, LANES), lambda i, *_: (i, 0)),
            scratch_shapes=[
                pltpu.VMEM((XS_SLOTS, TME, PACKED_D), jnp.int32),
                pltpu.SemaphoreType.DMA((XS_SLOTS,)),
                pltpu.VMEM((D_MODEL, D_EXPERT), _bf16),
                pltpu.VMEM((D_MODEL, D_EXPERT), _bf16),
                pltpu.VMEM((D_EXPERT, D_MODEL), _bf16),
            ],
        ),
        out_shape=jax.ShapeDtypeStruct((n_tiles * TME * ROW_CHUNKS, LANES), _f32),
        compiler_params=pltpu.CompilerParams(dimension_semantics=("arbitrary",),
                                             vmem_limit_bytes=VMEM_LIMIT),
        name="experts",
    )(t_exp, t_rows, xs, w_gate, w_up, w_down)


def _combine_call(pos2, y_sorted, h2, route):
    n_tok = h2.shape[0]
    return pl.pallas_call(
        _combine_kernel,
        grid_spec=pltpu.PrefetchScalarGridSpec(
            num_scalar_prefetch=2,
            grid=(n_tok // TMC,),
            in_specs=[
                pl.BlockSpec(memory_space=pl.ANY),
                pl.BlockSpec((TMC, D_MODEL), lambda i, *_: (i, 0)),
                pl.BlockSpec((TMC, ROUTE_W), lambda i, *_: (i, 0)),
            ],
            out_specs=pl.BlockSpec((TMC, D_MODEL), lambda i, *_: (i, 0)),
            scratch_shapes=[
                pltpu.VMEM((2, 2, TMC * ROW_CHUNKS, LANES), _f32),
                pltpu.SemaphoreType.DMA((2,)),
            ],
        ),
        out_shape=jax.ShapeDtypeStruct((n_tok, D_MODEL), _f32),
        compiler_params=pltpu.CompilerParams(dimension_semantics=("arbitrary",),
                                             vmem_limit_bytes=VMEM_LIMIT),
        name="combine",
    )(pos2[0], pos2[1], y_sorted, h2, route)
```

```python
import functools

import numpy as np
import jax
import jax.numpy as jnp
from jax import lax
from jax.experimental import pallas as pl
from jax.experimental.pallas import tpu as pltpu
from jax.experimental.pallas import tpu_sc as plsc

D_MODEL = 1024
N_META = 16
N_Q_HEADS = 8
N_KV_HEADS = 2
HEAD_DIM = 64
Q_GROUP = N_Q_HEADS // N_KV_HEADS
ATTN_DIM = N_Q_HEADS * HEAD_DIM
KV_DIM = N_KV_HEADS * HEAD_DIM
BLK = 128
PAD = BLK - N_META
N_BUCKETS = 32
MAX_DISTANCE = 128
CONV_CH = D_MODEL // 2
IN_PROJ = ATTN_DIM + 2 * KV_DIM + 3 * CONV_CH
N_GROUPS = 4
EXPERTS_PER_GROUP = 8
N_EXPERTS = N_GROUPS * EXPERTS_PER_GROUP
D_EXPERT = D_MODEL // 2
EPS = 1e-6
NEG_INF = -1e30
LOG2E = float(np.log2(np.e))

LANES = 128
SUBLANES = 8
ROW_CHUNKS = D_MODEL // LANES
TM = 512
TME = 512
TMC = TM
WIN = 64
MAX_WIN = N_EXPERTS + 2 * TMC // WIN
CMB_UNROLL = 8
XS_SLOTS = 3
PACKED_D = D_MODEL // 2
SC_WINDOW = 32
VMEM_LIMIT = 56 * 1024 * 1024

Q_OFF, K_OFF, V_OFF = 0, ATTN_DIM, ATTN_DIM + KV_DIM
CB_OFF = ATTN_DIM + 2 * KV_DIM
CC_OFF = CB_OFF + CONV_CH
CH_OFF = CC_OFF + CONV_CH

_f32 = jnp.float32
_bf16 = jnp.bfloat16


def _rms(x, g):
    return x * lax.rsqrt(jnp.mean(x * x, axis=-1, keepdims=True) + EPS) * g


def _dot(a, b):
    return jnp.dot(a, b, preferred_element_type=_f32)


def _dup_halves(x):
    lane = lax.broadcasted_iota(jnp.int32, x.shape, 1)
    sw = pltpu.roll(x, HEAD_DIM, axis=1)
    lo = lane < HEAD_DIM
    return jnp.where(lo, x, sw).astype(_bf16), jnp.where(lo, sw, x).astype(_bf16)


def _kv_state(hn_bf, win_ref, kg_ref, bdk_ref):
    kv = _dot(hn_bf, win_ref[:, K_OFF:K_OFF + 2 * KV_DIM])
    k = kv[:, :KV_DIM]
    v = kv[:, KV_DIM:]
    ssk = _dot((k * k).astype(_bf16), bdk_ref[...])
    kn = k * lax.rsqrt(ssk + EPS) * kg_ref[...]
    return _dup_halves(kn) + _dup_halves(v)


def _prep_kernel(rb_ref, meta_ref, mixg_ref, win_ref, kg_ref, bdk_ref, bucket_ref,
                 k0_ref, k1_ref, v0_ref, v1_ref, ut_ref, bias_ref):
    hn = _rms(meta_ref[...], mixg_ref[...]).astype(_bf16)
    k0, k1, v0, v1 = _kv_state(hn, win_ref, kg_ref, bdk_ref)
    k0_ref[...] = k0
    k1_ref[...] = k1
    v0_ref[...] = v0
    v1_ref[...] = v1
    cch = _dot(hn, win_ref[:, CC_OFF:CC_OFF + 2 * CONV_CH])
    u = cch[:, :CONV_CH] * cch[:, CONV_CH:]
    ut_ref[...] = u[BLK - SUBLANES:, :]
    for f in range(2):
        bk = bucket_ref[f]
        for h in range(N_Q_HEADS):
            acc = jnp.full((BLK, 2 * BLK), NEG_INF, _f32)
            for b in range(N_BUCKETS):
                acc = jnp.where(bk == b, rb_ref[b, h] * LOG2E, acc)
            bias_ref[f, h // Q_GROUP, (h % Q_GROUP) * BLK:(h % Q_GROUP + 1) * BLK, :] = acc


def _mixer_kernel(sinks_ref,
                  x_ref, mixg_ref, win_ref, qg_ref, kg_ref, bias_ref, convw_ref, ag_ref, cg_ref,
                  wout_ref, fg_ref, wr_ref, br_ref, k0m_ref, k1m_ref, v0m_ref, v1m_ref, utm_ref,
                  bdq_ref, bdk_ref, tri_ref,
                  h2_ref, hn2_ref, route_t_ref, cnt_ref, cntb_ref,
                  kp0, kp1, vp0, vp1, ubuf, a_scr, cnt_acc):
    b = pl.program_id(0)
    t = pl.program_id(1)

    @pl.when(t == 0)
    def _():
        kp0[...] = k0m_ref[...]
        kp1[...] = k1m_ref[...]
        vp0[...] = v0m_ref[...]
        vp1[...] = v1m_ref[...]
        ubuf[0:SUBLANES, :] = utm_ref[...]

    @pl.when(jnp.logical_and(b == 0, t == 0))
    def _():
        cnt_acc[...] = jnp.zeros_like(cnt_acc)

    x = x_ref[...]
    hn = _rms(x, mixg_ref[...]).astype(_bf16)

    q = _dot(hn, win_ref[:, Q_OFF:Q_OFF + ATTN_DIM])
    ssq = _dot((q * q).astype(_bf16), bdq_ref[...])
    qn = (q * lax.rsqrt(ssq + EPS) * qg_ref[...]).astype(_bf16)
    kd0, kd1, vd0, vd1 = _kv_state(hn, win_ref, kg_ref, bdk_ref)
    kd = (kd0, kd1)
    vd = (vd0, vd1)
    kp = (kp0, kp1)
    vp = (vp0, vp1)

    lane_q = lax.broadcasted_iota(jnp.int32, (BLK, LANES), 1)
    lo_half = lane_q < HEAD_DIM
    row4 = lax.broadcasted_iota(jnp.int32, (Q_GROUP * BLK, 1), 0) // BLK
    first = jnp.where(t == 0, 0, 1)
    zero_bf = jnp.zeros((BLK, LANES), _bf16)

    for j in range(TM // BLK):
        rows = slice(j * BLK, (j + 1) * BLK)
        for g in range(N_KV_HEADS):
            if j == 0:
                kcat = jnp.concatenate([kp[g][...], kd[g][rows]], axis=0)
                vcat = jnp.concatenate([vp[g][...], vd[g][rows]], axis=0)
                bias = bias_ref[first, g]
            else:
                kcat = kd[g][(j - 1) * BLK:(j + 1) * BLK]
                vcat = vd[g][(j - 1) * BLK:(j + 1) * BLK]
                bias = bias_ref[1, g]
            qs = []
            for hh in range(Q_GROUP):
                h = g * Q_GROUP + hh
                qc = qn[rows, (h // 2) * LANES:(h // 2 + 1) * LANES]
                keep = lo_half if h % 2 == 0 else jnp.logical_not(lo_half)
                qs.append(jnp.where(keep, qc, zero_bf))
            q4 = jnp.concatenate(qs, axis=0)
            s = lax.dot_general(q4, kcat, (((1,), (1,)), ((), ())),
                                preferred_element_type=_f32) + bias
            sink = jnp.full((Q_GROUP * BLK, 1), sinks_ref[g * Q_GROUP], _f32)
            for hh in range(1, Q_GROUP):
                sink = jnp.where(row4 == hh, sinks_ref[g * Q_GROUP + hh], sink)
            m = jnp.maximum(jnp.max(s, axis=-1, keepdims=True), sink)
            p = jnp.exp2(s - m)
            l = jnp.sum(p, axis=-1, keepdims=True) + jnp.exp2(sink - m)
            o = _dot(p.astype(_bf16), vcat) / l
            for pp in range(Q_GROUP // 2):
                ev = o[(2 * pp) * BLK:(2 * pp + 1) * BLK]
                od = o[(2 * pp + 1) * BLK:(2 * pp + 2) * BLK]
                col = g * (Q_GROUP // 2) + pp
                a_scr[rows, col * LANES:(col + 1) * LANES] = jnp.where(lo_half, ev, od)

    last = slice(TM - BLK, TM)
    kp0[...] = kd0[last]
    kp1[...] = kd1[last]
    vp0[...] = vd0[last]
    vp1[...] = vd1[last]

    cb = _dot(hn, win_ref[:, CB_OFF:CB_OFF + CONV_CH])
    cch = _dot(hn, win_ref[:, CC_OFF:CC_OFF + 2 * CONV_CH])
    u = cch[:, :CONV_CH] * cch[:, CONV_CH:]
    ubuf[SUBLANES:, :] = u
    u1 = ubuf[SUBLANES - 1:SUBLANES - 1 + TM, :]
    u2 = ubuf[SUBLANES - 2:SUBLANES - 2 + TM, :]
    c = cb * (convw_ref[0:1, :] * u2 + convw_ref[1:2, :] * u1 + convw_ref[2:3, :] * u)
    ubuf[0:SUBLANES, :] = u[TM - SUBLANES:, :]

    an = _rms(a_scr[...], ag_ref[...]).astype(_bf16)
    cn = _rms(c, cg_ref[...]).astype(_bf16)
    h2 = x + _dot(an, wout_ref[0:ATTN_DIM, :]) + _dot(cn, wout_ref[ATTN_DIM:, :])
    for cc in range(ROW_CHUNKS):
        h2_ref[pl.ds(cc, TM, stride=ROW_CHUNKS), :] = h2[:, cc * LANES:(cc + 1) * LANES]

    hn2 = _rms(h2, fg_ref[...])
    hn2_ref[...] = pltpu.pack_elementwise([hn2[:, :PACKED_D], hn2[:, PACKED_D:]], packed_dtype=_bf16)

    r1 = _dot(hn2.astype(_bf16), wr_ref[...])
    lg = r1 + pltpu.roll(r1, HEAD_DIM, axis=1) + br_ref[...]
    lane = lax.broadcasted_iota(jnp.int32, (TM, LANES), 1)
    lanef = lane.astype(_f32)
    ninf = jnp.float32(-jnp.inf)
    big = jnp.float32(LANES)

    gl = jnp.where(lane < N_GROUPS, lg, ninf)
    gmax = jnp.max(gl, axis=-1, keepdims=True)
    gsum = jnp.sum(jnp.exp(gl - gmax), axis=-1, keepdims=True)
    g_p = 1.0 / gsum
    g_idx = jnp.min(jnp.where(gl == gmax, lanef, big), axis=-1, keepdims=True)
    e_lo = N_GROUPS + EXPERTS_PER_GROUP * g_idx
    el = jnp.where(jnp.logical_and(lanef >= e_lo, lanef < e_lo + EXPERTS_PER_GROUP), lg, ninf)
    m1 = jnp.max(el, axis=-1, keepdims=True)
    i1 = jnp.min(jnp.where(el == m1, lanef, big), axis=-1, keepdims=True)
    el2 = jnp.where(lanef == i1, ninf, el)
    m2 = jnp.max(el2, axis=-1, keepdims=True)
    i2 = jnp.min(jnp.where(el2 == m2, lanef, big), axis=-1, keepdims=True)
    ex = jnp.exp(m2 - m1)
    den = 1.0 / (1.0 + ex)
    gate0 = g_p * den
    gate1 = g_p * ex * den
    e0 = i1 - N_GROUPS
    e1 = i2 - N_GROUPS

    oh0 = lanef == e0
    oh1 = lanef == e1
    cmat = (jnp.where(oh0, 1.0, 0.0) + jnp.where(oh1, 1.0, 0.0))
    prefix = _dot(tri_ref[...], cmat.astype(_bf16)) + cnt_acc[...]
    rank0 = jnp.sum(jnp.where(oh0, prefix, 0.0), axis=-1, keepdims=True)
    rank1 = jnp.sum(jnp.where(oh1, prefix, 0.0), axis=-1, keepdims=True)
    cntb_ref[...] = jnp.broadcast_to(cnt_acc[...], (SUBLANES, LANES))
    cnt_new = cnt_acc[...] + jnp.sum(cmat, axis=0, keepdims=True)
    cnt_acc[...] = cnt_new
    cnt_ref[...] = cnt_new

    lane_r = lax.broadcasted_iota(jnp.int32, (TM, LANES), 1)
    wide = jnp.zeros((TM, LANES), _f32)
    for idx, val in enumerate((e0, e1, rank0, rank1, gate0, gate1)):
        wide = jnp.where(lane_r == idx, val, wide)
    route_t_ref[...] = wide.T[:SUBLANES, :]


def _row_tile(ref, row):
    return ref.at[pl.ds(pl.multiple_of(row * ROW_CHUNKS, ROW_CHUNKS), ROW_CHUNKS), :]


def _dispatch_rows(hn2_rows, pos0, pos1, n_rows):
    n_tok = hn2_rows.shape[0]
    info = plsc.get_sparse_core_info()
    n_workers = info.num_cores * info.num_subcores
    per_worker = n_tok // n_workers
    n_chunks = per_worker // SC_WINDOW
    assert per_worker * n_workers == n_tok and n_chunks * SC_WINDOW == per_worker and n_chunks % 2 == 0
    mesh = plsc.VectorSubcoreMesh(core_axis_name="core", subcore_axis_name="subcore")

    @functools.partial(
        pl.kernel,
        out_type=jax.ShapeDtypeStruct((n_rows, PACKED_D), jnp.int32),
        mesh=mesh,
        scratch_types=[pltpu.VMEM((SC_WINDOW,), jnp.int32), pltpu.VMEM((SC_WINDOW,), jnp.int32),
                       pltpu.VMEM((SC_WINDOW,), jnp.int32), pltpu.VMEM((SC_WINDOW,), jnp.int32),
                       pltpu.VMEM((SC_WINDOW, PACKED_D), jnp.int32), pltpu.VMEM((SC_WINDOW, PACKED_D), jnp.int32),
                       pltpu.SemaphoreType.DMA((2,)), pltpu.SemaphoreType.DMA((2,))],
        compiler_params=pltpu.CompilerParams(use_tc_tiling_on_sc=True),
        name="dispatch",
    )
    def dispatch(x_hbm, i0_hbm, i1_hbm, o_hbm, i0_a, i0_b, i1_a, i1_b, rows_a, rows_b, sem_ld, sem_st):
        wid = lax.axis_index("subcore") * info.num_cores + lax.axis_index("core")
        base = wid * per_worker
        i0_v, i1_v, rows_v = (i0_a, i0_b), (i1_a, i1_b), (rows_a, rows_b)

        def loads(chunk, b):
            off = pl.multiple_of(base + chunk * SC_WINDOW, SC_WINDOW)
            return (pltpu.make_async_copy(i0_hbm.at[pl.ds(off, SC_WINDOW)], i0_v[b], sem_ld.at[b]),
                    pltpu.make_async_copy(i1_hbm.at[pl.ds(off, SC_WINDOW)], i1_v[b], sem_ld.at[b]),
                    pltpu.make_async_copy(x_hbm.at[pl.ds(off, SC_WINDOW)], rows_v[b], sem_ld.at[b]))

        def stores(b):
            return (pltpu.make_async_copy(rows_v[b], o_hbm.at[i0_v[b]], sem_st.at[b]),
                    pltpu.make_async_copy(rows_v[b], o_hbm.at[i1_v[b]], sem_st.at[b]))

        for d in loads(0, 0):
            d.start()

        @pl.loop(0, n_chunks, step=2)
        def _(c):
            for b in range(2):
                chunk = c + b
                for d in loads(chunk, b):
                    d.wait()
                for d in stores(b):
                    d.start()

                @pl.when(chunk >= 1)
                def _():
                    for d in stores(1 - b):
                        d.wait()

                @pl.when(chunk + 1 < n_chunks)
                def _():
                    for d in loads(chunk + 1, 1 - b):
                        d.start()

        for d in stores(1):
            d.wait()

    return dispatch(hn2_rows, pos0, pos1)


def _experts_kernel(te_ref, tn_ref,
                    xs_hbm, wg_ref, wu_ref, wd_ref,
                    y_ref,
                    xbuf, sem, wgb, wub, wdb):
    i = pl.program_id(0)
    n = pl.num_programs(0)

    def fetch(tile):
        sl = tile % XS_SLOTS
        return pltpu.make_async_copy(xs_hbm.at[pl.ds(pl.multiple_of(tile * TME, TME), TME), :],
                                     xbuf.at[sl], sem.at[sl])

    @pl.when(i == 0)
    def _():
        for tile in range(XS_SLOTS - 1):
            fetch(tile).start()

    @pl.when(i + XS_SLOTS - 1 < n)
    def _():
        fetch(i + XS_SLOTS - 1).start()

    valid = tn_ref[i] > 0
    changed = jnp.logical_or(i == 0, te_ref[i] != te_ref[jnp.maximum(i - 1, 0)])

    @pl.when(jnp.logical_and(changed, valid))
    def _():
        wgb[...] = wg_ref[...].astype(_bf16)
        wub[...] = wu_ref[...].astype(_bf16)
        wdb[...] = wd_ref[...].astype(_bf16)

    fetch(i).wait()

    @pl.when(valid)
    def _():
        packed = xbuf[i % XS_SLOTS]
        halves = [pltpu.unpack_elementwise(packed, index=k, packed_dtype=_bf16, unpacked_dtype=_f32)
                  for k in range(2)]
        xb = jnp.concatenate(halves, axis=1).astype(_bf16)
        live = lax.broadcasted_iota(jnp.int32, (TME, 1), 0) < tn_ref[i]
        xb = jnp.where(live, xb, jnp.zeros_like(xb))
        gate = _dot(xb, wgb[...])
        up = _dot(xb, wub[...])
        act = (gate * jax.nn.sigmoid(gate) * up).astype(_bf16)
        y = _dot(act, wdb[...])
        for cc in range(ROW_CHUNKS):
            y_ref[pl.ds(cc, TME, stride=ROW_CHUNKS), :] = y[:, cc * LANES:(cc + 1) * LANES]

    @pl.when(jnp.logical_not(valid))
    def _():
        y_ref[...] = jnp.zeros_like(y_ref)


def _combine_kernel(nwin_ref, wstart_ref, widx0_ref, widx1_ref, gate0_ref, gate1_ref,
                    y_hbm, h2_ref,
                    out_ref,
                    wbuf, sem, otile):
    i = pl.program_id(0)
    n = pl.num_programs(0)
    slot = i % 2
    win_rows = WIN * ROW_CHUNKS

    def window(tile, sl, s):
        src = pl.multiple_of(wstart_ref[tile * MAX_WIN + s] * ROW_CHUNKS, ROW_CHUNKS)
        return pltpu.make_async_copy(y_hbm.at[pl.ds(src, win_rows), :],
                                     wbuf.at[sl, pl.ds(pl.multiple_of(s * win_rows, win_rows), win_rows), :],
                                     sem.at[sl])

    def issue(tile, sl):
        def body(s, carry):
            window(tile, sl, s).start()
            return carry

        lax.fori_loop(0, nwin_ref[tile], body, 0)

    @pl.when(i == 0)
    def _():
        issue(0, 0)

    @pl.when(i + 1 < n)
    def _():
        issue(i + 1, 1 - slot)

    def wait_body(s, carry):
        window(i, slot, s).wait()
        return carry

    lax.fori_loop(0, nwin_ref[i], wait_body, 0)

    def token_body(it, carry):
        for j in range(CMB_UNROLL):
            r = it * CMB_UNROLL + j
            tok = i * TMC + r
            y0 = _row_tile(wbuf.at[slot], widx0_ref[tok])[...]
            y1 = _row_tile(wbuf.at[slot], widx1_ref[tok])[...]
            row = pl.ds(pl.multiple_of(r * ROW_CHUNKS, ROW_CHUNKS), ROW_CHUNKS)
            otile[row, :] = h2_ref[row, :] + (gate0_ref[tok] * y0 + gate1_ref[tok] * y1)
        return carry

    lax.fori_loop(0, TMC // CMB_UNROLL, token_body, 0)
    for cc in range(ROW_CHUNKS):
        out_ref[:, cc * LANES:(cc + 1) * LANES] = otile[pl.ds(cc, TMC, stride=ROW_CHUNKS), :]


def _t5_bucket(n):
    max_exact = N_BUCKETS // 2
    nf = jnp.maximum(n, 1).astype(_f32)
    large = max_exact + (jnp.log(nf / max_exact) / np.log(MAX_DISTANCE / max_exact)
                         * (N_BUCKETS - max_exact)).astype(jnp.int32)
    large = jnp.minimum(large, N_BUCKETS - 1)
    return jnp.where(n < max_exact, n, large)


def _bucket_maps():
    qi = jnp.arange(BLK)[:, None]
    sj = jnp.arange(2 * BLK)[None, :]
    dist = BLK + qi - sj
    band = (dist >= 0) & (dist < BLK)
    bucket = _t5_bucket(jnp.maximum(dist, 0))
    generic = jnp.where(band, bucket, -1)
    first = jnp.where(band & (sj >= PAD), bucket, -1)
    return jnp.stack([first, generic]).astype(jnp.int32)


def _const(shape):
    nd = len(shape)
    return pl.BlockSpec(shape, lambda *_: (0,) * nd)


def _block_diag_mean(n):
    idx = np.arange(n) // HEAD_DIM
    return jnp.asarray((idx[:, None] == idx[None, :]).astype(np.float32) / HEAD_DIM, dtype=_bf16)


def kernel(x, meta_tokens, rel_bias, mix_norm_g, w_in, q_norm_g, k_norm_g, attn_sinks, conv_w, attn_out_norm_g, conv_out_norm_g, w_out, ffn_norm_g, w_group_router, b_group_router, w_expert_router, b_expert_router, w_gate, w_up, w_down):
    bsz, seq, _ = x.shape
    assert seq % TM == 0 and (bsz * seq) % TMC == 0
    n_tok = bsz * seq
    nt = seq // TM

    win = w_in[0].astype(_bf16)
    wout = w_out[0].astype(_bf16)
    mixg = mix_norm_g[0].reshape(1, D_MODEL)
    fg = ffn_norm_g[0].reshape(1, D_MODEL)
    qg = (jnp.tile(q_norm_g[0], N_Q_HEADS) * (HEAD_DIM ** -0.5 * LOG2E)).reshape(1, ATTN_DIM)
    kg = jnp.tile(k_norm_g[0], N_KV_HEADS).reshape(1, KV_DIM)
    ag = attn_out_norm_g[0].reshape(1, ATTN_DIM)
    cg = conv_out_norm_g[0].reshape(1, CONV_CH)
    convw = conv_w[0]
    sinks = attn_sinks[0] * LOG2E
    w_r = jnp.concatenate([w_group_router[0], w_expert_router[0].reshape(D_MODEL, N_EXPERTS)], axis=1)
    n_r = N_GROUPS + N_EXPERTS
    w_r_hi = w_r.astype(_bf16)
    w_r_lo = (w_r - w_r_hi.astype(_f32)).astype(_bf16)
    wr = jnp.zeros((D_MODEL, LANES), _bf16)
    wr = wr.at[:, :n_r].set(w_r_hi).at[:, HEAD_DIM:HEAD_DIM + n_r].set(w_r_lo)
    br = jnp.zeros((1, LANES), _f32).at[0, :n_r].set(
        jnp.concatenate([b_group_router[0], b_expert_router[0].reshape(N_EXPERTS)]))
    meta_blk = jnp.concatenate([jnp.zeros((PAD, D_MODEL), x.dtype), meta_tokens.astype(x.dtype)], axis=0)
    bdq = _block_diag_mean(ATTN_DIM)
    bdk = _block_diag_mean(KV_DIM)
    tri = jnp.asarray(np.tril(np.ones((TM, TM), np.float32), -1), dtype=_bf16)

    kv_sd = jax.ShapeDtypeStruct((BLK, KV_DIM), _bf16)
    k0m, k1m, v0m, v1m, utm, bias = pl.pallas_call(
        _prep_kernel,
        out_shape=(kv_sd, kv_sd, kv_sd, kv_sd,
                   jax.ShapeDtypeStruct((SUBLANES, CONV_CH), _f32),
                   jax.ShapeDtypeStruct((2, N_KV_HEADS, Q_GROUP * BLK, 2 * BLK), _f32)),
        in_specs=[pl.BlockSpec(memory_space=pltpu.SMEM)] + [pl.BlockSpec(memory_space=pltpu.VMEM)] * 6,
        out_specs=tuple(pl.BlockSpec(memory_space=pltpu.VMEM) for _ in range(6)),
        compiler_params=pltpu.CompilerParams(vmem_limit_bytes=VMEM_LIMIT),
        name="prep",
    )(rel_bias, meta_blk, mixg, win, kg, bdk, _bucket_maps())

    consts = (mixg, win, qg, kg, bias, convw, ag, cg, wout, fg, wr, br, k0m, k1m, v0m, v1m, utm, bdq, bdk, tri)
    h2_tiles, hn2, route_t, cnt, cntb = _mixer_call(sinks, x.reshape(n_tok, D_MODEL), consts, nt)
    pos2, t_exp, t_rows, n_tiles, tables = _plan(route_t, cnt, cntb, n_tok)
    xs = _dispatch_rows(hn2, pos2[0], pos2[1], n_tiles * TME)
    y_sorted = _experts_call(t_exp, t_rows, xs, w_gate[0], w_up[0], w_down[0], n_tiles)
    out = _combine_call(tables, y_sorted, h2_tiles)
    return out.reshape(bsz, seq, D_MODEL)


def _mixer_call(sinks, x_rows, consts, nt):
    n_tok = x_rows.shape[0]
    tile_idx = lambda b, t, *_: (b * nt + t, 0)
    grid_spec = pltpu.PrefetchScalarGridSpec(
        num_scalar_prefetch=1,
        grid=(n_tok // (nt * TM), nt),
        in_specs=[
            pl.BlockSpec((TM, D_MODEL), tile_idx),
            _const((1, D_MODEL)), _const((D_MODEL, IN_PROJ)), _const((1, ATTN_DIM)), _const((1, KV_DIM)),
            _const((2, N_KV_HEADS, Q_GROUP * BLK, 2 * BLK)), _const((3, CONV_CH)),
            _const((1, ATTN_DIM)), _const((1, CONV_CH)), _const((D_MODEL, D_MODEL)), _const((1, D_MODEL)),
            _const((D_MODEL, LANES)), _const((1, LANES)),
            _const((BLK, KV_DIM)), _const((BLK, KV_DIM)), _const((BLK, KV_DIM)), _const((BLK, KV_DIM)),
            _const((SUBLANES, CONV_CH)),
            _const((ATTN_DIM, ATTN_DIM)), _const((KV_DIM, KV_DIM)), _const((TM, TM)),
        ],
        out_specs=[
            pl.BlockSpec((TM * ROW_CHUNKS, LANES), tile_idx),
            pl.BlockSpec((TM, PACKED_D), tile_idx),
            pl.BlockSpec((SUBLANES, TM), lambda b, t, *_: (0, b * nt + t)),
            _const((1, LANES)),
            pl.BlockSpec((SUBLANES, LANES), tile_idx),
        ],
        scratch_shapes=[
            pltpu.VMEM((BLK, KV_DIM), _bf16), pltpu.VMEM((BLK, KV_DIM), _bf16),
            pltpu.VMEM((BLK, KV_DIM), _bf16), pltpu.VMEM((BLK, KV_DIM), _bf16),
            pltpu.VMEM((TM + SUBLANES, CONV_CH), _f32),
            pltpu.VMEM((TM, ATTN_DIM), _f32),
            pltpu.VMEM((1, LANES), _f32),
        ],
    )
    return pl.pallas_call(
        _mixer_kernel,
        grid_spec=grid_spec,
        out_shape=(jax.ShapeDtypeStruct((n_tok * ROW_CHUNKS, LANES), _f32),
                   jax.ShapeDtypeStruct((n_tok, PACKED_D), jnp.int32),
                   jax.ShapeDtypeStruct((SUBLANES, n_tok), _f32),
                   jax.ShapeDtypeStruct((1, LANES), _f32),
                   jax.ShapeDtypeStruct((n_tok // TM * SUBLANES, LANES), _f32)),
        compiler_params=pltpu.CompilerParams(dimension_semantics=("arbitrary", "arbitrary"),
                                             vmem_limit_bytes=VMEM_LIMIT),
        name="mixer",
    )(sinks, x_rows, *consts)


def _plan(route_t, cnt, cntb, n_tok):
    n_tiles = (n_tok * 2) // TME + N_EXPERTS
    counts = cnt[0, :N_EXPERTS].astype(jnp.int32)
    ntile = (counts + TME - 1) // TME
    tile_end = jnp.cumsum(ntile)
    tile_start = tile_end - ntile
    eid = route_t[0:2].astype(jnp.int32)
    rank = route_t[2:4].astype(jnp.int32)
    gates = route_t[4:6]
    experts = jnp.arange(N_EXPERTS, dtype=jnp.int32)
    start_of = jnp.sum(jnp.where(eid[None] == experts[:, None, None], tile_start[:, None, None], 0), axis=0)
    pos2 = start_of * TME + rank
    tiles = jnp.arange(n_tiles, dtype=jnp.int32)
    n_used = tile_end[-1]
    t_exp = jnp.sum((jnp.minimum(tiles, n_used - 1)[:, None] >= tile_end[None, :]).astype(jnp.int32), axis=-1)
    t_exp = jnp.minimum(t_exp, N_EXPERTS - 1)
    own = t_exp[:, None] == experts
    t_rows = jnp.sum(jnp.where(own, counts - (tiles[:, None] - tile_start) * TME, 0), axis=-1)
    t_rows = jnp.where(tiles < n_used, jnp.clip(t_rows, 0, TME), 0).astype(jnp.int32)

    n_mix = n_tok // TMC
    before = cntb[::SUBLANES, :N_EXPERTS].astype(jnp.int32)
    held = jnp.concatenate([before[1:], counts[None]], axis=0) - before
    first = tile_start[None, :] * TME + before
    nw = (held + WIN - 1) // WIN
    slot_end = jnp.cumsum(nw, axis=1)
    slot_beg = slot_end - nw
    nwin = slot_end[:, -1]
    slots = jnp.arange(MAX_WIN, dtype=jnp.int32)
    owner = jnp.minimum(jnp.sum((slots[None, :, None] >= slot_end[:, None, :]).astype(jnp.int32), axis=-1),
                        N_EXPERTS - 1)
    is_owner = owner[:, :, None] == experts
    wstart = jnp.sum(jnp.where(is_owner, first[:, None, :] + (slots[None, :, None] - slot_beg[:, None, :]) * WIN,
                               0), axis=-1)
    wstart = jnp.where(slots[None, :] < nwin[:, None], wstart, 0).astype(jnp.int32).reshape(-1)
    mine = eid.reshape(2, n_mix, TMC, 1) == experts
    beg_tok = jnp.sum(jnp.where(mine, slot_beg[None, :, None, :], 0), axis=-1)
    before_tok = jnp.sum(jnp.where(mine, before[None, :, None, :], 0), axis=-1)
    widx = (beg_tok * WIN + rank.reshape(2, n_mix, TMC) - before_tok).reshape(2, n_tok).astype(jnp.int32)
    return pos2, t_exp, t_rows, n_tiles, (nwin.astype(jnp.int32), wstart, widx[0], widx[1], gates[0], gates[1])


def _experts_call(t_exp, t_rows, xs, w_gate, w_up, w_down, n_tiles):
    return pl.pallas_call(
        _experts_kernel,
        grid_spec=pltpu.PrefetchScalarGridSpec(
            num_scalar_prefetch=2,
            grid=(n_tiles,),
            in_specs=[
                pl.BlockSpec(memory_space=pl.ANY),
                pl.BlockSpec((None, D_MODEL, D_EXPERT), lambda i, te, tn: (te[i], 0, 0)),
                pl.BlockSpec((None, D_MODEL, D_EXPERT), lambda i, te, tn: (te[i], 0, 0)),
                pl.BlockSpec((None, D_EXPERT, D_MODEL), lambda i, te, tn: (te[i], 0, 0)),
            ],
            out_specs=pl.BlockSpec((TME * ROW_CHUNKS, LANES), lambda i, *_: (i, 0)),
            scratch_shapes=[
                pltpu.VMEM((XS_SLOTS, TME, PACKED_D), jnp.int32),
                pltpu.SemaphoreType.DMA((XS_SLOTS,)),
                pltpu.VMEM((D_MODEL, D_EXPERT), _bf16),
                pltpu.VMEM((D_MODEL, D_EXPERT), _bf16),
                pltpu.VMEM((D_EXPERT, D_MODEL), _bf16),
            ],
        ),
        out_shape=jax.ShapeDtypeStruct(((n_tiles + 1) * TME * ROW_CHUNKS, LANES), _f32),
        compiler_params=pltpu.CompilerParams(dimension_semantics=("arbitrary",),
                                             vmem_limit_bytes=VMEM_LIMIT),
        name="experts",
    )(t_exp, t_rows, xs, w_gate, w_up, w_down)


def _combine_call(tables, y_sorted, h2_tiles):
    n_tok = h2_tiles.shape[0] // ROW_CHUNKS
    return pl.pallas_call(
        _combine_kernel,
        grid_spec=pltpu.PrefetchScalarGridSpec(
            num_scalar_prefetch=len(tables),
            grid=(n_tok // TMC,),
            in_specs=[
                pl.BlockSpec(memory_space=pl.ANY),
                pl.BlockSpec((TMC * ROW_CHUNKS, LANES), lambda i, *_: (i, 0)),
            ],
            out_specs=pl.BlockSpec((TMC, D_MODEL), lambda i, *_: (i, 0)),
            scratch_shapes=[
                pltpu.VMEM((2, MAX_WIN * WIN * ROW_CHUNKS, LANES), _f32),
                pltpu.SemaphoreType.DMA((2,)),
                pltpu.VMEM((TMC * ROW_CHUNKS, LANES), _f32),
            ],
        ),
        out_shape=jax.ShapeDtypeStruct((n_tok, D_MODEL), _f32),
        compiler_params=pltpu.CompilerParams(dimension_semantics=("arbitrary",),
                                             vmem_limit_bytes=VMEM_LIMIT),
        name="combine",
    )(*tables, y_sorted, h2_tiles)
```

```python
import functools

import numpy as np
import jax
import jax.numpy as jnp
from jax import lax
from jax.experimental import pallas as pl
from jax.experimental.pallas import tpu as pltpu
from jax.experimental.pallas import tpu_sc as plsc

D_MODEL = 1024
N_META = 16
N_Q_HEADS = 8
N_KV_HEADS = 2
HEAD_DIM = 64
Q_GROUP = N_Q_HEADS // N_KV_HEADS
ATTN_DIM = N_Q_HEADS * HEAD_DIM
KV_DIM = N_KV_HEADS * HEAD_DIM
BLK = 128
PAD = BLK - N_META
N_BUCKETS = 32
MAX_DISTANCE = 128
CONV_CH = D_MODEL // 2
IN_PROJ = ATTN_DIM + 2 * KV_DIM + 3 * CONV_CH
N_GROUPS = 4
EXPERTS_PER_GROUP = 8
N_EXPERTS = N_GROUPS * EXPERTS_PER_GROUP
D_EXPERT = D_MODEL // 2
EPS = 1e-6
NEG_INF = -1e30
LOG2E = float(np.log2(np.e))

LANES = 128
SUBLANES = 8
ROW_CHUNKS = D_MODEL // LANES
TM = 512
TME = 512
TMC = TM
WIN = 16
MAX_WIN = N_EXPERTS + 2 * TMC // WIN
CMB_UNROLL = 16
XS_SLOTS = 3
PACKED_D = D_MODEL // 2
SC_WINDOW = 32
VMEM_LIMIT = 56 * 1024 * 1024

Q_OFF, K_OFF, V_OFF = 0, ATTN_DIM, ATTN_DIM + KV_DIM
CB_OFF = ATTN_DIM + 2 * KV_DIM
CC_OFF = CB_OFF + CONV_CH
CH_OFF = CC_OFF + CONV_CH

_f32 = jnp.float32
_bf16 = jnp.bfloat16


def _rms(x, g):
    return x * lax.rsqrt(jnp.mean(x * x, axis=-1, keepdims=True) + EPS) * g


def _dot(a, b):
    return jnp.dot(a, b, preferred_element_type=_f32)


def _dup_halves(x):
    lane = lax.broadcasted_iota(jnp.int32, x.shape, 1)
    sw = pltpu.roll(x, HEAD_DIM, axis=1)
    lo = lane < HEAD_DIM
    return jnp.where(lo, x, sw).astype(_bf16), jnp.where(lo, sw, x).astype(_bf16)


def _kv_state(hn_bf, win_ref, kg_ref, bdk_ref):
    kv = _dot(hn_bf, win_ref[:, K_OFF:K_OFF + 2 * KV_DIM])
    k = kv[:, :KV_DIM]
    v = kv[:, KV_DIM:]
    ssk = _dot((k * k).astype(_bf16), bdk_ref[...])
    kn = k * lax.rsqrt(ssk + EPS) * kg_ref[...]
    return _dup_halves(kn) + _dup_halves(v)


def _prep_kernel(rb_ref, meta_ref, mixg_ref, win_ref, kg_ref, bdk_ref, bucket_ref,
                 k0_ref, k1_ref, v0_ref, v1_ref, ut_ref, bias_ref):
    hn = _rms(meta_ref[...], mixg_ref[...]).astype(_bf16)
    k0, k1, v0, v1 = _kv_state(hn, win_ref, kg_ref, bdk_ref)
    k0_ref[...] = k0
    k1_ref[...] = k1
    v0_ref[...] = v0
    v1_ref[...] = v1
    cch = _dot(hn, win_ref[:, CC_OFF:CC_OFF + 2 * CONV_CH])
    u = cch[:, :CONV_CH] * cch[:, CONV_CH:]
    ut_ref[...] = u[BLK - SUBLANES:, :]
    for f in range(2):
        bk = bucket_ref[f]
        for h in range(N_Q_HEADS):
            acc = jnp.full((BLK, 2 * BLK), NEG_INF, _f32)
            for b in range(N_BUCKETS):
                acc = jnp.where(bk == b, rb_ref[b, h] * LOG2E, acc)
            bias_ref[f, h // Q_GROUP, (h % Q_GROUP) * BLK:(h % Q_GROUP + 1) * BLK, :] = acc


def _mixer_kernel(sinks_ref,
                  x_ref, mixg_ref, win_ref, qg_ref, kg_ref, bias_ref, convw_ref, ag_ref, cg_ref,
                  wout_ref, fg_ref, wr_ref, br_ref, k0m_ref, k1m_ref, v0m_ref, v1m_ref, utm_ref,
                  bdq_ref, bdk_ref, tri_ref,
                  h2_ref, hn2_ref, route_t_ref, cnt_ref, cntb_ref,
                  kp0, kp1, vp0, vp1, ubuf, a_scr, cnt_acc):
    b = pl.program_id(0)
    t = pl.program_id(1)

    @pl.when(t == 0)
    def _():
        kp0[...] = k0m_ref[...]
        kp1[...] = k1m_ref[...]
        vp0[...] = v0m_ref[...]
        vp1[...] = v1m_ref[...]
        ubuf[0:SUBLANES, :] = utm_ref[...]

    @pl.when(jnp.logical_and(b == 0, t == 0))
    def _():
        cnt_acc[...] = jnp.zeros_like(cnt_acc)

    x = x_ref[...]
    hn = _rms(x, mixg_ref[...]).astype(_bf16)

    q = _dot(hn, win_ref[:, Q_OFF:Q_OFF + ATTN_DIM])
    ssq = _dot((q * q).astype(_bf16), bdq_ref[...])
    qn = (q * lax.rsqrt(ssq + EPS) * qg_ref[...]).astype(_bf16)
    kd0, kd1, vd0, vd1 = _kv_state(hn, win_ref, kg_ref, bdk_ref)
    kd = (kd0, kd1)
    vd = (vd0, vd1)
    kp = (kp0, kp1)
    vp = (vp0, vp1)

    lane_q = lax.broadcasted_iota(jnp.int32, (BLK, LANES), 1)
    lo_half = lane_q < HEAD_DIM
    row4 = lax.broadcasted_iota(jnp.int32, (Q_GROUP * BLK, 1), 0) // BLK
    first = jnp.where(t == 0, 0, 1)
    zero_bf = jnp.zeros((BLK, LANES), _bf16)

    for j in range(TM // BLK):
        rows = slice(j * BLK, (j + 1) * BLK)
        for g in range(N_KV_HEADS):
            if j == 0:
                kcat = jnp.concatenate([kp[g][...], kd[g][rows]], axis=0)
                vcat = jnp.concatenate([vp[g][...], vd[g][rows]], axis=0)
                bias = bias_ref[first, g]
            else:
                kcat = kd[g][(j - 1) * BLK:(j + 1) * BLK]
                vcat = vd[g][(j - 1) * BLK:(j + 1) * BLK]
                bias = bias_ref[1, g]
            qs = []
            for hh in range(Q_GROUP):
                h = g * Q_GROUP + hh
                qc = qn[rows, (h // 2) * LANES:(h // 2 + 1) * LANES]
                keep = lo_half if h % 2 == 0 else jnp.logical_not(lo_half)
                qs.append(jnp.where(keep, qc, zero_bf))
            q4 = jnp.concatenate(qs, axis=0)
            s = lax.dot_general(q4, kcat, (((1,), (1,)), ((), ())),
                                preferred_element_type=_f32) + bias
            sink = jnp.full((Q_GROUP * BLK, 1), sinks_ref[g * Q_GROUP], _f32)
            for hh in range(1, Q_GROUP):
                sink = jnp.where(row4 == hh, sinks_ref[g * Q_GROUP + hh], sink)
            m = jnp.maximum(jnp.max(s, axis=-1, keepdims=True), sink)
            p = jnp.exp2(s - m)
            l = jnp.sum(p, axis=-1, keepdims=True) + jnp.exp2(sink - m)
            o = _dot(p.astype(_bf16), vcat) / l
            for pp in range(Q_GROUP // 2):
                ev = o[(2 * pp) * BLK:(2 * pp + 1) * BLK]
                od = o[(2 * pp + 1) * BLK:(2 * pp + 2) * BLK]
                col = g * (Q_GROUP // 2) + pp
                a_scr[rows, col * LANES:(col + 1) * LANES] = jnp.where(lo_half, ev, od)

    last = slice(TM - BLK, TM)
    kp0[...] = kd0[last]
    kp1[...] = kd1[last]
    vp0[...] = vd0[last]
    vp1[...] = vd1[last]

    cb = _dot(hn, win_ref[:, CB_OFF:CB_OFF + CONV_CH])
    cch = _dot(hn, win_ref[:, CC_OFF:CC_OFF + 2 * CONV_CH])
    u = cch[:, :CONV_CH] * cch[:, CONV_CH:]
    ubuf[SUBLANES:, :] = u
    u1 = ubuf[SUBLANES - 1:SUBLANES - 1 + TM, :]
    u2 = ubuf[SUBLANES - 2:SUBLANES - 2 + TM, :]
    c = cb * (convw_ref[0:1, :] * u2 + convw_ref[1:2, :] * u1 + convw_ref[2:3, :] * u)
    ubuf[0:SUBLANES, :] = u[TM - SUBLANES:, :]

    an = _rms(a_scr[...], ag_ref[...]).astype(_bf16)
    cn = _rms(c, cg_ref[...]).astype(_bf16)
    h2 = x + _dot(an, wout_ref[0:ATTN_DIM, :]) + _dot(cn, wout_ref[ATTN_DIM:, :])
    h2_ref[...] = h2

    hn2 = _rms(h2, fg_ref[...])
    hn2_ref[...] = pltpu.pack_elementwise([hn2[:, :PACKED_D], hn2[:, PACKED_D:]], packed_dtype=_bf16)

    r1 = _dot(hn2.astype(_bf16), wr_ref[...])
    lg = r1 + pltpu.roll(r1, HEAD_DIM, axis=1) + br_ref[...]
    lane = lax.broadcasted_iota(jnp.int32, (TM, LANES), 1)
    lanef = lane.astype(_f32)
    ninf = jnp.float32(-jnp.inf)
    big = jnp.float32(LANES)

    gl = jnp.where(lane < N_GROUPS, lg, ninf)
    gmax = jnp.max(gl, axis=-1, keepdims=True)
    gsum = jnp.sum(jnp.exp(gl - gmax), axis=-1, keepdims=True)
    g_p = 1.0 / gsum
    g_idx = jnp.min(jnp.where(gl == gmax, lanef, big), axis=-1, keepdims=True)
    e_lo = N_GROUPS + EXPERTS_PER_GROUP * g_idx
    el = jnp.where(jnp.logical_and(lanef >= e_lo, lanef < e_lo + EXPERTS_PER_GROUP), lg, ninf)
    m1 = jnp.max(el, axis=-1, keepdims=True)
    i1 = jnp.min(jnp.where(el == m1, lanef, big), axis=-1, keepdims=True)
    el2 = jnp.where(lanef == i1, ninf, el)
    m2 = jnp.max(el2, axis=-1, keepdims=True)
    i2 = jnp.min(jnp.where(el2 == m2, lanef, big), axis=-1, keepdims=True)
    ex = jnp.exp(m2 - m1)
    den = 1.0 / (1.0 + ex)
    gate0 = g_p * den
    gate1 = g_p * ex * den
    e0 = i1 - N_GROUPS
    e1 = i2 - N_GROUPS

    oh0 = lanef == e0
    oh1 = lanef == e1
    cmat = (jnp.where(oh0, 1.0, 0.0) + jnp.where(oh1, 1.0, 0.0))
    prefix = _dot(tri_ref[...], cmat.astype(_bf16)) + cnt_acc[...]
    rank0 = jnp.sum(jnp.where(oh0, prefix, 0.0), axis=-1, keepdims=True)
    rank1 = jnp.sum(jnp.where(oh1, prefix, 0.0), axis=-1, keepdims=True)
    cntb_ref[...] = jnp.broadcast_to(cnt_acc[...], (SUBLANES, LANES))
    cnt_new = cnt_acc[...] + jnp.sum(cmat, axis=0, keepdims=True)
    cnt_acc[...] = cnt_new
    cnt_ref[...] = cnt_new

    lane_r = lax.broadcasted_iota(jnp.int32, (TM, LANES), 1)
    wide = jnp.zeros((TM, LANES), _f32)
    for idx, val in enumerate((e0, e1, rank0, rank1, gate0, gate1)):
        wide = jnp.where(lane_r == idx, val, wide)
    route_t_ref[...] = wide.T[:SUBLANES, :]


def _row_tile(ref, row):
    return ref.at[pl.ds(pl.multiple_of(row * ROW_CHUNKS, ROW_CHUNKS), ROW_CHUNKS), :]


def _dispatch_rows(hn2_rows, pos0, pos1, n_rows):
    n_tok = hn2_rows.shape[0]
    info = plsc.get_sparse_core_info()
    n_workers = info.num_cores * info.num_subcores
    per_worker = n_tok // n_workers
    n_chunks = per_worker // SC_WINDOW
    assert per_worker * n_workers == n_tok and n_chunks * SC_WINDOW == per_worker and n_chunks % 2 == 0
    mesh = plsc.VectorSubcoreMesh(core_axis_name="core", subcore_axis_name="subcore")

    @functools.partial(
        pl.kernel,
        out_type=jax.ShapeDtypeStruct((n_rows, PACKED_D), jnp.int32),
        mesh=mesh,
        scratch_types=[pltpu.VMEM((SC_WINDOW,), jnp.int32), pltpu.VMEM((SC_WINDOW,), jnp.int32),
                       pltpu.VMEM((SC_WINDOW,), jnp.int32), pltpu.VMEM((SC_WINDOW,), jnp.int32),
                       pltpu.VMEM((SC_WINDOW, PACKED_D), jnp.int32), pltpu.VMEM((SC_WINDOW, PACKED_D), jnp.int32),
                       pltpu.SemaphoreType.DMA((2,)), pltpu.SemaphoreType.DMA((2,))],
        compiler_params=pltpu.CompilerParams(use_tc_tiling_on_sc=True),
        name="dispatch",
    )
    def dispatch(x_hbm, i0_hbm, i1_hbm, o_hbm, i0_a, i0_b, i1_a, i1_b, rows_a, rows_b, sem_ld, sem_st):
        wid = lax.axis_index("subcore") * info.num_cores + lax.axis_index("core")
        base = wid * per_worker
        i0_v, i1_v, rows_v = (i0_a, i0_b), (i1_a, i1_b), (rows_a, rows_b)

        def loads(chunk, b):
            off = pl.multiple_of(base + chunk * SC_WINDOW, SC_WINDOW)
            return (pltpu.make_async_copy(i0_hbm.at[pl.ds(off, SC_WINDOW)], i0_v[b], sem_ld.at[b]),
                    pltpu.make_async_copy(i1_hbm.at[pl.ds(off, SC_WINDOW)], i1_v[b], sem_ld.at[b]),
                    pltpu.make_async_copy(x_hbm.at[pl.ds(off, SC_WINDOW)], rows_v[b], sem_ld.at[b]))

        def stores(b):
            return (pltpu.make_async_copy(rows_v[b], o_hbm.at[i0_v[b]], sem_st.at[b]),
                    pltpu.make_async_copy(rows_v[b], o_hbm.at[i1_v[b]], sem_st.at[b]))

        for d in loads(0, 0):
            d.start()

        @pl.loop(0, n_chunks, step=2)
        def _(c):
            for b in range(2):
                chunk = c + b
                for d in loads(chunk, b):
                    d.wait()
                for d in stores(b):
                    d.start()

                @pl.when(chunk >= 1)
                def _():
                    for d in stores(1 - b):
                        d.wait()

                @pl.when(chunk + 1 < n_chunks)
                def _():
                    for d in loads(chunk + 1, 1 - b):
                        d.start()

        for d in stores(1):
            d.wait()

    return dispatch(hn2_rows, pos0, pos1)


def _experts_kernel(te_ref, tn_ref,
                    xs_hbm, wg_ref, wu_ref, wd_ref,
                    y_ref,
                    xbuf, sem, wgb, wub, wdb):
    i = pl.program_id(0)
    n = pl.num_programs(0)

    def fetch(tile):
        sl = tile % XS_SLOTS
        return pltpu.make_async_copy(xs_hbm.at[pl.ds(pl.multiple_of(tile * TME, TME), TME), :],
                                     xbuf.at[sl], sem.at[sl])

    @pl.when(i == 0)
    def _():
        for tile in range(XS_SLOTS - 1):
            fetch(tile).start()

    @pl.when(i + XS_SLOTS - 1 < n)
    def _():
        fetch(i + XS_SLOTS - 1).start()

    valid = tn_ref[i] > 0
    changed = jnp.logical_or(i == 0, te_ref[i] != te_ref[jnp.maximum(i - 1, 0)])

    @pl.when(jnp.logical_and(changed, valid))
    def _():
        wgb[...] = wg_ref[...].astype(_bf16)
        wub[...] = wu_ref[...].astype(_bf16)
        wdb[...] = wd_ref[...].astype(_bf16)

    fetch(i).wait()

    @pl.when(valid)
    def _():
        packed = xbuf[i % XS_SLOTS]
        halves = [pltpu.unpack_elementwise(packed, index=k, packed_dtype=_bf16, unpacked_dtype=_f32)
                  for k in range(2)]
        xb = jnp.concatenate(halves, axis=1).astype(_bf16)
        live = lax.broadcasted_iota(jnp.int32, (TME, 1), 0) < tn_ref[i]
        xb = jnp.where(live, xb, jnp.zeros_like(xb))
        gate = _dot(xb, wgb[...])
        up = _dot(xb, wub[...])
        act = (gate * jax.nn.sigmoid(gate) * up).astype(_bf16)
        y = _dot(act, wdb[...])
        for cc in range(ROW_CHUNKS):
            y_ref[pl.ds(cc, TME, stride=ROW_CHUNKS), :] = y[:, cc * LANES:(cc + 1) * LANES]

    @pl.when(jnp.logical_not(valid))
    def _():
        y_ref[...] = jnp.zeros_like(y_ref)


def _combine_kernel(nwin_ref, wstart_ref, widx0_ref, widx1_ref, gate0_ref, gate1_ref,
                    y_hbm, h2_ref,
                    out_ref,
                    wbuf, sem, otile):
    i = pl.program_id(0)
    n = pl.num_programs(0)
    slot = i % 2
    win_rows = WIN * ROW_CHUNKS

    def window(tile, sl, s):
        src = pl.multiple_of(wstart_ref[tile * MAX_WIN + s] * ROW_CHUNKS, ROW_CHUNKS)
        return pltpu.make_async_copy(y_hbm.at[pl.ds(src, win_rows), :],
                                     wbuf.at[sl, pl.ds(pl.multiple_of(s * win_rows, win_rows), win_rows), :],
                                     sem.at[sl])

    def issue(tile, sl):
        def body(s, carry):
            window(tile, sl, s).start()
            return carry

        lax.fori_loop(0, nwin_ref[tile], body, 0)

    @pl.when(i == 0)
    def _():
        issue(0, 0)

    @pl.when(i + 1 < n)
    def _():
        issue(i + 1, 1 - slot)

    def wait_body(s, carry):
        window(i, slot, s).wait()
        return carry

    lax.fori_loop(0, nwin_ref[i], wait_body, 0)

    def token_body(it, carry):
        for j in range(CMB_UNROLL):
            r = it * CMB_UNROLL + j
            tok = i * TMC + r
            y0 = wbuf[slot, pl.ds(pl.multiple_of(widx0_ref[tok], ROW_CHUNKS), ROW_CHUNKS), :]
            y1 = wbuf[slot, pl.ds(pl.multiple_of(widx1_ref[tok], ROW_CHUNKS), ROW_CHUNKS), :]
            row = pl.ds(pl.multiple_of(r * ROW_CHUNKS, ROW_CHUNKS), ROW_CHUNKS)
            otile[row, :] = gate0_ref[tok] * y0 + gate1_ref[tok] * y1
        return carry

    lax.fori_loop(0, TMC // CMB_UNROLL, token_body, 0)
    for cc in range(ROW_CHUNKS):
        cols = slice(cc * LANES, (cc + 1) * LANES)
        out_ref[:, cols] = h2_ref[:, cols] + otile[pl.ds(cc, TMC, stride=ROW_CHUNKS), :]


def _t5_bucket(n):
    max_exact = N_BUCKETS // 2
    nf = jnp.maximum(n, 1).astype(_f32)
    large = max_exact + (jnp.log(nf / max_exact) / np.log(MAX_DISTANCE / max_exact)
                         * (N_BUCKETS - max_exact)).astype(jnp.int32)
    large = jnp.minimum(large, N_BUCKETS - 1)
    return jnp.where(n < max_exact, n, large)


def _bucket_maps():
    qi = jnp.arange(BLK)[:, None]
    sj = jnp.arange(2 * BLK)[None, :]
    dist = BLK + qi - sj
    band = (dist >= 0) & (dist < BLK)
    bucket = _t5_bucket(jnp.maximum(dist, 0))
    generic = jnp.where(band, bucket, -1)
    first = jnp.where(band & (sj >= PAD), bucket, -1)
    return jnp.stack([first, generic]).astype(jnp.int32)


def _const(shape):
    nd = len(shape)
    return pl.BlockSpec(shape, lambda *_: (0,) * nd)


def _block_diag_mean(n):
    idx = np.arange(n) // HEAD_DIM
    return jnp.asarray((idx[:, None] == idx[None, :]).astype(np.float32) / HEAD_DIM, dtype=_bf16)


def kernel(x, meta_tokens, rel_bias, mix_norm_g, w_in, q_norm_g, k_norm_g, attn_sinks, conv_w, attn_out_norm_g, conv_out_norm_g, w_out, ffn_norm_g, w_group_router, b_group_router, w_expert_router, b_expert_router, w_gate, w_up, w_down):
    bsz, seq, _ = x.shape
    assert seq % TM == 0 and (bsz * seq) % TMC == 0
    n_tok = bsz * seq
    nt = seq // TM

    win = w_in[0].astype(_bf16)
    wout = w_out[0].astype(_bf16)
    mixg = mix_norm_g[0].reshape(1, D_MODEL)
    fg = ffn_norm_g[0].reshape(1, D_MODEL)
    qg = (jnp.tile(q_norm_g[0], N_Q_HEADS) * (HEAD_DIM ** -0.5 * LOG2E)).reshape(1, ATTN_DIM)
    kg = jnp.tile(k_norm_g[0], N_KV_HEADS).reshape(1, KV_DIM)
    ag = attn_out_norm_g[0].reshape(1, ATTN_DIM)
    cg = conv_out_norm_g[0].reshape(1, CONV_CH)
    convw = conv_w[0]
    sinks = attn_sinks[0] * LOG2E
    w_r = jnp.concatenate([w_group_router[0], w_expert_router[0].reshape(D_MODEL, N_EXPERTS)], axis=1)
    n_r = N_GROUPS + N_EXPERTS
    w_r_hi = w_r.astype(_bf16)
    w_r_lo = (w_r - w_r_hi.astype(_f32)).astype(_bf16)
    wr = jnp.zeros((D_MODEL, LANES), _bf16)
    wr = wr.at[:, :n_r].set(w_r_hi).at[:, HEAD_DIM:HEAD_DIM + n_r].set(w_r_lo)
    br = jnp.zeros((1, LANES), _f32).at[0, :n_r].set(
        jnp.concatenate([b_group_router[0], b_expert_router[0].reshape(N_EXPERTS)]))
    meta_blk = jnp.concatenate([jnp.zeros((PAD, D_MODEL), x.dtype), meta_tokens.astype(x.dtype)], axis=0)
    bdq = _block_diag_mean(ATTN_DIM)
    bdk = _block_diag_mean(KV_DIM)
    tri = jnp.asarray(np.tril(np.ones((TM, TM), np.float32), -1), dtype=_bf16)

    kv_sd = jax.ShapeDtypeStruct((BLK, KV_DIM), _bf16)
    k0m, k1m, v0m, v1m, utm, bias = pl.pallas_call(
        _prep_kernel,
        out_shape=(kv_sd, kv_sd, kv_sd, kv_sd,
                   jax.ShapeDtypeStruct((SUBLANES, CONV_CH), _f32),
                   jax.ShapeDtypeStruct((2, N_KV_HEADS, Q_GROUP * BLK, 2 * BLK), _f32)),
        in_specs=[pl.BlockSpec(memory_space=pltpu.SMEM)] + [pl.BlockSpec(memory_space=pltpu.VMEM)] * 6,
        out_specs=tuple(pl.BlockSpec(memory_space=pltpu.VMEM) for _ in range(6)),
        compiler_params=pltpu.CompilerParams(vmem_limit_bytes=VMEM_LIMIT),
        name="prep",
    )(rel_bias, meta_blk, mixg, win, kg, bdk, _bucket_maps())

    consts = (mixg, win, qg, kg, bias, convw, ag, cg, wout, fg, wr, br, k0m, k1m, v0m, v1m, utm, bdq, bdk, tri)
    h2, hn2, route_t, cnt, cntb = _mixer_call(sinks, x.reshape(n_tok, D_MODEL), consts, nt)
    pos2, t_exp, t_rows, n_tiles, tables = _plan(route_t, cnt, cntb, n_tok)
    xs = _dispatch_rows(hn2, pos2[0], pos2[1], n_tiles * TME)
    y_sorted = _experts_call(t_exp, t_rows, xs, w_gate[0], w_up[0], w_down[0], n_tiles)
    out = _combine_call(tables, y_sorted, h2)
    return out.reshape(bsz, seq, D_MODEL)


def _mixer_call(sinks, x_rows, consts, nt):
    n_tok = x_rows.shape[0]
    tile_idx = lambda b, t, *_: (b * nt + t, 0)
    grid_spec = pltpu.PrefetchScalarGridSpec(
        num_scalar_prefetch=1,
        grid=(n_tok // (nt * TM), nt),
        in_specs=[
            pl.BlockSpec((TM, D_MODEL), tile_idx),
            _const((1, D_MODEL)), _const((D_MODEL, IN_PROJ)), _const((1, ATTN_DIM)), _const((1, KV_DIM)),
            _const((2, N_KV_HEADS, Q_GROUP * BLK, 2 * BLK)), _const((3, CONV_CH)),
            _const((1, ATTN_DIM)), _const((1, CONV_CH)), _const((D_MODEL, D_MODEL)), _const((1, D_MODEL)),
            _const((D_MODEL, LANES)), _const((1, LANES)),
            _const((BLK, KV_DIM)), _const((BLK, KV_DIM)), _const((BLK, KV_DIM)), _const((BLK, KV_DIM)),
            _const((SUBLANES, CONV_CH)),
            _const((ATTN_DIM, ATTN_DIM)), _const((KV_DIM, KV_DIM)), _const((TM, TM)),
        ],
        out_specs=[
            pl.BlockSpec((TM, D_MODEL), tile_idx),
            pl.BlockSpec((TM, PACKED_D), tile_idx),
            pl.BlockSpec((SUBLANES, TM), lambda b, t, *_: (0, b * nt + t)),
            _const((1, LANES)),
            pl.BlockSpec((SUBLANES, LANES), tile_idx),
        ],
        scratch_shapes=[
            pltpu.VMEM((BLK, KV_DIM), _bf16), pltpu.VMEM((BLK, KV_DIM), _bf16),
            pltpu.VMEM((BLK, KV_DIM), _bf16), pltpu.VMEM((BLK, KV_DIM), _bf16),
            pltpu.VMEM((TM + SUBLANES, CONV_CH), _f32),
            pltpu.VMEM((TM, ATTN_DIM), _f32),
            pltpu.VMEM((1, LANES), _f32),
        ],
    )
    return pl.pallas_call(
        _mixer_kernel,
        grid_spec=grid_spec,
        out_shape=(jax.ShapeDtypeStruct((n_tok, D_MODEL), _f32),
                   jax.ShapeDtypeStruct((n_tok, PACKED_D), jnp.int32),
                   jax.ShapeDtypeStruct((SUBLANES, n_tok), _f32),
                   jax.ShapeDtypeStruct((1, LANES), _f32),
                   jax.ShapeDtypeStruct((n_tok // TM * SUBLANES, LANES), _f32)),
        compiler_params=pltpu.CompilerParams(dimension_semantics=("arbitrary", "arbitrary"),
                                             vmem_limit_bytes=VMEM_LIMIT),
        name="mixer",
    )(sinks, x_rows, *consts)


def _plan(route_t, cnt, cntb, n_tok):
    n_tiles = (n_tok * 2) // TME + N_EXPERTS
    counts = cnt[0, :N_EXPERTS].astype(jnp.int32)
    ntile = (counts + TME - 1) // TME
    tile_end = jnp.cumsum(ntile)
    tile_start = tile_end - ntile
    eid = route_t[0:2].astype(jnp.int32)
    rank = route_t[2:4].astype(jnp.int32)
    gates = route_t[4:6]
    experts = jnp.arange(N_EXPERTS, dtype=jnp.int32)
    start_of = jnp.sum(jnp.where(eid[None] == experts[:, None, None], tile_start[:, None, None], 0), axis=0)
    pos2 = start_of * TME + rank
    tiles = jnp.arange(n_tiles, dtype=jnp.int32)
    n_used = tile_end[-1]
    t_exp = jnp.sum((jnp.minimum(tiles, n_used - 1)[:, None] >= tile_end[None, :]).astype(jnp.int32), axis=-1)
    t_exp = jnp.minimum(t_exp, N_EXPERTS - 1)
    own = t_exp[:, None] == experts
    t_rows = jnp.sum(jnp.where(own, counts - (tiles[:, None] - tile_start) * TME, 0), axis=-1)
    t_rows = jnp.where(tiles < n_used, jnp.clip(t_rows, 0, TME), 0).astype(jnp.int32)

    n_mix = n_tok // TMC
    before = cntb[::SUBLANES, :N_EXPERTS].astype(jnp.int32)
    held = jnp.concatenate([before[1:], counts[None]], axis=0) - before
    first = tile_start[None, :] * TME + before
    nw = (held + WIN - 1) // WIN
    slot_end = jnp.cumsum(nw, axis=1)
    slot_beg = slot_end - nw
    nwin = slot_end[:, -1]
    slots = jnp.arange(MAX_WIN, dtype=jnp.int32)
    owner = jnp.minimum(jnp.sum((slots[None, :, None] >= slot_end[:, None, :]).astype(jnp.int32), axis=-1),
                        N_EXPERTS - 1)
    is_owner = owner[:, :, None] == experts
    wstart = jnp.sum(jnp.where(is_owner, first[:, None, :] + (slots[None, :, None] - slot_beg[:, None, :]) * WIN,
                               0), axis=-1)
    wstart = jnp.where(slots[None, :] < nwin[:, None], wstart, 0).astype(jnp.int32).reshape(-1)
    mine = eid.reshape(2, n_mix, TMC, 1) == experts
    beg_tok = jnp.sum(jnp.where(mine, slot_beg[None, :, None, :], 0), axis=-1)
    before_tok = jnp.sum(jnp.where(mine, before[None, :, None, :], 0), axis=-1)
    widx = (beg_tok * WIN + rank.reshape(2, n_mix, TMC) - before_tok).reshape(2, n_tok).astype(jnp.int32)
    widx = widx * ROW_CHUNKS
    return pos2, t_exp, t_rows, n_tiles, (nwin.astype(jnp.int32), wstart, widx[0], widx[1], gates[0], gates[1])


def _experts_call(t_exp, t_rows, xs, w_gate, w_up, w_down, n_tiles):
    return pl.pallas_call(
        _experts_kernel,
        grid_spec=pltpu.PrefetchScalarGridSpec(
            num_scalar_prefetch=2,
            grid=(n_tiles,),
            in_specs=[
                pl.BlockSpec(memory_space=pl.ANY),
                pl.BlockSpec((None, D_MODEL, D_EXPERT), lambda i, te, tn: (te[i], 0, 0)),
                pl.BlockSpec((None, D_MODEL, D_EXPERT), lambda i, te, tn: (te[i], 0, 0)),
                pl.BlockSpec((None, D_EXPERT, D_MODEL), lambda i, te, tn: (te[i], 0, 0)),
            ],
            out_specs=pl.BlockSpec((TME * ROW_CHUNKS, LANES), lambda i, *_: (i, 0)),
            scratch_shapes=[
                pltpu.VMEM((XS_SLOTS, TME, PACKED_D), jnp.int32),
                pltpu.SemaphoreType.DMA((XS_SLOTS,)),
                pltpu.VMEM((D_MODEL, D_EXPERT), _bf16),
                pltpu.VMEM((D_MODEL, D_EXPERT), _bf16),
                pltpu.VMEM((D_EXPERT, D_MODEL), _bf16),
            ],
        ),
        out_shape=jax.ShapeDtypeStruct(((n_tiles + 1) * TME * ROW_CHUNKS, LANES), _f32),
        compiler_params=pltpu.CompilerParams(dimension_semantics=("arbitrary",),
                                             vmem_limit_bytes=VMEM_LIMIT),
        name="experts",
    )(t_exp, t_rows, xs, w_gate, w_up, w_down)


def _combine_call(tables, y_sorted, h2):
    n_tok = h2.shape[0]
    return pl.pallas_call(
        _combine_kernel,
        grid_spec=pltpu.PrefetchScalarGridSpec(
            num_scalar_prefetch=len(tables),
            grid=(n_tok // TMC,),
            in_specs=[
                pl.BlockSpec(memory_space=pl.ANY),
                pl.BlockSpec((TMC, D_MODEL), lambda i, *_: (i, 0)),
            ],
            out_specs=pl.BlockSpec((TMC, D_MODEL), lambda i, *_: (i, 0)),
            scratch_shapes=[
                pltpu.VMEM((2, MAX_WIN * WIN * ROW_CHUNKS, LANES), _f32),
                pltpu.SemaphoreType.DMA((2,)),
                pltpu.VMEM((TMC * ROW_CHUNKS, LANES), _f32),
            ],
        ),
        out_shape=jax.ShapeDtypeStruct((n_tok, D_MODEL), _f32),
        compiler_params=pltpu.CompilerParams(dimension_semantics=("arbitrary",),
                                             vmem_limit_bytes=VMEM_LIMIT),
        name="combine",
    )(*tables, y_sorted, h2)
```

```python
import functools

import numpy as np
import jax
import jax.numpy as jnp
from jax import lax
from jax.experimental import pallas as pl
from jax.experimental.pallas import tpu as pltpu
from jax.experimental.pallas import tpu_sc as plsc

D_MODEL = 1024
N_META = 16
N_Q_HEADS = 8
N_KV_HEADS = 2
HEAD_DIM = 64
Q_GROUP = N_Q_HEADS // N_KV_HEADS
ATTN_DIM = N_Q_HEADS * HEAD_DIM
KV_DIM = N_KV_HEADS * HEAD_DIM
BLK = 128
PAD = BLK - N_META
N_BUCKETS = 32
MAX_DISTANCE = 128
CONV_CH = D_MODEL // 2
IN_PROJ = ATTN_DIM + 2 * KV_DIM + 3 * CONV_CH
N_GROUPS = 4
EXPERTS_PER_GROUP = 8
N_EXPERTS = N_GROUPS * EXPERTS_PER_GROUP
D_EXPERT = D_MODEL // 2
EPS = 1e-6
NEG_INF = -1e30
LOG2E = float(np.log2(np.e))

LANES = 128
SUBLANES = 8
ROW_CHUNKS = D_MODEL // LANES
TM = 512
TME = 512
TMC = TM
WIN = 16
MAX_WIN = N_EXPERTS + 2 * TMC // WIN
CMB_UNROLL = 16
XS_SLOTS = 3
PACKED_D = D_MODEL // 2
SC_WINDOW = 32
VMEM_LIMIT = 56 * 1024 * 1024

Q_OFF, K_OFF, V_OFF = 0, ATTN_DIM, ATTN_DIM + KV_DIM
CB_OFF = ATTN_DIM + 2 * KV_DIM
CC_OFF = CB_OFF + CONV_CH
CH_OFF = CC_OFF + CONV_CH

_f32 = jnp.float32
_bf16 = jnp.bfloat16


def _rms(x, g):
    return x * lax.rsqrt(jnp.mean(x * x, axis=-1, keepdims=True) + EPS) * g


def _dot(a, b):
    return jnp.dot(a, b, preferred_element_type=_f32)


def _dup_halves(x):
    lane = lax.broadcasted_iota(jnp.int32, x.shape, 1)
    sw = pltpu.roll(x, HEAD_DIM, axis=1)
    lo = lane < HEAD_DIM
    return jnp.where(lo, x, sw).astype(_bf16), jnp.where(lo, sw, x).astype(_bf16)


def _kv_state(hn_bf, win_ref, kg_ref, bdk_ref):
    kv = _dot(hn_bf, win_ref[:, K_OFF:K_OFF + 2 * KV_DIM])
    k = kv[:, :KV_DIM]
    v = kv[:, KV_DIM:]
    ssk = _dot((k * k).astype(_bf16), bdk_ref[...])
    kn = k * lax.rsqrt(ssk + EPS) * kg_ref[...]
    return _dup_halves(kn) + _dup_halves(v)


def _prep_kernel(rb_ref, meta_ref, mixg_ref, win_ref, kg_ref, bdk_ref, bucket_ref,
                 k0_ref, k1_ref, v0_ref, v1_ref, ut_ref, bias_ref):
    hn = _rms(meta_ref[...], mixg_ref[...]).astype(_bf16)
    k0, k1, v0, v1 = _kv_state(hn, win_ref, kg_ref, bdk_ref)
    k0_ref[...] = k0
    k1_ref[...] = k1
    v0_ref[...] = v0
    v1_ref[...] = v1
    cch = _dot(hn, win_ref[:, CC_OFF:CC_OFF + 2 * CONV_CH])
    u = cch[:, :CONV_CH] * cch[:, CONV_CH:]
    ut_ref[...] = u[BLK - SUBLANES:, :]
    for f in range(2):
        bk = bucket_ref[f]
        for h in range(N_Q_HEADS):
            acc = jnp.full((BLK, 2 * BLK), NEG_INF, _f32)
            for b in range(N_BUCKETS):
                acc = jnp.where(bk == b, rb_ref[b, h] * LOG2E, acc)
            bias_ref[f, h // Q_GROUP, (h % Q_GROUP) * BLK:(h % Q_GROUP + 1) * BLK, :] = acc


def _mixer_kernel(sinks_ref,
                  x_ref, mixg_ref, win_ref, qg_ref, kg_ref, bias_ref, convw_ref, ag_ref, cg_ref,
                  wout_ref, fg_ref, wr_ref, br_ref, k0m_ref, k1m_ref, v0m_ref, v1m_ref, utm_ref,
                  bdq_ref, bdk_ref, tri_ref,
                  h2_ref, hn2_ref, route_t_ref, cnt_ref, cntb_ref,
                  kp0, kp1, vp0, vp1, ubuf, a_scr, cnt_acc):
    b = pl.program_id(0)
    t = pl.program_id(1)

    @pl.when(t == 0)
    def _():
        kp0[...] = k0m_ref[...]
        kp1[...] = k1m_ref[...]
        vp0[...] = v0m_ref[...]
        vp1[...] = v1m_ref[...]
        ubuf[0:SUBLANES, :] = utm_ref[...]

    @pl.when(jnp.logical_and(b == 0, t == 0))
    def _():
        cnt_acc[...] = jnp.zeros_like(cnt_acc)

    x = x_ref[...]
    hn = _rms(x, mixg_ref[...]).astype(_bf16)

    q = _dot(hn, win_ref[:, Q_OFF:Q_OFF + ATTN_DIM])
    ssq = _dot((q * q).astype(_bf16), bdq_ref[...])
    qn = (q * lax.rsqrt(ssq + EPS) * qg_ref[...]).astype(_bf16)
    kd0, kd1, vd0, vd1 = _kv_state(hn, win_ref, kg_ref, bdk_ref)
    kd = (kd0, kd1)
    vd = (vd0, vd1)
    kp = (kp0, kp1)
    vp = (vp0, vp1)

    lane_q = lax.broadcasted_iota(jnp.int32, (BLK, LANES), 1)
    lo_half = lane_q < HEAD_DIM
    row4 = lax.broadcasted_iota(jnp.int32, (Q_GROUP * BLK, 1), 0) // BLK
    first = jnp.where(t == 0, 0, 1)
    zero_bf = jnp.zeros((BLK, LANES), _bf16)

    for j in range(TM // BLK):
        rows = slice(j * BLK, (j + 1) * BLK)
        for g in range(N_KV_HEADS):
            if j == 0:
                kcat = jnp.concatenate([kp[g][...], kd[g][rows]], axis=0)
                vcat = jnp.concatenate([vp[g][...], vd[g][rows]], axis=0)
                bias = bias_ref[first, g]
            else:
                kcat = kd[g][(j - 1) * BLK:(j + 1) * BLK]
                vcat = vd[g][(j - 1) * BLK:(j + 1) * BLK]
                bias = bias_ref[1, g]
            qs = []
            for hh in range(Q_GROUP):
                h = g * Q_GROUP + hh
                qc = qn[rows, (h // 2) * LANES:(h // 2 + 1) * LANES]
                keep = lo_half if h % 2 == 0 else jnp.logical_not(lo_half)
                qs.append(jnp.where(keep, qc, zero_bf))
            q4 = jnp.concatenate(qs, axis=0)
            s = lax.dot_general(q4, kcat, (((1,), (1,)), ((), ())),
                                preferred_element_type=_f32) + bias
            sink = jnp.full((Q_GROUP * BLK, 1), sinks_ref[g * Q_GROUP], _f32)
            for hh in range(1, Q_GROUP):
                sink = jnp.where(row4 == hh, sinks_ref[g * Q_GROUP + hh], sink)
            m = jnp.maximum(jnp.max(s, axis=-1, keepdims=True), sink)
            p = jnp.exp2(s - m)
            l = jnp.sum(p, axis=-1, keepdims=True) + jnp.exp2(sink - m)
            o = _dot(p.astype(_bf16), vcat) / l
            for pp in range(Q_GROUP // 2):
                ev = o[(2 * pp) * BLK:(2 * pp + 1) * BLK]
                od = o[(2 * pp + 1) * BLK:(2 * pp + 2) * BLK]
                col = g * (Q_GROUP // 2) + pp
                a_scr[rows, col * LANES:(col + 1) * LANES] = jnp.where(lo_half, ev, od)

    last = slice(TM - BLK, TM)
    kp0[...] = kd0[last]
    kp1[...] = kd1[last]
    vp0[...] = vd0[last]
    vp1[...] = vd1[last]

    cb = _dot(hn, win_ref[:, CB_OFF:CB_OFF + CONV_CH])
    cch = _dot(hn, win_ref[:, CC_OFF:CC_OFF + 2 * CONV_CH])
    u = cch[:, :CONV_CH] * cch[:, CONV_CH:]
    ubuf[SUBLANES:, :] = u
    u1 = ubuf[SUBLANES - 1:SUBLANES - 1 + TM, :]
    u2 = ubuf[SUBLANES - 2:SUBLANES - 2 + TM, :]
    c = cb * (convw_ref[0:1, :] * u2 + convw_ref[1:2, :] * u1 + convw_ref[2:3, :] * u)
    ubuf[0:SUBLANES, :] = u[TM - SUBLANES:, :]

    an = _rms(a_scr[...], ag_ref[...]).astype(_bf16)
    cn = _rms(c, cg_ref[...]).astype(_bf16)
    h2 = x + _dot(an, wout_ref[0:ATTN_DIM, :]) + _dot(cn, wout_ref[ATTN_DIM:, :])
    for cc in range(ROW_CHUNKS):
        h2_ref[pl.ds(cc, TM, stride=ROW_CHUNKS), :] = h2[:, cc * LANES:(cc + 1) * LANES]

    hn2 = _rms(h2, fg_ref[...])
    hn2_ref[...] = pltpu.pack_elementwise([hn2[:, :PACKED_D], hn2[:, PACKED_D:]], packed_dtype=_bf16)

    r1 = _dot(hn2.astype(_bf16), wr_ref[...])
    lg = r1 + pltpu.roll(r1, HEAD_DIM, axis=1) + br_ref[...]
    lane = lax.broadcasted_iota(jnp.int32, (TM, LANES), 1)
    lanef = lane.astype(_f32)
    ninf = jnp.float32(-jnp.inf)
    big = jnp.float32(LANES)

    gl = jnp.where(lane < N_GROUPS, lg, ninf)
    gmax = jnp.max(gl, axis=-1, keepdims=True)
    gsum = jnp.sum(jnp.exp(gl - gmax), axis=-1, keepdims=True)
    g_p = 1.0 / gsum
    g_idx = jnp.min(jnp.where(gl == gmax, lanef, big), axis=-1, keepdims=True)
    e_lo = N_GROUPS + EXPERTS_PER_GROUP * g_idx
    el = jnp.where(jnp.logical_and(lanef >= e_lo, lanef < e_lo + EXPERTS_PER_GROUP), lg, ninf)
    m1 = jnp.max(el, axis=-1, keepdims=True)
    i1 = jnp.min(jnp.where(el == m1, lanef, big), axis=-1, keepdims=True)
    el2 = jnp.where(lanef == i1, ninf, el)
    m2 = jnp.max(el2, axis=-1, keepdims=True)
    i2 = jnp.min(jnp.where(el2 == m2, lanef, big), axis=-1, keepdims=True)
    ex = jnp.exp(m2 - m1)
    den = 1.0 / (1.0 + ex)
    gate0 = g_p * den
    gate1 = g_p * ex * den
    e0 = i1 - N_GROUPS
    e1 = i2 - N_GROUPS

    oh0 = lanef == e0
    oh1 = lanef == e1
    cmat = (jnp.where(oh0, 1.0, 0.0) + jnp.where(oh1, 1.0, 0.0))
    prefix = _dot(tri_ref[...], cmat.astype(_bf16)) + cnt_acc[...]
    rank0 = jnp.sum(jnp.where(oh0, prefix, 0.0), axis=-1, keepdims=True)
    rank1 = jnp.sum(jnp.where(oh1, prefix, 0.0), axis=-1, keepdims=True)
    cntb_ref[...] = jnp.broadcast_to(cnt_acc[...], (SUBLANES, LANES))
    cnt_new = cnt_acc[...] + jnp.sum(cmat, axis=0, keepdims=True)
    cnt_acc[...] = cnt_new
    cnt_ref[...] = cnt_new

    lane_r = lax.broadcasted_iota(jnp.int32, (TM, LANES), 1)
    wide = jnp.zeros((TM, LANES), _f32)
    for idx, val in enumerate((e0, e1, rank0, rank1, gate0, gate1)):
        wide = jnp.where(lane_r == idx, val, wide)
    route_t_ref[...] = wide.T[:SUBLANES, :]


def _row_tile(ref, row):
    return ref.at[pl.ds(pl.multiple_of(row * ROW_CHUNKS, ROW_CHUNKS), ROW_CHUNKS), :]


def _dispatch_rows(hn2_rows, pos0, pos1, n_rows):
    n_tok = hn2_rows.shape[0]
    info = plsc.get_sparse_core_info()
    n_workers = info.num_cores * info.num_subcores
    per_worker = n_tok // n_workers
    n_chunks = per_worker // SC_WINDOW
    assert per_worker * n_workers == n_tok and n_chunks * SC_WINDOW == per_worker and n_chunks % 2 == 0
    mesh = plsc.VectorSubcoreMesh(core_axis_name="core", subcore_axis_name="subcore")

    @functools.partial(
        pl.kernel,
        out_type=jax.ShapeDtypeStruct((n_rows, PACKED_D), jnp.int32),
        mesh=mesh,
        scratch_types=[pltpu.VMEM((SC_WINDOW,), jnp.int32), pltpu.VMEM((SC_WINDOW,), jnp.int32),
                       pltpu.VMEM((SC_WINDOW,), jnp.int32), pltpu.VMEM((SC_WINDOW,), jnp.int32),
                       pltpu.VMEM((SC_WINDOW, PACKED_D), jnp.int32), pltpu.VMEM((SC_WINDOW, PACKED_D), jnp.int32),
                       pltpu.SemaphoreType.DMA((2,)), pltpu.SemaphoreType.DMA((2,))],
        compiler_params=pltpu.CompilerParams(use_tc_tiling_on_sc=True),
        name="dispatch",
    )
    def dispatch(x_hbm, i0_hbm, i1_hbm, o_hbm, i0_a, i0_b, i1_a, i1_b, rows_a, rows_b, sem_ld, sem_st):
        wid = lax.axis_index("subcore") * info.num_cores + lax.axis_index("core")
        base = wid * per_worker
        i0_v, i1_v, rows_v = (i0_a, i0_b), (i1_a, i1_b), (rows_a, rows_b)

        def loads(chunk, b):
            off = pl.multiple_of(base + chunk * SC_WINDOW, SC_WINDOW)
            return (pltpu.make_async_copy(i0_hbm.at[pl.ds(off, SC_WINDOW)], i0_v[b], sem_ld.at[b]),
                    pltpu.make_async_copy(i1_hbm.at[pl.ds(off, SC_WINDOW)], i1_v[b], sem_ld.at[b]),
                    pltpu.make_async_copy(x_hbm.at[pl.ds(off, SC_WINDOW)], rows_v[b], sem_ld.at[b]))

        def stores(b):
            return (pltpu.make_async_copy(rows_v[b], o_hbm.at[i0_v[b]], sem_st.at[b]),
                    pltpu.make_async_copy(rows_v[b], o_hbm.at[i1_v[b]], sem_st.at[b]))

        for d in loads(0, 0):
            d.start()

        @pl.loop(0, n_chunks, step=2)
        def _(c):
            for b in range(2):
                chunk = c + b
                for d in loads(chunk, b):
                    d.wait()
                for d in stores(b):
                    d.start()

                @pl.when(chunk >= 1)
                def _():
                    for d in stores(1 - b):
                        d.wait()

                @pl.when(chunk + 1 < n_chunks)
                def _():
                    for d in loads(chunk + 1, 1 - b):
                        d.start()

        for d in stores(1):
            d.wait()

    return dispatch(hn2_rows, pos0, pos1)


def _experts_kernel(te_ref, tn_ref,
                    xs_hbm, wg_ref, wu_ref, wd_ref,
                    y_ref,
                    xbuf, sem, wgb, wub, wdb):
    i = pl.program_id(0)
    n = pl.num_programs(0)

    def fetch(tile):
        sl = tile % XS_SLOTS
        return pltpu.make_async_copy(xs_hbm.at[pl.ds(pl.multiple_of(tile * TME, TME), TME), :],
                                     xbuf.at[sl], sem.at[sl])

    @pl.when(i == 0)
    def _():
        for tile in range(XS_SLOTS - 1):
            fetch(tile).start()

    @pl.when(i + XS_SLOTS - 1 < n)
    def _():
        fetch(i + XS_SLOTS - 1).start()

    valid = tn_ref[i] > 0
    changed = jnp.logical_or(i == 0, te_ref[i] != te_ref[jnp.maximum(i - 1, 0)])

    @pl.when(jnp.logical_and(changed, valid))
    def _():
        wgb[...] = wg_ref[...].astype(_bf16)
        wub[...] = wu_ref[...].astype(_bf16)
        wdb[...] = wd_ref[...].astype(_bf16)

    fetch(i).wait()

    @pl.when(valid)
    def _():
        packed = xbuf[i % XS_SLOTS]
        halves = [pltpu.unpack_elementwise(packed, index=k, packed_dtype=_bf16, unpacked_dtype=_f32)
                  for k in range(2)]
        xb = jnp.concatenate(halves, axis=1).astype(_bf16)
        live = lax.broadcasted_iota(jnp.int32, (TME, 1), 0) < tn_ref[i]
        xb = jnp.where(live, xb, jnp.zeros_like(xb))
        gate = _dot(xb, wgb[...])
        up = _dot(xb, wub[...])
        act = (gate * jax.nn.sigmoid(gate) * up).astype(_bf16)
        y = _dot(act, wdb[...])
        for cc in range(ROW_CHUNKS):
            y_ref[pl.ds(cc, TME, stride=ROW_CHUNKS), :] = y[:, cc * LANES:(cc + 1) * LANES]

    @pl.when(jnp.logical_not(valid))
    def _():
        y_ref[...] = jnp.zeros_like(y_ref)


def _combine_kernel(nwin_ref, wstart_ref, widx0_ref, widx1_ref, gate0_ref, gate1_ref,
                    y_hbm, h2_ref,
                    out_ref,
                    wbuf, sem, otile):
    i = pl.program_id(0)
    n = pl.num_programs(0)
    slot = i % 2
    win_rows = WIN * ROW_CHUNKS

    def window(tile, sl, s):
        src = pl.multiple_of(wstart_ref[tile * MAX_WIN + s] * ROW_CHUNKS, ROW_CHUNKS)
        return pltpu.make_async_copy(y_hbm.at[pl.ds(src, win_rows), :],
                                     wbuf.at[sl, pl.ds(pl.multiple_of(s * win_rows, win_rows), win_rows), :],
                                     sem.at[sl])

    def issue(tile, sl):
        def body(s, carry):
            window(tile, sl, s).start()
            return carry

        lax.fori_loop(0, nwin_ref[tile], body, 0)

    @pl.when(i == 0)
    def _():
        issue(0, 0)

    @pl.when(i + 1 < n)
    def _():
        issue(i + 1, 1 - slot)

    def wait_body(s, carry):
        window(i, slot, s).wait()
        return carry

    lax.fori_loop(0, nwin_ref[i], wait_body, 0)

    def token_body(it, carry):
        for j in range(CMB_UNROLL):
            r = it * CMB_UNROLL + j
            tok = i * TMC + r
            y0 = wbuf[slot, pl.ds(pl.multiple_of(widx0_ref[tok], ROW_CHUNKS), ROW_CHUNKS), :]
            y1 = wbuf[slot, pl.ds(pl.multiple_of(widx1_ref[tok], ROW_CHUNKS), ROW_CHUNKS), :]
            row = pl.ds(pl.multiple_of(r * ROW_CHUNKS, ROW_CHUNKS), ROW_CHUNKS)
            otile[row, :] = h2_ref[row, :] + (gate0_ref[tok] * y0 + gate1_ref[tok] * y1)
        return carry

    lax.fori_loop(0, TMC // CMB_UNROLL, token_body, 0)
    for cc in range(ROW_CHUNKS):
        out_ref[:, cc * LANES:(cc + 1) * LANES] = otile[pl.ds(cc, TMC, stride=ROW_CHUNKS), :]


def _t5_bucket(n):
    max_exact = N_BUCKETS // 2
    nf = jnp.maximum(n, 1).astype(_f32)
    large = max_exact + (jnp.log(nf / max_exact) / np.log(MAX_DISTANCE / max_exact)
                         * (N_BUCKETS - max_exact)).astype(jnp.int32)
    large = jnp.minimum(large, N_BUCKETS - 1)
    return jnp.where(n < max_exact, n, large)


def _bucket_maps():
    qi = jnp.arange(BLK)[:, None]
    sj = jnp.arange(2 * BLK)[None, :]
    dist = BLK + qi - sj
    band = (dist >= 0) & (dist < BLK)
    bucket = _t5_bucket(jnp.maximum(dist, 0))
    generic = jnp.where(band, bucket, -1)
    first = jnp.where(band & (sj >= PAD), bucket, -1)
    return jnp.stack([first, generic]).astype(jnp.int32)


def _const(shape):
    nd = len(shape)
    return pl.BlockSpec(shape, lambda *_: (0,) * nd)


def _block_diag_mean(n):
    idx = np.arange(n) // HEAD_DIM
    return jnp.asarray((idx[:, None] == idx[None, :]).astype(np.float32) / HEAD_DIM, dtype=_bf16)


def kernel(x, meta_tokens, rel_bias, mix_norm_g, w_in, q_norm_g, k_norm_g, attn_sinks, conv_w, attn_out_norm_g, conv_out_norm_g, w_out, ffn_norm_g, w_group_router, b_group_router, w_expert_router, b_expert_router, w_gate, w_up, w_down):
    bsz, seq, _ = x.shape
    assert seq % TM == 0 and (bsz * seq) % TMC == 0
    n_tok = bsz * seq
    nt = seq // TM

    win = w_in[0].astype(_bf16)
    wout = w_out[0].astype(_bf16)
    mixg = mix_norm_g[0].reshape(1, D_MODEL)
    fg = ffn_norm_g[0].reshape(1, D_MODEL)
    qg = (jnp.tile(q_norm_g[0], N_Q_HEADS) * (HEAD_DIM ** -0.5 * LOG2E)).reshape(1, ATTN_DIM)
    kg = jnp.tile(k_norm_g[0], N_KV_HEADS).reshape(1, KV_DIM)
    ag = attn_out_norm_g[0].reshape(1, ATTN_DIM)
    cg = conv_out_norm_g[0].reshape(1, CONV_CH)
    convw = conv_w[0]
    sinks = attn_sinks[0] * LOG2E
    w_r = jnp.concatenate([w_group_router[0], w_expert_router[0].reshape(D_MODEL, N_EXPERTS)], axis=1)
    n_r = N_GROUPS + N_EXPERTS
    w_r_hi = w_r.astype(_bf16)
    w_r_lo = (w_r - w_r_hi.astype(_f32)).astype(_bf16)
    wr = jnp.zeros((D_MODEL, LANES), _bf16)
    wr = wr.at[:, :n_r].set(w_r_hi).at[:, HEAD_DIM:HEAD_DIM + n_r].set(w_r_lo)
    br = jnp.zeros((1, LANES), _f32).at[0, :n_r].set(
        jnp.concatenate([b_group_router[0], b_expert_router[0].reshape(N_EXPERTS)]))
    meta_blk = jnp.concatenate([jnp.zeros((PAD, D_MODEL), x.dtype), meta_tokens.astype(x.dtype)], axis=0)
    bdq = _block_diag_mean(ATTN_DIM)
    bdk = _block_diag_mean(KV_DIM)
    tri = jnp.asarray(np.tril(np.ones((TM, TM), np.float32), -1), dtype=_bf16)

    kv_sd = jax.ShapeDtypeStruct((BLK, KV_DIM), _bf16)
    k0m, k1m, v0m, v1m, utm, bias = pl.pallas_call(
        _prep_kernel,
        out_shape=(kv_sd, kv_sd, kv_sd, kv_sd,
                   jax.ShapeDtypeStruct((SUBLANES, CONV_CH), _f32),
                   jax.ShapeDtypeStruct((2, N_KV_HEADS, Q_GROUP * BLK, 2 * BLK), _f32)),
        in_specs=[pl.BlockSpec(memory_space=pltpu.SMEM)] + [pl.BlockSpec(memory_space=pltpu.VMEM)] * 6,
        out_specs=tuple(pl.BlockSpec(memory_space=pltpu.VMEM) for _ in range(6)),
        compiler_params=pltpu.CompilerParams(vmem_limit_bytes=VMEM_LIMIT),
        name="prep",
    )(rel_bias, meta_blk, mixg, win, kg, bdk, _bucket_maps())

    consts = (mixg, win, qg, kg, bias, convw, ag, cg, wout, fg, wr, br, k0m, k1m, v0m, v1m, utm, bdq, bdk, tri)
    h2, hn2, route_t, cnt, cntb = _mixer_call(sinks, x.reshape(n_tok, D_MODEL), consts, nt)
    pos2, t_exp, t_rows, n_tiles, tables = _plan(route_t, cnt, cntb, n_tok)
    xs = _dispatch_rows(hn2, pos2[0], pos2[1], n_tiles * TME)
    y_sorted = _experts_call(t_exp, t_rows, xs, w_gate[0], w_up[0], w_down[0], n_tiles)
    out = _combine_call(tables, y_sorted, h2)
    return out.reshape(bsz, seq, D_MODEL)


def _mixer_call(sinks, x_rows, consts, nt):
    n_tok = x_rows.shape[0]
    tile_idx = lambda b, t, *_: (b * nt + t, 0)
    grid_spec = pltpu.PrefetchScalarGridSpec(
        num_scalar_prefetch=1,
        grid=(n_tok // (nt * TM), nt),
        in_specs=[
            pl.BlockSpec((TM, D_MODEL), tile_idx),
            _const((1, D_MODEL)), _const((D_MODEL, IN_PROJ)), _const((1, ATTN_DIM)), _const((1, KV_DIM)),
            _const((2, N_KV_HEADS, Q_GROUP * BLK, 2 * BLK)), _const((3, CONV_CH)),
            _const((1, ATTN_DIM)), _const((1, CONV_CH)), _const((D_MODEL, D_MODEL)), _const((1, D_MODEL)),
            _const((D_MODEL, LANES)), _const((1, LANES)),
            _const((BLK, KV_DIM)), _const((BLK, KV_DIM)), _const((BLK, KV_DIM)), _const((BLK, KV_DIM)),
            _const((SUBLANES, CONV_CH)),
            _const((ATTN_DIM, ATTN_DIM)), _const((KV_DIM, KV_DIM)), _const((TM, TM)),
        ],
        out_specs=[
            pl.BlockSpec((TM * ROW_CHUNKS, LANES), tile_idx),
            pl.BlockSpec((TM, PACKED_D), tile_idx),
            pl.BlockSpec((SUBLANES, TM), lambda b, t, *_: (0, b * nt + t)),
            _const((1, LANES)),
            pl.BlockSpec((SUBLANES, LANES), tile_idx),
        ],
        scratch_shapes=[
            pltpu.VMEM((BLK, KV_DIM), _bf16), pltpu.VMEM((BLK, KV_DIM), _bf16),
            pltpu.VMEM((BLK, KV_DIM), _bf16), pltpu.VMEM((BLK, KV_DIM), _bf16),
            pltpu.VMEM((TM + SUBLANES, CONV_CH), _f32),
            pltpu.VMEM((TM, ATTN_DIM), _f32),
            pltpu.VMEM((1, LANES), _f32),
        ],
    )
    return pl.pallas_call(
        _mixer_kernel,
        grid_spec=grid_spec,
        out_shape=(jax.ShapeDtypeStruct((n_tok * ROW_CHUNKS, LANES), _f32),
                   jax.ShapeDtypeStruct((n_tok, PACKED_D), jnp.int32),
                   jax.ShapeDtypeStruct((SUBLANES, n_tok), _f32),
                   jax.ShapeDtypeStruct((1, LANES), _f32),
                   jax.ShapeDtypeStruct((n_tok // TM * SUBLANES, LANES), _f32)),
        compiler_params=pltpu.CompilerParams(dimension_semantics=("arbitrary", "arbitrary"),
                                             vmem_limit_bytes=VMEM_LIMIT),
        name="mixer",
    )(sinks, x_rows, *consts)


def _plan(route_t, cnt, cntb, n_tok):
    n_tiles = (n_tok * 2) // TME + N_EXPERTS + 1
    counts = cnt[0, :N_EXPERTS].astype(jnp.int32)
    ntile = (counts + TME - 1) // TME
    tile_end = jnp.cumsum(ntile)
    tile_start = tile_end - ntile
    eid = route_t[0:2].astype(jnp.int32)
    rank = route_t[2:4].astype(jnp.int32)
    gates = route_t[4:6]
    experts = jnp.arange(N_EXPERTS, dtype=jnp.int32)
    start_of = jnp.sum(jnp.where(eid[None] == experts[:, None, None], tile_start[:, None, None], 0), axis=0)
    pos2 = start_of * TME + rank
    tiles = jnp.arange(n_tiles, dtype=jnp.int32)
    n_used = tile_end[-1]
    t_exp = jnp.sum((jnp.minimum(tiles, n_used - 1)[:, None] >= tile_end[None, :]).astype(jnp.int32), axis=-1)
    t_exp = jnp.minimum(t_exp, N_EXPERTS - 1)
    own = t_exp[:, None] == experts
    t_rows = jnp.sum(jnp.where(own, counts - (tiles[:, None] - tile_start) * TME, 0), axis=-1)
    t_rows = jnp.where(tiles < n_used, jnp.clip(t_rows, 0, TME), 0).astype(jnp.int32)

    n_mix = n_tok // TMC
    before = cntb[::SUBLANES, :N_EXPERTS].astype(jnp.int32)
    held = jnp.concatenate([before[1:], counts[None]], axis=0) - before
    first = tile_start[None, :] * TME + before
    nw = (held + WIN - 1) // WIN
    slot_end = jnp.cumsum(nw, axis=1)
    slot_beg = slot_end - nw
    nwin = slot_end[:, -1]
    slots = jnp.arange(MAX_WIN, dtype=jnp.int32)
    owner = jnp.minimum(jnp.sum((slots[None, :, None] >= slot_end[:, None, :]).astype(jnp.int32), axis=-1),
                        N_EXPERTS - 1)
    is_owner = owner[:, :, None] == experts
    wstart = jnp.sum(jnp.where(is_owner, first[:, None, :] + (slots[None, :, None] - slot_beg[:, None, :]) * WIN,
                               0), axis=-1)
    wstart = jnp.where(slots[None, :] < nwin[:, None], wstart, 0).astype(jnp.int32).reshape(-1)
    mine = eid.reshape(2, n_mix, TMC, 1) == experts
    beg_tok = jnp.sum(jnp.where(mine, slot_beg[None, :, None, :], 0), axis=-1)
    before_tok = jnp.sum(jnp.where(mine, before[None, :, None, :], 0), axis=-1)
    widx = (beg_tok * WIN + rank.reshape(2, n_mix, TMC) - before_tok).reshape(2, n_tok).astype(jnp.int32)
    widx = widx * ROW_CHUNKS
    return pos2, t_exp, t_rows, n_tiles, (nwin.astype(jnp.int32), wstart, widx[0], widx[1], gates[0], gates[1])


def _experts_call(t_exp, t_rows, xs, w_gate, w_up, w_down, n_tiles):
    return pl.pallas_call(
        _experts_kernel,
        grid_spec=pltpu.PrefetchScalarGridSpec(
            num_scalar_prefetch=2,
            grid=(n_tiles,),
            in_specs=[
                pl.BlockSpec(memory_space=pl.ANY),
                pl.BlockSpec((None, D_MODEL, D_EXPERT), lambda i, te, tn: (te[i], 0, 0)),
                pl.BlockSpec((None, D_MODEL, D_EXPERT), lambda i, te, tn: (te[i], 0, 0)),
                pl.BlockSpec((None, D_EXPERT, D_MODEL), lambda i, te, tn: (te[i], 0, 0)),
            ],
            out_specs=pl.BlockSpec((TME * ROW_CHUNKS, LANES), lambda i, *_: (i, 0)),
            scratch_shapes=[
                pltpu.VMEM((XS_SLOTS, TME, PACKED_D), jnp.int32),
                pltpu.SemaphoreType.DMA((XS_SLOTS,)),
                pltpu.VMEM((D_MODEL, D_EXPERT), _bf16),
                pltpu.VMEM((D_MODEL, D_EXPERT), _bf16),
                pltpu.VMEM((D_EXPERT, D_MODEL), _bf16),
            ],
        ),
        out_shape=jax.ShapeDtypeStruct((n_tiles * TME * ROW_CHUNKS, LANES), _f32),
        compiler_params=pltpu.CompilerParams(dimension_semantics=("arbitrary",),
                                             vmem_limit_bytes=VMEM_LIMIT),
        name="experts",
    )(t_exp, t_rows, xs, w_gate, w_up, w_down)


def _combine_call(tables, y_sorted, h2):
    n_tok = h2.shape[0] // ROW_CHUNKS
    return pl.pallas_call(
        _combine_kernel,
        grid_spec=pltpu.PrefetchScalarGridSpec(
            num_scalar_prefetch=len(tables),
            grid=(n_tok // TMC,),
            in_specs=[
                pl.BlockSpec(memory_space=pl.ANY),
                pl.BlockSpec((TMC * ROW_CHUNKS, LANES), lambda i, *_: (i, 0)),
            ],
            out_specs=pl.BlockSpec((TMC, D_MODEL), lambda i, *_: (i, 0)),
            scratch_shapes=[
                pltpu.VMEM((2, MAX_WIN * WIN * ROW_CHUNKS, LANES), _f32),
                pltpu.SemaphoreType.DMA((2,)),
                pltpu.VMEM((TMC * ROW_CHUNKS, LANES), _f32),
            ],
        ),
        out_shape=jax.ShapeDtypeStruct((n_tok, D_MODEL), _f32),
        compiler_params=pltpu.CompilerParams(dimension_semantics=("arbitrary",),
                                             vmem_limit_bytes=VMEM_LIMIT),
        name="combine",
    )(*tables, y_sorted, h2)
```

```python
import functools

import numpy as np
import jax
import jax.numpy as jnp
from jax import lax
from jax.experimental import pallas as pl
from jax.experimental.pallas import tpu as pltpu
from jax.experimental.pallas import tpu_sc as plsc

D_MODEL = 1024
N_META = 16
N_Q_HEADS = 8
N_KV_HEADS = 2
HEAD_DIM = 64
Q_GROUP = N_Q_HEADS // N_KV_HEADS
ATTN_DIM = N_Q_HEADS * HEAD_DIM
KV_DIM = N_KV_HEADS * HEAD_DIM
BLK = 128
PAD = BLK - N_META
N_BUCKETS = 32
MAX_DISTANCE = 128
CONV_CH = D_MODEL // 2
IN_PROJ = ATTN_DIM + 2 * KV_DIM + 3 * CONV_CH
N_GROUPS = 4
EXPERTS_PER_GROUP = 8
N_EXPERTS = N_GROUPS * EXPERTS_PER_GROUP
D_EXPERT = D_MODEL // 2
EPS = 1e-6
NEG_INF = -1e30
LOG2E = float(np.log2(np.e))

LANES = 128
SUBLANES = 8
ROW_CHUNKS = D_MODEL // LANES
TM = 512
TME = 512
TMC = TM
WIN = 16
MAX_WIN = N_EXPERTS + 2 * TMC // WIN
CMB_UNROLL = 32
XS_SLOTS = 3
PACKED_D = D_MODEL // 2
SC_WINDOW = 64
VMEM_LIMIT = 56 * 1024 * 1024

Q_OFF, K_OFF, V_OFF = 0, ATTN_DIM, ATTN_DIM + KV_DIM
CB_OFF = ATTN_DIM + 2 * KV_DIM
CC_OFF = CB_OFF + CONV_CH
CH_OFF = CC_OFF + CONV_CH

_f32 = jnp.float32
_bf16 = jnp.bfloat16


def _rms(x, g):
    return x * lax.rsqrt(jnp.mean(x * x, axis=-1, keepdims=True) + EPS) * g


def _dot(a, b):
    return jnp.dot(a, b, preferred_element_type=_f32)


def _dup_halves(x):
    lane = lax.broadcasted_iota(jnp.int32, x.shape, 1)
    sw = pltpu.roll(x, HEAD_DIM, axis=1)
    lo = lane < HEAD_DIM
    return jnp.where(lo, x, sw).astype(_bf16), jnp.where(lo, sw, x).astype(_bf16)


def _kv_state(hn_bf, win_ref, kg_ref, bdk_ref):
    kv = _dot(hn_bf, win_ref[:, K_OFF:K_OFF + 2 * KV_DIM])
    k = kv[:, :KV_DIM]
    v = kv[:, KV_DIM:]
    ssk = _dot((k * k).astype(_bf16), bdk_ref[...])
    kn = k * lax.rsqrt(ssk + EPS) * kg_ref[...]
    return _dup_halves(kn) + _dup_halves(v)


def _prep_kernel(rb_ref, meta_ref, mixg_ref, win_ref, kg_ref, bdk_ref, bucket_ref,
                 k0_ref, k1_ref, v0_ref, v1_ref, ut_ref, bias_ref):
    hn = _rms(meta_ref[...], mixg_ref[...]).astype(_bf16)
    k0, k1, v0, v1 = _kv_state(hn, win_ref, kg_ref, bdk_ref)
    k0_ref[...] = k0
    k1_ref[...] = k1
    v0_ref[...] = v0
    v1_ref[...] = v1
    cch = _dot(hn, win_ref[:, CC_OFF:CC_OFF + 2 * CONV_CH])
    u = cch[:, :CONV_CH] * cch[:, CONV_CH:]
    ut_ref[...] = u[BLK - SUBLANES:, :]
    for f in range(2):
        bk = bucket_ref[f]
        for h in range(N_Q_HEADS):
            acc = jnp.full((BLK, 2 * BLK), NEG_INF, _f32)
            for b in range(N_BUCKETS):
                acc = jnp.where(bk == b, rb_ref[b, h] * LOG2E, acc)
            bias_ref[f, h // Q_GROUP, (h % Q_GROUP) * BLK:(h % Q_GROUP + 1) * BLK, :] = acc


def _mixer_kernel(sinks_ref,
                  x_ref, mixg_ref, win_ref, qg_ref, kg_ref, bias_ref, convw_ref, ag_ref, cg_ref,
                  wout_ref, fg_ref, wr_ref, br_ref, k0m_ref, k1m_ref, v0m_ref, v1m_ref, utm_ref,
                  bdq_ref, bdk_ref, tri_ref,
                  h2_ref, hn2_ref, route_t_ref, cnt_ref, cntb_ref,
                  kp0, kp1, vp0, vp1, ubuf, a_scr, cnt_acc):
    b = pl.program_id(0)
    t = pl.program_id(1)

    @pl.when(t == 0)
    def _():
        kp0[...] = k0m_ref[...]
        kp1[...] = k1m_ref[...]
        vp0[...] = v0m_ref[...]
        vp1[...] = v1m_ref[...]
        ubuf[0:SUBLANES, :] = utm_ref[...]

    @pl.when(jnp.logical_and(b == 0, t == 0))
    def _():
        cnt_acc[...] = jnp.zeros_like(cnt_acc)

    x = x_ref[...]
    hn = _rms(x, mixg_ref[...]).astype(_bf16)

    q = _dot(hn, win_ref[:, Q_OFF:Q_OFF + ATTN_DIM])
    ssq = _dot((q * q).astype(_bf16), bdq_ref[...])
    qn = (q * lax.rsqrt(ssq + EPS) * qg_ref[...]).astype(_bf16)
    kd0, kd1, vd0, vd1 = _kv_state(hn, win_ref, kg_ref, bdk_ref)
    kd = (kd0, kd1)
    vd = (vd0, vd1)
    kp = (kp0, kp1)
    vp = (vp0, vp1)

    lane_q = lax.broadcasted_iota(jnp.int32, (BLK, LANES), 1)
    lo_half = lane_q < HEAD_DIM
    row4 = lax.broadcasted_iota(jnp.int32, (Q_GROUP * BLK, 1), 0) // BLK
    first = jnp.where(t == 0, 0, 1)
    zero_bf = jnp.zeros((BLK, LANES), _bf16)

    for j in range(TM // BLK):
        rows = slice(j * BLK, (j + 1) * BLK)
        for g in range(N_KV_HEADS):
            if j == 0:
                kcat = jnp.concatenate([kp[g][...], kd[g][rows]], axis=0)
                vcat = jnp.concatenate([vp[g][...], vd[g][rows]], axis=0)
                bias = bias_ref[first, g]
            else:
                kcat = kd[g][(j - 1) * BLK:(j + 1) * BLK]
                vcat = vd[g][(j - 1) * BLK:(j + 1) * BLK]
                bias = bias_ref[1, g]
            qs = []
            for hh in range(Q_GROUP):
                h = g * Q_GROUP + hh
                qc = qn[rows, (h // 2) * LANES:(h // 2 + 1) * LANES]
                keep = lo_half if h % 2 == 0 else jnp.logical_not(lo_half)
                qs.append(jnp.where(keep, qc, zero_bf))
            q4 = jnp.concatenate(qs, axis=0)
            s = lax.dot_general(q4, kcat, (((1,), (1,)), ((), ())),
                                preferred_element_type=_f32) + bias
            sink = jnp.full((Q_GROUP * BLK, 1), sinks_ref[g * Q_GROUP], _f32)
            for hh in range(1, Q_GROUP):
                sink = jnp.where(row4 == hh, sinks_ref[g * Q_GROUP + hh], sink)
            m = jnp.maximum(jnp.max(s, axis=-1, keepdims=True), sink)
            p = jnp.exp2(s - m)
            l = jnp.sum(p, axis=-1, keepdims=True) + jnp.exp2(sink - m)
            o = _dot(p.astype(_bf16), vcat) / l
            for pp in range(Q_GROUP // 2):
                ev = o[(2 * pp) * BLK:(2 * pp + 1) * BLK]
                od = o[(2 * pp + 1) * BLK:(2 * pp + 2) * BLK]
                col = g * (Q_GROUP // 2) + pp
                a_scr[rows, col * LANES:(col + 1) * LANES] = jnp.where(lo_half, ev, od)

    last = slice(TM - BLK, TM)
    kp0[...] = kd0[last]
    kp1[...] = kd1[last]
    vp0[...] = vd0[last]
    vp1[...] = vd1[last]

    cb = _dot(hn, win_ref[:, CB_OFF:CB_OFF + CONV_CH])
    cch = _dot(hn, win_ref[:, CC_OFF:CC_OFF + 2 * CONV_CH])
    u = cch[:, :CONV_CH] * cch[:, CONV_CH:]
    ubuf[SUBLANES:, :] = u
    u1 = ubuf[SUBLANES - 1:SUBLANES - 1 + TM, :]
    u2 = ubuf[SUBLANES - 2:SUBLANES - 2 + TM, :]
    c = cb * (convw_ref[0:1, :] * u2 + convw_ref[1:2, :] * u1 + convw_ref[2:3, :] * u)
    ubuf[0:SUBLANES, :] = u[TM - SUBLANES:, :]

    an = _rms(a_scr[...], ag_ref[...]).astype(_bf16)
    cn = _rms(c, cg_ref[...]).astype(_bf16)
    h2 = x + _dot(an, wout_ref[0:ATTN_DIM, :]) + _dot(cn, wout_ref[ATTN_DIM:, :])
    for cc in range(ROW_CHUNKS):
        h2_ref[pl.ds(cc, TM, stride=ROW_CHUNKS), :] = h2[:, cc * LANES:(cc + 1) * LANES]

    hn2 = _rms(h2, fg_ref[...])
    hn2_ref[...] = pltpu.pack_elementwise([hn2[:, :PACKED_D], hn2[:, PACKED_D:]], packed_dtype=_bf16)

    r1 = _dot(hn2.astype(_bf16), wr_ref[...])
    lg = r1 + pltpu.roll(r1, HEAD_DIM, axis=1) + br_ref[...]
    lane = lax.broadcasted_iota(jnp.int32, (TM, LANES), 1)
    lanef = lane.astype(_f32)
    ninf = jnp.float32(-jnp.inf)
    big = jnp.float32(LANES)

    gl = jnp.where(lane < N_GROUPS, lg, ninf)
    gmax = jnp.max(gl, axis=-1, keepdims=True)
    gsum = jnp.sum(jnp.exp(gl - gmax), axis=-1, keepdims=True)
    g_p = 1.0 / gsum
    g_idx = jnp.min(jnp.where(gl == gmax, lanef, big), axis=-1, keepdims=True)
    e_lo = N_GROUPS + EXPERTS_PER_GROUP * g_idx
    el = jnp.where(jnp.logical_and(lanef >= e_lo, lanef < e_lo + EXPERTS_PER_GROUP), lg, ninf)
    m1 = jnp.max(el, axis=-1, keepdims=True)
    i1 = jnp.min(jnp.where(el == m1, lanef, big), axis=-1, keepdims=True)
    el2 = jnp.where(lanef == i1, ninf, el)
    m2 = jnp.max(el2, axis=-1, keepdims=True)
    i2 = jnp.min(jnp.where(el2 == m2, lanef, big), axis=-1, keepdims=True)
    ex = jnp.exp(m2 - m1)
    den = 1.0 / (1.0 + ex)
    gate0 = g_p * den
    gate1 = g_p * ex * den
    e0 = i1 - N_GROUPS
    e1 = i2 - N_GROUPS

    oh0 = lanef == e0
    oh1 = lanef == e1
    cmat = (jnp.where(oh0, 1.0, 0.0) + jnp.where(oh1, 1.0, 0.0))
    prefix = _dot(tri_ref[...], cmat.astype(_bf16)) + cnt_acc[...]
    rank0 = jnp.sum(jnp.where(oh0, prefix, 0.0), axis=-1, keepdims=True)
    rank1 = jnp.sum(jnp.where(oh1, prefix, 0.0), axis=-1, keepdims=True)
    cntb_ref[...] = jnp.broadcast_to(cnt_acc[...], (SUBLANES, LANES))
    cnt_new = cnt_acc[...] + jnp.sum(cmat, axis=0, keepdims=True)
    cnt_acc[...] = cnt_new
    cnt_ref[...] = cnt_new

    lane_r = lax.broadcasted_iota(jnp.int32, (TM, LANES), 1)
    wide = jnp.zeros((TM, LANES), _f32)
    for idx, val in enumerate((e0, e1, rank0, rank1, gate0, gate1)):
        wide = jnp.where(lane_r == idx, val, wide)
    route_t_ref[...] = wide.T[:SUBLANES, :]


def _row_tile(ref, row):
    return ref.at[pl.ds(pl.multiple_of(row * ROW_CHUNKS, ROW_CHUNKS), ROW_CHUNKS), :]


def _dispatch_rows(hn2_rows, pos0, pos1, n_rows):
    n_tok = hn2_rows.shape[0]
    info = plsc.get_sparse_core_info()
    n_workers = info.num_cores * info.num_subcores
    per_worker = n_tok // n_workers
    n_chunks = per_worker // SC_WINDOW
    assert per_worker * n_workers == n_tok and n_chunks * SC_WINDOW == per_worker and n_chunks % 2 == 0
    mesh = plsc.VectorSubcoreMesh(core_axis_name="core", subcore_axis_name="subcore")

    @functools.partial(
        pl.kernel,
        out_type=jax.ShapeDtypeStruct((n_rows, PACKED_D), jnp.int32),
        mesh=mesh,
        scratch_types=[pltpu.VMEM((SC_WINDOW,), jnp.int32), pltpu.VMEM((SC_WINDOW,), jnp.int32),
                       pltpu.VMEM((SC_WINDOW,), jnp.int32), pltpu.VMEM((SC_WINDOW,), jnp.int32),
                       pltpu.VMEM((SC_WINDOW, PACKED_D), jnp.int32), pltpu.VMEM((SC_WINDOW, PACKED_D), jnp.int32),
                       pltpu.SemaphoreType.DMA((2,)), pltpu.SemaphoreType.DMA((2,))],
        compiler_params=pltpu.CompilerParams(use_tc_tiling_on_sc=True),
        name="dispatch",
    )
    def dispatch(x_hbm, i0_hbm, i1_hbm, o_hbm, i0_a, i0_b, i1_a, i1_b, rows_a, rows_b, sem_ld, sem_st):
        wid = lax.axis_index("subcore") * info.num_cores + lax.axis_index("core")
        base = wid * per_worker
        i0_v, i1_v, rows_v = (i0_a, i0_b), (i1_a, i1_b), (rows_a, rows_b)

        def loads(chunk, b):
            off = pl.multiple_of(base + chunk * SC_WINDOW, SC_WINDOW)
            return (pltpu.make_async_copy(i0_hbm.at[pl.ds(off, SC_WINDOW)], i0_v[b], sem_ld.at[b]),
                    pltpu.make_async_copy(i1_hbm.at[pl.ds(off, SC_WINDOW)], i1_v[b], sem_ld.at[b]),
                    pltpu.make_async_copy(x_hbm.at[pl.ds(off, SC_WINDOW)], rows_v[b], sem_ld.at[b]))

        def stores(b):
            return (pltpu.make_async_copy(rows_v[b], o_hbm.at[i0_v[b]], sem_st.at[b]),
                    pltpu.make_async_copy(rows_v[b], o_hbm.at[i1_v[b]], sem_st.at[b]))

        for d in loads(0, 0):
            d.start()

        @pl.loop(0, n_chunks, step=2)
        def _(c):
            for b in range(2):
                chunk = c + b
                for d in loads(chunk, b):
                    d.wait()
                for d in stores(b):
                    d.start()

                @pl.when(chunk >= 1)
                def _():
                    for d in stores(1 - b):
                        d.wait()

                @pl.when(chunk + 1 < n_chunks)
                def _():
                    for d in loads(chunk + 1, 1 - b):
                        d.start()

        for d in stores(1):
            d.wait()

    return dispatch(hn2_rows, pos0, pos1)


def _experts_kernel(te_ref, tn_ref,
                    xs_hbm, wg_ref, wu_ref, wd_ref,
                    y_ref,
                    xbuf, sem, wgb, wub, wdb):
    i = pl.program_id(0)
    n = pl.num_programs(0)

    def fetch(tile):
        sl = tile % XS_SLOTS
        return pltpu.make_async_copy(xs_hbm.at[pl.ds(pl.multiple_of(tile * TME, TME), TME), :],
                                     xbuf.at[sl], sem.at[sl])

    @pl.when(i == 0)
    def _():
        for tile in range(XS_SLOTS - 1):
            fetch(tile).start()

    @pl.when(i + XS_SLOTS - 1 < n)
    def _():
        fetch(i + XS_SLOTS - 1).start()

    valid = tn_ref[i] > 0
    changed = jnp.logical_or(i == 0, te_ref[i] != te_ref[jnp.maximum(i - 1, 0)])

    @pl.when(jnp.logical_and(changed, valid))
    def _():
        wgb[...] = wg_ref[...].astype(_bf16)
        wub[...] = wu_ref[...].astype(_bf16)
        wdb[...] = wd_ref[...].astype(_bf16)

    fetch(i).wait()

    @pl.when(valid)
    def _():
        packed = xbuf[i % XS_SLOTS]
        halves = [pltpu.unpack_elementwise(packed, index=k, packed_dtype=_bf16, unpacked_dtype=_f32)
                  for k in range(2)]
        xb = jnp.concatenate(halves, axis=1).astype(_bf16)
        live = lax.broadcasted_iota(jnp.int32, (TME, 1), 0) < tn_ref[i]
        xb = jnp.where(live, xb, jnp.zeros_like(xb))
        gate = _dot(xb, wgb[...])
        up = _dot(xb, wub[...])
        act = (gate * jax.nn.sigmoid(gate) * up).astype(_bf16)
        y = _dot(act, wdb[...])
        for cc in range(ROW_CHUNKS):
            y_ref[pl.ds(cc, TME, stride=ROW_CHUNKS), :] = y[:, cc * LANES:(cc + 1) * LANES]

    @pl.when(jnp.logical_not(valid))
    def _():
        y_ref[...] = jnp.zeros_like(y_ref)


def _combine_kernel(nwin_ref, wstart_ref, widx0_ref, widx1_ref, gate0_ref, gate1_ref,
                    y_hbm, h2_ref,
                    out_ref,
                    wbuf, sem, otile):
    i = pl.program_id(0)
    n = pl.num_programs(0)
    slot = i % 2
    win_rows = WIN * ROW_CHUNKS

    def window(tile, sl, s):
        src = pl.multiple_of(wstart_ref[tile * MAX_WIN + s] * ROW_CHUNKS, ROW_CHUNKS)
        return pltpu.make_async_copy(y_hbm.at[pl.ds(src, win_rows), :],
                                     wbuf.at[sl, pl.ds(pl.multiple_of(s * win_rows, win_rows), win_rows), :],
                                     sem.at[sl])

    def issue(tile, sl):
        def body(s, carry):
            window(tile, sl, s).start()
            return carry

        lax.fori_loop(0, nwin_ref[tile], body, 0)

    @pl.when(i == 0)
    def _():
        issue(0, 0)

    @pl.when(i + 1 < n)
    def _():
        issue(i + 1, 1 - slot)

    def wait_body(s, carry):
        window(i, slot, s).wait()
        return carry

    lax.fori_loop(0, nwin_ref[i], wait_body, 0)

    def token_body(it, carry):
        for j in range(CMB_UNROLL):
            r = it * CMB_UNROLL + j
            tok = i * TMC + r
            y0 = wbuf[slot, pl.ds(pl.multiple_of(widx0_ref[tok], ROW_CHUNKS), ROW_CHUNKS), :]
            y1 = wbuf[slot, pl.ds(pl.multiple_of(widx1_ref[tok], ROW_CHUNKS), ROW_CHUNKS), :]
            row = pl.ds(pl.multiple_of(r * ROW_CHUNKS, ROW_CHUNKS), ROW_CHUNKS)
            otile[row, :] = h2_ref[row, :] + (gate0_ref[tok] * y0 + gate1_ref[tok] * y1)
        return carry

    lax.fori_loop(0, TMC // CMB_UNROLL, token_body, 0)
    for cc in range(ROW_CHUNKS):
        out_ref[:, cc * LANES:(cc + 1) * LANES] = otile[pl.ds(cc, TMC, stride=ROW_CHUNKS), :]


def _t5_bucket(n):
    max_exact = N_BUCKETS // 2
    nf = jnp.maximum(n, 1).astype(_f32)
    large = max_exact + (jnp.log(nf / max_exact) / np.log(MAX_DISTANCE / max_exact)
                         * (N_BUCKETS - max_exact)).astype(jnp.int32)
    large = jnp.minimum(large, N_BUCKETS - 1)
    return jnp.where(n < max_exact, n, large)


def _bucket_maps():
    qi = jnp.arange(BLK)[:, None]
    sj = jnp.arange(2 * BLK)[None, :]
    dist = BLK + qi - sj
    band = (dist >= 0) & (dist < BLK)
    bucket = _t5_bucket(jnp.maximum(dist, 0))
    generic = jnp.where(band, bucket, -1)
    first = jnp.where(band & (sj >= PAD), bucket, -1)
    return jnp.stack([first, generic]).astype(jnp.int32)


def _const(shape):
    nd = len(shape)
    return pl.BlockSpec(shape, lambda *_: (0,) * nd)


def _block_diag_mean(n):
    idx = np.arange(n) // HEAD_DIM
    return jnp.asarray((idx[:, None] == idx[None, :]).astype(np.float32) / HEAD_DIM, dtype=_bf16)


def kernel(x, meta_tokens, rel_bias, mix_norm_g, w_in, q_norm_g, k_norm_g, attn_sinks, conv_w, attn_out_norm_g, conv_out_norm_g, w_out, ffn_norm_g, w_group_router, b_group_router, w_expert_router, b_expert_router, w_gate, w_up, w_down):
    bsz, seq, _ = x.shape
    assert seq % TM == 0 and (bsz * seq) % TMC == 0
    n_tok = bsz * seq
    nt = seq // TM

    win = w_in[0].astype(_bf16)
    wout = w_out[0].astype(_bf16)
    mixg = mix_norm_g[0].reshape(1, D_MODEL)
    fg = ffn_norm_g[0].reshape(1, D_MODEL)
    qg = (jnp.tile(q_norm_g[0], N_Q_HEADS) * (HEAD_DIM ** -0.5 * LOG2E)).reshape(1, ATTN_DIM)
    kg = jnp.tile(k_norm_g[0], N_KV_HEADS).reshape(1, KV_DIM)
    ag = attn_out_norm_g[0].reshape(1, ATTN_DIM)
    cg = conv_out_norm_g[0].reshape(1, CONV_CH)
    convw = conv_w[0]
    sinks = attn_sinks[0] * LOG2E
    w_r = jnp.concatenate([w_group_router[0], w_expert_router[0].reshape(D_MODEL, N_EXPERTS)], axis=1)
    n_r = N_GROUPS + N_EXPERTS
    w_r_hi = w_r.astype(_bf16)
    w_r_lo = (w_r - w_r_hi.astype(_f32)).astype(_bf16)
    wr = jnp.zeros((D_MODEL, LANES), _bf16)
    wr = wr.at[:, :n_r].set(w_r_hi).at[:, HEAD_DIM:HEAD_DIM + n_r].set(w_r_lo)
    br = jnp.zeros((1, LANES), _f32).at[0, :n_r].set(
        jnp.concatenate([b_group_router[0], b_expert_router[0].reshape(N_EXPERTS)]))
    meta_blk = jnp.concatenate([jnp.zeros((PAD, D_MODEL), x.dtype), meta_tokens.astype(x.dtype)], axis=0)
    bdq = _block_diag_mean(ATTN_DIM)
    bdk = _block_diag_mean(KV_DIM)
    tri = jnp.asarray(np.tril(np.ones((TM, TM), np.float32), -1), dtype=_bf16)

    kv_sd = jax.ShapeDtypeStruct((BLK, KV_DIM), _bf16)
    k0m, k1m, v0m, v1m, utm, bias = pl.pallas_call(
        _prep_kernel,
        out_shape=(kv_sd, kv_sd, kv_sd, kv_sd,
                   jax.ShapeDtypeStruct((SUBLANES, CONV_CH), _f32),
                   jax.ShapeDtypeStruct((2, N_KV_HEADS, Q_GROUP * BLK, 2 * BLK), _f32)),
        in_specs=[pl.BlockSpec(memory_space=pltpu.SMEM)] + [pl.BlockSpec(memory_space=pltpu.VMEM)] * 6,
        out_specs=tuple(pl.BlockSpec(memory_space=pltpu.VMEM) for _ in range(6)),
        compiler_params=pltpu.CompilerParams(vmem_limit_bytes=VMEM_LIMIT),
        name="prep",
    )(rel_bias, meta_blk, mixg, win, kg, bdk, _bucket_maps())

    consts = (mixg, win, qg, kg, bias, convw, ag, cg, wout, fg, wr, br, k0m, k1m, v0m, v1m, utm, bdq, bdk, tri)
    h2, hn2, route_t, cnt, cntb = _mixer_call(sinks, x.reshape(n_tok, D_MODEL), consts, nt)
    pos2, t_exp, t_rows, n_tiles, tables = _plan(route_t, cnt, cntb, n_tok)
    xs = _dispatch_rows(hn2, pos2[0], pos2[1], n_tiles * TME)
    y_sorted = _experts_call(t_exp, t_rows, xs, w_gate[0], w_up[0], w_down[0], n_tiles)
    out = _combine_call(tables, y_sorted, h2)
    return out.reshape(bsz, seq, D_MODEL)


def _mixer_call(sinks, x_rows, consts, nt):
    n_tok = x_rows.shape[0]
    tile_idx = lambda b, t, *_: (b * nt + t, 0)
    grid_spec = pltpu.PrefetchScalarGridSpec(
        num_scalar_prefetch=1,
        grid=(n_tok // (nt * TM), nt),
        in_specs=[
            pl.BlockSpec((TM, D_MODEL), tile_idx),
            _const((1, D_MODEL)), _const((D_MODEL, IN_PROJ)), _const((1, ATTN_DIM)), _const((1, KV_DIM)),
            _const((2, N_KV_HEADS, Q_GROUP * BLK, 2 * BLK)), _const((3, CONV_CH)),
            _const((1, ATTN_DIM)), _const((1, CONV_CH)), _const((D_MODEL, D_MODEL)), _const((1, D_MODEL)),
            _const((D_MODEL, LANES)), _const((1, LANES)),
            _const((BLK, KV_DIM)), _const((BLK, KV_DIM)), _const((BLK, KV_DIM)), _const((BLK, KV_DIM)),
            _const((SUBLANES, CONV_CH)),
            _const((ATTN_DIM, ATTN_DIM)), _const((KV_DIM, KV_DIM)), _const((TM, TM)),
        ],
        out_specs=[
            pl.BlockSpec((TM * ROW_CHUNKS, LANES), tile_idx),
            pl.BlockSpec((TM, PACKED_D), tile_idx),
            pl.BlockSpec((SUBLANES, TM), lambda b, t, *_: (0, b * nt + t)),
            _const((1, LANES)),
            pl.BlockSpec((SUBLANES, LANES), tile_idx),
        ],
        scratch_shapes=[
            pltpu.VMEM((BLK, KV_DIM), _bf16), pltpu.VMEM((BLK, KV_DIM), _bf16),
            pltpu.VMEM((BLK, KV_DIM), _bf16), pltpu.VMEM((BLK, KV_DIM), _bf16),
            pltpu.VMEM((TM + SUBLANES, CONV_CH), _f32),
            pltpu.VMEM((TM, ATTN_DIM), _f32),
            pltpu.VMEM((1, LANES), _f32),
        ],
    )
    return pl.pallas_call(
        _mixer_kernel,
        grid_spec=grid_spec,
        out_shape=(jax.ShapeDtypeStruct((n_tok * ROW_CHUNKS, LANES), _f32),
                   jax.ShapeDtypeStruct((n_tok, PACKED_D), jnp.int32),
                   jax.ShapeDtypeStruct((SUBLANES, n_tok), _f32),
                   jax.ShapeDtypeStruct((1, LANES), _f32),
                   jax.ShapeDtypeStruct((n_tok // TM * SUBLANES, LANES), _f32)),
        compiler_params=pltpu.CompilerParams(dimension_semantics=("arbitrary", "arbitrary"),
                                             vmem_limit_bytes=VMEM_LIMIT),
        name="mixer",
    )(sinks, x_rows, *consts)


def _plan(route_t, cnt, cntb, n_tok):
    n_tiles = (n_tok * 2) // TME + N_EXPERTS + 1
    counts = cnt[0, :N_EXPERTS].astype(jnp.int32)
    ntile = (counts + TME - 1) // TME
    tile_end = jnp.cumsum(ntile)
    tile_start = tile_end - ntile
    eid = route_t[0:2].astype(jnp.int32)
    rank = route_t[2:4].astype(jnp.int32)
    gates = route_t[4:6]
    experts = jnp.arange(N_EXPERTS, dtype=jnp.int32)
    start_of = jnp.sum(jnp.where(eid[None] == experts[:, None, None], tile_start[:, None, None], 0), axis=0)
    pos2 = start_of * TME + rank
    tiles = jnp.arange(n_tiles, dtype=jnp.int32)
    n_used = tile_end[-1]
    t_exp = jnp.sum((jnp.minimum(tiles, n_used - 1)[:, None] >= tile_end[None, :]).astype(jnp.int32), axis=-1)
    t_exp = jnp.minimum(t_exp, N_EXPERTS - 1)
    own = t_exp[:, None] == experts
    t_rows = jnp.sum(jnp.where(own, counts - (tiles[:, None] - tile_start) * TME, 0), axis=-1)
    t_rows = jnp.where(tiles < n_used, jnp.clip(t_rows, 0, TME), 0).astype(jnp.int32)

    n_mix = n_tok // TMC
    before = cntb[::SUBLANES, :N_EXPERTS].astype(jnp.int32)
    held = jnp.concatenate([before[1:], counts[None]], axis=0) - before
    first = tile_start[None, :] * TME + before
    nw = (held + WIN - 1) // WIN
    slot_end = jnp.cumsum(nw, axis=1)
    slot_beg = slot_end - nw
    nwin = slot_end[:, -1]
    slots = jnp.arange(MAX_WIN, dtype=jnp.int32)
    owner = jnp.minimum(jnp.sum((slots[None, :, None] >= slot_end[:, None, :]).astype(jnp.int32), axis=-1),
                        N_EXPERTS - 1)
    is_owner = owner[:, :, None] == experts
    wstart = jnp.sum(jnp.where(is_owner, first[:, None, :] + (slots[None, :, None] - slot_beg[:, None, :]) * WIN,
                               0), axis=-1)
    wstart = jnp.where(slots[None, :] < nwin[:, None], wstart, 0).astype(jnp.int32).reshape(-1)
    mine = eid.reshape(2, n_mix, TMC, 1) == experts
    beg_tok = jnp.sum(jnp.where(mine, slot_beg[None, :, None, :], 0), axis=-1)
    before_tok = jnp.sum(jnp.where(mine, before[None, :, None, :], 0), axis=-1)
    widx = (beg_tok * WIN + rank.reshape(2, n_mix, TMC) - before_tok).reshape(2, n_tok).astype(jnp.int32)
    widx = widx * ROW_CHUNKS
    return pos2, t_exp, t_rows, n_tiles, (nwin.astype(jnp.int32), wstart, widx[0], widx[1], gates[0], gates[1])


def _experts_call(t_exp, t_rows, xs, w_gate, w_up, w_down, n_tiles):
    return pl.pallas_call(
        _experts_kernel,
        grid_spec=pltpu.PrefetchScalarGridSpec(
            num_scalar_prefetch=2,
            grid=(n_tiles,),
            in_specs=[
                pl.BlockSpec(memory_space=pl.ANY),
                pl.BlockSpec((None, D_MODEL, D_EXPERT), lambda i, te, tn: (te[i], 0, 0)),
                pl.BlockSpec((None, D_MODEL, D_EXPERT), lambda i, te, tn: (te[i], 0, 0)),
                pl.BlockSpec((None, D_EXPERT, D_MODEL), lambda i, te, tn: (te[i], 0, 0)),
            ],
            out_specs=pl.BlockSpec((TME * ROW_CHUNKS, LANES), lambda i, *_: (i, 0)),
            scratch_shapes=[
                pltpu.VMEM((XS_SLOTS, TME, PACKED_D), jnp.int32),
                pltpu.SemaphoreType.DMA((XS_SLOTS,)),
                pltpu.VMEM((D_MODEL, D_EXPERT), _bf16),
                pltpu.VMEM((D_MODEL, D_EXPERT), _bf16),
                pltpu.VMEM((D_EXPERT, D_MODEL), _bf16),
            ],
        ),
        out_shape=jax.ShapeDtypeStruct((n_tiles * TME * ROW_CHUNKS, LANES), _f32),
        compiler_params=pltpu.CompilerParams(dimension_semantics=("arbitrary",),
                                             vmem_limit_bytes=VMEM_LIMIT),
        name="experts",
    )(t_exp, t_rows, xs, w_gate, w_up, w_down)


def _combine_call(tables, y_sorted, h2):
    n_tok = h2.shape[0] // ROW_CHUNKS
    return pl.pallas_call(
        _combine_kernel,
        grid_spec=pltpu.PrefetchScalarGridSpec(
            num_scalar_prefetch=len(tables),
            grid=(n_tok // TMC,),
            in_specs=[
                pl.BlockSpec(memory_space=pl.ANY),
                pl.BlockSpec((TMC * ROW_CHUNKS, LANES), lambda i, *_: (i, 0)),
            ],
            out_specs=pl.BlockSpec((TMC, D_MODEL), lambda i, *_: (i, 0)),
            scratch_shapes=[
                pltpu.VMEM((2, MAX_WIN * WIN * ROW_CHUNKS, LANES), _f32),
                pltpu.SemaphoreType.DMA((2,)),
                pltpu.VMEM((TMC * ROW_CHUNKS, LANES), _f32),
            ],
        ),
        out_shape=jax.ShapeDtypeStruct((n_tok, D_MODEL), _f32),
        compiler_params=pltpu.CompilerParams(dimension_semantics=("arbitrary",),
                                             vmem_limit_bytes=VMEM_LIMIT),
        name="combine",
    )(*tables, y_sorted, h2)
```

```python
import functools

import numpy as np
import jax
import jax.numpy as jnp
from jax import lax
from jax.experimental import pallas as pl
from jax.experimental.pallas import tpu as pltpu
from jax.experimental.pallas import tpu_sc as plsc

D_MODEL = 1024
N_META = 16
N_Q_HEADS = 8
N_KV_HEADS = 2
HEAD_DIM = 64
Q_GROUP = N_Q_HEADS // N_KV_HEADS
ATTN_DIM = N_Q_HEADS * HEAD_DIM
KV_DIM = N_KV_HEADS * HEAD_DIM
BLK = 128
PAD = BLK - N_META
N_BUCKETS = 32
MAX_DISTANCE = 128
CONV_CH = D_MODEL // 2
IN_PROJ = ATTN_DIM + 2 * KV_DIM + 3 * CONV_CH
N_GROUPS = 4
EXPERTS_PER_GROUP = 8
N_EXPERTS = N_GROUPS * EXPERTS_PER_GROUP
D_EXPERT = D_MODEL // 2
EPS = 1e-6
NEG_INF = -1e30
LOG2E = float(np.log2(np.e))

LANES = 128
SUBLANES = 8
ROW_CHUNKS = D_MODEL // LANES
TM = 512
TME = 512
TMC = TM
WIN = 16
MAX_WIN = N_EXPERTS + 2 * TMC // WIN
CMB_UNROLL = 32
XS_SLOTS = 3
PACKED_D = D_MODEL // 2
SC_WINDOW = 64
VMEM_LIMIT = 56 * 1024 * 1024

Q_OFF, K_OFF = 0, ATTN_DIM
CB_OFF = ATTN_DIM + 2 * KV_DIM
CC_OFF = CB_OFF + CONV_CH
CH_OFF = CC_OFF + CONV_CH

_f32 = jnp.float32
_bf16 = jnp.bfloat16


def _rms(x, g):
    return x * lax.rsqrt(jnp.mean(x * x, axis=-1, keepdims=True) + EPS) * g


def _dot(a, b):
    return jnp.dot(a, b, preferred_element_type=_f32)


def _dup_halves(x):
    lane = lax.broadcasted_iota(jnp.int32, x.shape, 1)
    sw = pltpu.roll(x, HEAD_DIM, axis=1)
    lo = lane < HEAD_DIM
    return jnp.where(lo, x, sw).astype(_bf16), jnp.where(lo, sw, x).astype(_bf16)


def _kv_state(hn_bf, win_ref, kg_ref, bdk_ref):
    kv = _dot(hn_bf, win_ref[:, K_OFF:K_OFF + 2 * KV_DIM])
    k = kv[:, :KV_DIM]
    v = kv[:, KV_DIM:]
    ssk = _dot((k * k).astype(_bf16), bdk_ref[...])
    kn = k * lax.rsqrt(ssk + EPS) * kg_ref[...]
    return _dup_halves(kn) + _dup_halves(v)


def _prep_kernel(rb_ref, meta_ref, mixg_ref, win_ref, kg_ref, bdk_ref, bucket_ref,
                 k0_ref, k1_ref, v0_ref, v1_ref, ut_ref, bias_ref):
    hn = _rms(meta_ref[...], mixg_ref[...]).astype(_bf16)
    k0, k1, v0, v1 = _kv_state(hn, win_ref, kg_ref, bdk_ref)
    k0_ref[...] = k0
    k1_ref[...] = k1
    v0_ref[...] = v0
    v1_ref[...] = v1
    cch = _dot(hn, win_ref[:, CC_OFF:CC_OFF + 2 * CONV_CH])
    u = cch[:, :CONV_CH] * cch[:, CONV_CH:]
    ut_ref[...] = u[BLK - SUBLANES:, :]
    for f in range(2):
        bk = bucket_ref[f]
        for h in range(N_Q_HEADS):
            acc = jnp.full((BLK, 2 * BLK), NEG_INF, _f32)
            for b in range(N_BUCKETS):
                acc = jnp.where(bk == b, rb_ref[b, h] * LOG2E, acc)
            bias_ref[f, h // Q_GROUP, (h % Q_GROUP) * BLK:(h % Q_GROUP + 1) * BLK, :] = acc


def _mixer_kernel(sinks_ref,
                  x_ref, mixg_ref, win_ref, qg_ref, kg_ref, bias_ref, convw_ref, ag_ref, cg_ref,
                  wout_ref, fg_ref, wr_ref, br_ref, k0m_ref, k1m_ref, v0m_ref, v1m_ref, utm_ref,
                  bdq_ref, bdk_ref, tri_ref,
                  h2_ref, hn2_ref, route_t_ref, cnt_ref, cntb_ref,
                  kp0, kp1, vp0, vp1, ubuf, a_scr, cnt_acc):
    b = pl.program_id(0)
    t = pl.program_id(1)

    @pl.when(t == 0)
    def _():
        kp0[...] = k0m_ref[...]
        kp1[...] = k1m_ref[...]
        vp0[...] = v0m_ref[...]
        vp1[...] = v1m_ref[...]
        ubuf[0:SUBLANES, :] = utm_ref[...]

    @pl.when(jnp.logical_and(b == 0, t == 0))
    def _():
        cnt_acc[...] = jnp.zeros_like(cnt_acc)

    x = x_ref[...]
    hn = _rms(x, mixg_ref[...]).astype(_bf16)

    q = _dot(hn, win_ref[:, Q_OFF:Q_OFF + ATTN_DIM])
    ssq = _dot((q * q).astype(_bf16), bdq_ref[...])
    qn = (q * lax.rsqrt(ssq + EPS) * qg_ref[...]).astype(_bf16)
    kd0, kd1, vd0, vd1 = _kv_state(hn, win_ref, kg_ref, bdk_ref)
    kd = (kd0, kd1)
    vd = (vd0, vd1)
    kp = (kp0, kp1)
    vp = (vp0, vp1)

    lane_q = lax.broadcasted_iota(jnp.int32, (BLK, LANES), 1)
    lo_half = lane_q < HEAD_DIM
    row4 = lax.broadcasted_iota(jnp.int32, (Q_GROUP * BLK, 1), 0) // BLK
    first = jnp.where(t == 0, 0, 1)
    zero_bf = jnp.zeros((BLK, LANES), _bf16)

    for j in range(TM // BLK):
        rows = slice(j * BLK, (j + 1) * BLK)
        for g in range(N_KV_HEADS):
            if j == 0:
                kcat = jnp.concatenate([kp[g][...], kd[g][rows]], axis=0)
                vcat = jnp.concatenate([vp[g][...], vd[g][rows]], axis=0)
                bias = bias_ref[first, g]
            else:
                kcat = kd[g][(j - 1) * BLK:(j + 1) * BLK]
                vcat = vd[g][(j - 1) * BLK:(j + 1) * BLK]
                bias = bias_ref[1, g]
            qs = []
            for hh in range(Q_GROUP):
                h = g * Q_GROUP + hh
                qc = qn[rows, (h // 2) * LANES:(h // 2 + 1) * LANES]
                keep = lo_half if h % 2 == 0 else jnp.logical_not(lo_half)
                qs.append(jnp.where(keep, qc, zero_bf))
            q4 = jnp.concatenate(qs, axis=0)
            s = lax.dot_general(q4, kcat, (((1,), (1,)), ((), ())),
                                preferred_element_type=_f32) + bias
            sink = jnp.full((Q_GROUP * BLK, 1), sinks_ref[g * Q_GROUP], _f32)
            for hh in range(1, Q_GROUP):
                sink = jnp.where(row4 == hh, sinks_ref[g * Q_GROUP + hh], sink)
            m = jnp.maximum(jnp.max(s, axis=-1, keepdims=True), sink)
            p = jnp.exp2(s - m)
            l = jnp.sum(p, axis=-1, keepdims=True) + jnp.exp2(sink - m)
            o = _dot(p.astype(_bf16), vcat) / l
            for pp in range(Q_GROUP // 2):
                ev = o[(2 * pp) * BLK:(2 * pp + 1) * BLK]
                od = o[(2 * pp + 1) * BLK:(2 * pp + 2) * BLK]
                col = g * (Q_GROUP // 2) + pp
                a_scr[rows, col * LANES:(col + 1) * LANES] = jnp.where(lo_half, ev, od)

    last = slice(TM - BLK, TM)
    kp0[...] = kd0[last]
    kp1[...] = kd1[last]
    vp0[...] = vd0[last]
    vp1[...] = vd1[last]

    cb = _dot(hn, win_ref[:, CB_OFF:CB_OFF + CONV_CH])
    cch = _dot(hn, win_ref[:, CC_OFF:CC_OFF + 2 * CONV_CH])
    u = cch[:, :CONV_CH] * cch[:, CONV_CH:]
    ubuf[SUBLANES:, :] = u
    u1 = ubuf[SUBLANES - 1:SUBLANES - 1 + TM, :]
    u2 = ubuf[SUBLANES - 2:SUBLANES - 2 + TM, :]
    c = cb * (convw_ref[0:1, :] * u2 + convw_ref[1:2, :] * u1 + convw_ref[2:3, :] * u)
    ubuf[0:SUBLANES, :] = u[TM - SUBLANES:, :]

    an = _rms(a_scr[...], ag_ref[...]).astype(_bf16)
    cn = _rms(c, cg_ref[...]).astype(_bf16)
    h2 = x + _dot(an, wout_ref[0:ATTN_DIM, :]) + _dot(cn, wout_ref[ATTN_DIM:, :])
    for cc in range(ROW_CHUNKS):
        h2_ref[pl.ds(cc, TM, stride=ROW_CHUNKS), :] = h2[:, cc * LANES:(cc + 1) * LANES]

    hn2 = _rms(h2, fg_ref[...])
    hn2_ref[...] = pltpu.pack_elementwise([hn2[:, :PACKED_D], hn2[:, PACKED_D:]], packed_dtype=_bf16)

    r1 = _dot(hn2.astype(_bf16), wr_ref[...])
    lg = r1 + pltpu.roll(r1, HEAD_DIM, axis=1) + br_ref[...]
    lane = lax.broadcasted_iota(jnp.int32, (TM, LANES), 1)
    lanef = lane.astype(_f32)
    ninf = jnp.float32(-jnp.inf)
    big = jnp.float32(LANES)

    gl = jnp.where(lane < N_GROUPS, lg, ninf)
    gmax = jnp.max(gl, axis=-1, keepdims=True)
    gsum = jnp.sum(jnp.exp(gl - gmax), axis=-1, keepdims=True)
    g_p = 1.0 / gsum
    g_idx = jnp.min(jnp.where(gl == gmax, lanef, big), axis=-1, keepdims=True)
    e_lo = N_GROUPS + EXPERTS_PER_GROUP * g_idx
    el = jnp.where(jnp.logical_and(lanef >= e_lo, lanef < e_lo + EXPERTS_PER_GROUP), lg, ninf)
    m1 = jnp.max(el, axis=-1, keepdims=True)
    i1 = jnp.min(jnp.where(el == m1, lanef, big), axis=-1, keepdims=True)
    el2 = jnp.where(lanef == i1, ninf, el)
    m2 = jnp.max(el2, axis=-1, keepdims=True)
    i2 = jnp.min(jnp.where(el2 == m2, lanef, big), axis=-1, keepdims=True)
    ex = jnp.exp(m2 - m1)
    den = 1.0 / (1.0 + ex)
    gate0 = g_p * den
    gate1 = g_p * ex * den
    e0 = i1 - N_GROUPS
    e1 = i2 - N_GROUPS

    oh0 = lanef == e0
    oh1 = lanef == e1
    cmat = (jnp.where(oh0, 1.0, 0.0) + jnp.where(oh1, 1.0, 0.0))
    prefix = _dot(tri_ref[...], cmat.astype(_bf16)) + cnt_acc[...]
    rank0 = jnp.sum(jnp.where(oh0, prefix, 0.0), axis=-1, keepdims=True)
    rank1 = jnp.sum(jnp.where(oh1, prefix, 0.0), axis=-1, keepdims=True)
    cntb_ref[...] = jnp.broadcast_to(cnt_acc[...], (SUBLANES, LANES))
    cnt_new = cnt_acc[...] + jnp.sum(cmat, axis=0, keepdims=True)
    cnt_acc[...] = cnt_new
    cnt_ref[...] = cnt_new

    lane_r = lax.broadcasted_iota(jnp.int32, (TM, LANES), 1)
    wide = jnp.zeros((TM, LANES), _f32)
    for idx, val in enumerate((e0, e1, rank0, rank1, gate0, gate1)):
        wide = jnp.where(lane_r == idx, val, wide)
    route_t_ref[...] = wide.T[:SUBLANES, :]


def _dispatch_rows(hn2_rows, pos0, pos1, n_rows):
    n_tok = hn2_rows.shape[0]
    info = plsc.get_sparse_core_info()
    n_workers = info.num_cores * info.num_subcores
    per_worker = n_tok // n_workers
    n_chunks = per_worker // SC_WINDOW
    assert per_worker * n_workers == n_tok and n_chunks * SC_WINDOW == per_worker and n_chunks % 2 == 0
    mesh = plsc.VectorSubcoreMesh(core_axis_name="core", subcore_axis_name="subcore")

    @functools.partial(
        pl.kernel,
        out_type=jax.ShapeDtypeStruct((n_rows, PACKED_D), jnp.int32),
        mesh=mesh,
        scratch_types=[pltpu.VMEM((SC_WINDOW,), jnp.int32), pltpu.VMEM((SC_WINDOW,), jnp.int32),
                       pltpu.VMEM((SC_WINDOW,), jnp.int32), pltpu.VMEM((SC_WINDOW,), jnp.int32),
                       pltpu.VMEM((SC_WINDOW, PACKED_D), jnp.int32), pltpu.VMEM((SC_WINDOW, PACKED_D), jnp.int32),
                       pltpu.SemaphoreType.DMA((2,)), pltpu.SemaphoreType.DMA((2,))],
        compiler_params=pltpu.CompilerParams(use_tc_tiling_on_sc=True),
        name="dispatch",
    )
    def dispatch(x_hbm, i0_hbm, i1_hbm, o_hbm, i0_a, i0_b, i1_a, i1_b, rows_a, rows_b, sem_ld, sem_st):
        wid = lax.axis_index("subcore") * info.num_cores + lax.axis_index("core")
        base = wid * per_worker
        i0_v, i1_v, rows_v = (i0_a, i0_b), (i1_a, i1_b), (rows_a, rows_b)

        def loads(chunk, b):
            off = pl.multiple_of(base + chunk * SC_WINDOW, SC_WINDOW)
            return (pltpu.make_async_copy(i0_hbm.at[pl.ds(off, SC_WINDOW)], i0_v[b], sem_ld.at[b]),
                    pltpu.make_async_copy(i1_hbm.at[pl.ds(off, SC_WINDOW)], i1_v[b], sem_ld.at[b]),
                    pltpu.make_async_copy(x_hbm.at[pl.ds(off, SC_WINDOW)], rows_v[b], sem_ld.at[b]))

        def stores(b):
            return (pltpu.make_async_copy(rows_v[b], o_hbm.at[i0_v[b]], sem_st.at[b]),
                    pltpu.make_async_copy(rows_v[b], o_hbm.at[i1_v[b]], sem_st.at[b]))

        for d in loads(0, 0):
            d.start()

        @pl.loop(0, n_chunks, step=2)
        def _(c):
            for b in range(2):
                chunk = c + b
                for d in loads(chunk, b):
                    d.wait()
                for d in stores(b):
                    d.start()

                @pl.when(chunk >= 1)
                def _():
                    for d in stores(1 - b):
                        d.wait()

                @pl.when(chunk + 1 < n_chunks)
                def _():
                    for d in loads(chunk + 1, 1 - b):
                        d.start()

        for d in stores(1):
            d.wait()

    return dispatch(hn2_rows, pos0, pos1)


def _experts_kernel(te_ref, tn_ref,
                    xs_hbm, wg_ref, wu_ref, wd_ref,
                    y_ref,
                    xbuf, sem, wgb, wub, wdb):
    i = pl.program_id(0)
    n = pl.num_programs(0)

    def fetch(tile):
        sl = tile % XS_SLOTS
        return pltpu.make_async_copy(xs_hbm.at[pl.ds(pl.multiple_of(tile * TME, TME), TME), :],
                                     xbuf.at[sl], sem.at[sl])

    @pl.when(i == 0)
    def _():
        for tile in range(XS_SLOTS - 1):
            fetch(tile).start()

    @pl.when(i + XS_SLOTS - 1 < n)
    def _():
        fetch(i + XS_SLOTS - 1).start()

    valid = tn_ref[i] > 0
    changed = jnp.logical_or(i == 0, te_ref[i] != te_ref[jnp.maximum(i - 1, 0)])

    @pl.when(jnp.logical_and(changed, valid))
    def _():
        wgb[...] = wg_ref[...].astype(_bf16)
        wub[...] = wu_ref[...].astype(_bf16)
        wdb[...] = wd_ref[...].astype(_bf16)

    fetch(i).wait()

    @pl.when(valid)
    def _():
        packed = xbuf[i % XS_SLOTS]
        halves = [pltpu.unpack_elementwise(packed, index=k, packed_dtype=_bf16, unpacked_dtype=_f32)
                  for k in range(2)]
        xb = jnp.concatenate(halves, axis=1).astype(_bf16)
        live = lax.broadcasted_iota(jnp.int32, (TME, 1), 0) < tn_ref[i]
        xb = jnp.where(live, xb, jnp.zeros_like(xb))
        gate = _dot(xb, wgb[...])
        up = _dot(xb, wub[...])
        act = (gate * jax.nn.sigmoid(gate) * up).astype(_bf16)
        y = _dot(act, wdb[...])
        for cc in range(ROW_CHUNKS):
            y_ref[pl.ds(cc, TME, stride=ROW_CHUNKS), :] = y[:, cc * LANES:(cc + 1) * LANES]

    @pl.when(jnp.logical_not(valid))
    def _():
        y_ref[...] = jnp.zeros_like(y_ref)


def _combine_kernel(nwin_ref, wstart_ref, widx0_ref, widx1_ref, gate0_ref, gate1_ref,
                    y_hbm, h2_ref,
                    out_ref,
                    wbuf, sem, otile):
    i = pl.program_id(0)
    n = pl.num_programs(0)
    slot = i % 2
    win_rows = WIN * ROW_CHUNKS

    def window(tile, sl, s):
        src = pl.multiple_of(wstart_ref[tile * MAX_WIN + s] * ROW_CHUNKS, ROW_CHUNKS)
        return pltpu.make_async_copy(y_hbm.at[pl.ds(src, win_rows), :],
                                     wbuf.at[sl, pl.ds(pl.multiple_of(s * win_rows, win_rows), win_rows), :],
                                     sem.at[sl])

    def issue(tile, sl):
        def body(s, carry):
            window(tile, sl, s).start()
            return carry

        lax.fori_loop(0, nwin_ref[tile], body, 0)

    @pl.when(i == 0)
    def _():
        issue(0, 0)

    @pl.when(i + 1 < n)
    def _():
        issue(i + 1, 1 - slot)

    def wait_body(s, carry):
        window(i, slot, s).wait()
        return carry

    lax.fori_loop(0, nwin_ref[i], wait_body, 0)

    def token_body(it, carry):
        for j in range(CMB_UNROLL):
            r = it * CMB_UNROLL + j
            tok = i * TMC + r
            y0 = wbuf[slot, pl.ds(pl.multiple_of(widx0_ref[tok], ROW_CHUNKS), ROW_CHUNKS), :]
            y1 = wbuf[slot, pl.ds(pl.multiple_of(widx1_ref[tok], ROW_CHUNKS), ROW_CHUNKS), :]
            row = pl.ds(pl.multiple_of(r * ROW_CHUNKS, ROW_CHUNKS), ROW_CHUNKS)
            otile[row, :] = h2_ref[row, :] + (gate0_ref[tok] * y0 + gate1_ref[tok] * y1)
        return carry

    lax.fori_loop(0, TMC // CMB_UNROLL, token_body, 0)
    for cc in range(ROW_CHUNKS):
        out_ref[:, cc * LANES:(cc + 1) * LANES] = otile[pl.ds(cc, TMC, stride=ROW_CHUNKS), :]


def _t5_bucket(n):
    max_exact = N_BUCKETS // 2
    nf = jnp.maximum(n, 1).astype(_f32)
    large = max_exact + (jnp.log(nf / max_exact) / np.log(MAX_DISTANCE / max_exact)
                         * (N_BUCKETS - max_exact)).astype(jnp.int32)
    large = jnp.minimum(large, N_BUCKETS - 1)
    return jnp.where(n < max_exact, n, large)


def _bucket_maps():
    qi = jnp.arange(BLK)[:, None]
    sj = jnp.arange(2 * BLK)[None, :]
    dist = BLK + qi - sj
    band = (dist >= 0) & (dist < BLK)
    bucket = _t5_bucket(jnp.maximum(dist, 0))
    generic = jnp.where(band, bucket, -1)
    first = jnp.where(band & (sj >= PAD), bucket, -1)
    return jnp.stack([first, generic]).astype(jnp.int32)


def _const(shape):
    nd = len(shape)
    return pl.BlockSpec(shape, lambda *_: (0,) * nd)


def _block_diag_mean(n):
    idx = np.arange(n) // HEAD_DIM
    return jnp.asarray((idx[:, None] == idx[None, :]).astype(np.float32) / HEAD_DIM, dtype=_bf16)


def kernel(x, meta_tokens, rel_bias, mix_norm_g, w_in, q_norm_g, k_norm_g, attn_sinks, conv_w, attn_out_norm_g, conv_out_norm_g, w_out, ffn_norm_g, w_group_router, b_group_router, w_expert_router, b_expert_router, w_gate, w_up, w_down):
    bsz, seq, _ = x.shape
    assert seq % TM == 0 and (bsz * seq) % TMC == 0
    n_tok = bsz * seq
    nt = seq // TM

    win = w_in[0].astype(_bf16)
    wout = w_out[0].astype(_bf16)
    mixg = mix_norm_g[0].reshape(1, D_MODEL)
    fg = ffn_norm_g[0].reshape(1, D_MODEL)
    qg = (jnp.tile(q_norm_g[0], N_Q_HEADS) * (HEAD_DIM ** -0.5 * LOG2E)).reshape(1, ATTN_DIM)
    kg = jnp.tile(k_norm_g[0], N_KV_HEADS).reshape(1, KV_DIM)
    ag = attn_out_norm_g[0].reshape(1, ATTN_DIM)
    cg = conv_out_norm_g[0].reshape(1, CONV_CH)
    convw = conv_w[0]
    sinks = attn_sinks[0] * LOG2E
    w_r = jnp.concatenate([w_group_router[0], w_expert_router[0].reshape(D_MODEL, N_EXPERTS)], axis=1)
    n_r = N_GROUPS + N_EXPERTS
    w_r_hi = w_r.astype(_bf16)
    w_r_lo = (w_r - w_r_hi.astype(_f32)).astype(_bf16)
    wr = jnp.zeros((D_MODEL, LANES), _bf16)
    wr = wr.at[:, :n_r].set(w_r_hi).at[:, HEAD_DIM:HEAD_DIM + n_r].set(w_r_lo)
    br = jnp.zeros((1, LANES), _f32).at[0, :n_r].set(
        jnp.concatenate([b_group_router[0], b_expert_router[0].reshape(N_EXPERTS)]))
    meta_blk = jnp.concatenate([jnp.zeros((PAD, D_MODEL), x.dtype), meta_tokens.astype(x.dtype)], axis=0)
    bdq = _block_diag_mean(ATTN_DIM)
    bdk = _block_diag_mean(KV_DIM)
    tri = jnp.asarray(np.tril(np.ones((TM, TM), np.float32), -1), dtype=_bf16)

    kv_sd = jax.ShapeDtypeStruct((BLK, KV_DIM), _bf16)
    k0m, k1m, v0m, v1m, utm, bias = pl.pallas_call(
        _prep_kernel,
        out_shape=(kv_sd, kv_sd, kv_sd, kv_sd,
                   jax.ShapeDtypeStruct((SUBLANES, CONV_CH), _f32),
                   jax.ShapeDtypeStruct((2, N_KV_HEADS, Q_GROUP * BLK, 2 * BLK), _f32)),
        in_specs=[pl.BlockSpec(memory_space=pltpu.SMEM)] + [pl.BlockSpec(memory_space=pltpu.VMEM)] * 6,
        out_specs=tuple(pl.BlockSpec(memory_space=pltpu.VMEM) for _ in range(6)),
        compiler_params=pltpu.CompilerParams(vmem_limit_bytes=VMEM_LIMIT),
        name="prep",
    )(rel_bias, meta_blk, mixg, win, kg, bdk, _bucket_maps())

    consts = (mixg, win, qg, kg, bias, convw, ag, cg, wout, fg, wr, br, k0m, k1m, v0m, v1m, utm, bdq, bdk, tri)
    h2, hn2, route_t, cnt, cntb = _mixer_call(sinks, x.reshape(n_tok, D_MODEL), consts, nt)
    pos2, t_exp, t_rows, n_tiles, tables = _plan(route_t, cnt, cntb, n_tok)
    xs = _dispatch_rows(hn2, pos2[0], pos2[1], n_tiles * TME)
    y_sorted = _experts_call(t_exp, t_rows, xs, w_gate[0], w_up[0], w_down[0], n_tiles)
    out = _combine_call(tables, y_sorted, h2)
    return out.reshape(bsz, seq, D_MODEL)


def _mixer_call(sinks, x_rows, consts, nt):
    n_tok = x_rows.shape[0]
    tile_idx = lambda b, t, *_: (b * nt + t, 0)
    grid_spec = pltpu.PrefetchScalarGridSpec(
        num_scalar_prefetch=1,
        grid=(n_tok // (nt * TM), nt),
        in_specs=[
            pl.BlockSpec((TM, D_MODEL), tile_idx),
            _const((1, D_MODEL)), _const((D_MODEL, IN_PROJ)), _const((1, ATTN_DIM)), _const((1, KV_DIM)),
            _const((2, N_KV_HEADS, Q_GROUP * BLK, 2 * BLK)), _const((3, CONV_CH)),
            _const((1, ATTN_DIM)), _const((1, CONV_CH)), _const((D_MODEL, D_MODEL)), _const((1, D_MODEL)),
            _const((D_MODEL, LANES)), _const((1, LANES)),
            _const((BLK, KV_DIM)), _const((BLK, KV_DIM)), _const((BLK, KV_DIM)), _const((BLK, KV_DIM)),
            _const((SUBLANES, CONV_CH)),
            _const((ATTN_DIM, ATTN_DIM)), _const((KV_DIM, KV_DIM)), _const((TM, TM)),
        ],
        out_specs=[
            pl.BlockSpec((TM * ROW_CHUNKS, LANES), tile_idx),
            pl.BlockSpec((TM, PACKED_D), tile_idx),
            pl.BlockSpec((SUBLANES, TM), lambda b, t, *_: (0, b * nt + t)),
            _const((1, LANES)),
            pl.BlockSpec((SUBLANES, LANES), tile_idx),
        ],
        scratch_shapes=[
            pltpu.VMEM((BLK, KV_DIM), _bf16), pltpu.VMEM((BLK, KV_DIM), _bf16),
            pltpu.VMEM((BLK, KV_DIM), _bf16), pltpu.VMEM((BLK, KV_DIM), _bf16),
            pltpu.VMEM((TM + SUBLANES, CONV_CH), _f32),
            pltpu.VMEM((TM, ATTN_DIM), _f32),
            pltpu.VMEM((1, LANES), _f32),
        ],
    )
    return pl.pallas_call(
        _mixer_kernel,
        grid_spec=grid_spec,
        out_shape=(jax.ShapeDtypeStruct((n_tok * ROW_CHUNKS, LANES), _f32),
                   jax.ShapeDtypeStruct((n_tok, PACKED_D), jnp.int32),
                   jax.ShapeDtypeStruct((SUBLANES, n_tok), _f32),
                   jax.ShapeDtypeStruct((1, LANES), _f32),
                   jax.ShapeDtypeStruct((n_tok // TM * SUBLANES, LANES), _f32)),
        compiler_params=pltpu.CompilerParams(dimension_semantics=("arbitrary", "arbitrary"),
                                             vmem_limit_bytes=VMEM_LIMIT),
        name="mixer",
    )(sinks, x_rows, *consts)


def _plan(route_t, cnt, cntb, n_tok):
    n_tiles = (n_tok * 2) // TME + N_EXPERTS + 1
    counts = cnt[0, :N_EXPERTS].astype(jnp.int32)
    ntile = (counts + TME - 1) // TME
    tile_end = jnp.cumsum(ntile)
    tile_start = tile_end - ntile
    eid = route_t[0:2].astype(jnp.int32)
    rank = route_t[2:4].astype(jnp.int32)
    gates = route_t[4:6]
    experts = jnp.arange(N_EXPERTS, dtype=jnp.int32)
    start_of = jnp.sum(jnp.where(eid[None] == experts[:, None, None], tile_start[:, None, None], 0), axis=0)
    pos2 = start_of * TME + rank
    tiles = jnp.arange(n_tiles, dtype=jnp.int32)
    n_used = tile_end[-1]
    t_exp = jnp.sum((jnp.minimum(tiles, n_used - 1)[:, None] >= tile_end[None, :]).astype(jnp.int32), axis=-1)
    t_exp = jnp.minimum(t_exp, N_EXPERTS - 1)
    own = t_exp[:, None] == experts
    t_rows = jnp.sum(jnp.where(own, counts - (tiles[:, None] - tile_start) * TME, 0), axis=-1)
    t_rows = jnp.where(tiles < n_used, jnp.clip(t_rows, 0, TME), 0).astype(jnp.int32)

    n_mix = n_tok // TMC
    before = cntb[::SUBLANES, :N_EXPERTS].astype(jnp.int32)
    held = jnp.concatenate([before[1:], counts[None]], axis=0) - before
    first = tile_start[None, :] * TME + before
    nw = (held + WIN - 1) // WIN
    slot_end = jnp.cumsum(nw, axis=1)
    slot_beg = slot_end - nw
    nwin = slot_end[:, -1]
    slots = jnp.arange(MAX_WIN, dtype=jnp.int32)
    owner = jnp.minimum(jnp.sum((slots[None, :, None] >= slot_end[:, None, :]).astype(jnp.int32), axis=-1),
                        N_EXPERTS - 1)
    is_owner = owner[:, :, None] == experts
    wstart = jnp.sum(jnp.where(is_owner, first[:, None, :] + (slots[None, :, None] - slot_beg[:, None, :]) * WIN,
                               0), axis=-1)
    wstart = jnp.where(slots[None, :] < nwin[:, None], wstart, 0).astype(jnp.int32).reshape(-1)
    mine = eid.reshape(2, n_mix, TMC, 1) == experts
    beg_tok = jnp.sum(jnp.where(mine, slot_beg[None, :, None, :], 0), axis=-1)
    before_tok = jnp.sum(jnp.where(mine, before[None, :, None, :], 0), axis=-1)
    widx = (beg_tok * WIN + rank.reshape(2, n_mix, TMC) - before_tok).reshape(2, n_tok).astype(jnp.int32)
    widx = widx * ROW_CHUNKS
    return pos2, t_exp, t_rows, n_tiles, (nwin.astype(jnp.int32), wstart, widx[0], widx[1], gates[0], gates[1])


def _experts_call(t_exp, t_rows, xs, w_gate, w_up, w_down, n_tiles):
    return pl.pallas_call(
        _experts_kernel,
        grid_spec=pltpu.PrefetchScalarGridSpec(
            num_scalar_prefetch=2,
            grid=(n_tiles,),
            in_specs=[
                pl.BlockSpec(memory_space=pl.ANY),
                pl.BlockSpec((None, D_MODEL, D_EXPERT), lambda i, te, tn: (te[i], 0, 0)),
                pl.BlockSpec((None, D_MODEL, D_EXPERT), lambda i, te, tn: (te[i], 0, 0)),
                pl.BlockSpec((None, D_EXPERT, D_MODEL), lambda i, te, tn: (te[i], 0, 0)),
            ],
            out_specs=pl.BlockSpec((TME * ROW_CHUNKS, LANES), lambda i, *_: (i, 0)),
            scratch_shapes=[
                pltpu.VMEM((XS_SLOTS, TME, PACKED_D), jnp.int32),
                pltpu.SemaphoreType.DMA((XS_SLOTS,)),
                pltpu.VMEM((D_MODEL, D_EXPERT), _bf16),
                pltpu.VMEM((D_MODEL, D_EXPERT), _bf16),
                pltpu.VMEM((D_EXPERT, D_MODEL), _bf16),
            ],
        ),
        out_shape=jax.ShapeDtypeStruct((n_tiles * TME * ROW_CHUNKS, LANES), _f32),
        compiler_params=pltpu.CompilerParams(dimension_semantics=("arbitrary",),
                                             vmem_limit_bytes=VMEM_LIMIT),
        name="experts",
    )(t_exp, t_rows, xs, w_gate, w_up, w_down)


def _combine_call(tables, y_sorted, h2):
    n_tok = h2.shape[0] // ROW_CHUNKS
    return pl.pallas_call(
        _combine_kernel,
        grid_spec=pltpu.PrefetchScalarGridSpec(
            num_scalar_prefetch=len(tables),
            grid=(n_tok // TMC,),
            in_specs=[
                pl.BlockSpec(memory_space=pl.ANY),
                pl.BlockSpec((TMC * ROW_CHUNKS, LANES), lambda i, *_: (i, 0)),
            ],
            out_specs=pl.BlockSpec((TMC, D_MODEL), lambda i, *_: (i, 0)),
            scratch_shapes=[
                pltpu.VMEM((2, MAX_WIN * WIN * ROW_CHUNKS, LANES), _f32),
                pltpu.SemaphoreType.DMA((2,)),
                pltpu.VMEM((TMC * ROW_CHUNKS, LANES), _f32),
            ],
        ),
        out_shape=jax.ShapeDtypeStruct((n_tok, D_MODEL), _f32),
        compiler_params=pltpu.CompilerParams(dimension_semantics=("arbitrary",),
                                             vmem_limit_bytes=VMEM_LIMIT),
        name="combine",
    )(*tables, y_sorted, h2)
```

```python
import functools

import numpy as np
import jax
import jax.numpy as jnp
from jax import lax
from jax.experimental import pallas as pl
from jax.experimental.pallas import tpu as pltpu
from jax.experimental.pallas import tpu_sc as plsc

D_MODEL = 1024
N_META = 16
N_Q_HEADS = 8
N_KV_HEADS = 2
HEAD_DIM = 64
Q_GROUP = N_Q_HEADS // N_KV_HEADS
ATTN_DIM = N_Q_HEADS * HEAD_DIM
KV_DIM = N_KV_HEADS * HEAD_DIM
BLK = 128
PAD = BLK - N_META
N_BUCKETS = 32
MAX_DISTANCE = 128
CONV_CH = D_MODEL // 2
IN_PROJ = ATTN_DIM + 2 * KV_DIM + 3 * CONV_CH
N_GROUPS = 4
EXPERTS_PER_GROUP = 8
N_EXPERTS = N_GROUPS * EXPERTS_PER_GROUP
D_EXPERT = D_MODEL // 2
EPS = 1e-6
NEG_INF = -1e30
LOG2E = float(np.log2(np.e))

LANES = 128
SUBLANES = 8
ROW_CHUNKS = D_MODEL // LANES
TM = 512
TME = 512
TMC = TM
WIN = 16
MAX_WIN = N_EXPERTS + 2 * TMC // WIN
CMB_UNROLL = 64
XS_SLOTS = 4
PACKED_D = D_MODEL // 2
SC_WINDOW = 64
VMEM_LIMIT = 56 * 1024 * 1024

Q_OFF, K_OFF = 0, ATTN_DIM
CB_OFF = ATTN_DIM + 2 * KV_DIM
CC_OFF = CB_OFF + CONV_CH
CH_OFF = CC_OFF + CONV_CH

_f32 = jnp.float32
_bf16 = jnp.bfloat16


def _rms(x, g):
    return x * lax.rsqrt(jnp.mean(x * x, axis=-1, keepdims=True) + EPS) * g


def _dot(a, b):
    return jnp.dot(a, b, preferred_element_type=_f32)


def _dup_halves(x):
    lane = lax.broadcasted_iota(jnp.int32, x.shape, 1)
    sw = pltpu.roll(x, HEAD_DIM, axis=1)
    lo = lane < HEAD_DIM
    return jnp.where(lo, x, sw).astype(_bf16), jnp.where(lo, sw, x).astype(_bf16)


def _kv_state(hn_bf, win_ref, kg_ref, bdk_ref):
    kv = _dot(hn_bf, win_ref[:, K_OFF:K_OFF + 2 * KV_DIM])
    k = kv[:, :KV_DIM]
    v = kv[:, KV_DIM:]
    ssk = _dot((k * k).astype(_bf16), bdk_ref[...])
    kn = k * lax.rsqrt(ssk + EPS) * kg_ref[...]
    return _dup_halves(kn) + _dup_halves(v)


def _prep_kernel(rb_ref, meta_ref, mixg_ref, win_ref, kg_ref, bdk_ref, bucket_ref,
                 k0_ref, k1_ref, v0_ref, v1_ref, ut_ref, bias_ref):
    hn = _rms(meta_ref[...], mixg_ref[...]).astype(_bf16)
    k0, k1, v0, v1 = _kv_state(hn, win_ref, kg_ref, bdk_ref)
    k0_ref[...] = k0
    k1_ref[...] = k1
    v0_ref[...] = v0
    v1_ref[...] = v1
    cch = _dot(hn, win_ref[:, CC_OFF:CC_OFF + 2 * CONV_CH])
    u = cch[:, :CONV_CH] * cch[:, CONV_CH:]
    ut_ref[...] = u[BLK - SUBLANES:, :]
    for f in range(2):
        bk = bucket_ref[f]
        for h in range(N_Q_HEADS):
            acc = jnp.full((BLK, 2 * BLK), NEG_INF, _f32)
            for b in range(N_BUCKETS):
                acc = jnp.where(bk == b, rb_ref[b, h] * LOG2E, acc)
            bias_ref[f, h // Q_GROUP, (h % Q_GROUP) * BLK:(h % Q_GROUP + 1) * BLK, :] = acc


def _mixer_kernel(sinks_ref,
                  x_ref, mixg_ref, win_ref, qg_ref, kg_ref, bias_ref, convw_ref, ag_ref, cg_ref,
                  wout_ref, fg_ref, wr_ref, br_ref, k0m_ref, k1m_ref, v0m_ref, v1m_ref, utm_ref,
                  bdq_ref, bdk_ref, tri_ref,
                  h2_ref, hn2_ref, route_t_ref, cnt_ref, cntb_ref,
                  kp0, kp1, vp0, vp1, ubuf, a_scr, cnt_acc):
    b = pl.program_id(0)
    t = pl.program_id(1)

    @pl.when(t == 0)
    def _():
        kp0[...] = k0m_ref[...]
        kp1[...] = k1m_ref[...]
        vp0[...] = v0m_ref[...]
        vp1[...] = v1m_ref[...]
        ubuf[0:SUBLANES, :] = utm_ref[...]

    @pl.when(jnp.logical_and(b == 0, t == 0))
    def _():
        cnt_acc[...] = jnp.zeros_like(cnt_acc)

    x = x_ref[...]
    hn = _rms(x, mixg_ref[...]).astype(_bf16)

    q = _dot(hn, win_ref[:, Q_OFF:Q_OFF + ATTN_DIM])
    ssq = _dot((q * q).astype(_bf16), bdq_ref[...])
    qn = (q * lax.rsqrt(ssq + EPS) * qg_ref[...]).astype(_bf16)
    kd0, kd1, vd0, vd1 = _kv_state(hn, win_ref, kg_ref, bdk_ref)
    kd = (kd0, kd1)
    vd = (vd0, vd1)
    kp = (kp0, kp1)
    vp = (vp0, vp1)

    lane_q = lax.broadcasted_iota(jnp.int32, (BLK, LANES), 1)
    lo_half = lane_q < HEAD_DIM
    row4 = lax.broadcasted_iota(jnp.int32, (Q_GROUP * BLK, 1), 0) // BLK
    first = jnp.where(t == 0, 0, 1)
    zero_bf = jnp.zeros((BLK, LANES), _bf16)

    for j in range(TM // BLK):
        rows = slice(j * BLK, (j + 1) * BLK)
        for g in range(N_KV_HEADS):
            if j == 0:
                kcat = jnp.concatenate([kp[g][...], kd[g][rows]], axis=0)
                vcat = jnp.concatenate([vp[g][...], vd[g][rows]], axis=0)
                bias = bias_ref[first, g]
            else:
                kcat = kd[g][(j - 1) * BLK:(j + 1) * BLK]
                vcat = vd[g][(j - 1) * BLK:(j + 1) * BLK]
                bias = bias_ref[1, g]
            qs = []
            for hh in range(Q_GROUP):
                h = g * Q_GROUP + hh
                qc = qn[rows, (h // 2) * LANES:(h // 2 + 1) * LANES]
                keep = lo_half if h % 2 == 0 else jnp.logical_not(lo_half)
                qs.append(jnp.where(keep, qc, zero_bf))
            q4 = jnp.concatenate(qs, axis=0)
            s = lax.dot_general(q4, kcat, (((1,), (1,)), ((), ())),
                                preferred_element_type=_f32) + bias
            sink = jnp.full((Q_GROUP * BLK, 1), sinks_ref[g * Q_GROUP], _f32)
            for hh in range(1, Q_GROUP):
                sink = jnp.where(row4 == hh, sinks_ref[g * Q_GROUP + hh], sink)
            m = jnp.maximum(jnp.max(s, axis=-1, keepdims=True), sink)
            p = jnp.exp2(s - m)
            l = jnp.sum(p, axis=-1, keepdims=True) + jnp.exp2(sink - m)
            o = _dot(p.astype(_bf16), vcat) / l
            for pp in range(Q_GROUP // 2):
                ev = o[(2 * pp) * BLK:(2 * pp + 1) * BLK]
                od = o[(2 * pp + 1) * BLK:(2 * pp + 2) * BLK]
                col = g * (Q_GROUP // 2) + pp
                a_scr[rows, col * LANES:(col + 1) * LANES] = jnp.where(lo_half, ev, od)

    last = slice(TM - BLK, TM)
    kp0[...] = kd0[last]
    kp1[...] = kd1[last]
    vp0[...] = vd0[last]
    vp1[...] = vd1[last]

    cb = _dot(hn, win_ref[:, CB_OFF:CB_OFF + CONV_CH])
    cch = _dot(hn, win_ref[:, CC_OFF:CC_OFF + 2 * CONV_CH])
    u = cch[:, :CONV_CH] * cch[:, CONV_CH:]
    ubuf[SUBLANES:, :] = u
    u1 = ubuf[SUBLANES - 1:SUBLANES - 1 + TM, :]
    u2 = ubuf[SUBLANES - 2:SUBLANES - 2 + TM, :]
    c = cb * (convw_ref[0:1, :] * u2 + convw_ref[1:2, :] * u1 + convw_ref[2:3, :] * u)
    ubuf[0:SUBLANES, :] = u[TM - SUBLANES:, :]

    an = _rms(a_scr[...], ag_ref[...]).astype(_bf16)
    cn = _rms(c, cg_ref[...]).astype(_bf16)
    h2 = x + _dot(an, wout_ref[0:ATTN_DIM, :]) + _dot(cn, wout_ref[ATTN_DIM:, :])
    for cc in range(ROW_CHUNKS):
        h2_ref[pl.ds(cc, TM, stride=ROW_CHUNKS), :] = h2[:, cc * LANES:(cc + 1) * LANES]

    hn2 = _rms(h2, fg_ref[...])
    hn2_ref[...] = pltpu.pack_elementwise([hn2[:, :PACKED_D], hn2[:, PACKED_D:]], packed_dtype=_bf16)

    r1 = _dot(hn2.astype(_bf16), wr_ref[...])
    lg = r1 + pltpu.roll(r1, HEAD_DIM, axis=1) + br_ref[...]
    lane = lax.broadcasted_iota(jnp.int32, (TM, LANES), 1)
    lanef = lane.astype(_f32)
    ninf = jnp.float32(-jnp.inf)
    big = jnp.float32(LANES)

    gl = jnp.where(lane < N_GROUPS, lg, ninf)
    gmax = jnp.max(gl, axis=-1, keepdims=True)
    gsum = jnp.sum(jnp.exp(gl - gmax), axis=-1, keepdims=True)
    g_p = 1.0 / gsum
    g_idx = jnp.min(jnp.where(gl == gmax, lanef, big), axis=-1, keepdims=True)
    e_lo = N_GROUPS + EXPERTS_PER_GROUP * g_idx
    el = jnp.where(jnp.logical_and(lanef >= e_lo, lanef < e_lo + EXPERTS_PER_GROUP), lg, ninf)
    m1 = jnp.max(el, axis=-1, keepdims=True)
    i1 = jnp.min(jnp.where(el == m1, lanef, big), axis=-1, keepdims=True)
    el2 = jnp.where(lanef == i1, ninf, el)
    m2 = jnp.max(el2, axis=-1, keepdims=True)
    i2 = jnp.min(jnp.where(el2 == m2, lanef, big), axis=-1, keepdims=True)
    ex = jnp.exp(m2 - m1)
    den = 1.0 / (1.0 + ex)
    gate0 = g_p * den
    gate1 = g_p * ex * den
    e0 = i1 - N_GROUPS
    e1 = i2 - N_GROUPS

    oh0 = lanef == e0
    oh1 = lanef == e1
    cmat = (jnp.where(oh0, 1.0, 0.0) + jnp.where(oh1, 1.0, 0.0))
    prefix = _dot(tri_ref[...], cmat.astype(_bf16)) + cnt_acc[...]
    rank0 = jnp.sum(jnp.where(oh0, prefix, 0.0), axis=-1, keepdims=True)
    rank1 = jnp.sum(jnp.where(oh1, prefix, 0.0), axis=-1, keepdims=True)
    cntb_ref[...] = jnp.broadcast_to(cnt_acc[...], (SUBLANES, LANES))
    cnt_new = cnt_acc[...] + jnp.sum(cmat, axis=0, keepdims=True)
    cnt_acc[...] = cnt_new
    cnt_ref[...] = cnt_new

    lane_r = lax.broadcasted_iota(jnp.int32, (TM, LANES), 1)
    wide = jnp.zeros((TM, LANES), _f32)
    for idx, val in enumerate((e0, e1, rank0, rank1, gate0, gate1)):
        wide = jnp.where(lane_r == idx, val, wide)
    route_t_ref[...] = wide.T[:SUBLANES, :]


def _dispatch_rows(hn2_rows, pos0, pos1, n_rows):
    n_tok = hn2_rows.shape[0]
    info = plsc.get_sparse_core_info()
    n_workers = info.num_cores * info.num_subcores
    per_worker = n_tok // n_workers
    n_chunks = per_worker // SC_WINDOW
    assert per_worker * n_workers == n_tok and n_chunks * SC_WINDOW == per_worker and n_chunks % 2 == 0
    mesh = plsc.VectorSubcoreMesh(core_axis_name="core", subcore_axis_name="subcore")

    @functools.partial(
        pl.kernel,
        out_type=jax.ShapeDtypeStruct((n_rows, PACKED_D), jnp.int32),
        mesh=mesh,
        scratch_types=[pltpu.VMEM((SC_WINDOW,), jnp.int32), pltpu.VMEM((SC_WINDOW,), jnp.int32),
                       pltpu.VMEM((SC_WINDOW,), jnp.int32), pltpu.VMEM((SC_WINDOW,), jnp.int32),
                       pltpu.VMEM((SC_WINDOW, PACKED_D), jnp.int32), pltpu.VMEM((SC_WINDOW, PACKED_D), jnp.int32),
                       pltpu.SemaphoreType.DMA((2,)), pltpu.SemaphoreType.DMA((2,))],
        compiler_params=pltpu.CompilerParams(use_tc_tiling_on_sc=True),
        name="dispatch",
    )
    def dispatch(x_hbm, i0_hbm, i1_hbm, o_hbm, i0_a, i0_b, i1_a, i1_b, rows_a, rows_b, sem_ld, sem_st):
        wid = lax.axis_index("subcore") * info.num_cores + lax.axis_index("core")
        base = wid * per_worker
        i0_v, i1_v, rows_v = (i0_a, i0_b), (i1_a, i1_b), (rows_a, rows_b)

        def loads(chunk, b):
            off = pl.multiple_of(base + chunk * SC_WINDOW, SC_WINDOW)
            return (pltpu.make_async_copy(i0_hbm.at[pl.ds(off, SC_WINDOW)], i0_v[b], sem_ld.at[b]),
                    pltpu.make_async_copy(i1_hbm.at[pl.ds(off, SC_WINDOW)], i1_v[b], sem_ld.at[b]),
                    pltpu.make_async_copy(x_hbm.at[pl.ds(off, SC_WINDOW)], rows_v[b], sem_ld.at[b]))

        def stores(b):
            return (pltpu.make_async_copy(rows_v[b], o_hbm.at[i0_v[b]], sem_st.at[b]),
                    pltpu.make_async_copy(rows_v[b], o_hbm.at[i1_v[b]], sem_st.at[b]))

        for d in loads(0, 0):
            d.start()

        @pl.loop(0, n_chunks, step=2)
        def _(c):
            for b in range(2):
                chunk = c + b
                for d in loads(chunk, b):
                    d.wait()
                for d in stores(b):
                    d.start()

                @pl.when(chunk >= 1)
                def _():
                    for d in stores(1 - b):
                        d.wait()

                @pl.when(chunk + 1 < n_chunks)
                def _():
                    for d in loads(chunk + 1, 1 - b):
                        d.start()

        for d in stores(1):
            d.wait()

    return dispatch(hn2_rows, pos0, pos1)


def _experts_kernel(te_ref, tn_ref,
                    xs_hbm, wg_ref, wu_ref, wd_ref,
                    y_ref,
                    xbuf, sem, wgb, wub, wdb):
    i = pl.program_id(0)
    n = pl.num_programs(0)

    def fetch(tile):
        sl = tile % XS_SLOTS
        return pltpu.make_async_copy(xs_hbm.at[pl.ds(pl.multiple_of(tile * TME, TME), TME), :],
                                     xbuf.at[sl], sem.at[sl])

    @pl.when(i == 0)
    def _():
        for tile in range(XS_SLOTS - 1):
            fetch(tile).start()

    @pl.when(i + XS_SLOTS - 1 < n)
    def _():
        fetch(i + XS_SLOTS - 1).start()

    valid = tn_ref[i] > 0
    changed = jnp.logical_or(i == 0, te_ref[i] != te_ref[jnp.maximum(i - 1, 0)])

    @pl.when(jnp.logical_and(changed, valid))
    def _():
        wgb[...] = wg_ref[...].astype(_bf16)
        wub[...] = wu_ref[...].astype(_bf16)
        wdb[...] = wd_ref[...].astype(_bf16)

    fetch(i).wait()

    @pl.when(valid)
    def _():
        packed = xbuf[i % XS_SLOTS]
        halves = [pltpu.unpack_elementwise(packed, index=k, packed_dtype=_bf16, unpacked_dtype=_f32)
                  for k in range(2)]
        xb = jnp.concatenate(halves, axis=1).astype(_bf16)
        live = lax.broadcasted_iota(jnp.int32, (TME, 1), 0) < tn_ref[i]
        xb = jnp.where(live, xb, jnp.zeros_like(xb))
        gate = _dot(xb, wgb[...])
        up = _dot(xb, wub[...])
        act = (gate * jax.nn.sigmoid(gate) * up).astype(_bf16)
        y = _dot(act, wdb[...])
        for cc in range(ROW_CHUNKS):
            y_ref[pl.ds(cc, TME, stride=ROW_CHUNKS), :] = y[:, cc * LANES:(cc + 1) * LANES]

    @pl.when(jnp.logical_not(valid))
    def _():
        y_ref[...] = jnp.zeros_like(y_ref)


def _combine_kernel(nwin_ref, wstart_ref, widx0_ref, widx1_ref, gate0_ref, gate1_ref,
                    y_hbm, h2_ref,
                    out_ref,
                    wbuf, sem, otile):
    i = pl.program_id(0)
    n = pl.num_programs(0)
    slot = i % 2
    win_rows = WIN * ROW_CHUNKS

    def window(tile, sl, s):
        src = pl.multiple_of(wstart_ref[tile * MAX_WIN + s] * ROW_CHUNKS, ROW_CHUNKS)
        return pltpu.make_async_copy(y_hbm.at[pl.ds(src, win_rows), :],
                                     wbuf.at[sl, pl.ds(pl.multiple_of(s * win_rows, win_rows), win_rows), :],
                                     sem.at[sl])

    def issue(tile, sl):
        def body(s, carry):
            window(tile, sl, s).start()
            return carry

        lax.fori_loop(0, nwin_ref[tile], body, 0)

    @pl.when(i == 0)
    def _():
        issue(0, 0)

    @pl.when(i + 1 < n)
    def _():
        issue(i + 1, 1 - slot)

    def wait_body(s, carry):
        window(i, slot, s).wait()
        return carry

    lax.fori_loop(0, nwin_ref[i], wait_body, 0)

    def token_body(it, carry):
        for j in range(CMB_UNROLL):
            r = it * CMB_UNROLL + j
            tok = i * TMC + r
            y0 = wbuf[slot, pl.ds(pl.multiple_of(widx0_ref[tok], ROW_CHUNKS), ROW_CHUNKS), :]
            y1 = wbuf[slot, pl.ds(pl.multiple_of(widx1_ref[tok], ROW_CHUNKS), ROW_CHUNKS), :]
            row = pl.ds(pl.multiple_of(r * ROW_CHUNKS, ROW_CHUNKS), ROW_CHUNKS)
            otile[row, :] = h2_ref[row, :] + (gate0_ref[tok] * y0 + gate1_ref[tok] * y1)
        return carry

    lax.fori_loop(0, TMC // CMB_UNROLL, token_body, 0)
    for cc in range(ROW_CHUNKS):
        out_ref[:, cc * LANES:(cc + 1) * LANES] = otile[pl.ds(cc, TMC, stride=ROW_CHUNKS), :]


def _t5_bucket(n):
    max_exact = N_BUCKETS // 2
    nf = jnp.maximum(n, 1).astype(_f32)
    large = max_exact + (jnp.log(nf / max_exact) / np.log(MAX_DISTANCE / max_exact)
                         * (N_BUCKETS - max_exact)).astype(jnp.int32)
    large = jnp.minimum(large, N_BUCKETS - 1)
    return jnp.where(n < max_exact, n, large)


def _bucket_maps():
    qi = jnp.arange(BLK)[:, None]
    sj = jnp.arange(2 * BLK)[None, :]
    dist = BLK + qi - sj
    band = (dist >= 0) & (dist < BLK)
    bucket = _t5_bucket(jnp.maximum(dist, 0))
    generic = jnp.where(band, bucket, -1)
    first = jnp.where(band & (sj >= PAD), bucket, -1)
    return jnp.stack([first, generic]).astype(jnp.int32)


def _const(shape):
    nd = len(shape)
    return pl.BlockSpec(shape, lambda *_: (0,) * nd)


def _block_diag_mean(n):
    idx = np.arange(n) // HEAD_DIM
    return jnp.asarray((idx[:, None] == idx[None, :]).astype(np.float32) / HEAD_DIM, dtype=_bf16)


def kernel(x, meta_tokens, rel_bias, mix_norm_g, w_in, q_norm_g, k_norm_g, attn_sinks, conv_w, attn_out_norm_g, conv_out_norm_g, w_out, ffn_norm_g, w_group_router, b_group_router, w_expert_router, b_expert_router, w_gate, w_up, w_down):
    bsz, seq, _ = x.shape
    assert seq % TM == 0 and (bsz * seq) % TMC == 0
    n_tok = bsz * seq
    nt = seq // TM

    win = w_in[0].astype(_bf16)
    wout = w_out[0].astype(_bf16)
    mixg = mix_norm_g[0].reshape(1, D_MODEL)
    fg = ffn_norm_g[0].reshape(1, D_MODEL)
    qg = (jnp.tile(q_norm_g[0], N_Q_HEADS) * (HEAD_DIM ** -0.5 * LOG2E)).reshape(1, ATTN_DIM)
    kg = jnp.tile(k_norm_g[0], N_KV_HEADS).reshape(1, KV_DIM)
    ag = attn_out_norm_g[0].reshape(1, ATTN_DIM)
    cg = conv_out_norm_g[0].reshape(1, CONV_CH)
    convw = conv_w[0]
    sinks = attn_sinks[0] * LOG2E
    w_r = jnp.concatenate([w_group_router[0], w_expert_router[0].reshape(D_MODEL, N_EXPERTS)], axis=1)
    n_r = N_GROUPS + N_EXPERTS
    w_r_hi = w_r.astype(_bf16)
    w_r_lo = (w_r - w_r_hi.astype(_f32)).astype(_bf16)
    wr = jnp.zeros((D_MODEL, LANES), _bf16)
    wr = wr.at[:, :n_r].set(w_r_hi).at[:, HEAD_DIM:HEAD_DIM + n_r].set(w_r_lo)
    br = jnp.zeros((1, LANES), _f32).at[0, :n_r].set(
        jnp.concatenate([b_group_router[0], b_expert_router[0].reshape(N_EXPERTS)]))
    meta_blk = jnp.concatenate([jnp.zeros((PAD, D_MODEL), x.dtype), meta_tokens.astype(x.dtype)], axis=0)
    bdq = _block_diag_mean(ATTN_DIM)
    bdk = _block_diag_mean(KV_DIM)
    tri = jnp.asarray(np.tril(np.ones((TM, TM), np.float32), -1), dtype=_bf16)

    kv_sd = jax.ShapeDtypeStruct((BLK, KV_DIM), _bf16)
    k0m, k1m, v0m, v1m, utm, bias = pl.pallas_call(
        _prep_kernel,
        out_shape=(kv_sd, kv_sd, kv_sd, kv_sd,
                   jax.ShapeDtypeStruct((SUBLANES, CONV_CH), _f32),
                   jax.ShapeDtypeStruct((2, N_KV_HEADS, Q_GROUP * BLK, 2 * BLK), _f32)),
        in_specs=[pl.BlockSpec(memory_space=pltpu.SMEM)] + [pl.BlockSpec(memory_space=pltpu.VMEM)] * 6,
        out_specs=tuple(pl.BlockSpec(memory_space=pltpu.VMEM) for _ in range(6)),
        compiler_params=pltpu.CompilerParams(vmem_limit_bytes=VMEM_LIMIT),
        name="prep",
    )(rel_bias, meta_blk, mixg, win, kg, bdk, _bucket_maps())

    consts = (mixg, win, qg, kg, bias, convw, ag, cg, wout, fg, wr, br, k0m, k1m, v0m, v1m, utm, bdq, bdk, tri)
    h2, hn2, route_t, cnt, cntb = _mixer_call(sinks, x.reshape(n_tok, D_MODEL), consts, nt)
    pos2, t_exp, t_rows, n_tiles, tables = _plan(route_t, cnt, cntb, n_tok)
    xs = _dispatch_rows(hn2, pos2[0], pos2[1], n_tiles * TME)
    y_sorted = _experts_call(t_exp, t_rows, xs, w_gate[0], w_up[0], w_down[0], n_tiles)
    out = _combine_call(tables, y_sorted, h2)
    return out.reshape(bsz, seq, D_MODEL)


def _mixer_call(sinks, x_rows, consts, nt):
    n_tok = x_rows.shape[0]
    tile_idx = lambda b, t, *_: (b * nt + t, 0)
    grid_spec = pltpu.PrefetchScalarGridSpec(
        num_scalar_prefetch=1,
        grid=(n_tok // (nt * TM), nt),
        in_specs=[
            pl.BlockSpec((TM, D_MODEL), tile_idx),
            _const((1, D_MODEL)), _const((D_MODEL, IN_PROJ)), _const((1, ATTN_DIM)), _const((1, KV_DIM)),
            _const((2, N_KV_HEADS, Q_GROUP * BLK, 2 * BLK)), _const((3, CONV_CH)),
            _const((1, ATTN_DIM)), _const((1, CONV_CH)), _const((D_MODEL, D_MODEL)), _const((1, D_MODEL)),
            _const((D_MODEL, LANES)), _const((1, LANES)),
            _const((BLK, KV_DIM)), _const((BLK, KV_DIM)), _const((BLK, KV_DIM)), _const((BLK, KV_DIM)),
            _const((SUBLANES, CONV_CH)),
            _const((ATTN_DIM, ATTN_DIM)), _const((KV_DIM, KV_DIM)), _const((TM, TM)),
        ],
        out_specs=[
            pl.BlockSpec((TM * ROW_CHUNKS, LANES), tile_idx),
            pl.BlockSpec((TM, PACKED_D), tile_idx),
            pl.BlockSpec((SUBLANES, TM), lambda b, t, *_: (0, b * nt + t)),
            _const((1, LANES)),
            pl.BlockSpec((SUBLANES, LANES), tile_idx),
        ],
        scratch_shapes=[
            pltpu.VMEM((BLK, KV_DIM), _bf16), pltpu.VMEM((BLK, KV_DIM), _bf16),
            pltpu.VMEM((BLK, KV_DIM), _bf16), pltpu.VMEM((BLK, KV_DIM), _bf16),
            pltpu.VMEM((TM + SUBLANES, CONV_CH), _f32),
            pltpu.VMEM((TM, ATTN_DIM), _f32),
            pltpu.VMEM((1, LANES), _f32),
        ],
    )
    return pl.pallas_call(
        _mixer_kernel,
        grid_spec=grid_spec,
        out_shape=(jax.ShapeDtypeStruct((n_tok * ROW_CHUNKS, LANES), _f32),
                   jax.ShapeDtypeStruct((n_tok, PACKED_D), jnp.int32),
                   jax.ShapeDtypeStruct((SUBLANES, n_tok), _f32),
                   jax.ShapeDtypeStruct((1, LANES), _f32),
                   jax.ShapeDtypeStruct((n_tok // TM * SUBLANES, LANES), _f32)),
        compiler_params=pltpu.CompilerParams(dimension_semantics=("arbitrary", "arbitrary"),
                                             vmem_limit_bytes=VMEM_LIMIT),
        name="mixer",
    )(sinks, x_rows, *consts)


def _plan(route_t, cnt, cntb, n_tok):
    n_tiles = (n_tok * 2) // TME + N_EXPERTS + 1
    counts = cnt[0, :N_EXPERTS].astype(jnp.int32)
    ntile = (counts + TME - 1) // TME
    tile_end = jnp.cumsum(ntile)
    tile_start = tile_end - ntile
    eid = route_t[0:2].astype(jnp.int32)
    rank = route_t[2:4].astype(jnp.int32)
    gates = route_t[4:6]
    experts = jnp.arange(N_EXPERTS, dtype=jnp.int32)
    start_of = jnp.sum(jnp.where(eid[None] == experts[:, None, None], tile_start[:, None, None], 0), axis=0)
    pos2 = start_of * TME + rank
    tiles = jnp.arange(n_tiles, dtype=jnp.int32)
    n_used = tile_end[-1]
    t_exp = jnp.sum((jnp.minimum(tiles, n_used - 1)[:, None] >= tile_end[None, :]).astype(jnp.int32), axis=-1)
    t_exp = jnp.minimum(t_exp, N_EXPERTS - 1)
    own = t_exp[:, None] == experts
    t_rows = jnp.sum(jnp.where(own, counts - (tiles[:, None] - tile_start) * TME, 0), axis=-1)
    t_rows = jnp.where(tiles < n_used, jnp.clip(t_rows, 0, TME), 0).astype(jnp.int32)

    n_mix = n_tok // TMC
    before = cntb[::SUBLANES, :N_EXPERTS].astype(jnp.int32)
    held = jnp.concatenate([before[1:], counts[None]], axis=0) - before
    first = tile_start[None, :] * TME + before
    nw = (held + WIN - 1) // WIN
    slot_end = jnp.cumsum(nw, axis=1)
    slot_beg = slot_end - nw
    nwin = slot_end[:, -1]
    slots = jnp.arange(MAX_WIN, dtype=jnp.int32)
    owner = jnp.minimum(jnp.sum((slots[None, :, None] >= slot_end[:, None, :]).astype(jnp.int32), axis=-1),
                        N_EXPERTS - 1)
    is_owner = owner[:, :, None] == experts
    wstart = jnp.sum(jnp.where(is_owner, first[:, None, :] + (slots[None, :, None] - slot_beg[:, None, :]) * WIN,
                               0), axis=-1)
    wstart = jnp.where(slots[None, :] < nwin[:, None], wstart, 0).astype(jnp.int32).reshape(-1)
    mine = eid.reshape(2, n_mix, TMC, 1) == experts
    beg_tok = jnp.sum(jnp.where(mine, slot_beg[None, :, None, :], 0), axis=-1)
    before_tok = jnp.sum(jnp.where(mine, before[None, :, None, :], 0), axis=-1)
    widx = (beg_tok * WIN + rank.reshape(2, n_mix, TMC) - before_tok).reshape(2, n_tok).astype(jnp.int32)
    widx = widx * ROW_CHUNKS
    return pos2, t_exp, t_rows, n_tiles, (nwin.astype(jnp.int32), wstart, widx[0], widx[1], gates[0], gates[1])


def _experts_call(t_exp, t_rows, xs, w_gate, w_up, w_down, n_tiles):
    return pl.pallas_call(
        _experts_kernel,
        grid_spec=pltpu.PrefetchScalarGridSpec(
            num_scalar_prefetch=2,
            grid=(n_tiles,),
            in_specs=[
                pl.BlockSpec(memory_space=pl.ANY),
                pl.BlockSpec((None, D_MODEL, D_EXPERT), lambda i, te, tn: (te[i], 0, 0)),
                pl.BlockSpec((None, D_MODEL, D_EXPERT), lambda i, te, tn: (te[i], 0, 0)),
                pl.BlockSpec((None, D_EXPERT, D_MODEL), lambda i, te, tn: (te[i], 0, 0)),
            ],
            out_specs=pl.BlockSpec((TME * ROW_CHUNKS, LANES), lambda i, *_: (i, 0)),
            scratch_shapes=[
                pltpu.VMEM((XS_SLOTS, TME, PACKED_D), jnp.int32),
                pltpu.SemaphoreType.DMA((XS_SLOTS,)),
                pltpu.VMEM((D_MODEL, D_EXPERT), _bf16),
                pltpu.VMEM((D_MODEL, D_EXPERT), _bf16),
                pltpu.VMEM((D_EXPERT, D_MODEL), _bf16),
            ],
        ),
        out_shape=jax.ShapeDtypeStruct((n_tiles * TME * ROW_CHUNKS, LANES), _f32),
        compiler_params=pltpu.CompilerParams(dimension_semantics=("arbitrary",),
                                             vmem_limit_bytes=VMEM_LIMIT),
        name="experts",
    )(t_exp, t_rows, xs, w_gate, w_up, w_down)


def _combine_call(tables, y_sorted, h2):
    n_tok = h2.shape[0] // ROW_CHUNKS
    return pl.pallas_call(
        _combine_kernel,
        grid_spec=pltpu.PrefetchScalarGridSpec(
            num_scalar_prefetch=len(tables),
            grid=(n_tok // TMC,),
            in_specs=[
                pl.BlockSpec(memory_space=pl.ANY),
                pl.BlockSpec((TMC * ROW_CHUNKS, LANES), lambda i, *_: (i, 0)),
            ],
            out_specs=pl.BlockSpec((TMC, D_MODEL), lambda i, *_: (i, 0)),
            scratch_shapes=[
                pltpu.VMEM((2, MAX_WIN * WIN * ROW_CHUNKS, LANES), _f32),
                pltpu.SemaphoreType.DMA((2,)),
                pltpu.VMEM((TMC * ROW_CHUNKS, LANES), _f32),
            ],
        ),
        out_shape=jax.ShapeDtypeStruct((n_tok, D_MODEL), _f32),
        compiler_params=pltpu.CompilerParams(dimension_semantics=("arbitrary",),
                                             vmem_limit_bytes=VMEM_LIMIT),
        name="combine",
    )(*tables, y_sorted, h2)
```

```python
import functools

import numpy as np
import jax
import jax.numpy as jnp
from jax import lax
from jax.experimental import pallas as pl
from jax.experimental.pallas import tpu as pltpu
from jax.experimental.pallas import tpu_sc as plsc

D_MODEL = 1024
N_META = 16
N_Q_HEADS = 8
N_KV_HEADS = 2
HEAD_DIM = 64
Q_GROUP = N_Q_HEADS // N_KV_HEADS
ATTN_DIM = N_Q_HEADS * HEAD_DIM
KV_DIM = N_KV_HEADS * HEAD_DIM
BLK = 128
PAD = BLK - N_META
N_BUCKETS = 32
MAX_DISTANCE = 128
CONV_CH = D_MODEL // 2
IN_PROJ = ATTN_DIM + 2 * KV_DIM + 3 * CONV_CH
N_GROUPS = 4
EXPERTS_PER_GROUP = 8
N_EXPERTS = N_GROUPS * EXPERTS_PER_GROUP
D_EXPERT = D_MODEL // 2
EPS = 1e-6
NEG_INF = -1e30
LOG2E = float(np.log2(np.e))

LANES = 128
SUBLANES = 8
ROW_CHUNKS = D_MODEL // LANES
TM = 512
TME = 512
TMC = TM
WIN = 16
MAX_WIN = N_EXPERTS + 2 * TMC // WIN
CMB_UNROLL = 128
XS_SLOTS = 3
PACKED_D = D_MODEL // 2
SC_WINDOW = 64
VMEM_LIMIT = 56 * 1024 * 1024

Q_OFF, K_OFF = 0, ATTN_DIM
CB_OFF = ATTN_DIM + 2 * KV_DIM
CC_OFF = CB_OFF + CONV_CH
CH_OFF = CC_OFF + CONV_CH

_f32 = jnp.float32
_bf16 = jnp.bfloat16


def _rms(x, g):
    return x * lax.rsqrt(jnp.mean(x * x, axis=-1, keepdims=True) + EPS) * g


def _dot(a, b):
    return jnp.dot(a, b, preferred_element_type=_f32)


def _dup_halves(x):
    lane = lax.broadcasted_iota(jnp.int32, x.shape, 1)
    sw = pltpu.roll(x, HEAD_DIM, axis=1)
    lo = lane < HEAD_DIM
    return jnp.where(lo, x, sw).astype(_bf16), jnp.where(lo, sw, x).astype(_bf16)


def _kv_state(hn_bf, win_ref, kg_ref, bdk_ref):
    kv = _dot(hn_bf, win_ref[:, K_OFF:K_OFF + 2 * KV_DIM])
    k = kv[:, :KV_DIM]
    v = kv[:, KV_DIM:]
    ssk = _dot((k * k).astype(_bf16), bdk_ref[...])
    kn = k * lax.rsqrt(ssk + EPS) * kg_ref[...]
    return _dup_halves(kn) + _dup_halves(v)


def _prep_kernel(rb_ref, meta_ref, mixg_ref, win_ref, kg_ref, bdk_ref, bucket_ref,
                 k0_ref, k1_ref, v0_ref, v1_ref, ut_ref, bias_ref):
    hn = _rms(meta_ref[...], mixg_ref[...]).astype(_bf16)
    k0, k1, v0, v1 = _kv_state(hn, win_ref, kg_ref, bdk_ref)
    k0_ref[...] = k0
    k1_ref[...] = k1
    v0_ref[...] = v0
    v1_ref[...] = v1
    cch = _dot(hn, win_ref[:, CC_OFF:CC_OFF + 2 * CONV_CH])
    u = cch[:, :CONV_CH] * cch[:, CONV_CH:]
    ut_ref[...] = u[BLK - SUBLANES:, :]
    for f in range(2):
        bk = bucket_ref[f]
        for h in range(N_Q_HEADS):
            acc = jnp.full((BLK, 2 * BLK), NEG_INF, _f32)
            for b in range(N_BUCKETS):
                acc = jnp.where(bk == b, rb_ref[b, h] * LOG2E, acc)
            bias_ref[f, h // Q_GROUP, (h % Q_GROUP) * BLK:(h % Q_GROUP + 1) * BLK, :] = acc


def _mixer_kernel(sinks_ref,
                  x_ref, mixg_ref, win_ref, qg_ref, kg_ref, bias_ref, convw_ref, ag_ref, cg_ref,
                  wout_ref, fg_ref, wr_ref, br_ref, k0m_ref, k1m_ref, v0m_ref, v1m_ref, utm_ref,
                  bdq_ref, bdk_ref, tri_ref,
                  h2_ref, hn2_ref, route_t_ref, cnt_ref, cntb_ref,
                  kp0, kp1, vp0, vp1, ubuf, a_scr, cnt_acc):
    b = pl.program_id(0)
    t = pl.program_id(1)

    @pl.when(t == 0)
    def _():
        kp0[...] = k0m_ref[...]
        kp1[...] = k1m_ref[...]
        vp0[...] = v0m_ref[...]
        vp1[...] = v1m_ref[...]
        ubuf[0:SUBLANES, :] = utm_ref[...]

    @pl.when(jnp.logical_and(b == 0, t == 0))
    def _():
        cnt_acc[...] = jnp.zeros_like(cnt_acc)

    x = x_ref[...]
    hn = _rms(x, mixg_ref[...]).astype(_bf16)

    q = _dot(hn, win_ref[:, Q_OFF:Q_OFF + ATTN_DIM])
    ssq = _dot((q * q).astype(_bf16), bdq_ref[...])
    qn = (q * lax.rsqrt(ssq + EPS) * qg_ref[...]).astype(_bf16)
    kd0, kd1, vd0, vd1 = _kv_state(hn, win_ref, kg_ref, bdk_ref)
    kd = (kd0, kd1)
    vd = (vd0, vd1)
    kp = (kp0, kp1)
    vp = (vp0, vp1)

    lane_q = lax.broadcasted_iota(jnp.int32, (BLK, LANES), 1)
    lo_half = lane_q < HEAD_DIM
    row4 = lax.broadcasted_iota(jnp.int32, (Q_GROUP * BLK, 1), 0) // BLK
    first = jnp.where(t == 0, 0, 1)
    zero_bf = jnp.zeros((BLK, LANES), _bf16)

    for j in range(TM // BLK):
        rows = slice(j * BLK, (j + 1) * BLK)
        for g in range(N_KV_HEADS):
            if j == 0:
                kcat = jnp.concatenate([kp[g][...], kd[g][rows]], axis=0)
                vcat = jnp.concatenate([vp[g][...], vd[g][rows]], axis=0)
                bias = bias_ref[first, g]
            else:
                kcat = kd[g][(j - 1) * BLK:(j + 1) * BLK]
                vcat = vd[g][(j - 1) * BLK:(j + 1) * BLK]
                bias = bias_ref[1, g]
            qs = []
            for hh in range(Q_GROUP):
                h = g * Q_GROUP + hh
                qc = qn[rows, (h // 2) * LANES:(h // 2 + 1) * LANES]
                keep = lo_half if h % 2 == 0 else jnp.logical_not(lo_half)
                qs.append(jnp.where(keep, qc, zero_bf))
            q4 = jnp.concatenate(qs, axis=0)
            s = lax.dot_general(q4, kcat, (((1,), (1,)), ((), ())),
                                preferred_element_type=_f32) + bias
            sink = jnp.full((Q_GROUP * BLK, 1), sinks_ref[g * Q_GROUP], _f32)
            for hh in range(1, Q_GROUP):
                sink = jnp.where(row4 == hh, sinks_ref[g * Q_GROUP + hh], sink)
            m = jnp.maximum(jnp.max(s, axis=-1, keepdims=True), sink)
            p = jnp.exp2(s - m)
            l = jnp.sum(p, axis=-1, keepdims=True) + jnp.exp2(sink - m)
            o = _dot(p.astype(_bf16), vcat) / l
            for pp in range(Q_GROUP // 2):
                ev = o[(2 * pp) * BLK:(2 * pp + 1) * BLK]
                od = o[(2 * pp + 1) * BLK:(2 * pp + 2) * BLK]
                col = g * (Q_GROUP // 2) + pp
                a_scr[rows, col * LANES:(col + 1) * LANES] = jnp.where(lo_half, ev, od)

    last = slice(TM - BLK, TM)
    kp0[...] = kd0[last]
    kp1[...] = kd1[last]
    vp0[...] = vd0[last]
    vp1[...] = vd1[last]

    cb = _dot(hn, win_ref[:, CB_OFF:CB_OFF + CONV_CH])
    cch = _dot(hn, win_ref[:, CC_OFF:CC_OFF + 2 * CONV_CH])
    u = cch[:, :CONV_CH] * cch[:, CONV_CH:]
    ubuf[SUBLANES:, :] = u
    u1 = ubuf[SUBLANES - 1:SUBLANES - 1 + TM, :]
    u2 = ubuf[SUBLANES - 2:SUBLANES - 2 + TM, :]
    c = cb * (convw_ref[0:1, :] * u2 + convw_ref[1:2, :] * u1 + convw_ref[2:3, :] * u)
    ubuf[0:SUBLANES, :] = u[TM - SUBLANES:, :]

    an = _rms(a_scr[...], ag_ref[...]).astype(_bf16)
    cn = _rms(c, cg_ref[...]).astype(_bf16)
    h2 = x + _dot(an, wout_ref[0:ATTN_DIM, :]) + _dot(cn, wout_ref[ATTN_DIM:, :])
    for cc in range(ROW_CHUNKS):
        h2_ref[pl.ds(cc, TM, stride=ROW_CHUNKS), :] = h2[:, cc * LANES:(cc + 1) * LANES]

    hn2 = _rms(h2, fg_ref[...])
    hn2_ref[...] = pltpu.pack_elementwise([hn2[:, :PACKED_D], hn2[:, PACKED_D:]], packed_dtype=_bf16)

    r1 = _dot(hn2.astype(_bf16), wr_ref[...])
    lg = r1 + pltpu.roll(r1, HEAD_DIM, axis=1) + br_ref[...]
    lane = lax.broadcasted_iota(jnp.int32, (TM, LANES), 1)
    lanef = lane.astype(_f32)
    ninf = jnp.float32(-jnp.inf)
    big = jnp.float32(LANES)

    gl = jnp.where(lane < N_GROUPS, lg, ninf)
    gmax = jnp.max(gl, axis=-1, keepdims=True)
    gsum = jnp.sum(jnp.exp(gl - gmax), axis=-1, keepdims=True)
    g_p = 1.0 / gsum
    g_idx = jnp.min(jnp.where(gl == gmax, lanef, big), axis=-1, keepdims=True)
    e_lo = N_GROUPS + EXPERTS_PER_GROUP * g_idx
    el = jnp.where(jnp.logical_and(lanef >= e_lo, lanef < e_lo + EXPERTS_PER_GROUP), lg, ninf)
    m1 = jnp.max(el, axis=-1, keepdims=True)
    i1 = jnp.min(jnp.where(el == m1, lanef, big), axis=-1, keepdims=True)
    el2 = jnp.where(lanef == i1, ninf, el)
    m2 = jnp.max(el2, axis=-1, keepdims=True)
    i2 = jnp.min(jnp.where(el2 == m2, lanef, big), axis=-1, keepdims=True)
    ex = jnp.exp(m2 - m1)
    den = 1.0 / (1.0 + ex)
    gate0 = g_p * den
    gate1 = g_p * ex * den
    e0 = i1 - N_GROUPS
    e1 = i2 - N_GROUPS

    oh0 = lanef == e0
    oh1 = lanef == e1
    cmat = (jnp.where(oh0, 1.0, 0.0) + jnp.where(oh1, 1.0, 0.0))
    prefix = _dot(tri_ref[...], cmat.astype(_bf16)) + cnt_acc[...]
    rank0 = jnp.sum(jnp.where(oh0, prefix, 0.0), axis=-1, keepdims=True)
    rank1 = jnp.sum(jnp.where(oh1, prefix, 0.0), axis=-1, keepdims=True)
    cntb_ref[...] = jnp.broadcast_to(cnt_acc[...], (SUBLANES, LANES))
    cnt_new = cnt_acc[...] + jnp.sum(cmat, axis=0, keepdims=True)
    cnt_acc[...] = cnt_new
    cnt_ref[...] = cnt_new

    lane_r = lax.broadcasted_iota(jnp.int32, (TM, LANES), 1)
    wide = jnp.zeros((TM, LANES), _f32)
    for idx, val in enumerate((e0, e1, rank0, rank1, gate0, gate1)):
        wide = jnp.where(lane_r == idx, val, wide)
    route_t_ref[...] = wide.T[:SUBLANES, :]


def _dispatch_rows(hn2_rows, pos0, pos1, n_rows):
    n_tok = hn2_rows.shape[0]
    info = plsc.get_sparse_core_info()
    n_workers = info.num_cores * info.num_subcores
    per_worker = n_tok // n_workers
    n_chunks = per_worker // SC_WINDOW
    assert per_worker * n_workers == n_tok and n_chunks * SC_WINDOW == per_worker and n_chunks % 2 == 0
    mesh = plsc.VectorSubcoreMesh(core_axis_name="core", subcore_axis_name="subcore")

    @functools.partial(
        pl.kernel,
        out_type=jax.ShapeDtypeStruct((n_rows, PACKED_D), jnp.int32),
        mesh=mesh,
        scratch_types=[pltpu.VMEM((SC_WINDOW,), jnp.int32), pltpu.VMEM((SC_WINDOW,), jnp.int32),
                       pltpu.VMEM((SC_WINDOW,), jnp.int32), pltpu.VMEM((SC_WINDOW,), jnp.int32),
                       pltpu.VMEM((SC_WINDOW, PACKED_D), jnp.int32), pltpu.VMEM((SC_WINDOW, PACKED_D), jnp.int32),
                       pltpu.SemaphoreType.DMA((2,)), pltpu.SemaphoreType.DMA((2,))],
        compiler_params=pltpu.CompilerParams(use_tc_tiling_on_sc=True),
        name="dispatch",
    )
    def dispatch(x_hbm, i0_hbm, i1_hbm, o_hbm, i0_a, i0_b, i1_a, i1_b, rows_a, rows_b, sem_ld, sem_st):
        wid = lax.axis_index("subcore") * info.num_cores + lax.axis_index("core")
        base = wid * per_worker
        i0_v, i1_v, rows_v = (i0_a, i0_b), (i1_a, i1_b), (rows_a, rows_b)

        def loads(chunk, b):
            off = pl.multiple_of(base + chunk * SC_WINDOW, SC_WINDOW)
            return (pltpu.make_async_copy(i0_hbm.at[pl.ds(off, SC_WINDOW)], i0_v[b], sem_ld.at[b]),
                    pltpu.make_async_copy(i1_hbm.at[pl.ds(off, SC_WINDOW)], i1_v[b], sem_ld.at[b]),
                    pltpu.make_async_copy(x_hbm.at[pl.ds(off, SC_WINDOW)], rows_v[b], sem_ld.at[b]))

        def stores(b):
            return (pltpu.make_async_copy(rows_v[b], o_hbm.at[i0_v[b]], sem_st.at[b]),
                    pltpu.make_async_copy(rows_v[b], o_hbm.at[i1_v[b]], sem_st.at[b]))

        for d in loads(0, 0):
            d.start()

        @pl.loop(0, n_chunks, step=2)
        def _(c):
            for b in range(2):
                chunk = c + b
                for d in loads(chunk, b):
                    d.wait()
                for d in stores(b):
                    d.start()

                @pl.when(chunk >= 1)
                def _():
                    for d in stores(1 - b):
                        d.wait()

                @pl.when(chunk + 1 < n_chunks)
                def _():
                    for d in loads(chunk + 1, 1 - b):
                        d.start()

        for d in stores(1):
            d.wait()

    return dispatch(hn2_rows, pos0, pos1)


def _experts_kernel(te_ref, tn_ref,
                    xs_hbm, wg_ref, wu_ref, wd_ref,
                    y_ref,
                    xbuf, sem, wgb, wub, wdb):
    i = pl.program_id(0)
    n = pl.num_programs(0)

    def fetch(tile):
        sl = tile % XS_SLOTS
        return pltpu.make_async_copy(xs_hbm.at[pl.ds(pl.multiple_of(tile * TME, TME), TME), :],
                                     xbuf.at[sl], sem.at[sl])

    @pl.when(i == 0)
    def _():
        for tile in range(XS_SLOTS - 1):
            fetch(tile).start()

    @pl.when(i + XS_SLOTS - 1 < n)
    def _():
        fetch(i + XS_SLOTS - 1).start()

    valid = tn_ref[i] > 0
    changed = jnp.logical_or(i == 0, te_ref[i] != te_ref[jnp.maximum(i - 1, 0)])

    @pl.when(jnp.logical_and(changed, valid))
    def _():
        wgb[...] = wg_ref[...].astype(_bf16)
        wub[...] = wu_ref[...].astype(_bf16)
        wdb[...] = wd_ref[...].astype(_bf16)

    fetch(i).wait()

    @pl.when(valid)
    def _():
        packed = xbuf[i % XS_SLOTS]
        halves = [pltpu.unpack_elementwise(packed, index=k, packed_dtype=_bf16, unpacked_dtype=_f32)
                  for k in range(2)]
        xb = jnp.concatenate(halves, axis=1).astype(_bf16)
        live = lax.broadcasted_iota(jnp.int32, (TME, 1), 0) < tn_ref[i]
        xb = jnp.where(live, xb, jnp.zeros_like(xb))
        gate = _dot(xb, wgb[...])
        up = _dot(xb, wub[...])
        act = (gate * jax.nn.sigmoid(gate) * up).astype(_bf16)
        y = _dot(act, wdb[...])
        for cc in range(ROW_CHUNKS):
            y_ref[pl.ds(cc, TME, stride=ROW_CHUNKS), :] = y[:, cc * LANES:(cc + 1) * LANES]

    @pl.when(jnp.logical_not(valid))
    def _():
        y_ref[...] = jnp.zeros_like(y_ref)


def _combine_kernel(nwin_ref, wstart_ref, widx0_ref, widx1_ref, gate0_ref, gate1_ref,
                    y_hbm, h2_ref,
                    out_ref,
                    wbuf, sem, otile):
    i = pl.program_id(0)
    n = pl.num_programs(0)
    slot = i % 2
    win_rows = WIN * ROW_CHUNKS

    def window(tile, sl, s):
        src = pl.multiple_of(wstart_ref[tile * MAX_WIN + s] * ROW_CHUNKS, ROW_CHUNKS)
        return pltpu.make_async_copy(y_hbm.at[pl.ds(src, win_rows), :],
                                     wbuf.at[sl, pl.ds(pl.multiple_of(s * win_rows, win_rows), win_rows), :],
                                     sem.at[sl])

    def issue(tile, sl):
        def body(s, carry):
            window(tile, sl, s).start()
            return carry

        lax.fori_loop(0, nwin_ref[tile], body, 0)

    @pl.when(i == 0)
    def _():
        issue(0, 0)

    @pl.when(i + 1 < n)
    def _():
        issue(i + 1, 1 - slot)

    def wait_body(s, carry):
        window(i, slot, s).wait()
        return carry

    lax.fori_loop(0, nwin_ref[i], wait_body, 0)

    def token_body(it, carry):
        for j in range(CMB_UNROLL):
            r = it * CMB_UNROLL + j
            tok = i * TMC + r
            y0 = wbuf[slot, pl.ds(pl.multiple_of(widx0_ref[tok], ROW_CHUNKS), ROW_CHUNKS), :]
            y1 = wbuf[slot, pl.ds(pl.multiple_of(widx1_ref[tok], ROW_CHUNKS), ROW_CHUNKS), :]
            row = pl.ds(pl.multiple_of(r * ROW_CHUNKS, ROW_CHUNKS), ROW_CHUNKS)
            otile[row, :] = h2_ref[row, :] + (gate0_ref[tok] * y0 + gate1_ref[tok] * y1)
        return carry

    lax.fori_loop(0, TMC // CMB_UNROLL, token_body, 0)
    for cc in range(ROW_CHUNKS):
        out_ref[:, cc * LANES:(cc + 1) * LANES] = otile[pl.ds(cc, TMC, stride=ROW_CHUNKS), :]


def _t5_bucket(n):
    max_exact = N_BUCKETS // 2
    nf = jnp.maximum(n, 1).astype(_f32)
    large = max_exact + (jnp.log(nf / max_exact) / np.log(MAX_DISTANCE / max_exact)
                         * (N_BUCKETS - max_exact)).astype(jnp.int32)
    large = jnp.minimum(large, N_BUCKETS - 1)
    return jnp.where(n < max_exact, n, large)


def _bucket_maps():
    qi = jnp.arange(BLK)[:, None]
    sj = jnp.arange(2 * BLK)[None, :]
    dist = BLK + qi - sj
    band = (dist >= 0) & (dist < BLK)
    bucket = _t5_bucket(jnp.maximum(dist, 0))
    generic = jnp.where(band, bucket, -1)
    first = jnp.where(band & (sj >= PAD), bucket, -1)
    return jnp.stack([first, generic]).astype(jnp.int32)


def _const(shape):
    nd = len(shape)
    return pl.BlockSpec(shape, lambda *_: (0,) * nd)


def _block_diag_mean(n):
    idx = np.arange(n) // HEAD_DIM
    return jnp.asarray((idx[:, None] == idx[None, :]).astype(np.float32) / HEAD_DIM, dtype=_bf16)


def kernel(x, meta_tokens, rel_bias, mix_norm_g, w_in, q_norm_g, k_norm_g, attn_sinks, conv_w, attn_out_norm_g, conv_out_norm_g, w_out, ffn_norm_g, w_group_router, b_group_router, w_expert_router, b_expert_router, w_gate, w_up, w_down):
    bsz, seq, _ = x.shape
    assert seq % TM == 0 and (bsz * seq) % TMC == 0
    n_tok = bsz * seq
    nt = seq // TM

    win = w_in[0].astype(_bf16)
    wout = w_out[0].astype(_bf16)
    mixg = mix_norm_g[0].reshape(1, D_MODEL)
    fg = ffn_norm_g[0].reshape(1, D_MODEL)
    qg = (jnp.tile(q_norm_g[0], N_Q_HEADS) * (HEAD_DIM ** -0.5 * LOG2E)).reshape(1, ATTN_DIM)
    kg = jnp.tile(k_norm_g[0], N_KV_HEADS).reshape(1, KV_DIM)
    ag = attn_out_norm_g[0].reshape(1, ATTN_DIM)
    cg = conv_out_norm_g[0].reshape(1, CONV_CH)
    convw = conv_w[0]
    sinks = attn_sinks[0] * LOG2E
    w_r = jnp.concatenate([w_group_router[0], w_expert_router[0].reshape(D_MODEL, N_EXPERTS)], axis=1)
    n_r = N_GROUPS + N_EXPERTS
    w_r_hi = w_r.astype(_bf16)
    w_r_lo = (w_r - w_r_hi.astype(_f32)).astype(_bf16)
    wr = jnp.zeros((D_MODEL, LANES), _bf16)
    wr = wr.at[:, :n_r].set(w_r_hi).at[:, HEAD_DIM:HEAD_DIM + n_r].set(w_r_lo)
    br = jnp.zeros((1, LANES), _f32).at[0, :n_r].set(
        jnp.concatenate([b_group_router[0], b_expert_router[0].reshape(N_EXPERTS)]))
    meta_blk = jnp.concatenate([jnp.zeros((PAD, D_MODEL), x.dtype), meta_tokens.astype(x.dtype)], axis=0)
    bdq = _block_diag_mean(ATTN_DIM)
    bdk = _block_diag_mean(KV_DIM)
    tri = jnp.asarray(np.tril(np.ones((TM, TM), np.float32), -1), dtype=_bf16)

    kv_sd = jax.ShapeDtypeStruct((BLK, KV_DIM), _bf16)
    k0m, k1m, v0m, v1m, utm, bias = pl.pallas_call(
        _prep_kernel,
        out_shape=(kv_sd, kv_sd, kv_sd, kv_sd,
                   jax.ShapeDtypeStruct((SUBLANES, CONV_CH), _f32),
                   jax.ShapeDtypeStruct((2, N_KV_HEADS, Q_GROUP * BLK, 2 * BLK), _f32)),
        in_specs=[pl.BlockSpec(memory_space=pltpu.SMEM)] + [pl.BlockSpec(memory_space=pltpu.VMEM)] * 6,
        out_specs=tuple(pl.BlockSpec(memory_space=pltpu.VMEM) for _ in range(6)),
        compiler_params=pltpu.CompilerParams(vmem_limit_bytes=VMEM_LIMIT),
        name="prep",
    )(rel_bias, meta_blk, mixg, win, kg, bdk, _bucket_maps())

    consts = (mixg, win, qg, kg, bias, convw, ag, cg, wout, fg, wr, br, k0m, k1m, v0m, v1m, utm, bdq, bdk, tri)
    h2, hn2, route_t, cnt, cntb = _mixer_call(sinks, x.reshape(n_tok, D_MODEL), consts, nt)
    pos2, t_exp, t_rows, n_tiles, tables = _plan(route_t, cnt, cntb, n_tok)
    xs = _dispatch_rows(hn2, pos2[0], pos2[1], n_tiles * TME)
    y_sorted = _experts_call(t_exp, t_rows, xs, w_gate[0], w_up[0], w_down[0], n_tiles)
    out = _combine_call(tables, y_sorted, h2)
    return out.reshape(bsz, seq, D_MODEL)


def _mixer_call(sinks, x_rows, consts, nt):
    n_tok = x_rows.shape[0]
    tile_idx = lambda b, t, *_: (b * nt + t, 0)
    grid_spec = pltpu.PrefetchScalarGridSpec(
        num_scalar_prefetch=1,
        grid=(n_tok // (nt * TM), nt),
        in_specs=[
            pl.BlockSpec((TM, D_MODEL), tile_idx),
            _const((1, D_MODEL)), _const((D_MODEL, IN_PROJ)), _const((1, ATTN_DIM)), _const((1, KV_DIM)),
            _const((2, N_KV_HEADS, Q_GROUP * BLK, 2 * BLK)), _const((3, CONV_CH)),
            _const((1, ATTN_DIM)), _const((1, CONV_CH)), _const((D_MODEL, D_MODEL)), _const((1, D_MODEL)),
            _const((D_MODEL, LANES)), _const((1, LANES)),
            _const((BLK, KV_DIM)), _const((BLK, KV_DIM)), _const((BLK, KV_DIM)), _const((BLK, KV_DIM)),
            _const((SUBLANES, CONV_CH)),
            _const((ATTN_DIM, ATTN_DIM)), _const((KV_DIM, KV_DIM)), _const((TM, TM)),
        ],
        out_specs=[
            pl.BlockSpec((TM * ROW_CHUNKS, LANES), tile_idx),
            pl.BlockSpec((TM, PACKED_D), tile_idx),
            pl.BlockSpec((SUBLANES, TM), lambda b, t, *_: (0, b * nt + t)),
            _const((1, LANES)),
            pl.BlockSpec((SUBLANES, LANES), tile_idx),
        ],
        scratch_shapes=[
            pltpu.VMEM((BLK, KV_DIM), _bf16), pltpu.VMEM((BLK, KV_DIM), _bf16),
            pltpu.VMEM((BLK, KV_DIM), _bf16), pltpu.VMEM((BLK, KV_DIM), _bf16),
            pltpu.VMEM((TM + SUBLANES, CONV_CH), _f32),
            pltpu.VMEM((TM, ATTN_DIM), _f32),
            pltpu.VMEM((1, LANES), _f32),
        ],
    )
    return pl.pallas_call(
        _mixer_kernel,
        grid_spec=grid_spec,
        out_shape=(jax.ShapeDtypeStruct((n_tok * ROW_CHUNKS, LANES), _f32),
                   jax.ShapeDtypeStruct((n_tok, PACKED_D), jnp.int32),
                   jax.ShapeDtypeStruct((SUBLANES, n_tok), _f32),
                   jax.ShapeDtypeStruct((1, LANES), _f32),
                   jax.ShapeDtypeStruct((n_tok // TM * SUBLANES, LANES), _f32)),
        compiler_params=pltpu.CompilerParams(dimension_semantics=("arbitrary", "arbitrary"),
                                             vmem_limit_bytes=VMEM_LIMIT),
        name="mixer",
    )(sinks, x_rows, *consts)


def _plan(route_t, cnt, cntb, n_tok):
    n_tiles = (n_tok * 2) // TME + N_EXPERTS + 1
    counts = cnt[0, :N_EXPERTS].astype(jnp.int32)
    ntile = (counts + TME - 1) // TME
    tile_end = jnp.cumsum(ntile)
    tile_start = tile_end - ntile
    eid = route_t[0:2].astype(jnp.int32)
    rank = route_t[2:4].astype(jnp.int32)
    gates = route_t[4:6]
    experts = jnp.arange(N_EXPERTS, dtype=jnp.int32)
    start_of = jnp.sum(jnp.where(eid[None] == experts[:, None, None], tile_start[:, None, None], 0), axis=0)
    pos2 = start_of * TME + rank
    tiles = jnp.arange(n_tiles, dtype=jnp.int32)
    n_used = tile_end[-1]
    t_exp = jnp.sum((jnp.minimum(tiles, n_used - 1)[:, None] >= tile_end[None, :]).astype(jnp.int32), axis=-1)
    t_exp = jnp.minimum(t_exp, N_EXPERTS - 1)
    own = t_exp[:, None] == experts
    t_rows = jnp.sum(jnp.where(own, counts - (tiles[:, None] - tile_start) * TME, 0), axis=-1)
    t_rows = jnp.where(tiles < n_used, jnp.clip(t_rows, 0, TME), 0).astype(jnp.int32)

    n_mix = n_tok // TMC
    before = cntb[::SUBLANES, :N_EXPERTS].astype(jnp.int32)
    held = jnp.concatenate([before[1:], counts[None]], axis=0) - before
    first = tile_start[None, :] * TME + before
    nw = (held + WIN - 1) // WIN
    slot_end = jnp.cumsum(nw, axis=1)
    slot_beg = slot_end - nw
    nwin = slot_end[:, -1]
    slots = jnp.arange(MAX_WIN, dtype=jnp.int32)
    owner = jnp.minimum(jnp.sum((slots[None, :, None] >= slot_end[:, None, :]).astype(jnp.int32), axis=-1),
                        N_EXPERTS - 1)
    is_owner = owner[:, :, None] == experts
    wstart = jnp.sum(jnp.where(is_owner, first[:, None, :] + (slots[None, :, None] - slot_beg[:, None, :]) * WIN,
                               0), axis=-1)
    wstart = jnp.where(slots[None, :] < nwin[:, None], wstart, 0).astype(jnp.int32).reshape(-1)
    mine = eid.reshape(2, n_mix, TMC, 1) == experts
    beg_tok = jnp.sum(jnp.where(mine, slot_beg[None, :, None, :], 0), axis=-1)
    before_tok = jnp.sum(jnp.where(mine, before[None, :, None, :], 0), axis=-1)
    widx = (beg_tok * WIN + rank.reshape(2, n_mix, TMC) - before_tok).reshape(2, n_tok).astype(jnp.int32)
    widx = widx * ROW_CHUNKS
    return pos2, t_exp, t_rows, n_tiles, (nwin.astype(jnp.int32), wstart, widx[0], widx[1], gates[0], gates[1])


def _experts_call(t_exp, t_rows, xs, w_gate, w_up, w_down, n_tiles):
    return pl.pallas_call(
        _experts_kernel,
        grid_spec=pltpu.PrefetchScalarGridSpec(
            num_scalar_prefetch=2,
            grid=(n_tiles,),
            in_specs=[
                pl.BlockSpec(memory_space=pl.ANY),
                pl.BlockSpec((None, D_MODEL, D_EXPERT), lambda i, te, tn: (te[i], 0, 0)),
                pl.BlockSpec((None, D_MODEL, D_EXPERT), lambda i, te, tn: (te[i], 0, 0)),
                pl.BlockSpec((None, D_EXPERT, D_MODEL), lambda i, te, tn: (te[i], 0, 0)),
            ],
            out_specs=pl.BlockSpec((TME * ROW_CHUNKS, LANES), lambda i, *_: (i, 0)),
            scratch_shapes=[
                pltpu.VMEM((XS_SLOTS, TME, PACKED_D), jnp.int32),
                pltpu.SemaphoreType.DMA((XS_SLOTS,)),
                pltpu.VMEM((D_MODEL, D_EXPERT), _bf16),
                pltpu.VMEM((D_MODEL, D_EXPERT), _bf16),
                pltpu.VMEM((D_EXPERT, D_MODEL), _bf16),
            ],
        ),
        out_shape=jax.ShapeDtypeStruct((n_tiles * TME * ROW_CHUNKS, LANES), _f32),
        compiler_params=pltpu.CompilerParams(dimension_semantics=("arbitrary",),
                                             vmem_limit_bytes=VMEM_LIMIT),
        name="experts",
    )(t_exp, t_rows, xs, w_gate, w_up, w_down)


def _combine_call(tables, y_sorted, h2):
    n_tok = h2.shape[0] // ROW_CHUNKS
    return pl.pallas_call(
        _combine_kernel,
        grid_spec=pltpu.PrefetchScalarGridSpec(
            num_scalar_prefetch=len(tables),
            grid=(n_tok // TMC,),
            in_specs=[
                pl.BlockSpec(memory_space=pl.ANY),
                pl.BlockSpec((TMC * ROW_CHUNKS, LANES), lambda i, *_: (i, 0)),
            ],
            out_specs=pl.BlockSpec((TMC, D_MODEL), lambda i, *_: (i, 0)),
            scratch_shapes=[
                pltpu.VMEM((2, MAX_WIN * WIN * ROW_CHUNKS, LANES), _f32),
                pltpu.SemaphoreType.DMA((2,)),
                pltpu.VMEM((TMC * ROW_CHUNKS, LANES), _f32),
            ],
        ),
        out_shape=jax.ShapeDtypeStruct((n_tok, D_MODEL), _f32),
        compiler_params=pltpu.CompilerParams(dimension_semantics=("arbitrary",),
                                             vmem_limit_bytes=VMEM_LIMIT),
        name="combine",
    )(*tables, y_sorted, h2)
```

```python
import functools

import numpy as np
import jax
import jax.numpy as jnp
from jax import lax
from jax.experimental import pallas as pl
from jax.experimental.pallas import tpu as pltpu
from jax.experimental.pallas import tpu_sc as plsc

D_MODEL = 1024
N_META = 16
N_Q_HEADS = 8
N_KV_HEADS = 2
HEAD_DIM = 64
Q_GROUP = N_Q_HEADS // N_KV_HEADS
ATTN_DIM = N_Q_HEADS * HEAD_DIM
KV_DIM = N_KV_HEADS * HEAD_DIM
BLK = 128
PAD = BLK - N_META
N_BUCKETS = 32
MAX_DISTANCE = 128
CONV_CH = D_MODEL // 2
IN_PROJ = ATTN_DIM + 2 * KV_DIM + 3 * CONV_CH
N_GROUPS = 4
EXPERTS_PER_GROUP = 8
N_EXPERTS = N_GROUPS * EXPERTS_PER_GROUP
D_EXPERT = D_MODEL // 2
EPS = 1e-6
NEG_INF = -1e30
LOG2E = float(np.log2(np.e))

LANES = 128
SUBLANES = 8
ROW_CHUNKS = D_MODEL // LANES
TM = 512
TME = 512
TMC = TM
WIN = 16
MAX_WIN = N_EXPERTS + 2 * TMC // WIN
CMB_UNROLL = 128
WAIT_GROUP = 8
XS_SLOTS = 3
PACKED_D = D_MODEL // 2
SC_WINDOW = 64
VMEM_LIMIT = 56 * 1024 * 1024

Q_OFF, K_OFF = 0, ATTN_DIM
CB_OFF = ATTN_DIM + 2 * KV_DIM
CC_OFF = CB_OFF + CONV_CH
CH_OFF = CC_OFF + CONV_CH

_f32 = jnp.float32
_bf16 = jnp.bfloat16


def _rms(x, g):
    return x * lax.rsqrt(jnp.mean(x * x, axis=-1, keepdims=True) + EPS) * g


def _dot(a, b):
    return jnp.dot(a, b, preferred_element_type=_f32)


def _dup_halves(x):
    lane = lax.broadcasted_iota(jnp.int32, x.shape, 1)
    sw = pltpu.roll(x, HEAD_DIM, axis=1)
    lo = lane < HEAD_DIM
    return jnp.where(lo, x, sw).astype(_bf16), jnp.where(lo, sw, x).astype(_bf16)


def _kv_state(hn_bf, win_ref, kg_ref, bdk_ref):
    kv = _dot(hn_bf, win_ref[:, K_OFF:K_OFF + 2 * KV_DIM])
    k = kv[:, :KV_DIM]
    v = kv[:, KV_DIM:]
    ssk = _dot((k * k).astype(_bf16), bdk_ref[...])
    kn = k * lax.rsqrt(ssk + EPS) * kg_ref[...]
    return _dup_halves(kn) + _dup_halves(v)


def _prep_kernel(rb_ref, meta_ref, mixg_ref, win_ref, kg_ref, bdk_ref, bucket_ref,
                 k0_ref, k1_ref, v0_ref, v1_ref, ut_ref, bias_ref):
    hn = _rms(meta_ref[...], mixg_ref[...]).astype(_bf16)
    k0, k1, v0, v1 = _kv_state(hn, win_ref, kg_ref, bdk_ref)
    k0_ref[...] = k0
    k1_ref[...] = k1
    v0_ref[...] = v0
    v1_ref[...] = v1
    cch = _dot(hn, win_ref[:, CC_OFF:CC_OFF + 2 * CONV_CH])
    u = cch[:, :CONV_CH] * cch[:, CONV_CH:]
    ut_ref[...] = u[BLK - SUBLANES:, :]
    for f in range(2):
        bk = bucket_ref[f]
        for h in range(N_Q_HEADS):
            acc = jnp.full((BLK, 2 * BLK), NEG_INF, _f32)
            for b in range(N_BUCKETS):
                acc = jnp.where(bk == b, rb_ref[b, h] * LOG2E, acc)
            bias_ref[f, h // Q_GROUP, (h % Q_GROUP) * BLK:(h % Q_GROUP + 1) * BLK, :] = acc


def _mixer_kernel(sinks_ref,
                  x_ref, mixg_ref, win_ref, qg_ref, kg_ref, bias_ref, convw_ref, ag_ref, cg_ref,
                  wout_ref, fg_ref, wr_ref, br_ref, k0m_ref, k1m_ref, v0m_ref, v1m_ref, utm_ref,
                  bdq_ref, bdk_ref, tri_ref,
                  h2_ref, hn2_ref, route_t_ref, cnt_ref, cntb_ref,
                  kp0, kp1, vp0, vp1, ubuf, a_scr, cnt_acc):
    b = pl.program_id(0)
    t = pl.program_id(1)

    @pl.when(t == 0)
    def _():
        kp0[...] = k0m_ref[...]
        kp1[...] = k1m_ref[...]
        vp0[...] = v0m_ref[...]
        vp1[...] = v1m_ref[...]
        ubuf[0:SUBLANES, :] = utm_ref[...]

    @pl.when(jnp.logical_and(b == 0, t == 0))
    def _():
        cnt_acc[...] = jnp.zeros_like(cnt_acc)

    x = x_ref[...]
    hn = _rms(x, mixg_ref[...]).astype(_bf16)

    q = _dot(hn, win_ref[:, Q_OFF:Q_OFF + ATTN_DIM])
    ssq = _dot((q * q).astype(_bf16), bdq_ref[...])
    qn = (q * lax.rsqrt(ssq + EPS) * qg_ref[...]).astype(_bf16)
    kd0, kd1, vd0, vd1 = _kv_state(hn, win_ref, kg_ref, bdk_ref)
    kd = (kd0, kd1)
    vd = (vd0, vd1)
    kp = (kp0, kp1)
    vp = (vp0, vp1)

    lane_q = lax.broadcasted_iota(jnp.int32, (BLK, LANES), 1)
    lo_half = lane_q < HEAD_DIM
    row4 = lax.broadcasted_iota(jnp.int32, (Q_GROUP * BLK, 1), 0) // BLK
    first = jnp.where(t == 0, 0, 1)
    zero_bf = jnp.zeros((BLK, LANES), _bf16)

    for j in range(TM // BLK):
        rows = slice(j * BLK, (j + 1) * BLK)
        for g in range(N_KV_HEADS):
            if j == 0:
                kcat = jnp.concatenate([kp[g][...], kd[g][rows]], axis=0)
                vcat = jnp.concatenate([vp[g][...], vd[g][rows]], axis=0)
                bias = bias_ref[first, g]
            else:
                kcat = kd[g][(j - 1) * BLK:(j + 1) * BLK]
                vcat = vd[g][(j - 1) * BLK:(j + 1) * BLK]
                bias = bias_ref[1, g]
            qs = []
            for hh in range(Q_GROUP):
                h = g * Q_GROUP + hh
                qc = qn[rows, (h // 2) * LANES:(h // 2 + 1) * LANES]
                keep = lo_half if h % 2 == 0 else jnp.logical_not(lo_half)
                qs.append(jnp.where(keep, qc, zero_bf))
            q4 = jnp.concatenate(qs, axis=0)
            s = lax.dot_general(q4, kcat, (((1,), (1,)), ((), ())),
                                preferred_element_type=_f32) + bias
            sink = jnp.full((Q_GROUP * BLK, 1), sinks_ref[g * Q_GROUP], _f32)
            for hh in range(1, Q_GROUP):
                sink = jnp.where(row4 == hh, sinks_ref[g * Q_GROUP + hh], sink)
            m = jnp.maximum(jnp.max(s, axis=-1, keepdims=True), sink)
            p = jnp.exp2(s - m)
            l = jnp.sum(p, axis=-1, keepdims=True) + jnp.exp2(sink - m)
            o = _dot(p.astype(_bf16), vcat) / l
            for pp in range(Q_GROUP // 2):
                ev = o[(2 * pp) * BLK:(2 * pp + 1) * BLK]
                od = o[(2 * pp + 1) * BLK:(2 * pp + 2) * BLK]
                col = g * (Q_GROUP // 2) + pp
                a_scr[rows, col * LANES:(col + 1) * LANES] = jnp.where(lo_half, ev, od)

    last = slice(TM - BLK, TM)
    kp0[...] = kd0[last]
    kp1[...] = kd1[last]
    vp0[...] = vd0[last]
    vp1[...] = vd1[last]

    cb = _dot(hn, win_ref[:, CB_OFF:CB_OFF + CONV_CH])
    cch = _dot(hn, win_ref[:, CC_OFF:CC_OFF + 2 * CONV_CH])
    u = cch[:, :CONV_CH] * cch[:, CONV_CH:]
    ubuf[SUBLANES:, :] = u
    u1 = ubuf[SUBLANES - 1:SUBLANES - 1 + TM, :]
    u2 = ubuf[SUBLANES - 2:SUBLANES - 2 + TM, :]
    c = cb * (convw_ref[0:1, :] * u2 + convw_ref[1:2, :] * u1 + convw_ref[2:3, :] * u)
    ubuf[0:SUBLANES, :] = u[TM - SUBLANES:, :]

    an = _rms(a_scr[...], ag_ref[...]).astype(_bf16)
    cn = _rms(c, cg_ref[...]).astype(_bf16)
    h2 = x + _dot(an, wout_ref[0:ATTN_DIM, :]) + _dot(cn, wout_ref[ATTN_DIM:, :])
    for cc in range(ROW_CHUNKS):
        h2_ref[pl.ds(cc, TM, stride=ROW_CHUNKS), :] = h2[:, cc * LANES:(cc + 1) * LANES]

    hn2 = _rms(h2, fg_ref[...])
    hn2_ref[...] = pltpu.pack_elementwise([hn2[:, :PACKED_D], hn2[:, PACKED_D:]], packed_dtype=_bf16)

    r1 = _dot(hn2.astype(_bf16), wr_ref[...])
    lg = r1 + pltpu.roll(r1, HEAD_DIM, axis=1) + br_ref[...]
    lane = lax.broadcasted_iota(jnp.int32, (TM, LANES), 1)
    lanef = lane.astype(_f32)
    ninf = jnp.float32(-jnp.inf)
    big = jnp.float32(LANES)

    gl = jnp.where(lane < N_GROUPS, lg, ninf)
    gmax = jnp.max(gl, axis=-1, keepdims=True)
    gsum = jnp.sum(jnp.exp(gl - gmax), axis=-1, keepdims=True)
    g_p = 1.0 / gsum
    g_idx = jnp.min(jnp.where(gl == gmax, lanef, big), axis=-1, keepdims=True)
    e_lo = N_GROUPS + EXPERTS_PER_GROUP * g_idx
    el = jnp.where(jnp.logical_and(lanef >= e_lo, lanef < e_lo + EXPERTS_PER_GROUP), lg, ninf)
    m1 = jnp.max(el, axis=-1, keepdims=True)
    i1 = jnp.min(jnp.where(el == m1, lanef, big), axis=-1, keepdims=True)
    el2 = jnp.where(lanef == i1, ninf, el)
    m2 = jnp.max(el2, axis=-1, keepdims=True)
    i2 = jnp.min(jnp.where(el2 == m2, lanef, big), axis=-1, keepdims=True)
    ex = jnp.exp(m2 - m1)
    den = 1.0 / (1.0 + ex)
    gate0 = g_p * den
    gate1 = g_p * ex * den
    e0 = i1 - N_GROUPS
    e1 = i2 - N_GROUPS

    oh0 = lanef == e0
    oh1 = lanef == e1
    cmat = (jnp.where(oh0, 1.0, 0.0) + jnp.where(oh1, 1.0, 0.0))
    prefix = _dot(tri_ref[...], cmat.astype(_bf16)) + cnt_acc[...]
    rank0 = jnp.sum(jnp.where(oh0, prefix, 0.0), axis=-1, keepdims=True)
    rank1 = jnp.sum(jnp.where(oh1, prefix, 0.0), axis=-1, keepdims=True)
    cntb_ref[...] = jnp.broadcast_to(cnt_acc[...], (SUBLANES, LANES))
    cnt_new = cnt_acc[...] + jnp.sum(cmat, axis=0, keepdims=True)
    cnt_acc[...] = cnt_new
    cnt_ref[...] = cnt_new

    lane_r = lax.broadcasted_iota(jnp.int32, (TM, LANES), 1)
    wide = jnp.zeros((TM, LANES), _f32)
    for idx, val in enumerate((e0, e1, rank0, rank1, gate0, gate1)):
        wide = jnp.where(lane_r == idx, val, wide)
    route_t_ref[...] = wide.T[:SUBLANES, :]


def _dispatch_rows(hn2_rows, pos0, pos1, n_rows):
    n_tok = hn2_rows.shape[0]
    info = plsc.get_sparse_core_info()
    n_workers = info.num_cores * info.num_subcores
    per_worker = n_tok // n_workers
    n_chunks = per_worker // SC_WINDOW
    assert per_worker * n_workers == n_tok and n_chunks * SC_WINDOW == per_worker and n_chunks % 2 == 0
    mesh = plsc.VectorSubcoreMesh(core_axis_name="core", subcore_axis_name="subcore")

    @functools.partial(
        pl.kernel,
        out_type=jax.ShapeDtypeStruct((n_rows, PACKED_D), jnp.int32),
        mesh=mesh,
        scratch_types=[pltpu.VMEM((SC_WINDOW,), jnp.int32), pltpu.VMEM((SC_WINDOW,), jnp.int32),
                       pltpu.VMEM((SC_WINDOW,), jnp.int32), pltpu.VMEM((SC_WINDOW,), jnp.int32),
                       pltpu.VMEM((SC_WINDOW, PACKED_D), jnp.int32), pltpu.VMEM((SC_WINDOW, PACKED_D), jnp.int32),
                       pltpu.SemaphoreType.DMA((2,)), pltpu.SemaphoreType.DMA((2,))],
        compiler_params=pltpu.CompilerParams(use_tc_tiling_on_sc=True),
        name="dispatch",
    )
    def dispatch(x_hbm, i0_hbm, i1_hbm, o_hbm, i0_a, i0_b, i1_a, i1_b, rows_a, rows_b, sem_ld, sem_st):
        wid = lax.axis_index("subcore") * info.num_cores + lax.axis_index("core")
        base = wid * per_worker
        i0_v, i1_v, rows_v = (i0_a, i0_b), (i1_a, i1_b), (rows_a, rows_b)

        def loads(chunk, b):
            off = pl.multiple_of(base + chunk * SC_WINDOW, SC_WINDOW)
            return (pltpu.make_async_copy(i0_hbm.at[pl.ds(off, SC_WINDOW)], i0_v[b], sem_ld.at[b]),
                    pltpu.make_async_copy(i1_hbm.at[pl.ds(off, SC_WINDOW)], i1_v[b], sem_ld.at[b]),
                    pltpu.make_async_copy(x_hbm.at[pl.ds(off, SC_WINDOW)], rows_v[b], sem_ld.at[b]))

        def stores(b):
            return (pltpu.make_async_copy(rows_v[b], o_hbm.at[i0_v[b]], sem_st.at[b]),
                    pltpu.make_async_copy(rows_v[b], o_hbm.at[i1_v[b]], sem_st.at[b]))

        for d in loads(0, 0):
            d.start()

        @pl.loop(0, n_chunks, step=2)
        def _(c):
            for b in range(2):
                chunk = c + b
                for d in loads(chunk, b):
                    d.wait()
                for d in stores(b):
                    d.start()

                @pl.when(chunk >= 1)
                def _():
                    for d in stores(1 - b):
                        d.wait()

                @pl.when(chunk + 1 < n_chunks)
                def _():
                    for d in loads(chunk + 1, 1 - b):
                        d.start()

        for d in stores(1):
            d.wait()

    return dispatch(hn2_rows, pos0, pos1)


def _experts_kernel(te_ref, tn_ref,
                    xs_hbm, wg_ref, wu_ref, wd_ref,
                    y_ref,
                    xbuf, sem, wgb, wub, wdb):
    i = pl.program_id(0)
    n = pl.num_programs(0)

    def fetch(tile):
        sl = tile % XS_SLOTS
        return pltpu.make_async_copy(xs_hbm.at[pl.ds(pl.multiple_of(tile * TME, TME), TME), :],
                                     xbuf.at[sl], sem.at[sl])

    @pl.when(i == 0)
    def _():
        for tile in range(XS_SLOTS - 1):
            fetch(tile).start()

    @pl.when(i + XS_SLOTS - 1 < n)
    def _():
        fetch(i + XS_SLOTS - 1).start()

    valid = tn_ref[i] > 0
    changed = jnp.logical_or(i == 0, te_ref[i] != te_ref[jnp.maximum(i - 1, 0)])

    @pl.when(jnp.logical_and(changed, valid))
    def _():
        wgb[...] = wg_ref[...].astype(_bf16)
        wub[...] = wu_ref[...].astype(_bf16)
        wdb[...] = wd_ref[...].astype(_bf16)

    fetch(i).wait()

    @pl.when(valid)
    def _():
        packed = xbuf[i % XS_SLOTS]
        halves = [pltpu.unpack_elementwise(packed, index=k, packed_dtype=_bf16, unpacked_dtype=_f32)
                  for k in range(2)]
        xb = jnp.concatenate(halves, axis=1).astype(_bf16)
        live = lax.broadcasted_iota(jnp.int32, (TME, 1), 0) < tn_ref[i]
        xb = jnp.where(live, xb, jnp.zeros_like(xb))
        gate = _dot(xb, wgb[...])
        up = _dot(xb, wub[...])
        act = (gate * jax.nn.sigmoid(gate) * up).astype(_bf16)
        y = _dot(act, wdb[...])
        for cc in range(ROW_CHUNKS):
            y_ref[pl.ds(cc, TME, stride=ROW_CHUNKS), :] = y[:, cc * LANES:(cc + 1) * LANES]

    @pl.when(jnp.logical_not(valid))
    def _():
        y_ref[...] = jnp.zeros_like(y_ref)


def _combine_kernel(nwin_ref, wstart_ref, widx0_ref, widx1_ref, gate0_ref, gate1_ref,
                    y_hbm, h2_ref,
                    out_ref,
                    wbuf, sem, otile):
    i = pl.program_id(0)
    n = pl.num_programs(0)
    slot = i % 2
    win_rows = WIN * ROW_CHUNKS

    def window(tile, sl, s):
        src = pl.multiple_of(wstart_ref[tile * MAX_WIN + s] * ROW_CHUNKS, ROW_CHUNKS)
        return pltpu.make_async_copy(y_hbm.at[pl.ds(src, win_rows), :],
                                     wbuf.at[sl, pl.ds(pl.multiple_of(s * win_rows, win_rows), win_rows), :],
                                     sem.at[sl])

    def issue(tile, sl):
        def body(s, carry):
            window(tile, sl, s).start()
            return carry

        lax.fori_loop(0, nwin_ref[tile], body, 0)

    @pl.when(i == 0)
    def _():
        issue(0, 0)

    @pl.when(i + 1 < n)
    def _():
        issue(i + 1, 1 - slot)

    def wait_windows(k):
        rows = k * win_rows
        pltpu.make_async_copy(y_hbm.at[pl.ds(0, rows), :], wbuf.at[slot, pl.ds(0, rows), :], sem.at[slot]).wait()

    def wait_body(s, carry):
        wait_windows(WAIT_GROUP)
        return carry

    lax.fori_loop(0, nwin_ref[i] // WAIT_GROUP, wait_body, 0)
    k = WAIT_GROUP // 2
    while k >= 1:
        @pl.when((nwin_ref[i] & k) != 0)
        def _(k=k):
            wait_windows(k)
        k //= 2

    def token_body(it, carry):
        for j in range(CMB_UNROLL):
            r = it * CMB_UNROLL + j
            tok = i * TMC + r
            y0 = wbuf[slot, pl.ds(pl.multiple_of(widx0_ref[tok], ROW_CHUNKS), ROW_CHUNKS), :]
            y1 = wbuf[slot, pl.ds(pl.multiple_of(widx1_ref[tok], ROW_CHUNKS), ROW_CHUNKS), :]
            row = pl.ds(pl.multiple_of(r * ROW_CHUNKS, ROW_CHUNKS), ROW_CHUNKS)
            otile[row, :] = h2_ref[row, :] + (gate0_ref[tok] * y0 + gate1_ref[tok] * y1)
        return carry

    lax.fori_loop(0, TMC // CMB_UNROLL, token_body, 0)
    for cc in range(ROW_CHUNKS):
        out_ref[:, cc * LANES:(cc + 1) * LANES] = otile[pl.ds(cc, TMC, stride=ROW_CHUNKS), :]


def _t5_bucket(n):
    max_exact = N_BUCKETS // 2
    nf = jnp.maximum(n, 1).astype(_f32)
    large = max_exact + (jnp.log(nf / max_exact) / np.log(MAX_DISTANCE / max_exact)
                         * (N_BUCKETS - max_exact)).astype(jnp.int32)
    large = jnp.minimum(large, N_BUCKETS - 1)
    return jnp.where(n < max_exact, n, large)


def _bucket_maps():
    qi = jnp.arange(BLK)[:, None]
    sj = jnp.arange(2 * BLK)[None, :]
    dist = BLK + qi - sj
    band = (dist >= 0) & (dist < BLK)
    bucket = _t5_bucket(jnp.maximum(dist, 0))
    generic = jnp.where(band, bucket, -1)
    first = jnp.where(band & (sj >= PAD), bucket, -1)
    return jnp.stack([first, generic]).astype(jnp.int32)


def _const(shape):
    nd = len(shape)
    return pl.BlockSpec(shape, lambda *_: (0,) * nd)


def _block_diag_mean(n):
    idx = np.arange(n) // HEAD_DIM
    return jnp.asarray((idx[:, None] == idx[None, :]).astype(np.float32) / HEAD_DIM, dtype=_bf16)


def kernel(x, meta_tokens, rel_bias, mix_norm_g, w_in, q_norm_g, k_norm_g, attn_sinks, conv_w, attn_out_norm_g, conv_out_norm_g, w_out, ffn_norm_g, w_group_router, b_group_router, w_expert_router, b_expert_router, w_gate, w_up, w_down):
    bsz, seq, _ = x.shape
    assert seq % TM == 0 and (bsz * seq) % TMC == 0
    n_tok = bsz * seq
    nt = seq // TM

    win = w_in[0].astype(_bf16)
    wout = w_out[0].astype(_bf16)
    mixg = mix_norm_g[0].reshape(1, D_MODEL)
    fg = ffn_norm_g[0].reshape(1, D_MODEL)
    qg = (jnp.tile(q_norm_g[0], N_Q_HEADS) * (HEAD_DIM ** -0.5 * LOG2E)).reshape(1, ATTN_DIM)
    kg = jnp.tile(k_norm_g[0], N_KV_HEADS).reshape(1, KV_DIM)
    ag = attn_out_norm_g[0].reshape(1, ATTN_DIM)
    cg = conv_out_norm_g[0].reshape(1, CONV_CH)
    convw = conv_w[0]
    sinks = attn_sinks[0] * LOG2E
    w_r = jnp.concatenate([w_group_router[0], w_expert_router[0].reshape(D_MODEL, N_EXPERTS)], axis=1)
    n_r = N_GROUPS + N_EXPERTS
    w_r_hi = w_r.astype(_bf16)
    w_r_lo = (w_r - w_r_hi.astype(_f32)).astype(_bf16)
    wr = jnp.zeros((D_MODEL, LANES), _bf16)
    wr = wr.at[:, :n_r].set(w_r_hi).at[:, HEAD_DIM:HEAD_DIM + n_r].set(w_r_lo)
    br = jnp.zeros((1, LANES), _f32).at[0, :n_r].set(
        jnp.concatenate([b_group_router[0], b_expert_router[0].reshape(N_EXPERTS)]))
    meta_blk = jnp.concatenate([jnp.zeros((PAD, D_MODEL), x.dtype), meta_tokens.astype(x.dtype)], axis=0)
    bdq = _block_diag_mean(ATTN_DIM)
    bdk = _block_diag_mean(KV_DIM)
    tri = jnp.asarray(np.tril(np.ones((TM, TM), np.float32), -1), dtype=_bf16)

    kv_sd = jax.ShapeDtypeStruct((BLK, KV_DIM), _bf16)
    k0m, k1m, v0m, v1m, utm, bias = pl.pallas_call(
        _prep_kernel,
        out_shape=(kv_sd, kv_sd, kv_sd, kv_sd,
                   jax.ShapeDtypeStruct((SUBLANES, CONV_CH), _f32),
                   jax.ShapeDtypeStruct((2, N_KV_HEADS, Q_GROUP * BLK, 2 * BLK), _f32)),
        in_specs=[pl.BlockSpec(memory_space=pltpu.SMEM)] + [pl.BlockSpec(memory_space=pltpu.VMEM)] * 6,
        out_specs=tuple(pl.BlockSpec(memory_space=pltpu.VMEM) for _ in range(6)),
        compiler_params=pltpu.CompilerParams(vmem_limit_bytes=VMEM_LIMIT),
        name="prep",
    )(rel_bias, meta_blk, mixg, win, kg, bdk, _bucket_maps())

    consts = (mixg, win, qg, kg, bias, convw, ag, cg, wout, fg, wr, br, k0m, k1m, v0m, v1m, utm, bdq, bdk, tri)
    h2, hn2, route_t, cnt, cntb = _mixer_call(sinks, x.reshape(n_tok, D_MODEL), consts, nt)
    pos2, t_exp, t_rows, n_tiles, tables = _plan(route_t, cnt, cntb, n_tok)
    xs = _dispatch_rows(hn2, pos2[0], pos2[1], n_tiles * TME)
    y_sorted = _experts_call(t_exp, t_rows, xs, w_gate[0], w_up[0], w_down[0], n_tiles)
    out = _combine_call(tables, y_sorted, h2)
    return out.reshape(bsz, seq, D_MODEL)


def _mixer_call(sinks, x_rows, consts, nt):
    n_tok = x_rows.shape[0]
    tile_idx = lambda b, t, *_: (b * nt + t, 0)
    grid_spec = pltpu.PrefetchScalarGridSpec(
        num_scalar_prefetch=1,
        grid=(n_tok // (nt * TM), nt),
        in_specs=[
            pl.BlockSpec((TM, D_MODEL), tile_idx),
            _const((1, D_MODEL)), _const((D_MODEL, IN_PROJ)), _const((1, ATTN_DIM)), _const((1, KV_DIM)),
            _const((2, N_KV_HEADS, Q_GROUP * BLK, 2 * BLK)), _const((3, CONV_CH)),
            _const((1, ATTN_DIM)), _const((1, CONV_CH)), _const((D_MODEL, D_MODEL)), _const((1, D_MODEL)),
            _const((D_MODEL, LANES)), _const((1, LANES)),
            _const((BLK, KV_DIM)), _const((BLK, KV_DIM)), _const((BLK, KV_DIM)), _const((BLK, KV_DIM)),
            _const((SUBLANES, CONV_CH)),
            _const((ATTN_DIM, ATTN_DIM)), _const((KV_DIM, KV_DIM)), _const((TM, TM)),
        ],
        out_specs=[
            pl.BlockSpec((TM * ROW_CHUNKS, LANES), tile_idx),
            pl.BlockSpec((TM, PACKED_D), tile_idx),
            pl.BlockSpec((SUBLANES, TM), lambda b, t, *_: (0, b * nt + t)),
            _const((1, LANES)),
            pl.BlockSpec((SUBLANES, LANES), tile_idx),
        ],
        scratch_shapes=[
            pltpu.VMEM((BLK, KV_DIM), _bf16), pltpu.VMEM((BLK, KV_DIM), _bf16),
            pltpu.VMEM((BLK, KV_DIM), _bf16), pltpu.VMEM((BLK, KV_DIM), _bf16),
            pltpu.VMEM((TM + SUBLANES, CONV_CH), _f32),
            pltpu.VMEM((TM, ATTN_DIM), _f32),
            pltpu.VMEM((1, LANES), _f32),
        ],
    )
    return pl.pallas_call(
        _mixer_kernel,
        grid_spec=grid_spec,
        out_shape=(jax.ShapeDtypeStruct((n_tok * ROW_CHUNKS, LANES), _f32),
                   jax.ShapeDtypeStruct((n_tok, PACKED_D), jnp.int32),
                   jax.ShapeDtypeStruct((SUBLANES, n_tok), _f32),
                   jax.ShapeDtypeStruct((1, LANES), _f32),
                   jax.ShapeDtypeStruct((n_tok // TM * SUBLANES, LANES), _f32)),
        compiler_params=pltpu.CompilerParams(dimension_semantics=("arbitrary", "arbitrary"),
                                             vmem_limit_bytes=VMEM_LIMIT),
        name="mixer",
    )(sinks, x_rows, *consts)


def _plan(route_t, cnt, cntb, n_tok):
    n_tiles = (n_tok * 2) // TME + N_EXPERTS + 1
    counts = cnt[0, :N_EXPERTS].astype(jnp.int32)
    ntile = (counts + TME - 1) // TME
    tile_end = jnp.cumsum(ntile)
    tile_start = tile_end - ntile
    eid = route_t[0:2].astype(jnp.int32)
    rank = route_t[2:4].astype(jnp.int32)
    gates = route_t[4:6]
    experts = jnp.arange(N_EXPERTS, dtype=jnp.int32)
    start_of = jnp.sum(jnp.where(eid[None] == experts[:, None, None], tile_start[:, None, None], 0), axis=0)
    pos2 = start_of * TME + rank
    tiles = jnp.arange(n_tiles, dtype=jnp.int32)
    n_used = tile_end[-1]
    t_exp = jnp.sum((jnp.minimum(tiles, n_used - 1)[:, None] >= tile_end[None, :]).astype(jnp.int32), axis=-1)
    t_exp = jnp.minimum(t_exp, N_EXPERTS - 1)
    own = t_exp[:, None] == experts
    t_rows = jnp.sum(jnp.where(own, counts - (tiles[:, None] - tile_start) * TME, 0), axis=-1)
    t_rows = jnp.where(tiles < n_used, jnp.clip(t_rows, 0, TME), 0).astype(jnp.int32)

    n_mix = n_tok // TMC
    before = cntb[::SUBLANES, :N_EXPERTS].astype(jnp.int32)
    held = jnp.concatenate([before[1:], counts[None]], axis=0) - before
    first = tile_start[None, :] * TME + before
    nw = (held + WIN - 1) // WIN
    slot_end = jnp.cumsum(nw, axis=1)
    slot_beg = slot_end - nw
    nwin = slot_end[:, -1]
    slots = jnp.arange(MAX_WIN, dtype=jnp.int32)
    owner = jnp.minimum(jnp.sum((slots[None, :, None] >= slot_end[:, None, :]).astype(jnp.int32), axis=-1),
                        N_EXPERTS - 1)
    is_owner = owner[:, :, None] == experts
    wstart = jnp.sum(jnp.where(is_owner, first[:, None, :] + (slots[None, :, None] - slot_beg[:, None, :]) * WIN,
                               0), axis=-1)
    wstart = jnp.where(slots[None, :] < nwin[:, None], wstart, 0).astype(jnp.int32).reshape(-1)
    mine = eid.reshape(2, n_mix, TMC, 1) == experts
    beg_tok = jnp.sum(jnp.where(mine, slot_beg[None, :, None, :], 0), axis=-1)
    before_tok = jnp.sum(jnp.where(mine, before[None, :, None, :], 0), axis=-1)
    widx = (beg_tok * WIN + rank.reshape(2, n_mix, TMC) - before_tok).reshape(2, n_tok).astype(jnp.int32)
    widx = widx * ROW_CHUNKS
    return pos2, t_exp, t_rows, n_tiles, (nwin.astype(jnp.int32), wstart, widx[0], widx[1], gates[0], gates[1])


def _experts_call(t_exp, t_rows, xs, w_gate, w_up, w_down, n_tiles):
    return pl.pallas_call(
        _experts_kernel,
        grid_spec=pltpu.PrefetchScalarGridSpec(
            num_scalar_prefetch=2,
            grid=(n_tiles,),
            in_specs=[
                pl.BlockSpec(memory_space=pl.ANY),
                pl.BlockSpec((None, D_MODEL, D_EXPERT), lambda i, te, tn: (te[i], 0, 0)),
                pl.BlockSpec((None, D_MODEL, D_EXPERT), lambda i, te, tn: (te[i], 0, 0)),
                pl.BlockSpec((None, D_EXPERT, D_MODEL), lambda i, te, tn: (te[i], 0, 0)),
            ],
            out_specs=pl.BlockSpec((TME * ROW_CHUNKS, LANES), lambda i, *_: (i, 0)),
            scratch_shapes=[
                pltpu.VMEM((XS_SLOTS, TME, PACKED_D), jnp.int32),
                pltpu.SemaphoreType.DMA((XS_SLOTS,)),
                pltpu.VMEM((D_MODEL, D_EXPERT), _bf16),
                pltpu.VMEM((D_MODEL, D_EXPERT), _bf16),
                pltpu.VMEM((D_EXPERT, D_MODEL), _bf16),
            ],
        ),
        out_shape=jax.ShapeDtypeStruct((n_tiles * TME * ROW_CHUNKS, LANES), _f32),
        compiler_params=pltpu.CompilerParams(dimension_semantics=("arbitrary",),
                                             vmem_limit_bytes=VMEM_LIMIT),
        name="experts",
    )(t_exp, t_rows, xs, w_gate, w_up, w_down)


def _combine_call(tables, y_sorted, h2):
    n_tok = h2.shape[0] // ROW_CHUNKS
    return pl.pallas_call(
        _combine_kernel,
        grid_spec=pltpu.PrefetchScalarGridSpec(
            num_scalar_prefetch=len(tables),
            grid=(n_tok // TMC,),
            in_specs=[
                pl.BlockSpec(memory_space=pl.ANY),
                pl.BlockSpec((TMC * ROW_CHUNKS, LANES), lambda i, *_: (i, 0)),
            ],
            out_specs=pl.BlockSpec((TMC, D_MODEL), lambda i, *_: (i, 0)),
            scratch_shapes=[
                pltpu.VMEM((2, MAX_WIN * WIN * ROW_CHUNKS, LANES), _f32),
                pltpu.SemaphoreType.DMA((2,)),
                pltpu.VMEM((TMC * ROW_CHUNKS, LANES), _f32),
            ],
        ),
        out_shape=jax.ShapeDtypeStruct((n_tok, D_MODEL), _f32),
        compiler_params=pltpu.CompilerParams(dimension_semantics=("arbitrary",),
                                             vmem_limit_bytes=VMEM_LIMIT),
        name="combine",
    )(*tables, y_sorted, h2)
```

```python
import functools

import numpy as np
import jax
import jax.numpy as jnp
from jax import lax
from jax.experimental import pallas as pl
from jax.experimental.pallas import tpu as pltpu
from jax.experimental.pallas import tpu_sc as plsc

D_MODEL = 1024
N_META = 16
N_Q_HEADS = 8
N_KV_HEADS = 2
HEAD_DIM = 64
Q_GROUP = N_Q_HEADS // N_KV_HEADS
ATTN_DIM = N_Q_HEADS * HEAD_DIM
KV_DIM = N_KV_HEADS * HEAD_DIM
BLK = 128
PAD = BLK - N_META
N_BUCKETS = 32
MAX_DISTANCE = 128
CONV_CH = D_MODEL // 2
IN_PROJ = ATTN_DIM + 2 * KV_DIM + 3 * CONV_CH
N_GROUPS = 4
EXPERTS_PER_GROUP = 8
N_EXPERTS = N_GROUPS * EXPERTS_PER_GROUP
D_EXPERT = D_MODEL // 2
EPS = 1e-6
NEG_INF = -1e30
LOG2E = float(np.log2(np.e))

LANES = 128
SUBLANES = 8
ROW_CHUNKS = D_MODEL // LANES
TM = 512
TME = 512
TMC = TM
WIN = 16
MAX_WIN = N_EXPERTS + 2 * TMC // WIN
CMB_UNROLL = 128
WAIT_GROUP = 8
ROUTER_E_OFF = 8
XS_SLOTS = 3
PACKED_D = D_MODEL // 2
SC_WINDOW = 64
VMEM_LIMIT = 56 * 1024 * 1024

Q_OFF, K_OFF = 0, ATTN_DIM
CB_OFF = ATTN_DIM + 2 * KV_DIM
CC_OFF = CB_OFF + CONV_CH
CH_OFF = CC_OFF + CONV_CH

_f32 = jnp.float32
_bf16 = jnp.bfloat16


def _rms(x, g):
    return x * lax.rsqrt(jnp.mean(x * x, axis=-1, keepdims=True) + EPS) * g


def _dot(a, b):
    return jnp.dot(a, b, preferred_element_type=_f32)


def _dup_halves(x):
    lane = lax.broadcasted_iota(jnp.int32, x.shape, 1)
    sw = pltpu.roll(x, HEAD_DIM, axis=1)
    lo = lane < HEAD_DIM
    return jnp.where(lo, x, sw).astype(_bf16), jnp.where(lo, sw, x).astype(_bf16)


def _kv_state(hn_bf, win_ref, kg_ref, bdk_ref):
    kv = _dot(hn_bf, win_ref[:, K_OFF:K_OFF + 2 * KV_DIM])
    k = kv[:, :KV_DIM]
    v = kv[:, KV_DIM:]
    ssk = _dot((k * k).astype(_bf16), bdk_ref[...])
    kn = k * lax.rsqrt(ssk + EPS) * kg_ref[...]
    return _dup_halves(kn) + _dup_halves(v)


def _prep_kernel(rb_ref, meta_ref, mixg_ref, win_ref, kg_ref, bdk_ref, bucket_ref,
                 k0_ref, k1_ref, v0_ref, v1_ref, ut_ref, bias_ref):
    hn = _rms(meta_ref[...], mixg_ref[...]).astype(_bf16)
    k0, k1, v0, v1 = _kv_state(hn, win_ref, kg_ref, bdk_ref)
    k0_ref[...] = k0
    k1_ref[...] = k1
    v0_ref[...] = v0
    v1_ref[...] = v1
    cch = _dot(hn, win_ref[:, CC_OFF:CC_OFF + 2 * CONV_CH])
    u = cch[:, :CONV_CH] * cch[:, CONV_CH:]
    ut_ref[...] = u[BLK - SUBLANES:, :]
    for f in range(2):
        bk = bucket_ref[f]
        for h in range(N_Q_HEADS):
            acc = jnp.full((BLK, 2 * BLK), NEG_INF, _f32)
            for b in range(N_BUCKETS):
                acc = jnp.where(bk == b, rb_ref[b, h] * LOG2E, acc)
            bias_ref[f, h // Q_GROUP, (h % Q_GROUP) * BLK:(h % Q_GROUP + 1) * BLK, :] = acc


def _mixer_kernel(sinks_ref,
                  x_ref, mixg_ref, win_ref, qg_ref, kg_ref, bias_ref, convw_ref, ag_ref, cg_ref,
                  wout_ref, fg_ref, wr_ref, br_ref, k0m_ref, k1m_ref, v0m_ref, v1m_ref, utm_ref,
                  bdq_ref, bdk_ref, tri_ref,
                  h2_ref, hn2_ref, route_t_ref, cnt_ref, cntb_ref,
                  kp0, kp1, vp0, vp1, ubuf, a_scr, cnt_acc):
    b = pl.program_id(0)
    t = pl.program_id(1)

    @pl.when(t == 0)
    def _():
        kp0[...] = k0m_ref[...]
        kp1[...] = k1m_ref[...]
        vp0[...] = v0m_ref[...]
        vp1[...] = v1m_ref[...]
        ubuf[0:SUBLANES, :] = utm_ref[...]

    @pl.when(jnp.logical_and(b == 0, t == 0))
    def _():
        cnt_acc[...] = jnp.zeros_like(cnt_acc)

    x = x_ref[...]
    hn = _rms(x, mixg_ref[...]).astype(_bf16)

    q = _dot(hn, win_ref[:, Q_OFF:Q_OFF + ATTN_DIM])
    ssq = _dot((q * q).astype(_bf16), bdq_ref[...])
    qn = (q * lax.rsqrt(ssq + EPS) * qg_ref[...]).astype(_bf16)
    kd0, kd1, vd0, vd1 = _kv_state(hn, win_ref, kg_ref, bdk_ref)
    kd = (kd0, kd1)
    vd = (vd0, vd1)
    kp = (kp0, kp1)
    vp = (vp0, vp1)

    lane_q = lax.broadcasted_iota(jnp.int32, (BLK, LANES), 1)
    lo_half = lane_q < HEAD_DIM
    row4 = lax.broadcasted_iota(jnp.int32, (Q_GROUP * BLK, 1), 0) // BLK
    first = jnp.where(t == 0, 0, 1)
    zero_bf = jnp.zeros((BLK, LANES), _bf16)

    for j in range(TM // BLK):
        rows = slice(j * BLK, (j + 1) * BLK)
        for g in range(N_KV_HEADS):
            if j == 0:
                kcat = jnp.concatenate([kp[g][...], kd[g][rows]], axis=0)
                vcat = jnp.concatenate([vp[g][...], vd[g][rows]], axis=0)
                bias = bias_ref[first, g]
            else:
                kcat = kd[g][(j - 1) * BLK:(j + 1) * BLK]
                vcat = vd[g][(j - 1) * BLK:(j + 1) * BLK]
                bias = bias_ref[1, g]
            qs = []
            for hh in range(Q_GROUP):
                h = g * Q_GROUP + hh
                qc = qn[rows, (h // 2) * LANES:(h // 2 + 1) * LANES]
                keep = lo_half if h % 2 == 0 else jnp.logical_not(lo_half)
                qs.append(jnp.where(keep, qc, zero_bf))
            q4 = jnp.concatenate(qs, axis=0)
            s = lax.dot_general(q4, kcat, (((1,), (1,)), ((), ())),
                                preferred_element_type=_f32) + bias
            sink = jnp.full((Q_GROUP * BLK, 1), sinks_ref[g * Q_GROUP], _f32)
            for hh in range(1, Q_GROUP):
                sink = jnp.where(row4 == hh, sinks_ref[g * Q_GROUP + hh], sink)
            m = jnp.maximum(jnp.max(s, axis=-1, keepdims=True), sink)
            p = jnp.exp2(s - m)
            l = jnp.sum(p, axis=-1, keepdims=True) + jnp.exp2(sink - m)
            o = _dot(p.astype(_bf16), vcat) / l
            for pp in range(Q_GROUP // 2):
                ev = o[(2 * pp) * BLK:(2 * pp + 1) * BLK]
                od = o[(2 * pp + 1) * BLK:(2 * pp + 2) * BLK]
                col = g * (Q_GROUP // 2) + pp
                a_scr[rows, col * LANES:(col + 1) * LANES] = jnp.where(lo_half, ev, od)

    last = slice(TM - BLK, TM)
    kp0[...] = kd0[last]
    kp1[...] = kd1[last]
    vp0[...] = vd0[last]
    vp1[...] = vd1[last]

    cb = _dot(hn, win_ref[:, CB_OFF:CB_OFF + CONV_CH])
    cch = _dot(hn, win_ref[:, CC_OFF:CC_OFF + 2 * CONV_CH])
    u = cch[:, :CONV_CH] * cch[:, CONV_CH:]
    ubuf[SUBLANES:, :] = u
    u1 = ubuf[SUBLANES - 1:SUBLANES - 1 + TM, :]
    u2 = ubuf[SUBLANES - 2:SUBLANES - 2 + TM, :]
    c = cb * (convw_ref[0:1, :] * u2 + convw_ref[1:2, :] * u1 + convw_ref[2:3, :] * u)
    ubuf[0:SUBLANES, :] = u[TM - SUBLANES:, :]

    an = _rms(a_scr[...], ag_ref[...]).astype(_bf16)
    cn = _rms(c, cg_ref[...]).astype(_bf16)
    h2 = x + _dot(an, wout_ref[0:ATTN_DIM, :]) + _dot(cn, wout_ref[ATTN_DIM:, :])
    for cc in range(ROW_CHUNKS):
        h2_ref[pl.ds(cc, TM, stride=ROW_CHUNKS), :] = h2[:, cc * LANES:(cc + 1) * LANES]

    hn2 = _rms(h2, fg_ref[...])
    hn2_ref[...] = pltpu.pack_elementwise([hn2[:, :PACKED_D], hn2[:, PACKED_D:]], packed_dtype=_bf16)

    r1 = _dot(hn2.astype(_bf16), wr_ref[...])
    lg = r1 + pltpu.roll(r1, HEAD_DIM, axis=1) + br_ref[...]
    lgt = lg.T
    ninf = jnp.float32(-jnp.inf)
    big = jnp.float32(LANES)
    row8 = lax.broadcasted_iota(jnp.int32, (SUBLANES, TM), 0).astype(_f32)

    gl = jnp.where(row8 < N_GROUPS, lgt[0:SUBLANES, :], ninf)
    gmax = jnp.max(gl, axis=0, keepdims=True)
    gsum = jnp.sum(jnp.exp(gl - gmax), axis=0, keepdims=True)
    g_p = 1.0 / gsum
    g_idx = jnp.min(jnp.where(gl == gmax, row8, big), axis=0, keepdims=True)
    el = lgt[ROUTER_E_OFF:ROUTER_E_OFF + EXPERTS_PER_GROUP, :]
    for g in range(1, N_GROUPS):
        lo_row = ROUTER_E_OFF + g * EXPERTS_PER_GROUP
        el = jnp.where(g_idx == g, lgt[lo_row:lo_row + EXPERTS_PER_GROUP, :], el)
    m1 = jnp.max(el, axis=0, keepdims=True)
    i1 = jnp.min(jnp.where(el == m1, row8, big), axis=0, keepdims=True)
    el2 = jnp.where(row8 == i1, ninf, el)
    m2 = jnp.max(el2, axis=0, keepdims=True)
    i2 = jnp.min(jnp.where(el2 == m2, row8, big), axis=0, keepdims=True)
    ex = jnp.exp(m2 - m1)
    den = 1.0 / (1.0 + ex)
    gate0 = g_p * den
    gate1 = g_p * ex * den
    e0 = g_idx * EXPERTS_PER_GROUP + i1
    e1 = g_idx * EXPERTS_PER_GROUP + i2

    rowe = lax.broadcasted_iota(jnp.int32, (N_EXPERTS, TM), 0).astype(_f32)
    oh0 = rowe == e0
    oh1 = rowe == e1
    cmat = jnp.where(oh0, 1.0, 0.0) + jnp.where(oh1, 1.0, 0.0)
    before = cnt_acc[...]
    prefix = _dot(cmat.astype(_bf16), tri_ref[...]) + before[:, 0:1]
    rank0 = jnp.sum(jnp.where(oh0, prefix, 0.0), axis=0, keepdims=True)
    rank1 = jnp.sum(jnp.where(oh1, prefix, 0.0), axis=0, keepdims=True)
    cntb_ref[...] = before
    cnt_new = before + jnp.sum(cmat, axis=1, keepdims=True)
    cnt_acc[...] = cnt_new
    cnt_ref[...] = cnt_new

    zero_row = jnp.zeros((1, TM), _f32)
    route_t_ref[...] = jnp.concatenate([e0, e1, rank0, rank1, gate0, gate1, zero_row, zero_row], axis=0)


def _dispatch_rows(hn2_rows, pos0, pos1, n_rows):
    n_tok = hn2_rows.shape[0]
    info = plsc.get_sparse_core_info()
    n_workers = info.num_cores * info.num_subcores
    per_worker = n_tok // n_workers
    n_chunks = per_worker // SC_WINDOW
    assert per_worker * n_workers == n_tok and n_chunks * SC_WINDOW == per_worker and n_chunks % 2 == 0
    mesh = plsc.VectorSubcoreMesh(core_axis_name="core", subcore_axis_name="subcore")

    @functools.partial(
        pl.kernel,
        out_type=jax.ShapeDtypeStruct((n_rows, PACKED_D), jnp.int32),
        mesh=mesh,
        scratch_types=[pltpu.VMEM((SC_WINDOW,), jnp.int32), pltpu.VMEM((SC_WINDOW,), jnp.int32),
                       pltpu.VMEM((SC_WINDOW,), jnp.int32), pltpu.VMEM((SC_WINDOW,), jnp.int32),
                       pltpu.VMEM((SC_WINDOW, PACKED_D), jnp.int32), pltpu.VMEM((SC_WINDOW, PACKED_D), jnp.int32),
                       pltpu.SemaphoreType.DMA((2,)), pltpu.SemaphoreType.DMA((2,))],
        compiler_params=pltpu.CompilerParams(use_tc_tiling_on_sc=True),
        name="dispatch",
    )
    def dispatch(x_hbm, i0_hbm, i1_hbm, o_hbm, i0_a, i0_b, i1_a, i1_b, rows_a, rows_b, sem_ld, sem_st):
        wid = lax.axis_index("subcore") * info.num_cores + lax.axis_index("core")
        base = wid * per_worker
        i0_v, i1_v, rows_v = (i0_a, i0_b), (i1_a, i1_b), (rows_a, rows_b)

        def loads(chunk, b):
            off = pl.multiple_of(base + chunk * SC_WINDOW, SC_WINDOW)
            return (pltpu.make_async_copy(i0_hbm.at[pl.ds(off, SC_WINDOW)], i0_v[b], sem_ld.at[b]),
                    pltpu.make_async_copy(i1_hbm.at[pl.ds(off, SC_WINDOW)], i1_v[b], sem_ld.at[b]),
                    pltpu.make_async_copy(x_hbm.at[pl.ds(off, SC_WINDOW)], rows_v[b], sem_ld.at[b]))

        def stores(b):
            return (pltpu.make_async_copy(rows_v[b], o_hbm.at[i0_v[b]], sem_st.at[b]),
                    pltpu.make_async_copy(rows_v[b], o_hbm.at[i1_v[b]], sem_st.at[b]))

        for d in loads(0, 0):
            d.start()

        @pl.loop(0, n_chunks, step=2)
        def _(c):
            for b in range(2):
                chunk = c + b
                for d in loads(chunk, b):
                    d.wait()
                for d in stores(b):
                    d.start()

                @pl.when(chunk >= 1)
                def _():
                    for d in stores(1 - b):
                        d.wait()

                @pl.when(chunk + 1 < n_chunks)
                def _():
                    for d in loads(chunk + 1, 1 - b):
                        d.start()

        for d in stores(1):
            d.wait()

    return dispatch(hn2_rows, pos0, pos1)


def _experts_kernel(te_ref, tn_ref,
                    xs_hbm, wg_ref, wu_ref, wd_ref,
                    y_ref,
                    xbuf, sem, wgb, wub, wdb):
    i = pl.program_id(0)
    n = pl.num_programs(0)

    def fetch(tile):
        sl = tile % XS_SLOTS
        return pltpu.make_async_copy(xs_hbm.at[pl.ds(pl.multiple_of(tile * TME, TME), TME), :],
                                     xbuf.at[sl], sem.at[sl])

    @pl.when(i == 0)
    def _():
        for tile in range(XS_SLOTS - 1):
            fetch(tile).start()

    @pl.when(i + XS_SLOTS - 1 < n)
    def _():
        fetch(i + XS_SLOTS - 1).start()

    valid = tn_ref[i] > 0
    changed = jnp.logical_or(i == 0, te_ref[i] != te_ref[jnp.maximum(i - 1, 0)])

    @pl.when(jnp.logical_and(changed, valid))
    def _():
        wgb[...] = wg_ref[...].astype(_bf16)
        wub[...] = wu_ref[...].astype(_bf16)
        wdb[...] = wd_ref[...].astype(_bf16)

    fetch(i).wait()

    @pl.when(valid)
    def _():
        packed = xbuf[i % XS_SLOTS]
        halves = [pltpu.unpack_elementwise(packed, index=k, packed_dtype=_bf16, unpacked_dtype=_f32)
                  for k in range(2)]
        xb = jnp.concatenate(halves, axis=1).astype(_bf16)
        live = lax.broadcasted_iota(jnp.int32, (TME, 1), 0) < tn_ref[i]
        xb = jnp.where(live, xb, jnp.zeros_like(xb))
        gate = _dot(xb, wgb[...])
        up = _dot(xb, wub[...])
        act = (gate * jax.nn.sigmoid(gate) * up).astype(_bf16)
        y = _dot(act, wdb[...])
        for cc in range(ROW_CHUNKS):
            y_ref[pl.ds(cc, TME, stride=ROW_CHUNKS), :] = y[:, cc * LANES:(cc + 1) * LANES]

    @pl.when(jnp.logical_not(valid))
    def _():
        y_ref[...] = jnp.zeros_like(y_ref)


def _combine_kernel(nwin_ref, wstart_ref, widx0_ref, widx1_ref, gate0_ref, gate1_ref,
                    y_hbm, h2_ref,
                    out_ref,
                    wbuf, sem, otile):
    i = pl.program_id(0)
    n = pl.num_programs(0)
    slot = i % 2
    win_rows = WIN * ROW_CHUNKS

    def window(tile, sl, s):
        src = pl.multiple_of(wstart_ref[tile * MAX_WIN + s] * ROW_CHUNKS, ROW_CHUNKS)
        return pltpu.make_async_copy(y_hbm.at[pl.ds(src, win_rows), :],
                                     wbuf.at[sl, pl.ds(pl.multiple_of(s * win_rows, win_rows), win_rows), :],
                                     sem.at[sl])

    def issue(tile, sl):
        def body(s, carry):
            window(tile, sl, s).start()
            return carry

        lax.fori_loop(0, nwin_ref[tile], body, 0)

    @pl.when(i == 0)
    def _():
        issue(0, 0)

    @pl.when(i + 1 < n)
    def _():
        issue(i + 1, 1 - slot)

    def wait_windows(k):
        rows = k * win_rows
        pltpu.make_async_copy(y_hbm.at[pl.ds(0, rows), :], wbuf.at[slot, pl.ds(0, rows), :], sem.at[slot]).wait()

    def wait_body(s, carry):
        wait_windows(WAIT_GROUP)
        return carry

    lax.fori_loop(0, nwin_ref[i] // WAIT_GROUP, wait_body, 0)
    k = WAIT_GROUP // 2
    while k >= 1:
        @pl.when((nwin_ref[i] & k) != 0)
        def _(k=k):
            wait_windows(k)
        k //= 2

    def token_body(it, carry):
        for j in range(CMB_UNROLL):
            r = it * CMB_UNROLL + j
            tok = i * TMC + r
            y0 = wbuf[slot, pl.ds(pl.multiple_of(widx0_ref[tok], ROW_CHUNKS), ROW_CHUNKS), :]
            y1 = wbuf[slot, pl.ds(pl.multiple_of(widx1_ref[tok], ROW_CHUNKS), ROW_CHUNKS), :]
            row = pl.ds(pl.multiple_of(r * ROW_CHUNKS, ROW_CHUNKS), ROW_CHUNKS)
            otile[row, :] = h2_ref[row, :] + (gate0_ref[tok] * y0 + gate1_ref[tok] * y1)
        return carry

    lax.fori_loop(0, TMC // CMB_UNROLL, token_body, 0)
    for cc in range(ROW_CHUNKS):
        out_ref[:, cc * LANES:(cc + 1) * LANES] = otile[pl.ds(cc, TMC, stride=ROW_CHUNKS), :]


def _t5_bucket(n):
    max_exact = N_BUCKETS // 2
    nf = jnp.maximum(n, 1).astype(_f32)
    large = max_exact + (jnp.log(nf / max_exact) / np.log(MAX_DISTANCE / max_exact)
                         * (N_BUCKETS - max_exact)).astype(jnp.int32)
    large = jnp.minimum(large, N_BUCKETS - 1)
    return jnp.where(n < max_exact, n, large)


def _bucket_maps():
    qi = jnp.arange(BLK)[:, None]
    sj = jnp.arange(2 * BLK)[None, :]
    dist = BLK + qi - sj
    band = (dist >= 0) & (dist < BLK)
    bucket = _t5_bucket(jnp.maximum(dist, 0))
    generic = jnp.where(band, bucket, -1)
    first = jnp.where(band & (sj >= PAD), bucket, -1)
    return jnp.stack([first, generic]).astype(jnp.int32)


def _const(shape):
    nd = len(shape)
    return pl.BlockSpec(shape, lambda *_: (0,) * nd)


def _block_diag_mean(n):
    idx = np.arange(n) // HEAD_DIM
    return jnp.asarray((idx[:, None] == idx[None, :]).astype(np.float32) / HEAD_DIM, dtype=_bf16)


def kernel(x, meta_tokens, rel_bias, mix_norm_g, w_in, q_norm_g, k_norm_g, attn_sinks, conv_w, attn_out_norm_g, conv_out_norm_g, w_out, ffn_norm_g, w_group_router, b_group_router, w_expert_router, b_expert_router, w_gate, w_up, w_down):
    bsz, seq, _ = x.shape
    assert seq % TM == 0 and (bsz * seq) % TMC == 0
    n_tok = bsz * seq
    nt = seq // TM

    win = w_in[0].astype(_bf16)
    wout = w_out[0].astype(_bf16)
    mixg = mix_norm_g[0].reshape(1, D_MODEL)
    fg = ffn_norm_g[0].reshape(1, D_MODEL)
    qg = (jnp.tile(q_norm_g[0], N_Q_HEADS) * (HEAD_DIM ** -0.5 * LOG2E)).reshape(1, ATTN_DIM)
    kg = jnp.tile(k_norm_g[0], N_KV_HEADS).reshape(1, KV_DIM)
    ag = attn_out_norm_g[0].reshape(1, ATTN_DIM)
    cg = conv_out_norm_g[0].reshape(1, CONV_CH)
    convw = conv_w[0]
    sinks = attn_sinks[0] * LOG2E
    n_r = ROUTER_E_OFF + N_EXPERTS
    w_r = jnp.zeros((D_MODEL, n_r), _f32)
    w_r = w_r.at[:, :N_GROUPS].set(w_group_router[0])
    w_r = w_r.at[:, ROUTER_E_OFF:].set(w_expert_router[0].reshape(D_MODEL, N_EXPERTS))
    w_r_hi = w_r.astype(_bf16)
    w_r_lo = (w_r - w_r_hi.astype(_f32)).astype(_bf16)
    wr = jnp.zeros((D_MODEL, LANES), _bf16)
    wr = wr.at[:, :n_r].set(w_r_hi).at[:, HEAD_DIM:HEAD_DIM + n_r].set(w_r_lo)
    br = jnp.zeros((1, LANES), _f32).at[0, :N_GROUPS].set(b_group_router[0])
    br = br.at[0, ROUTER_E_OFF:n_r].set(b_expert_router[0].reshape(N_EXPERTS))
    meta_blk = jnp.concatenate([jnp.zeros((PAD, D_MODEL), x.dtype), meta_tokens.astype(x.dtype)], axis=0)
    bdq = _block_diag_mean(ATTN_DIM)
    bdk = _block_diag_mean(KV_DIM)
    tri = jnp.asarray(np.triu(np.ones((TM, TM), np.float32), 1), dtype=_bf16)

    kv_sd = jax.ShapeDtypeStruct((BLK, KV_DIM), _bf16)
    k0m, k1m, v0m, v1m, utm, bias = pl.pallas_call(
        _prep_kernel,
        out_shape=(kv_sd, kv_sd, kv_sd, kv_sd,
                   jax.ShapeDtypeStruct((SUBLANES, CONV_CH), _f32),
                   jax.ShapeDtypeStruct((2, N_KV_HEADS, Q_GROUP * BLK, 2 * BLK), _f32)),
        in_specs=[pl.BlockSpec(memory_space=pltpu.SMEM)] + [pl.BlockSpec(memory_space=pltpu.VMEM)] * 6,
        out_specs=tuple(pl.BlockSpec(memory_space=pltpu.VMEM) for _ in range(6)),
        compiler_params=pltpu.CompilerParams(vmem_limit_bytes=VMEM_LIMIT),
        name="prep",
    )(rel_bias, meta_blk, mixg, win, kg, bdk, _bucket_maps())

    consts = (mixg, win, qg, kg, bias, convw, ag, cg, wout, fg, wr, br, k0m, k1m, v0m, v1m, utm, bdq, bdk, tri)
    h2, hn2, route_t, cnt, cntb = _mixer_call(sinks, x.reshape(n_tok, D_MODEL), consts, nt)
    pos2, t_exp, t_rows, n_tiles, tables = _plan(route_t, cnt, cntb, n_tok)
    xs = _dispatch_rows(hn2, pos2[0], pos2[1], n_tiles * TME)
    y_sorted = _experts_call(t_exp, t_rows, xs, w_gate[0], w_up[0], w_down[0], n_tiles)
    out = _combine_call(tables, y_sorted, h2)
    return out.reshape(bsz, seq, D_MODEL)


def _mixer_call(sinks, x_rows, consts, nt):
    n_tok = x_rows.shape[0]
    tile_idx = lambda b, t, *_: (b * nt + t, 0)
    grid_spec = pltpu.PrefetchScalarGridSpec(
        num_scalar_prefetch=1,
        grid=(n_tok // (nt * TM), nt),
        in_specs=[
            pl.BlockSpec((TM, D_MODEL), tile_idx),
            _const((1, D_MODEL)), _const((D_MODEL, IN_PROJ)), _const((1, ATTN_DIM)), _const((1, KV_DIM)),
            _const((2, N_KV_HEADS, Q_GROUP * BLK, 2 * BLK)), _const((3, CONV_CH)),
            _const((1, ATTN_DIM)), _const((1, CONV_CH)), _const((D_MODEL, D_MODEL)), _const((1, D_MODEL)),
            _const((D_MODEL, LANES)), _const((1, LANES)),
            _const((BLK, KV_DIM)), _const((BLK, KV_DIM)), _const((BLK, KV_DIM)), _const((BLK, KV_DIM)),
            _const((SUBLANES, CONV_CH)),
            _const((ATTN_DIM, ATTN_DIM)), _const((KV_DIM, KV_DIM)), _const((TM, TM)),
        ],
        out_specs=[
            pl.BlockSpec((TM * ROW_CHUNKS, LANES), tile_idx),
            pl.BlockSpec((TM, PACKED_D), tile_idx),
            pl.BlockSpec((SUBLANES, TM), lambda b, t, *_: (0, b * nt + t)),
            _const((N_EXPERTS, LANES)),
            pl.BlockSpec((N_EXPERTS, LANES), tile_idx),
        ],
        scratch_shapes=[
            pltpu.VMEM((BLK, KV_DIM), _bf16), pltpu.VMEM((BLK, KV_DIM), _bf16),
            pltpu.VMEM((BLK, KV_DIM), _bf16), pltpu.VMEM((BLK, KV_DIM), _bf16),
            pltpu.VMEM((TM + SUBLANES, CONV_CH), _f32),
            pltpu.VMEM((TM, ATTN_DIM), _f32),
            pltpu.VMEM((N_EXPERTS, LANES), _f32),
        ],
    )
    return pl.pallas_call(
        _mixer_kernel,
        grid_spec=grid_spec,
        out_shape=(jax.ShapeDtypeStruct((n_tok * ROW_CHUNKS, LANES), _f32),
                   jax.ShapeDtypeStruct((n_tok, PACKED_D), jnp.int32),
                   jax.ShapeDtypeStruct((SUBLANES, n_tok), _f32),
                   jax.ShapeDtypeStruct((N_EXPERTS, LANES), _f32),
                   jax.ShapeDtypeStruct((n_tok // TM * N_EXPERTS, LANES), _f32)),
        compiler_params=pltpu.CompilerParams(dimension_semantics=("arbitrary", "arbitrary"),
                                             vmem_limit_bytes=VMEM_LIMIT),
        name="mixer",
    )(sinks, x_rows, *consts)


def _plan(route_t, cnt, cntb, n_tok):
    n_tiles = (n_tok * 2) // TME + N_EXPERTS + 1
    counts = cnt[:, 0].astype(jnp.int32)
    ntile = (counts + TME - 1) // TME
    tile_end = jnp.cumsum(ntile)
    tile_start = tile_end - ntile
    eid = route_t[0:2].astype(jnp.int32)
    rank = route_t[2:4].astype(jnp.int32)
    gates = route_t[4:6]
    experts = jnp.arange(N_EXPERTS, dtype=jnp.int32)
    start_of = jnp.sum(jnp.where(eid[None] == experts[:, None, None], tile_start[:, None, None], 0), axis=0)
    pos2 = start_of * TME + rank
    tiles = jnp.arange(n_tiles, dtype=jnp.int32)
    n_used = tile_end[-1]
    t_exp = jnp.sum((jnp.minimum(tiles, n_used - 1)[:, None] >= tile_end[None, :]).astype(jnp.int32), axis=-1)
    t_exp = jnp.minimum(t_exp, N_EXPERTS - 1)
    own = t_exp[:, None] == experts
    t_rows = jnp.sum(jnp.where(own, counts - (tiles[:, None] - tile_start) * TME, 0), axis=-1)
    t_rows = jnp.where(tiles < n_used, jnp.clip(t_rows, 0, TME), 0).astype(jnp.int32)

    n_mix = n_tok // TMC
    before = cntb.reshape(n_mix, N_EXPERTS, LANES)[:, :, 0].astype(jnp.int32)
    held = jnp.concatenate([before[1:], counts[None]], axis=0) - before
    first = tile_start[None, :] * TME + before
    nw = (held + WIN - 1) // WIN
    slot_end = jnp.cumsum(nw, axis=1)
    slot_beg = slot_end - nw
    nwin = slot_end[:, -1]
    slots = jnp.arange(MAX_WIN, dtype=jnp.int32)
    owner = jnp.minimum(jnp.sum((slots[None, :, None] >= slot_end[:, None, :]).astype(jnp.int32), axis=-1),
                        N_EXPERTS - 1)
    is_owner = owner[:, :, None] == experts
    wstart = jnp.sum(jnp.where(is_owner, first[:, None, :] + (slots[None, :, None] - slot_beg[:, None, :]) * WIN,
                               0), axis=-1)
    wstart = jnp.where(slots[None, :] < nwin[:, None], wstart, 0).astype(jnp.int32).reshape(-1)
    mine = eid.reshape(2, n_mix, TMC, 1) == experts
    beg_tok = jnp.sum(jnp.where(mine, slot_beg[None, :, None, :], 0), axis=-1)
    before_tok = jnp.sum(jnp.where(mine, before[None, :, None, :], 0), axis=-1)
    widx = (beg_tok * WIN + rank.reshape(2, n_mix, TMC) - before_tok).reshape(2, n_tok).astype(jnp.int32)
    widx = widx * ROW_CHUNKS
    return pos2, t_exp, t_rows, n_tiles, (nwin.astype(jnp.int32), wstart, widx[0], widx[1], gates[0], gates[1])


def _experts_call(t_exp, t_rows, xs, w_gate, w_up, w_down, n_tiles):
    return pl.pallas_call(
        _experts_kernel,
        grid_spec=pltpu.PrefetchScalarGridSpec(
            num_scalar_prefetch=2,
            grid=(n_tiles,),
            in_specs=[
                pl.BlockSpec(memory_space=pl.ANY),
                pl.BlockSpec((None, D_MODEL, D_EXPERT), lambda i, te, tn: (te[i], 0, 0)),
                pl.BlockSpec((None, D_MODEL, D_EXPERT), lambda i, te, tn: (te[i], 0, 0)),
                pl.BlockSpec((None, D_EXPERT, D_MODEL), lambda i, te, tn: (te[i], 0, 0)),
            ],
            out_specs=pl.BlockSpec((TME * ROW_CHUNKS, LANES), lambda i, *_: (i, 0)),
            scratch_shapes=[
                pltpu.VMEM((XS_SLOTS, TME, PACKED_D), jnp.int32),
                pltpu.SemaphoreType.DMA((XS_SLOTS,)),
                pltpu.VMEM((D_MODEL, D_EXPERT), _bf16),
                pltpu.VMEM((D_MODEL, D_EXPERT), _bf16),
                pltpu.VMEM((D_EXPERT, D_MODEL), _bf16),
            ],
        ),
        out_shape=jax.ShapeDtypeStruct((n_tiles * TME * ROW_CHUNKS, LANES), _f32),
        compiler_params=pltpu.CompilerParams(dimension_semantics=("arbitrary",),
                                             vmem_limit_bytes=VMEM_LIMIT),
        name="experts",
    )(t_exp, t_rows, xs, w_gate, w_up, w_down)


def _combine_call(tables, y_sorted, h2):
    n_tok = h2.shape[0] // ROW_CHUNKS
    return pl.pallas_call(
        _combine_kernel,
        grid_spec=pltpu.PrefetchScalarGridSpec(
            num_scalar_prefetch=len(tables),
            grid=(n_tok // TMC,),
            in_specs=[
                pl.BlockSpec(memory_space=pl.ANY),
                pl.BlockSpec((TMC * ROW_CHUNKS, LANES), lambda i, *_: (i, 0)),
            ],
            out_specs=pl.BlockSpec((TMC, D_MODEL), lambda i, *_: (i, 0)),
            scratch_shapes=[
                pltpu.VMEM((2, MAX_WIN * WIN * ROW_CHUNKS, LANES), _f32),
                pltpu.SemaphoreType.DMA((2,)),
                pltpu.VMEM((TMC * ROW_CHUNKS, LANES), _f32),
            ],
        ),
        out_shape=jax.ShapeDtypeStruct((n_tok, D_MODEL), _f32),
        compiler_params=pltpu.CompilerParams(dimension_semantics=("arbitrary",),
                                             vmem_limit_bytes=VMEM_LIMIT),
        name="combine",
    )(*tables, y_sorted, h2)
```

```python
import functools

import numpy as np
import jax
import jax.numpy as jnp
from jax import lax
from jax.experimental import pallas as pl
from jax.experimental.pallas import tpu as pltpu
from jax.experimental.pallas import tpu_sc as plsc

D_MODEL = 1024
N_META = 16
N_Q_HEADS = 8
N_KV_HEADS = 2
HEAD_DIM = 64
Q_GROUP = N_Q_HEADS // N_KV_HEADS
ATTN_DIM = N_Q_HEADS * HEAD_DIM
KV_DIM = N_KV_HEADS * HEAD_DIM
BLK = 128
PAD = BLK - N_META
N_BUCKETS = 32
MAX_DISTANCE = 128
CONV_CH = D_MODEL // 2
IN_PROJ = ATTN_DIM + 2 * KV_DIM + 3 * CONV_CH
N_GROUPS = 4
EXPERTS_PER_GROUP = 8
N_EXPERTS = N_GROUPS * EXPERTS_PER_GROUP
D_EXPERT = D_MODEL // 2
EPS = 1e-6
NEG_INF = -1e30
LOG2E = float(np.log2(np.e))

LANES = 128
SUBLANES = 8
ROW_CHUNKS = D_MODEL // LANES
TM = 512
TME = 512
TMC = TM
WIN = 16
MAX_WIN = N_EXPERTS + 2 * TMC // WIN
CMB_UNROLL = 128
WAIT_GROUP = 8
ROUTER_E_OFF = 8
XS_SLOTS = 3
PACKED_D = D_MODEL // 2
SC_WINDOW = 64
VMEM_LIMIT = 56 * 1024 * 1024

Q_OFF, K_OFF = 0, ATTN_DIM
CB_OFF = ATTN_DIM + 2 * KV_DIM
CC_OFF = CB_OFF + CONV_CH
CH_OFF = CC_OFF + CONV_CH

_f32 = jnp.float32
_bf16 = jnp.bfloat16


def _rms(x, g):
    return x * lax.rsqrt(jnp.mean(x * x, axis=-1, keepdims=True) + EPS) * g


def _dot(a, b):
    return jnp.dot(a, b, preferred_element_type=_f32)


def _dup_halves(x):
    lane = lax.broadcasted_iota(jnp.int32, x.shape, 1)
    sw = pltpu.roll(x, HEAD_DIM, axis=1)
    lo = lane < HEAD_DIM
    return jnp.where(lo, x, sw).astype(_bf16), jnp.where(lo, sw, x).astype(_bf16)


def _kv_state(hn_bf, win_ref, kg_ref, bdk_ref):
    kv = _dot(hn_bf, win_ref[:, K_OFF:K_OFF + 2 * KV_DIM])
    k = kv[:, :KV_DIM]
    v = kv[:, KV_DIM:]
    ssk = _dot((k * k).astype(_bf16), bdk_ref[...])
    kn = k * lax.rsqrt(ssk + EPS) * kg_ref[...]
    return _dup_halves(kn) + _dup_halves(v)


def _prep_kernel(rb_ref, meta_ref, mixg_ref, win_ref, kg_ref, bdk_ref, bucket_ref,
                 k0_ref, k1_ref, v0_ref, v1_ref, ut_ref, bias_ref):
    hn = _rms(meta_ref[...], mixg_ref[...]).astype(_bf16)
    k0, k1, v0, v1 = _kv_state(hn, win_ref, kg_ref, bdk_ref)
    k0_ref[...] = k0
    k1_ref[...] = k1
    v0_ref[...] = v0
    v1_ref[...] = v1
    cch = _dot(hn, win_ref[:, CC_OFF:CC_OFF + 2 * CONV_CH])
    u = cch[:, :CONV_CH] * cch[:, CONV_CH:]
    ut_ref[...] = u[BLK - SUBLANES:, :]
    for f in range(2):
        bk = bucket_ref[f]
        for h in range(N_Q_HEADS):
            acc = jnp.full((BLK, 2 * BLK), NEG_INF, _f32)
            for b in range(N_BUCKETS):
                acc = jnp.where(bk == b, rb_ref[b, h] * LOG2E, acc)
            bias_ref[f, h // Q_GROUP, (h % Q_GROUP) * BLK:(h % Q_GROUP + 1) * BLK, :] = acc


def _mixer_kernel(sinks_ref,
                  x_ref, mixg_ref, win_ref, qg_ref, kg_ref, bias_ref, convw_ref, ag_ref, cg_ref,
                  wout_ref, fg_ref, wr_ref, br_ref, k0m_ref, k1m_ref, v0m_ref, v1m_ref, utm_ref,
                  bdq_ref, bdk_ref, tri_ref,
                  h2_ref, hn2_ref, route_t_ref, cnt_ref, cntb_ref,
                  kp0, kp1, vp0, vp1, ubuf, a_scr, cnt_acc):
    b = pl.program_id(0)
    t = pl.program_id(1)

    @pl.when(t == 0)
    def _():
        kp0[...] = k0m_ref[...]
        kp1[...] = k1m_ref[...]
        vp0[...] = v0m_ref[...]
        vp1[...] = v1m_ref[...]
        ubuf[0:SUBLANES, :] = utm_ref[...]

    @pl.when(jnp.logical_and(b == 0, t == 0))
    def _():
        cnt_acc[...] = jnp.zeros_like(cnt_acc)

    x = x_ref[...]
    hn = _rms(x, mixg_ref[...]).astype(_bf16)

    q = _dot(hn, win_ref[:, Q_OFF:Q_OFF + ATTN_DIM])
    ssq = _dot((q * q).astype(_bf16), bdq_ref[...])
    qn = (q * lax.rsqrt(ssq + EPS) * qg_ref[...]).astype(_bf16)
    kd0, kd1, vd0, vd1 = _kv_state(hn, win_ref, kg_ref, bdk_ref)
    kd = (kd0, kd1)
    vd = (vd0, vd1)
    kp = (kp0, kp1)
    vp = (vp0, vp1)

    lane_q = lax.broadcasted_iota(jnp.int32, (BLK, LANES), 1)
    lo_half = lane_q < HEAD_DIM
    row4 = lax.broadcasted_iota(jnp.int32, (Q_GROUP * BLK, 1), 0) // BLK
    first = jnp.where(t == 0, 0, 1)
    zero_bf = jnp.zeros((BLK, LANES), _bf16)

    for j in range(TM // BLK):
        rows = slice(j * BLK, (j + 1) * BLK)
        for g in range(N_KV_HEADS):
            if j == 0:
                kcat = jnp.concatenate([kp[g][...], kd[g][rows]], axis=0)
                vcat = jnp.concatenate([vp[g][...], vd[g][rows]], axis=0)
                bias = bias_ref[first, g]
            else:
                kcat = kd[g][(j - 1) * BLK:(j + 1) * BLK]
                vcat = vd[g][(j - 1) * BLK:(j + 1) * BLK]
                bias = bias_ref[1, g]
            qs = []
            for hh in range(Q_GROUP):
                h = g * Q_GROUP + hh
                qc = qn[rows, (h // 2) * LANES:(h // 2 + 1) * LANES]
                keep = lo_half if h % 2 == 0 else jnp.logical_not(lo_half)
                qs.append(jnp.where(keep, qc, zero_bf))
            q4 = jnp.concatenate(qs, axis=0)
            s = lax.dot_general(q4, kcat, (((1,), (1,)), ((), ())),
                                preferred_element_type=_f32) + bias
            sink = jnp.full((Q_GROUP * BLK, 1), sinks_ref[g * Q_GROUP], _f32)
            for hh in range(1, Q_GROUP):
                sink = jnp.where(row4 == hh, sinks_ref[g * Q_GROUP + hh], sink)
            m = jnp.maximum(jnp.max(s, axis=-1, keepdims=True), sink)
            p = jnp.exp2(s - m)
            l = jnp.sum(p, axis=-1, keepdims=True) + jnp.exp2(sink - m)
            o = _dot(p.astype(_bf16), vcat) / l
            for pp in range(Q_GROUP // 2):
                ev = o[(2 * pp) * BLK:(2 * pp + 1) * BLK]
                od = o[(2 * pp + 1) * BLK:(2 * pp + 2) * BLK]
                col = g * (Q_GROUP // 2) + pp
                a_scr[rows, col * LANES:(col + 1) * LANES] = jnp.where(lo_half, ev, od)

    last = slice(TM - BLK, TM)
    kp0[...] = kd0[last]
    kp1[...] = kd1[last]
    vp0[...] = vd0[last]
    vp1[...] = vd1[last]

    cb = _dot(hn, win_ref[:, CB_OFF:CB_OFF + CONV_CH])
    cch = _dot(hn, win_ref[:, CC_OFF:CC_OFF + 2 * CONV_CH])
    u = cch[:, :CONV_CH] * cch[:, CONV_CH:]
    ubuf[SUBLANES:, :] = u
    u1 = ubuf[SUBLANES - 1:SUBLANES - 1 + TM, :]
    u2 = ubuf[SUBLANES - 2:SUBLANES - 2 + TM, :]
    c = cb * (convw_ref[0:1, :] * u2 + convw_ref[1:2, :] * u1 + convw_ref[2:3, :] * u)
    ubuf[0:SUBLANES, :] = u[TM - SUBLANES:, :]

    an = _rms(a_scr[...], ag_ref[...]).astype(_bf16)
    cn = _rms(c, cg_ref[...]).astype(_bf16)
    h2 = x + _dot(an, wout_ref[0:ATTN_DIM, :]) + _dot(cn, wout_ref[ATTN_DIM:, :])
    for cc in range(ROW_CHUNKS):
        h2_ref[pl.ds(cc, TM, stride=ROW_CHUNKS), :] = h2[:, cc * LANES:(cc + 1) * LANES]

    hn2 = _rms(h2, fg_ref[...])
    hn2_ref[...] = pltpu.pack_elementwise([hn2[:, :PACKED_D], hn2[:, PACKED_D:]], packed_dtype=_bf16)

    r1 = _dot(hn2.astype(_bf16), wr_ref[...])
    lg = r1 + pltpu.roll(r1, HEAD_DIM, axis=1) + br_ref[...]
    lgt = lg.T
    ninf = jnp.float32(-jnp.inf)
    big = jnp.float32(LANES)
    row8 = lax.broadcasted_iota(jnp.int32, (SUBLANES, TM), 0).astype(_f32)

    gl = jnp.where(row8 < N_GROUPS, lgt[0:SUBLANES, :], ninf)
    gmax = jnp.max(gl, axis=0, keepdims=True)
    gsum = jnp.sum(jnp.exp(gl - gmax), axis=0, keepdims=True)
    g_p = 1.0 / gsum
    g_idx = jnp.min(jnp.where(gl == gmax, row8, big), axis=0, keepdims=True)
    el = lgt[ROUTER_E_OFF:ROUTER_E_OFF + EXPERTS_PER_GROUP, :]
    for g in range(1, N_GROUPS):
        lo_row = ROUTER_E_OFF + g * EXPERTS_PER_GROUP
        el = jnp.where(g_idx == g, lgt[lo_row:lo_row + EXPERTS_PER_GROUP, :], el)
    m1 = jnp.max(el, axis=0, keepdims=True)
    i1 = jnp.min(jnp.where(el == m1, row8, big), axis=0, keepdims=True)
    el2 = jnp.where(row8 == i1, ninf, el)
    m2 = jnp.max(el2, axis=0, keepdims=True)
    i2 = jnp.min(jnp.where(el2 == m2, row8, big), axis=0, keepdims=True)
    ex = jnp.exp(m2 - m1)
    den = 1.0 / (1.0 + ex)
    gate0 = g_p * den
    gate1 = g_p * ex * den
    e0 = g_idx * EXPERTS_PER_GROUP + i1
    e1 = g_idx * EXPERTS_PER_GROUP + i2

    rowe = lax.broadcasted_iota(jnp.int32, (N_EXPERTS, TM), 0).astype(_f32)
    oh0 = rowe == e0
    oh1 = rowe == e1
    cmat = jnp.where(oh0, 1.0, 0.0) + jnp.where(oh1, 1.0, 0.0)
    before = cnt_acc[...]
    prefix = _dot(cmat.astype(_bf16), tri_ref[...]) + before[:, 0:1]
    rank0 = jnp.sum(jnp.where(oh0, prefix, 0.0), axis=0, keepdims=True)
    rank1 = jnp.sum(jnp.where(oh1, prefix, 0.0), axis=0, keepdims=True)
    cntb_ref[...] = before
    cnt_new = before + jnp.sum(cmat, axis=1, keepdims=True)
    cnt_acc[...] = cnt_new
    cnt_ref[...] = cnt_new

    zero_row = jnp.zeros((1, TM), _f32)
    route_t_ref[...] = jnp.concatenate([e0, e1, rank0, rank1, gate0, gate1, zero_row, zero_row], axis=0)


def _dispatch_rows(hn2_rows, pos0, pos1, n_rows):
    n_tok = hn2_rows.shape[0]
    info = plsc.get_sparse_core_info()
    n_workers = info.num_cores * info.num_subcores
    per_worker = n_tok // n_workers
    n_chunks = per_worker // SC_WINDOW
    assert per_worker * n_workers == n_tok and n_chunks * SC_WINDOW == per_worker and n_chunks % 2 == 0
    mesh = plsc.VectorSubcoreMesh(core_axis_name="core", subcore_axis_name="subcore")

    @functools.partial(
        pl.kernel,
        out_type=jax.ShapeDtypeStruct((n_rows, PACKED_D), jnp.int32),
        mesh=mesh,
        scratch_types=[pltpu.VMEM((SC_WINDOW,), jnp.int32), pltpu.VMEM((SC_WINDOW,), jnp.int32),
                       pltpu.VMEM((SC_WINDOW,), jnp.int32), pltpu.VMEM((SC_WINDOW,), jnp.int32),
                       pltpu.VMEM((SC_WINDOW, PACKED_D), jnp.int32), pltpu.VMEM((SC_WINDOW, PACKED_D), jnp.int32),
                       pltpu.SemaphoreType.DMA((2,)), pltpu.SemaphoreType.DMA((2,))],
        compiler_params=pltpu.CompilerParams(use_tc_tiling_on_sc=True),
        name="dispatch",
    )
    def dispatch(x_hbm, i0_hbm, i1_hbm, o_hbm, i0_a, i0_b, i1_a, i1_b, rows_a, rows_b, sem_ld, sem_st):
        wid = lax.axis_index("subcore") * info.num_cores + lax.axis_index("core")
        base = wid * per_worker
        i0_v, i1_v, rows_v = (i0_a, i0_b), (i1_a, i1_b), (rows_a, rows_b)

        def loads(chunk, b):
            off = pl.multiple_of(base + chunk * SC_WINDOW, SC_WINDOW)
            return (pltpu.make_async_copy(i0_hbm.at[pl.ds(off, SC_WINDOW)], i0_v[b], sem_ld.at[b]),
                    pltpu.make_async_copy(i1_hbm.at[pl.ds(off, SC_WINDOW)], i1_v[b], sem_ld.at[b]),
                    pltpu.make_async_copy(x_hbm.at[pl.ds(off, SC_WINDOW)], rows_v[b], sem_ld.at[b]))

        def stores(b):
            return (pltpu.make_async_copy(rows_v[b], o_hbm.at[i0_v[b]], sem_st.at[b]),
                    pltpu.make_async_copy(rows_v[b], o_hbm.at[i1_v[b]], sem_st.at[b]))

        for d in loads(0, 0):
            d.start()

        @pl.loop(0, n_chunks, step=2)
        def _(c):
            for b in range(2):
                chunk = c + b
                for d in loads(chunk, b):
                    d.wait()
                for d in stores(b):
                    d.start()

                @pl.when(chunk >= 1)
                def _():
                    for d in stores(1 - b):
                        d.wait()

                @pl.when(chunk + 1 < n_chunks)
                def _():
                    for d in loads(chunk + 1, 1 - b):
                        d.start()

        for d in stores(1):
            d.wait()

    return dispatch(hn2_rows, pos0, pos1)


def _experts_kernel(te_ref, tn_ref,
                    xs_hbm, wg_ref, wu_ref, wd_ref,
                    y_ref,
                    xbuf, sem, wgb, wub, wdb):
    i = pl.program_id(0)
    n = pl.num_programs(0)

    def fetch(tile):
        sl = tile % XS_SLOTS
        return pltpu.make_async_copy(xs_hbm.at[pl.ds(pl.multiple_of(tile * TME, TME), TME), :],
                                     xbuf.at[sl], sem.at[sl])

    @pl.when(i == 0)
    def _():
        for tile in range(XS_SLOTS - 1):
            fetch(tile).start()

    @pl.when(i + XS_SLOTS - 1 < n)
    def _():
        fetch(i + XS_SLOTS - 1).start()

    valid = tn_ref[i] > 0
    changed = jnp.logical_or(i == 0, te_ref[i] != te_ref[jnp.maximum(i - 1, 0)])

    @pl.when(jnp.logical_and(changed, valid))
    def _():
        wgb[...] = wg_ref[...].astype(_bf16)
        wub[...] = wu_ref[...].astype(_bf16)
        wdb[...] = wd_ref[...].astype(_bf16)

    fetch(i).wait()

    @pl.when(valid)
    def _():
        packed = xbuf[i % XS_SLOTS]
        halves = [pltpu.unpack_elementwise(packed, index=k, packed_dtype=_bf16, unpacked_dtype=_f32)
                  for k in range(2)]
        xb = jnp.concatenate(halves, axis=1).astype(_bf16)
        live = lax.broadcasted_iota(jnp.int32, (TME, 1), 0) < tn_ref[i]
        xb = jnp.where(live, xb, jnp.zeros_like(xb))
        gate = _dot(xb, wgb[...])
        up = _dot(xb, wub[...])
        act = (gate * jax.nn.sigmoid(gate) * up).astype(_bf16)
        y = _dot(act, wdb[...])
        for cc in range(ROW_CHUNKS):
            y_ref[pl.ds(cc, TME, stride=ROW_CHUNKS), :] = y[:, cc * LANES:(cc + 1) * LANES]

    @pl.when(jnp.logical_not(valid))
    def _():
        y_ref[...] = jnp.zeros_like(y_ref)


def _combine_kernel(nwin_ref, wstart_ref, widx0_ref, widx1_ref, gate0_ref, gate1_ref,
                    y_hbm, h2_ref,
                    out_ref,
                    wbuf, sem, otile):
    i = pl.program_id(0)
    n = pl.num_programs(0)
    slot = i % 2
    win_rows = WIN * ROW_CHUNKS

    def window(tile, sl, s):
        src = pl.multiple_of(wstart_ref[tile * MAX_WIN + s] * ROW_CHUNKS, ROW_CHUNKS)
        return pltpu.make_async_copy(y_hbm.at[pl.ds(src, win_rows), :],
                                     wbuf.at[sl, pl.ds(pl.multiple_of(s * win_rows, win_rows), win_rows), :],
                                     sem.at[sl])

    def issue(tile, sl):
        count = nwin_ref[tile]

        def body(g, carry):
            window(tile, sl, 2 * g).start(priority=0)
            window(tile, sl, 2 * g + 1).start(priority=1)
            return carry

        lax.fori_loop(0, count // 2, body, 0)

        @pl.when(count % 2 == 1)
        def _():
            window(tile, sl, count - 1).start(priority=0)

    @pl.when(i == 0)
    def _():
        issue(0, 0)

    @pl.when(i + 1 < n)
    def _():
        issue(i + 1, 1 - slot)

    def wait_windows(k):
        rows = k * win_rows
        pltpu.make_async_copy(y_hbm.at[pl.ds(0, rows), :], wbuf.at[slot, pl.ds(0, rows), :], sem.at[slot]).wait()

    def wait_body(s, carry):
        wait_windows(WAIT_GROUP)
        return carry

    lax.fori_loop(0, nwin_ref[i] // WAIT_GROUP, wait_body, 0)
    k = WAIT_GROUP // 2
    while k >= 1:
        @pl.when((nwin_ref[i] & k) != 0)
        def _(k=k):
            wait_windows(k)
        k //= 2

    def token_body(it, carry):
        for j in range(CMB_UNROLL):
            r = it * CMB_UNROLL + j
            tok = i * TMC + r
            y0 = wbuf[slot, pl.ds(pl.multiple_of(widx0_ref[tok], ROW_CHUNKS), ROW_CHUNKS), :]
            y1 = wbuf[slot, pl.ds(pl.multiple_of(widx1_ref[tok], ROW_CHUNKS), ROW_CHUNKS), :]
            row = pl.ds(pl.multiple_of(r * ROW_CHUNKS, ROW_CHUNKS), ROW_CHUNKS)
            otile[row, :] = h2_ref[row, :] + (gate0_ref[tok] * y0 + gate1_ref[tok] * y1)
        return carry

    lax.fori_loop(0, TMC // CMB_UNROLL, token_body, 0)
    for cc in range(ROW_CHUNKS):
        out_ref[:, cc * LANES:(cc + 1) * LANES] = otile[pl.ds(cc, TMC, stride=ROW_CHUNKS), :]


def _t5_bucket(n):
    max_exact = N_BUCKETS // 2
    nf = jnp.maximum(n, 1).astype(_f32)
    large = max_exact + (jnp.log(nf / max_exact) / np.log(MAX_DISTANCE / max_exact)
                         * (N_BUCKETS - max_exact)).astype(jnp.int32)
    large = jnp.minimum(large, N_BUCKETS - 1)
    return jnp.where(n < max_exact, n, large)


def _bucket_maps():
    qi = jnp.arange(BLK)[:, None]
    sj = jnp.arange(2 * BLK)[None, :]
    dist = BLK + qi - sj
    band = (dist >= 0) & (dist < BLK)
    bucket = _t5_bucket(jnp.maximum(dist, 0))
    generic = jnp.where(band, bucket, -1)
    first = jnp.where(band & (sj >= PAD), bucket, -1)
    return jnp.stack([first, generic]).astype(jnp.int32)


def _const(shape):
    nd = len(shape)
    return pl.BlockSpec(shape, lambda *_: (0,) * nd)


def _block_diag_mean(n):
    idx = np.arange(n) // HEAD_DIM
    return jnp.asarray((idx[:, None] == idx[None, :]).astype(np.float32) / HEAD_DIM, dtype=_bf16)


def kernel(x, meta_tokens, rel_bias, mix_norm_g, w_in, q_norm_g, k_norm_g, attn_sinks, conv_w, attn_out_norm_g, conv_out_norm_g, w_out, ffn_norm_g, w_group_router, b_group_router, w_expert_router, b_expert_router, w_gate, w_up, w_down):
    bsz, seq, _ = x.shape
    assert seq % TM == 0 and (bsz * seq) % TMC == 0
    n_tok = bsz * seq
    nt = seq // TM

    win = w_in[0].astype(_bf16)
    wout = w_out[0].astype(_bf16)
    mixg = mix_norm_g[0].reshape(1, D_MODEL)
    fg = ffn_norm_g[0].reshape(1, D_MODEL)
    qg = (jnp.tile(q_norm_g[0], N_Q_HEADS) * (HEAD_DIM ** -0.5 * LOG2E)).reshape(1, ATTN_DIM)
    kg = jnp.tile(k_norm_g[0], N_KV_HEADS).reshape(1, KV_DIM)
    ag = attn_out_norm_g[0].reshape(1, ATTN_DIM)
    cg = conv_out_norm_g[0].reshape(1, CONV_CH)
    convw = conv_w[0]
    sinks = attn_sinks[0] * LOG2E
    n_r = ROUTER_E_OFF + N_EXPERTS
    w_r = jnp.zeros((D_MODEL, n_r), _f32)
    w_r = w_r.at[:, :N_GROUPS].set(w_group_router[0])
    w_r = w_r.at[:, ROUTER_E_OFF:].set(w_expert_router[0].reshape(D_MODEL, N_EXPERTS))
    w_r_hi = w_r.astype(_bf16)
    w_r_lo = (w_r - w_r_hi.astype(_f32)).astype(_bf16)
    wr = jnp.zeros((D_MODEL, LANES), _bf16)
    wr = wr.at[:, :n_r].set(w_r_hi).at[:, HEAD_DIM:HEAD_DIM + n_r].set(w_r_lo)
    br = jnp.zeros((1, LANES), _f32).at[0, :N_GROUPS].set(b_group_router[0])
    br = br.at[0, ROUTER_E_OFF:n_r].set(b_expert_router[0].reshape(N_EXPERTS))
    meta_blk = jnp.concatenate([jnp.zeros((PAD, D_MODEL), x.dtype), meta_tokens.astype(x.dtype)], axis=0)
    bdq = _block_diag_mean(ATTN_DIM)
    bdk = _block_diag_mean(KV_DIM)
    tri = jnp.asarray(np.triu(np.ones((TM, TM), np.float32), 1), dtype=_bf16)

    kv_sd = jax.ShapeDtypeStruct((BLK, KV_DIM), _bf16)
    k0m, k1m, v0m, v1m, utm, bias = pl.pallas_call(
        _prep_kernel,
        out_shape=(kv_sd, kv_sd, kv_sd, kv_sd,
                   jax.ShapeDtypeStruct((SUBLANES, CONV_CH), _f32),
                   jax.ShapeDtypeStruct((2, N_KV_HEADS, Q_GROUP * BLK, 2 * BLK), _f32)),
        in_specs=[pl.BlockSpec(memory_space=pltpu.SMEM)] + [pl.BlockSpec(memory_space=pltpu.VMEM)] * 6,
        out_specs=tuple(pl.BlockSpec(memory_space=pltpu.VMEM) for _ in range(6)),
        compiler_params=pltpu.CompilerParams(vmem_limit_bytes=VMEM_LIMIT),
        name="prep",
    )(rel_bias, meta_blk, mixg, win, kg, bdk, _bucket_maps())

    consts = (mixg, win, qg, kg, bias, convw, ag, cg, wout, fg, wr, br, k0m, k1m, v0m, v1m, utm, bdq, bdk, tri)
    h2, hn2, route_t, cnt, cntb = _mixer_call(sinks, x.reshape(n_tok, D_MODEL), consts, nt)
    pos2, t_exp, t_rows, n_tiles, tables = _plan(route_t, cnt, cntb, n_tok)
    xs = _dispatch_rows(hn2, pos2[0], pos2[1], n_tiles * TME)
    y_sorted = _experts_call(t_exp, t_rows, xs, w_gate[0], w_up[0], w_down[0], n_tiles)
    out = _combine_call(tables, y_sorted, h2)
    return out.reshape(bsz, seq, D_MODEL)


def _mixer_call(sinks, x_rows, consts, nt):
    n_tok = x_rows.shape[0]
    tile_idx = lambda b, t, *_: (b * nt + t, 0)
    grid_spec = pltpu.PrefetchScalarGridSpec(
        num_scalar_prefetch=1,
        grid=(n_tok // (nt * TM), nt),
        in_specs=[
            pl.BlockSpec((TM, D_MODEL), tile_idx),
            _const((1, D_MODEL)), _const((D_MODEL, IN_PROJ)), _const((1, ATTN_DIM)), _const((1, KV_DIM)),
            _const((2, N_KV_HEADS, Q_GROUP * BLK, 2 * BLK)), _const((3, CONV_CH)),
            _const((1, ATTN_DIM)), _const((1, CONV_CH)), _const((D_MODEL, D_MODEL)), _const((1, D_MODEL)),
            _const((D_MODEL, LANES)), _const((1, LANES)),
            _const((BLK, KV_DIM)), _const((BLK, KV_DIM)), _const((BLK, KV_DIM)), _const((BLK, KV_DIM)),
            _const((SUBLANES, CONV_CH)),
            _const((ATTN_DIM, ATTN_DIM)), _const((KV_DIM, KV_DIM)), _const((TM, TM)),
        ],
        out_specs=[
            pl.BlockSpec((TM * ROW_CHUNKS, LANES), tile_idx),
            pl.BlockSpec((TM, PACKED_D), tile_idx),
            pl.BlockSpec((SUBLANES, TM), lambda b, t, *_: (0, b * nt + t)),
            _const((N_EXPERTS, LANES)),
            pl.BlockSpec((N_EXPERTS, LANES), tile_idx),
        ],
        scratch_shapes=[
            pltpu.VMEM((BLK, KV_DIM), _bf16), pltpu.VMEM((BLK, KV_DIM), _bf16),
            pltpu.VMEM((BLK, KV_DIM), _bf16), pltpu.VMEM((BLK, KV_DIM), _bf16),
            pltpu.VMEM((TM + SUBLANES, CONV_CH), _f32),
            pltpu.VMEM((TM, ATTN_DIM), _f32),
            pltpu.VMEM((N_EXPERTS, LANES), _f32),
        ],
    )
    return pl.pallas_call(
        _mixer_kernel,
        grid_spec=grid_spec,
        out_shape=(jax.ShapeDtypeStruct((n_tok * ROW_CHUNKS, LANES), _f32),
                   jax.ShapeDtypeStruct((n_tok, PACKED_D), jnp.int32),
                   jax.ShapeDtypeStruct((SUBLANES, n_tok), _f32),
                   jax.ShapeDtypeStruct((N_EXPERTS, LANES), _f32),
                   jax.ShapeDtypeStruct((n_tok // TM * N_EXPERTS, LANES), _f32)),
        compiler_params=pltpu.CompilerParams(dimension_semantics=("arbitrary", "arbitrary"),
                                             vmem_limit_bytes=VMEM_LIMIT),
        name="mixer",
    )(sinks, x_rows, *consts)


def _plan(route_t, cnt, cntb, n_tok):
    n_tiles = (n_tok * 2) // TME + N_EXPERTS + 1
    counts = cnt[:, 0].astype(jnp.int32)
    ntile = (counts + TME - 1) // TME
    tile_end = jnp.cumsum(ntile)
    tile_start = tile_end - ntile
    eid = route_t[0:2].astype(jnp.int32)
    rank = route_t[2:4].astype(jnp.int32)
    gates = route_t[4:6]
    experts = jnp.arange(N_EXPERTS, dtype=jnp.int32)
    start_of = jnp.sum(jnp.where(eid[None] == experts[:, None, None], tile_start[:, None, None], 0), axis=0)
    pos2 = start_of * TME + rank
    tiles = jnp.arange(n_tiles, dtype=jnp.int32)
    n_used = tile_end[-1]
    t_exp = jnp.sum((jnp.minimum(tiles, n_used - 1)[:, None] >= tile_end[None, :]).astype(jnp.int32), axis=-1)
    t_exp = jnp.minimum(t_exp, N_EXPERTS - 1)
    own = t_exp[:, None] == experts
    t_rows = jnp.sum(jnp.where(own, counts - (tiles[:, None] - tile_start) * TME, 0), axis=-1)
    t_rows = jnp.where(tiles < n_used, jnp.clip(t_rows, 0, TME), 0).astype(jnp.int32)

    n_mix = n_tok // TMC
    before = cntb.reshape(n_mix, N_EXPERTS, LANES)[:, :, 0].astype(jnp.int32)
    held = jnp.concatenate([before[1:], counts[None]], axis=0) - before
    first = tile_start[None, :] * TME + before
    nw = (held + WIN - 1) // WIN
    slot_end = jnp.cumsum(nw, axis=1)
    slot_beg = slot_end - nw
    nwin = slot_end[:, -1]
    slots = jnp.arange(MAX_WIN, dtype=jnp.int32)
    owner = jnp.minimum(jnp.sum((slots[None, :, None] >= slot_end[:, None, :]).astype(jnp.int32), axis=-1),
                        N_EXPERTS - 1)
    is_owner = owner[:, :, None] == experts
    wstart = jnp.sum(jnp.where(is_owner, first[:, None, :] + (slots[None, :, None] - slot_beg[:, None, :]) * WIN,
                               0), axis=-1)
    wstart = jnp.where(slots[None, :] < nwin[:, None], wstart, 0).astype(jnp.int32).reshape(-1)
    mine = eid.reshape(2, n_mix, TMC, 1) == experts
    beg_tok = jnp.sum(jnp.where(mine, slot_beg[None, :, None, :], 0), axis=-1)
    before_tok = jnp.sum(jnp.where(mine, before[None, :, None, :], 0), axis=-1)
    widx = (beg_tok * WIN + rank.reshape(2, n_mix, TMC) - before_tok).reshape(2, n_tok).astype(jnp.int32)
    widx = widx * ROW_CHUNKS
    return pos2, t_exp, t_rows, n_tiles, (nwin.astype(jnp.int32), wstart, widx[0], widx[1], gates[0], gates[1])


def _experts_call(t_exp, t_rows, xs, w_gate, w_up, w_down, n_tiles):
    return pl.pallas_call(
        _experts_kernel,
        grid_spec=pltpu.PrefetchScalarGridSpec(
            num_scalar_prefetch=2,
            grid=(n_tiles,),
            in_specs=[
                pl.BlockSpec(memory_space=pl.ANY),
                pl.BlockSpec((None, D_MODEL, D_EXPERT), lambda i, te, tn: (te[i], 0, 0)),
                pl.BlockSpec((None, D_MODEL, D_EXPERT), lambda i, te, tn: (te[i], 0, 0)),
                pl.BlockSpec((None, D_EXPERT, D_MODEL), lambda i, te, tn: (te[i], 0, 0)),
            ],
            out_specs=pl.BlockSpec((TME * ROW_CHUNKS, LANES), lambda i, *_: (i, 0)),
            scratch_shapes=[
                pltpu.VMEM((XS_SLOTS, TME, PACKED_D), jnp.int32),
                pltpu.SemaphoreType.DMA((XS_SLOTS,)),
                pltpu.VMEM((D_MODEL, D_EXPERT), _bf16),
                pltpu.VMEM((D_MODEL, D_EXPERT), _bf16),
                pltpu.VMEM((D_EXPERT, D_MODEL), _bf16),
            ],
        ),
        out_shape=jax.ShapeDtypeStruct((n_tiles * TME * ROW_CHUNKS, LANES), _f32),
        compiler_params=pltpu.CompilerParams(dimension_semantics=("arbitrary",),
                                             vmem_limit_bytes=VMEM_LIMIT),
        name="experts",
    )(t_exp, t_rows, xs, w_gate, w_up, w_down)


def _combine_call(tables, y_sorted, h2):
    n_tok = h2.shape[0] // ROW_CHUNKS
    return pl.pallas_call(
        _combine_kernel,
        grid_spec=pltpu.PrefetchScalarGridSpec(
            num_scalar_prefetch=len(tables),
            grid=(n_tok // TMC,),
            in_specs=[
                pl.BlockSpec(memory_space=pl.ANY),
                pl.BlockSpec((TMC * ROW_CHUNKS, LANES), lambda i, *_: (i, 0)),
            ],
            out_specs=pl.BlockSpec((TMC, D_MODEL), lambda i, *_: (i, 0)),
            scratch_shapes=[
                pltpu.VMEM((2, MAX_WIN * WIN * ROW_CHUNKS, LANES), _f32),
                pltpu.SemaphoreType.DMA((2,)),
                pltpu.VMEM((TMC * ROW_CHUNKS, LANES), _f32),
            ],
        ),
        out_shape=jax.ShapeDtypeStruct((n_tok, D_MODEL), _f32),
        compiler_params=pltpu.CompilerParams(dimension_semantics=("arbitrary",),
                                             vmem_limit_bytes=VMEM_LIMIT),
        name="combine",
    )(*tables, y_sorted, h2)
```
